```python
import math
import jax, jax.numpy as jnp
from jax import lax
import numpy as np

D_MODEL = 2048
BATCH = 8
SEQ = 4096
DEPTH = 2

N_BRANCH = 4
BRANCH_WIDTH = 512
HEAD_DIM = 64
ATT_BLOCK = 128
ROPE_THETA = 500000.0
ROPE_DIM = HEAD_DIM // 4
NORM_EPS = 1e-6
GLA_HEADS = 4
GLA_DK = 64
GLA_DV = BRANCH_WIDTH // GLA_HEADS
GLA_LOWRANK = 16
GLA_TAU = 16.0
GLA_CHUNK = 64
GLA_SUB = 16
S5_WIDTH = BRANCH_WIDTH
S5_GROUP = 16
S5_GROUPS = S5_WIDTH // S5_GROUP
S5_STATE = 64
S5_DT_MIN = 0.001
S5_DT_MAX = 0.1
DIL_HEADS = BRANCH_WIDTH // HEAD_DIM
DIL_CONFIGS = ((128, 1), (512, 4), (2048, 16))
DIL_MAX_DILATION = 16
SWA_HEADS = BRANCH_WIDTH // HEAD_DIM
SWA_KV_HEADS = 2
SWA_WINDOW = 128
FFN_MULTIPLE = 256
FFN_HIDDEN = -((-8 * D_MODEL) // (3 * FFN_MULTIPLE)) * FFN_MULTIPLE
IN_SPLITS = (GLA_HEADS * GLA_DK, GLA_HEADS * GLA_DK, GLA_HEADS * GLA_DV, GLA_HEADS * GLA_DV, GLA_LOWRANK,
             S5_WIDTH,
             DIL_HEADS * HEAD_DIM, DIL_HEADS * HEAD_DIM, DIL_HEADS * HEAD_DIM,
             SWA_HEADS * HEAD_DIM, SWA_KV_HEADS * HEAD_DIM, SWA_KV_HEADS * HEAD_DIM,
             N_BRANCH * D_MODEL)
D_IN = sum(IN_SPLITS)

kernel_name = 'hybrid_gla_s5_dilated_swa_block'


def rms_norm(x, g):
    xf = x.astype(jnp.float32)
    y = xf * lax.rsqrt(jnp.mean(xf * xf, axis=-1, keepdims=True) + NORM_EPS)
    return (y * g.astype(jnp.float32)).astype(x.dtype)


def rotary_partial(x, positions):
    half = ROPE_DIM // 2
    inv_freq = ROPE_THETA ** (-jnp.arange(half, dtype=jnp.float32) / half)
    ang = positions.astype(jnp.float32)[:, :, None] * inv_freq
    cos, sin = jnp.cos(ang)[:, :, None, :], jnp.sin(ang)[:, :, None, :]
    xr = x[..., :ROPE_DIM].astype(jnp.float32)
    x1, x2 = xr[..., :half], xr[..., half:]
    rot = jnp.concatenate([x1 * cos - x2 * sin, x2 * cos + x1 * sin], axis=-1)
    return jnp.concatenate([rot.astype(x.dtype), x[..., ROPE_DIM:]], axis=-1)


def banded_attention(q, k, v, max_dist):
    f = jnp.float32
    b, l, hq, hd = q.shape
    hkv = k.shape[2]
    grp = hq // hkv
    t = ATT_BLOCK
    nb = l // t
    qb = q.astype(f).reshape(b, nb, t, hkv, grp, hd)

    def with_prev(z):
        z = z.astype(f).reshape(b, nb, t, hkv, hd)
        prev = jnp.concatenate([jnp.zeros_like(z[:, :1]), z[:, :-1]], axis=1)
        return jnp.concatenate([prev, z], axis=2)

    kk, vv = with_prev(k), with_prev(v)
    s = jnp.einsum('bnqhgd,bnkhd->bnhgqk', qb, kk) * (hd ** -0.5)
    qi = jnp.arange(t)[:, None]
    kj = jnp.arange(2 * t)[None, :]
    dist = t + qi - kj
    blk = jnp.arange(nb)[:, None, None]
    valid = (dist >= 0) & (dist <= max_dist) & (blk * t + kj - t >= 0)
    s = jnp.where(valid[None, :, None, None], s, -jnp.inf)
    m = jnp.max(s, axis=-1, keepdims=True)
    p = jnp.exp(s - m)
    den = jnp.sum(p, axis=-1)
    o = jnp.einsum('bnhgqk,bnkhd->bnqhgd', p, vv) / den.transpose(0, 1, 4, 2, 3)[..., None]
    lse = (m[..., 0] + jnp.log(den)).transpose(0, 1, 4, 2, 3)
    return o.reshape(b, l, hq, hd).astype(q.dtype), lse.reshape(b, l, hq)


def to_strided(z, dil):
    b, lp, h, hd = z.shape
    return z.reshape(b, lp // dil, dil, h, hd).transpose(0, 2, 1, 3, 4).reshape(b * dil, lp // dil, h, hd)


def from_strided(z, b, dil):
    bd, ls = z.shape[:2]
    rest = z.shape[2:]
    z = z.reshape((b, dil, ls) + rest)
    z = jnp.swapaxes(z, 1, 2)
    return z.reshape((b, ls * dil) + rest)


def dilated_attention(q, k, v):
    f = jnp.float32
    b, l, h, hd = q.shape
    span = ATT_BLOCK * DIL_MAX_DILATION
    lp = -(-l // span) * span
    pad = ((0, 0), (0, lp - l), (0, 0), (0, 0))
    q, k, v = jnp.pad(q, pad), jnp.pad(k, pad), jnp.pad(v, pad)
    outs, lses = [], []
    for window, dil in DIL_CONFIGS:
        o, lse = banded_attention(to_strided(q, dil), to_strided(k, dil), to_strided(v, dil), window // dil)
        outs.append(from_strided(o, b, dil)[:, :l].astype(f))
        lses.append(from_strided(lse, b, dil)[:, :l])
    wts = jax.nn.softmax(jnp.stack(lses, axis=0), axis=0)
    return jnp.einsum('cblh,cblhd->blhd', wts, jnp.stack(outs, axis=0))


def gla_chunked(q, k, v, log_a):
    f = jnp.float32
    b, l, h, dk = q.shape
    dv = v.shape[-1]
    c = GLA_CHUNK
    n = l // c
    ns = c // GLA_SUB
    t = GLA_SUB

    def chunks(z):
        return z.astype(f).reshape(b, n, c, h, z.shape[-1]).transpose(0, 3, 1, 2, 4)

    q, k, v, g = chunks(q), chunks(k), chunks(v), chunks(log_a)
    q = q * (dk ** -0.5)
    cum = jnp.cumsum(g, axis=3)
    qs = q.reshape(b, h, n, ns, t, dk)
    ks = k.reshape(b, h, n, ns, t, dk)
    cs = cum.reshape(b, h, n, ns, t, dk)
    ref = cs[:, :, :, :, 0] - g.reshape(b, h, n, ns, t, dk)[:, :, :, :, 0]
    q_ref = qs * jnp.exp(cs - ref[:, :, :, :, None])
    earlier = (jnp.arange(c) // t)[None, :] < jnp.arange(ns)[:, None]
    k_ref = k[:, :, :, None] * jnp.exp(jnp.where(earlier[:, :, None], ref[:, :, :, :, None] - cum[:, :, :, None], -jnp.inf))
    a_off = jnp.einsum('bhnstk,bhnsjk->bhnstj', q_ref, k_ref)
    tri = jnp.arange(t)[:, None] >= jnp.arange(t)[None, :]
    dec = jnp.exp(jnp.where(tri[:, :, None], cs[..., :, None, :] - cs[..., None, :, :], -jnp.inf))
    a_diag = jnp.einsum('bhnstk,bhnsuk,bhnstuk->bhnstu', qs, ks, dec)
    a = a_off + (a_diag[:, :, :, :, :, None, :] * jnp.eye(ns, dtype=f)[:, None, :, None]).reshape(b, h, n, ns, t, c)
    o_intra = jnp.einsum('bhnstj,bhnjv->bhnstv', a, v).reshape(b, h, n, c, dv)
    last = cum[:, :, :, -1]
    kv = jnp.einsum('bhnck,bhncv->nbhkv', k * jnp.exp(last[:, :, :, None] - cum), v)

    def step(state, inp):
        decay, upd = inp
        return state * decay[..., None] + upd, state

    _, states = lax.scan(step, jnp.zeros((b, h, dk, dv), f), (jnp.exp(last).transpose(2, 0, 1, 3), kv))
    o_inter = jnp.einsum('bhnck,nbhkv->bhncv', q * jnp.exp(cum), states)
    return (o_intra + o_inter).transpose(0, 2, 3, 1, 4).reshape(b, l, h, dv)


def s5_mixer(u, lam_re, lam_im, log_dt, b_re, b_im, c_re, c_im, d, glu_w, glu_b):
    f = jnp.float32
    bsz, l, _ = u.shape
    uf = u.astype(f).reshape(bsz, l, S5_GROUPS, S5_GROUP)
    dt = jnp.exp(log_dt.astype(f))[:, None]
    lr, li = lam_re.astype(f), lam_im.astype(f)
    mag = jnp.exp(lr * dt)
    ab_re, ab_im = mag * jnp.cos(li * dt), mag * jnp.sin(li * dt)
    den = lr * lr + li * li
    z_re = ((ab_re - 1.0) * lr + ab_im * li) / den
    z_im = (ab_im * lr - (ab_re - 1.0) * li) / den
    br, bi = b_re.astype(f), b_im.astype(f)
    bb_re = z_re[..., None] * br - z_im[..., None] * bi
    bb_im = z_re[..., None] * bi + z_im[..., None] * br
    x_re = jnp.einsum('gnc,blgc->blgn', bb_re, uf)
    x_im = jnp.einsum('gnc,blgc->blgn', bb_im, uf)
    a_re = jnp.broadcast_to(ab_re, x_re.shape)
    a_im = jnp.broadcast_to(ab_im, x_im.shape)

    def combine(e1, e2):
        a1r, a1i, b1r, b1i = e1
        a2r, a2i, b2r, b2i = e2
        return (a1r * a2r - a1i * a2i, a1r * a2i + a1i * a2r,
                a2r * b1r - a2i * b1i + b2r, a2r * b1i + a2i * b1r + b2i)

    _, _, h_re, h_im = lax.associative_scan(combine, (a_re, a_im, x_re, x_im), axis=1)
    y = (jnp.einsum('gcn,blgn->blgc', c_re.astype(f), h_re)
         - jnp.einsum('gcn,blgn->blgc', c_im.astype(f), h_im)
         + d.astype(f) * uf).reshape(bsz, l, S5_WIDTH)
    z = jax.nn.gelu(y)
    return z * jax.nn.sigmoid(z @ glu_w.astype(f) + glu_b.astype(f))


def hybrid_mixer(h, positions, w_in, gla_a2, gla_a_b, gla_norm_g, s5_lambda_re, s5_lambda_im, s5_log_dt,
                 s5_b_re, s5_b_im, s5_c_re, s5_c_im, s5_d, s5_glu_w, s5_glu_b, swa_sinks, w_branch, w_out):
    f = jnp.float32
    b, l, _ = h.shape
    proj = h @ w_in
    points = np.cumsum(IN_SPLITS)[:-1].tolist()
    (gq, gk, gv, gr, glr, s5u, cq, ck, cv, sq, sk, sv, gates) = jnp.split(proj, points, axis=-1)

    def heads(z, nh):
        return z.reshape(b, l, nh, -1)

    log_a = jax.nn.log_sigmoid((glr @ gla_a2 + gla_a_b).astype(f)) / GLA_TAU
    o = gla_chunked(heads(gq, GLA_HEADS), heads(gk, GLA_HEADS), heads(gv, GLA_HEADS), heads(log_a, GLA_HEADS))
    o_gla = rms_norm(o, gla_norm_g) * jax.nn.silu(heads(gr, GLA_HEADS).astype(f))
    o_s5 = s5_mixer(s5u, s5_lambda_re, s5_lambda_im, s5_log_dt, s5_b_re, s5_b_im, s5_c_re, s5_c_im,
                    s5_d, s5_glu_w, s5_glu_b)
    o_dil = dilated_attention(rotary_partial(heads(cq, DIL_HEADS), positions),
                              rotary_partial(heads(ck, DIL_HEADS), positions), heads(cv, DIL_HEADS))
    o, lse = banded_attention(rotary_partial(heads(sq, SWA_HEADS), positions),
                              rotary_partial(heads(sk, SWA_KV_HEADS), positions),
                              heads(sv, SWA_KV_HEADS), SWA_WINDOW - 1)
    o_swa = o.astype(f) * jax.nn.sigmoid(lse - swa_sinks.astype(f))[..., None]
    branch_outs = (o_gla, o_s5, o_dil, o_swa)
    gate = jax.nn.sigmoid(gates.reshape(b, l, N_BRANCH, D_MODEL))
    mixed = sum(gate[:, :, m] * (br.reshape(b, l, BRANCH_WIDTH).astype(h.dtype) @ w_branch[m])
                for m, br in enumerate(branch_outs))
    return mixed @ w_out


def swiglu(h, w_gate, w_up, w_down):
    return (jax.nn.silu(h @ w_gate) * (h @ w_up)) @ w_down


def _fwd_setup_inputs(seed: int = 0) -> dict:
    key = jax.random.key(seed)
    ks = jax.random.split(key, 26)
    f = jnp.float32

    def nrm(k, shape, scale):
        return scale * jax.random.normal(k, shape, f)

    def gain(k, shape):
        return 1.0 + 0.02 * jax.random.normal(k, shape, f)

    n_idx = jnp.arange(S5_STATE, dtype=f)
    gshape = (DEPTH, S5_GROUPS, S5_STATE)
    return {
        'x': jax.random.normal(ks[0], (BATCH, SEQ, D_MODEL), f),
        'positions': jnp.arange(SEQ, dtype=jnp.int32)[None, :] + jax.random.randint(ks[1], (BATCH, 1), 0, 1024, dtype=jnp.int32),
        'norm1_g': gain(ks[2], (DEPTH, D_MODEL)),
        'w_in': nrm(ks[3], (DEPTH, D_MODEL, D_IN), D_MODEL ** -0.5),
        'gla_a2': nrm(ks[4], (DEPTH, GLA_LOWRANK, GLA_HEADS * GLA_DK), GLA_LOWRANK ** -0.5),
        'gla_a_b': nrm(ks[5], (DEPTH, GLA_HEADS * GLA_DK), 0.1),
        'gla_norm_g': gain(ks[6], (DEPTH, GLA_HEADS, GLA_DV)),
        's5_lambda_re': -0.5 + nrm(ks[7], gshape, 0.01),
        's5_lambda_im': math.pi * n_idx + nrm(ks[8], gshape, 0.01),
        's5_log_dt': jax.random.uniform(ks[9], (DEPTH, S5_GROUPS), f, math.log(S5_DT_MIN), math.log(S5_DT_MAX)),
        's5_b_re': nrm(ks[10], (DEPTH, S5_GROUPS, S5_STATE, S5_GROUP), (2 * S5_GROUP) ** -0.5),
        's5_b_im': nrm(ks[11], (DEPTH, S5_GROUPS, S5_STATE, S5_GROUP), (2 * S5_GROUP) ** -0.5),
        's5_c_re': nrm(ks[12], (DEPTH, S5_GROUPS, S5_GROUP, S5_STATE), S5_STATE ** -0.5),
        's5_c_im': nrm(ks[13], (DEPTH, S5_GROUPS, S5_GROUP, S5_STATE), S5_STATE ** -0.5),
        's5_d': nrm(ks[14], (DEPTH, S5_GROUPS, S5_GROUP), 1.0),
        's5_glu_w': nrm(ks[15], (DEPTH, S5_WIDTH, S5_WIDTH), S5_WIDTH ** -0.5),
        's5_glu_b': nrm(ks[16], (DEPTH, S5_WIDTH), 0.01),
        'swa_sinks': nrm(ks[17], (DEPTH, SWA_HEADS), 1.0),
        'w_branch': nrm(ks[18], (DEPTH, N_BRANCH, BRANCH_WIDTH, D_MODEL), BRANCH_WIDTH ** -0.5),
        'w_out': nrm(ks[19], (DEPTH, D_MODEL, D_MODEL), D_MODEL ** -0.5),
        'norm2_g': gain(ks[20], (DEPTH, D_MODEL)),
        'w_ffn_gate': nrm(ks[21], (DEPTH, D_MODEL, FFN_HIDDEN), D_MODEL ** -0.5),
        'w_ffn_up': nrm(ks[22], (DEPTH, D_MODEL, FFN_HIDDEN), D_MODEL ** -0.5),
        'w_ffn_down': nrm(ks[23], (DEPTH, FFN_HIDDEN, D_MODEL), FFN_HIDDEN ** -0.5),
        'final_norm_g': gain(ks[24], (D_MODEL,)),
    }


def _fwd_reference(x, positions, norm1_g, w_in, gla_a2, gla_a_b, gla_norm_g, s5_lambda_re, s5_lambda_im, s5_log_dt,
              s5_b_re, s5_b_im, s5_c_re, s5_c_im, s5_d, s5_glu_w, s5_glu_b, swa_sinks, w_branch, w_out,
              norm2_g, w_ffn_gate, w_ffn_up, w_ffn_down, final_norm_g):
    for i in range(DEPTH):
        h = rms_norm(x, norm1_g[i])
        x = x + hybrid_mixer(h, positions, w_in[i], gla_a2[i], gla_a_b[i], gla_norm_g[i], s5_lambda_re[i],
                             s5_lambda_im[i], s5_log_dt[i], s5_b_re[i], s5_b_im[i], s5_c_re[i], s5_c_im[i],
                             s5_d[i], s5_glu_w[i], s5_glu_b[i], swa_sinks[i], w_branch[i], w_out[i])
        h = rms_norm(x, norm2_g[i])
        x = x + swiglu(h, w_ffn_gate[i], w_ffn_up[i], w_ffn_down[i])
    return rms_norm(x, final_norm_g)


import jax as _jax
import jax.numpy as _jnp

TWIN_FORMAT = 'train_step'
FWD_PARAMS = ['x', 'positions', 'norm1_g', 'w_in', 'gla_a2', 'gla_a_b', 'gla_norm_g', 's5_lambda_re', 's5_lambda_im', 's5_log_dt', 's5_b_re', 's5_b_im', 's5_c_re', 's5_c_im', 's5_d', 's5_glu_w', 's5_glu_b', 'swa_sinks', 'w_branch', 'w_out', 'norm2_g', 'w_ffn_gate', 'w_ffn_up', 'w_ffn_down', 'final_norm_g']
TWIN_WEIGHTS = ['norm1_g', 'w_in', 'gla_a2', 'gla_a_b', 'gla_norm_g', 's5_lambda_re', 's5_lambda_im', 's5_log_dt', 's5_b_re', 's5_b_im', 's5_c_re', 's5_c_im', 's5_d', 's5_glu_w', 's5_glu_b', 'swa_sinks', 'w_branch', 'w_out', 'norm2_g', 'w_ffn_gate', 'w_ffn_up', 'w_ffn_down', 'final_norm_g']
TWIN_DIFF_INPUT = 'x'
TWIN_INPUTS = ['x', 'positions', 'norm1_g', 'w_in', 'gla_a2', 'gla_a_b', 'gla_norm_g', 's5_lambda_re', 's5_lambda_im', 's5_log_dt', 's5_b_re', 's5_b_im', 's5_c_re', 's5_c_im', 's5_d', 's5_glu_w', 's5_glu_b', 'swa_sinks', 'w_branch', 'w_out', 'norm2_g', 'w_ffn_gate', 'w_ffn_up', 'w_ffn_down', 'final_norm_g', 'loss_target', 'm_norm1_g', 'm_w_in', 'm_gla_a2', 'm_gla_a_b', 'm_gla_norm_g', 'm_s5_lambda_re', 'm_s5_lambda_im', 'm_s5_log_dt', 'm_s5_b_re', 'm_s5_b_im', 'm_s5_c_re', 'm_s5_c_im', 'm_s5_d', 'm_s5_glu_w', 'm_s5_glu_b', 'm_swa_sinks', 'm_w_branch', 'm_w_out', 'm_norm2_g', 'm_w_ffn_gate', 'm_w_ffn_up', 'm_w_ffn_down', 'm_final_norm_g', 'v_norm1_g', 'v_w_in', 'v_gla_a2', 'v_gla_a_b', 'v_gla_norm_g', 'v_s5_lambda_re', 'v_s5_lambda_im', 'v_s5_log_dt', 'v_s5_b_re', 'v_s5_b_im', 'v_s5_c_re', 'v_s5_c_im', 'v_s5_d', 'v_s5_glu_w', 'v_s5_glu_b', 'v_swa_sinks', 'v_w_branch', 'v_w_out', 'v_norm2_g', 'v_w_ffn_gate', 'v_w_ffn_up', 'v_w_ffn_down', 'v_final_norm_g']
TWIN_OUTPUTS = ['loss', 'grad_x', 'grad_norm1_g', 'grad_w_in', 'grad_gla_a2', 'grad_gla_a_b', 'grad_gla_norm_g', 'grad_s5_lambda_re', 'grad_s5_lambda_im', 'grad_s5_log_dt', 'grad_s5_b_re', 'grad_s5_b_im', 'grad_s5_c_re', 'grad_s5_c_im', 'grad_s5_d', 'grad_s5_glu_w', 'grad_s5_glu_b', 'grad_swa_sinks', 'grad_w_branch', 'grad_w_out', 'grad_norm2_g', 'grad_w_ffn_gate', 'grad_w_ffn_up', 'grad_w_ffn_down', 'grad_final_norm_g', 'delta_norm1_g', 'delta_w_in', 'delta_gla_a2', 'delta_gla_a_b', 'delta_gla_norm_g', 'delta_s5_lambda_re', 'delta_s5_lambda_im', 'delta_s5_log_dt', 'delta_s5_b_re', 'delta_s5_b_im', 'delta_s5_c_re', 'delta_s5_c_im', 'delta_s5_d', 'delta_s5_glu_w', 'delta_s5_glu_b', 'delta_swa_sinks', 'delta_w_branch', 'delta_w_out', 'delta_norm2_g', 'delta_w_ffn_gate', 'delta_w_ffn_up', 'delta_w_ffn_down', 'delta_final_norm_g', 'new_m_norm1_g', 'new_m_w_in', 'new_m_gla_a2', 'new_m_gla_a_b', 'new_m_gla_norm_g', 'new_m_s5_lambda_re', 'new_m_s5_lambda_im', 'new_m_s5_log_dt', 'new_m_s5_b_re', 'new_m_s5_b_im', 'new_m_s5_c_re', 'new_m_s5_c_im', 'new_m_s5_d', 'new_m_s5_glu_w', 'new_m_s5_glu_b', 'new_m_swa_sinks', 'new_m_w_branch', 'new_m_w_out', 'new_m_norm2_g', 'new_m_w_ffn_gate', 'new_m_w_ffn_up', 'new_m_w_ffn_down', 'new_m_final_norm_g', 'new_v_norm1_g', 'new_v_w_in', 'new_v_gla_a2', 'new_v_gla_a_b', 'new_v_gla_norm_g', 'new_v_s5_lambda_re', 'new_v_s5_lambda_im', 'new_v_s5_log_dt', 'new_v_s5_b_re', 'new_v_s5_b_im', 'new_v_s5_c_re', 'new_v_s5_c_im', 'new_v_s5_d', 'new_v_s5_glu_w', 'new_v_s5_glu_b', 'new_v_swa_sinks', 'new_v_w_branch', 'new_v_w_out', 'new_v_norm2_g', 'new_v_w_ffn_gate', 'new_v_w_ffn_up', 'new_v_w_ffn_down', 'new_v_final_norm_g']
TWIN_LEAF_KINDS = {'loss': 'loss', 'grad_x': 'grad_x', 'grad_norm1_g': 'grad_w', 'grad_w_in': 'grad_w', 'grad_gla_a2': 'grad_w', 'grad_gla_a_b': 'grad_w', 'grad_gla_norm_g': 'grad_w', 'grad_s5_lambda_re': 'grad_w', 'grad_s5_lambda_im': 'grad_w', 'grad_s5_log_dt': 'grad_w', 'grad_s5_b_re': 'grad_w', 'grad_s5_b_im': 'grad_w', 'grad_s5_c_re': 'grad_w', 'grad_s5_c_im': 'grad_w', 'grad_s5_d': 'grad_w', 'grad_s5_glu_w': 'grad_w', 'grad_s5_glu_b': 'grad_w', 'grad_swa_sinks': 'grad_w', 'grad_w_branch': 'grad_w', 'grad_w_out': 'grad_w', 'grad_norm2_g': 'grad_w', 'grad_w_ffn_gate': 'grad_w', 'grad_w_ffn_up': 'grad_w', 'grad_w_ffn_down': 'grad_w', 'grad_final_norm_g': 'grad_w', 'delta_norm1_g': 'delta_w', 'delta_w_in': 'delta_w', 'delta_gla_a2': 'delta_w', 'delta_gla_a_b': 'delta_w', 'delta_gla_norm_g': 'delta_w', 'delta_s5_lambda_re': 'delta_w', 'delta_s5_lambda_im': 'delta_w', 'delta_s5_log_dt': 'delta_w', 'delta_s5_b_re': 'delta_w', 'delta_s5_b_im': 'delta_w', 'delta_s5_c_re': 'delta_w', 'delta_s5_c_im': 'delta_w', 'delta_s5_d': 'delta_w', 'delta_s5_glu_w': 'delta_w', 'delta_s5_glu_b': 'delta_w', 'delta_swa_sinks': 'delta_w', 'delta_w_branch': 'delta_w', 'delta_w_out': 'delta_w', 'delta_norm2_g': 'delta_w', 'delta_w_ffn_gate': 'delta_w', 'delta_w_ffn_up': 'delta_w', 'delta_w_ffn_down': 'delta_w', 'delta_final_norm_g': 'delta_w', 'new_m_norm1_g': 'new_m', 'new_m_w_in': 'new_m', 'new_m_gla_a2': 'new_m', 'new_m_gla_a_b': 'new_m', 'new_m_gla_norm_g': 'new_m', 'new_m_s5_lambda_re': 'new_m', 'new_m_s5_lambda_im': 'new_m', 'new_m_s5_log_dt': 'new_m', 'new_m_s5_b_re': 'new_m', 'new_m_s5_b_im': 'new_m', 'new_m_s5_c_re': 'new_m', 'new_m_s5_c_im': 'new_m', 'new_m_s5_d': 'new_m', 'new_m_s5_glu_w': 'new_m', 'new_m_s5_glu_b': 'new_m', 'new_m_swa_sinks': 'new_m', 'new_m_w_branch': 'new_m', 'new_m_w_out': 'new_m', 'new_m_norm2_g': 'new_m', 'new_m_w_ffn_gate': 'new_m', 'new_m_w_ffn_up': 'new_m', 'new_m_w_ffn_down': 'new_m', 'new_m_final_norm_g': 'new_m', 'new_v_norm1_g': 'new_v', 'new_v_w_in': 'new_v', 'new_v_gla_a2': 'new_v', 'new_v_gla_a_b': 'new_v', 'new_v_gla_norm_g': 'new_v', 'new_v_s5_lambda_re': 'new_v', 'new_v_s5_lambda_im': 'new_v', 'new_v_s5_log_dt': 'new_v', 'new_v_s5_b_re': 'new_v', 'new_v_s5_b_im': 'new_v', 'new_v_s5_c_re': 'new_v', 'new_v_s5_c_im': 'new_v', 'new_v_s5_d': 'new_v', 'new_v_s5_glu_w': 'new_v', 'new_v_s5_glu_b': 'new_v', 'new_v_swa_sinks': 'new_v', 'new_v_w_branch': 'new_v', 'new_v_w_out': 'new_v', 'new_v_norm2_g': 'new_v', 'new_v_w_ffn_gate': 'new_v', 'new_v_w_ffn_up': 'new_v', 'new_v_w_ffn_down': 'new_v', 'new_v_final_norm_g': 'new_v'}


def _forward(args):
    return _fwd_reference(*[args[k] for k in FWD_PARAMS])


def _output_shape():
    def fwd():
        inp = _fwd_setup_inputs(0)
        return _fwd_reference(*[inp[k] for k in FWD_PARAMS])
    out = _jax.eval_shape(fwd)
    return out.shape, out.dtype

N_MICROBATCH = 1
ADAM_LR = 0.001
ADAM_B1 = 0.9
ADAM_B2 = 0.999
ADAM_EPS = 1e-08
ADAM_WD = 0.01
ADAM_STEP = 10
PER_EXAMPLE_BATCH_AXIS = {'x': 0, 'positions': 0, 'loss_target': 0}
SHARED_INPUTS = []
_WEIGHT_DTYPES = {'norm1_g': _jnp.float32, 'w_in': _jnp.float32, 'gla_a2': _jnp.float32, 'gla_a_b': _jnp.float32, 'gla_norm_g': _jnp.float32, 's5_lambda_re': _jnp.float32, 's5_lambda_im': _jnp.float32, 's5_log_dt': _jnp.float32, 's5_b_re': _jnp.float32, 's5_b_im': _jnp.float32, 's5_c_re': _jnp.float32, 's5_c_im': _jnp.float32, 's5_d': _jnp.float32, 's5_glu_w': _jnp.float32, 's5_glu_b': _jnp.float32, 'swa_sinks': _jnp.float32, 'w_branch': _jnp.float32, 'w_out': _jnp.float32, 'norm2_g': _jnp.float32, 'w_ffn_gate': _jnp.float32, 'w_ffn_up': _jnp.float32, 'w_ffn_down': _jnp.float32, 'final_norm_g': _jnp.float32}
MOMENT_SCALE = {'norm1_g': 7.250080e-02, 'w_in': 2.884413e-02, 'gla_a2': 1.017247e-02, 'gla_a_b': 4.062491e-02, 'gla_norm_g': 6.728127e-02, 's5_lambda_re': 2.541280e-03, 's5_lambda_im': 2.551954e-03, 's5_log_dt': 1.857211e+00, 's5_b_re': 1.742796e-03, 's5_b_im': 1.732123e-03, 's5_c_re': 2.481707e-03, 's5_c_im': 2.482498e-03, 's5_d': 4.330680e-02, 's5_glu_w': 1.065551e-02, 's5_glu_b': 1.703650e-02, 'swa_sinks': 2.199950e-02, 'w_branch': 1.948754e-02, 'w_out': 3.900747e-02, 'norm2_g': 6.474692e-02, 'w_ffn_gate': 2.824844e-02, 'w_ffn_up': 2.735772e-02, 'w_ffn_down': 4.539681e-02, 'final_norm_g': 1.599893e+01}


def _to_microbatches(a, axis):
    t = _jnp.moveaxis(a, axis, 0)
    t = t.reshape((N_MICROBATCH, t.shape[0] // N_MICROBATCH) + t.shape[1:])
    return _jnp.moveaxis(t, 1, axis + 1)


def setup_inputs(seed: int = 0) -> dict:
    inp = _fwd_setup_inputs(seed)
    key = _jax.random.fold_in(_jax.random.key(seed), 7919)
    shape, _ = _output_shape()
    out = dict(inp)
    out["loss_target"] = _jax.random.normal(_jax.random.fold_in(key, 0), shape, _jnp.float32)
    for i, name in enumerate(TWIN_WEIGHTS):
        w = inp[name].astype(_jnp.float32)
        if MOMENT_SCALE is None:
            s = _jnp.sqrt(_jnp.mean(_jnp.square(w)) + 1e-30)
        else:
            s = MOMENT_SCALE[name]
        km, kv = _jax.random.split(_jax.random.fold_in(key, i + 1))
        out[name] = w
        out["m_" + name] = s * _jax.random.normal(km, w.shape, _jnp.float32)
        out["v_" + name] = (s * s) * _jax.random.uniform(kv, w.shape, _jnp.float32, 0.5, 1.5)
    if N_MICROBATCH > 1:
        for name, axis in PER_EXAMPLE_BATCH_AXIS.items():
            out[name] = _to_microbatches(out[name], axis)
    return {'x': out['x'], 'positions': out['positions'], 'norm1_g': out['norm1_g'], 'w_in': out['w_in'], 'gla_a2': out['gla_a2'], 'gla_a_b': out['gla_a_b'], 'gla_norm_g': out['gla_norm_g'], 's5_lambda_re': out['s5_lambda_re'], 's5_lambda_im': out['s5_lambda_im'], 's5_log_dt': out['s5_log_dt'], 's5_b_re': out['s5_b_re'], 's5_b_im': out['s5_b_im'], 's5_c_re': out['s5_c_re'], 's5_c_im': out['s5_c_im'], 's5_d': out['s5_d'], 's5_glu_w': out['s5_glu_w'], 's5_glu_b': out['s5_glu_b'], 'swa_sinks': out['swa_sinks'], 'w_branch': out['w_branch'], 'w_out': out['w_out'], 'norm2_g': out['norm2_g'], 'w_ffn_gate': out['w_ffn_gate'], 'w_ffn_up': out['w_ffn_up'], 'w_ffn_down': out['w_ffn_down'], 'final_norm_g': out['final_norm_g'], 'loss_target': out['loss_target'], 'm_norm1_g': out['m_norm1_g'], 'm_w_in': out['m_w_in'], 'm_gla_a2': out['m_gla_a2'], 'm_gla_a_b': out['m_gla_a_b'], 'm_gla_norm_g': out['m_gla_norm_g'], 'm_s5_lambda_re': out['m_s5_lambda_re'], 'm_s5_lambda_im': out['m_s5_lambda_im'], 'm_s5_log_dt': out['m_s5_log_dt'], 'm_s5_b_re': out['m_s5_b_re'], 'm_s5_b_im': out['m_s5_b_im'], 'm_s5_c_re': out['m_s5_c_re'], 'm_s5_c_im': out['m_s5_c_im'], 'm_s5_d': out['m_s5_d'], 'm_s5_glu_w': out['m_s5_glu_w'], 'm_s5_glu_b': out['m_s5_glu_b'], 'm_swa_sinks': out['m_swa_sinks'], 'm_w_branch': out['m_w_branch'], 'm_w_out': out['m_w_out'], 'm_norm2_g': out['m_norm2_g'], 'm_w_ffn_gate': out['m_w_ffn_gate'], 'm_w_ffn_up': out['m_w_ffn_up'], 'm_w_ffn_down': out['m_w_ffn_down'], 'm_final_norm_g': out['m_final_norm_g'], 'v_norm1_g': out['v_norm1_g'], 'v_w_in': out['v_w_in'], 'v_gla_a2': out['v_gla_a2'], 'v_gla_a_b': out['v_gla_a_b'], 'v_gla_norm_g': out['v_gla_norm_g'], 'v_s5_lambda_re': out['v_s5_lambda_re'], 'v_s5_lambda_im': out['v_s5_lambda_im'], 'v_s5_log_dt': out['v_s5_log_dt'], 'v_s5_b_re': out['v_s5_b_re'], 'v_s5_b_im': out['v_s5_b_im'], 'v_s5_c_re': out['v_s5_c_re'], 'v_s5_c_im': out['v_s5_c_im'], 'v_s5_d': out['v_s5_d'], 'v_s5_glu_w': out['v_s5_glu_w'], 'v_s5_glu_b': out['v_s5_glu_b'], 'v_swa_sinks': out['v_swa_sinks'], 'v_w_branch': out['v_w_branch'], 'v_w_out': out['v_w_out'], 'v_norm2_g': out['v_norm2_g'], 'v_w_ffn_gate': out['v_w_ffn_gate'], 'v_w_ffn_up': out['v_w_ffn_up'], 'v_w_ffn_down': out['v_w_ffn_down'], 'v_final_norm_g': out['v_final_norm_g']}


def _loss(weights, diff, rest, loss_target):
    with _jax.named_scope("forward"):
        args = {**rest, TWIN_DIFF_INPUT: diff, **{k: w.astype(_WEIGHT_DTYPES[k]) for k, w in weights.items()}}
        y = _forward(args)
    with _jax.named_scope("loss_head"):
        err = _jnp.square(y.astype(_jnp.float32) - loss_target)
        return 0.5 * _jnp.sum(_jnp.mean(err, axis=-1)) if err.ndim else 0.5 * err


def _adamw(w, g, m, v):
    m = ADAM_B1 * m + (1.0 - ADAM_B1) * g
    v = ADAM_B2 * v + (1.0 - ADAM_B2) * _jnp.square(g)
    m_hat = m / (1.0 - ADAM_B1 ** ADAM_STEP)
    v_hat = v / (1.0 - ADAM_B2 ** ADAM_STEP)
    delta = -ADAM_LR * (m_hat / (_jnp.sqrt(v_hat) + ADAM_EPS) + ADAM_WD * w)
    return delta, m, v


def reference(x, positions, norm1_g, w_in, gla_a2, gla_a_b, gla_norm_g, s5_lambda_re, s5_lambda_im, s5_log_dt, s5_b_re, s5_b_im, s5_c_re, s5_c_im, s5_d, s5_glu_w, s5_glu_b, swa_sinks, w_branch, w_out, norm2_g, w_ffn_gate, w_ffn_up, w_ffn_down, final_norm_g, loss_target, m_norm1_g, m_w_in, m_gla_a2, m_gla_a_b, m_gla_norm_g, m_s5_lambda_re, m_s5_lambda_im, m_s5_log_dt, m_s5_b_re, m_s5_b_im, m_s5_c_re, m_s5_c_im, m_s5_d, m_s5_glu_w, m_s5_glu_b, m_swa_sinks, m_w_branch, m_w_out, m_norm2_g, m_w_ffn_gate, m_w_ffn_up, m_w_ffn_down, m_final_norm_g, v_norm1_g, v_w_in, v_gla_a2, v_gla_a_b, v_gla_norm_g, v_s5_lambda_re, v_s5_lambda_im, v_s5_log_dt, v_s5_b_re, v_s5_b_im, v_s5_c_re, v_s5_c_im, v_s5_d, v_s5_glu_w, v_s5_glu_b, v_swa_sinks, v_w_branch, v_w_out, v_norm2_g, v_w_ffn_gate, v_w_ffn_up, v_w_ffn_down, v_final_norm_g):
    given = dict(x=x, positions=positions, norm1_g=norm1_g, w_in=w_in, gla_a2=gla_a2, gla_a_b=gla_a_b, gla_norm_g=gla_norm_g, s5_lambda_re=s5_lambda_re, s5_lambda_im=s5_lambda_im, s5_log_dt=s5_log_dt, s5_b_re=s5_b_re, s5_b_im=s5_b_im, s5_c_re=s5_c_re, s5_c_im=s5_c_im, s5_d=s5_d, s5_glu_w=s5_glu_w, s5_glu_b=s5_glu_b, swa_sinks=swa_sinks, w_branch=w_branch, w_out=w_out, norm2_g=norm2_g, w_ffn_gate=w_ffn_gate, w_ffn_up=w_ffn_up, w_ffn_down=w_ffn_down, final_norm_g=final_norm_g, loss_target=loss_target, m_norm1_g=m_norm1_g, m_w_in=m_w_in, m_gla_a2=m_gla_a2, m_gla_a_b=m_gla_a_b, m_gla_norm_g=m_gla_norm_g, m_s5_lambda_re=m_s5_lambda_re, m_s5_lambda_im=m_s5_lambda_im, m_s5_log_dt=m_s5_log_dt, m_s5_b_re=m_s5_b_re, m_s5_b_im=m_s5_b_im, m_s5_c_re=m_s5_c_re, m_s5_c_im=m_s5_c_im, m_s5_d=m_s5_d, m_s5_glu_w=m_s5_glu_w, m_s5_glu_b=m_s5_glu_b, m_swa_sinks=m_swa_sinks, m_w_branch=m_w_branch, m_w_out=m_w_out, m_norm2_g=m_norm2_g, m_w_ffn_gate=m_w_ffn_gate, m_w_ffn_up=m_w_ffn_up, m_w_ffn_down=m_w_ffn_down, m_final_norm_g=m_final_norm_g, v_norm1_g=v_norm1_g, v_w_in=v_w_in, v_gla_a2=v_gla_a2, v_gla_a_b=v_gla_a_b, v_gla_norm_g=v_gla_norm_g, v_s5_lambda_re=v_s5_lambda_re, v_s5_lambda_im=v_s5_lambda_im, v_s5_log_dt=v_s5_log_dt, v_s5_b_re=v_s5_b_re, v_s5_b_im=v_s5_b_im, v_s5_c_re=v_s5_c_re, v_s5_c_im=v_s5_c_im, v_s5_d=v_s5_d, v_s5_glu_w=v_s5_glu_w, v_s5_glu_b=v_s5_glu_b, v_swa_sinks=v_swa_sinks, v_w_branch=v_w_branch, v_w_out=v_w_out, v_norm2_g=v_norm2_g, v_w_ffn_gate=v_w_ffn_gate, v_w_ffn_up=v_w_ffn_up, v_w_ffn_down=v_w_ffn_down, v_final_norm_g=v_final_norm_g)
    weights = {n: given[n] for n in TWIN_WEIGHTS}
    shared = {n: given[n] for n in SHARED_INPUTS}
    per_example = {n: given[n] for n in ['x', 'positions']}
    grad_fn = _jax.value_and_grad(_loss, argnums=(0, 1))

    def one_microbatch(ex, loss_target):
        ex = dict(ex)
        diff = ex.pop(TWIN_DIFF_INPUT)
        return grad_fn(weights, diff, {**shared, **ex}, loss_target)

    if N_MICROBATCH == 1:
        loss, (grad_w, grad_x) = one_microbatch(per_example, given["loss_target"])
    else:
        def body(carry, xs):
            loss_sum, grad_sum = carry
            l_k, (gw_k, gx_k) = one_microbatch(xs[0], xs[1])
            with _jax.named_scope("update"):
                return (loss_sum + l_k, _jax.tree.map(_jnp.add, grad_sum, gw_k)), gx_k

        init = (_jnp.zeros((), _jnp.float32), _jax.tree.map(_jnp.zeros_like, weights))
        (loss, grad_w), grad_x = _jax.lax.scan(body, init, (per_example, given["loss_target"]))
    with _jax.named_scope("update"):
        delta_w, new_m, new_v = {}, {}, {}
        for n in TWIN_WEIGHTS:
            delta_w[n], new_m[n], new_v[n] = _adamw(weights[n], grad_w[n], given["m_" + n], given["v_" + n])
    return (loss, grad_x, *[grad_w[n] for n in TWIN_WEIGHTS], *[delta_w[n] for n in TWIN_WEIGHTS],
            *[new_m[n] for n in TWIN_WEIGHTS], *[new_v[n] for n in TWIN_WEIGHTS])
```

```python
import functools
import math

import jax
import jax.numpy as jnp
from jax import lax
from jax.experimental import pallas as pl
from jax.experimental.pallas import tpu as pltpu

f32 = jnp.float32
bf16 = jnp.bfloat16
HI = lax.Precision.HIGHEST

N_DEV = 8
AXES = ("x", "y", "c")
NORM_EPS = 1e-6
ROPE_THETA = 500000.0
HEAD_DIM = 64
ROPE_DIM = 16
ATT_BLOCK = 128
BRANCH_WIDTH = 512
GLA_HEADS, GLA_DK, GLA_DV, GLA_LOWRANK, GLA_TAU, GLA_CHUNK, GLA_SUB = 4, 64, 128, 16, 16.0, 64, 16
S5_GROUPS, S5_GROUP, S5_STATE = 32, 16, 64
S5_CHUNK = 128
S5_LANE_BLOCKS = 4
DIL_CONFIGS = ((128, 1), (512, 4), (2048, 16))
SWA_HEADS, SWA_KV_HEADS, SWA_WINDOW = 8, 2, 128
ADAM_LR, ADAM_B1, ADAM_B2, ADAM_EPS, ADAM_WD, ADAM_STEP = 0.001, 0.9, 0.999, 1e-08, 0.01, 10
O_GLR, O_S5U, O_GATES = 1536, 1552, 4368
MIX_COLS = 4480
P_GQ, P_GK, P_GV, P_GR, P_S5U, P_CQ, P_CK, P_CV, P_SQ, P_SK, P_SV, P_GLR = (
    0, 256, 512, 1024, 1536, 2048, 2560, 3072, 3584, 4096, 4224, 4352)
VMEM_LIMIT = 56 * 1024 * 1024


def _tile(n, cap, q=128):
    if n <= cap:
        return n
    t = (cap // q) * q
    while t >= q:
        if n % t == 0:
            return t
        t -= q
    return n


def _params(sem=None):
    return pltpu.CompilerParams(dimension_semantics=sem, vmem_limit_bytes=VMEM_LIMIT)


@functools.partial(jax.custom_vjp, nondiff_argnums=(1,))
def _sroll(x, d):
    return pltpu.roll(x, d, 0)


def _sroll_fwd(x, d):
    return pltpu.roll(x, d, 0), None


def _sroll_bwd(d, _, g):
    n = g.shape[0]
    return (pltpu.roll(g, (n - d) % n, 0),)


_sroll.defvjp(_sroll_fwd, _sroll_bwd)


def _mesh_pos():
    return lax.axis_index("x"), lax.axis_index("y"), lax.axis_index("c")


def _all_gather(shards, name):
    n = len(shards)
    any_spec = pl.BlockSpec(memory_space=pl.ANY)

    def body(*refs):
        ins, outs = refs[:n], refs[n:2 * n]
        send_sems, recv_sems, local_sems = refs[2 * n:]
        x, y, c = _mesh_pos()
        me, sibling = (x, y, c), (x, y, 1 - c)
        chips = [(1 - x, y), (x, 1 - y), (1 - x, 1 - y)]

        def copy(a, k, block, to, src=None):
            slot = outs[a].at[4 * block[0] + 2 * block[1] + block[2]]
            return pltpu.make_async_remote_copy(
                src_ref=slot if src is None else src, dst_ref=slot,
                send_sem=send_sems.at[a, k], recv_sem=recv_sems.at[a, k],
                device_id=to, device_id_type=pl.DeviceIdType.MESH)

        started = []
        for a in range(n):
            mine = pltpu.make_async_copy(ins[a], outs[a].at[4 * x + 2 * y + c], local_sems.at[a])
            mine.start()
            first = [copy(a, 0, me, sibling, src=ins[a])]
            first += [copy(a, 1 + j, me, (*chip, c), src=ins[a]) for j, chip in enumerate(chips)]
            for cp in first:
                cp.start()
            started.append((mine, first))
        for a in range(n):
            mine, first = started[a]
            passed = [copy(a, 4 + j, (*chip, c), sibling) for j, chip in enumerate(chips)]
            for j, chip in enumerate(chips):
                copy(a, 1 + j, (*chip, c), me).wait_recv()
                passed[j].start()
            copy(a, 0, sibling, me).wait_recv()
            for j, chip in enumerate(chips):
                copy(a, 4 + j, (*chip, 1 - c), me).wait_recv()
            for cp in first + passed:
                cp.wait_send()
            mine.wait()

    outs = pl.pallas_call(
        body, name=name,
        out_shape=[jax.ShapeDtypeStruct((N_DEV,) + s.shape, s.dtype) for s in shards],
        in_specs=[any_spec] * n, out_specs=[any_spec] * n,
        scratch_shapes=[pltpu.SemaphoreType.DMA((n, 7)), pltpu.SemaphoreType.DMA((n, 7)),
                        pltpu.SemaphoreType.DMA((n,))],
    )(*shards)
    return list(outs)


def _all_to_all(sends, name):
    n = len(sends)
    any_spec = pl.BlockSpec(memory_space=pl.ANY)

    def body(*refs):
        ins, outs = refs[:n], refs[n:2 * n]
        send_sems, recv_sems, local_sems = refs[2 * n:]
        x, y, c = _mesh_pos()
        me = 4 * x + 2 * y + c
        copies = []
        for a in range(n):
            mine = pltpu.make_async_copy(ins[a].at[me], outs[a].at[me], local_sems.at[a])
            mine.start()
            copies.append(mine)
            for k in range(1, N_DEV):
                px = 1 - x if k & 4 else x
                py = 1 - y if k & 2 else y
                pc = 1 - c if k & 1 else c
                cp = pltpu.make_async_remote_copy(
                    src_ref=ins[a].at[4 * px + 2 * py + pc], dst_ref=outs[a].at[me],
                    send_sem=send_sems.at[a, k - 1], recv_sem=recv_sems.at[a, k - 1],
                    device_id=(px, py, pc), device_id_type=pl.DeviceIdType.MESH)
                cp.start()
                copies.append(cp)
        for cp in copies:
            cp.wait()

    outs = pl.pallas_call(
        body, name=name,
        out_shape=[jax.ShapeDtypeStruct(s.shape, s.dtype) for s in sends],
        in_specs=[any_spec] * n, out_specs=[any_spec] * n,
        scratch_shapes=[pltpu.SemaphoreType.DMA((n, 7)), pltpu.SemaphoreType.DMA((n, 7)),
                        pltpu.SemaphoreType.DMA((n,))],
    )(*sends)
    return list(outs)


def _matmul(a, b, *, mode="nn", out_dtype=f32, res=None, name):
    if mode == "tn":
        K, M = a.shape
    else:
        M, K = a.shape
    N = b.shape[0] if mode == "nt" else b.shape[1]
    tm, tn, tk = _tile(M, 1024), _tile(N, 1152), _tile(K, 512)
    nk = K // tk

    def body(*refs):
        if res is None:
            a_ref, b_ref, o_ref, acc = refs
        else:
            a_ref, b_ref, r_ref, o_ref, acc = refs
        k = pl.program_id(2)

        @pl.when(k == 0)
        def _():
            acc[...] = jnp.zeros_like(acc)

        av, bv = a_ref[...].astype(bf16), b_ref[...].astype(bf16)
        dims = {"nn": (((1,), (0,)), ((), ())), "nt": (((1,), (1,)), ((), ())), "tn": (((0,), (0,)), ((), ()))}[mode]
        acc[...] += lax.dot_general(av, bv, dims, preferred_element_type=f32)

        @pl.when(k == nk - 1)
        def _():
            r = acc[...]
            if res is not None:
                r = r + r_ref[...]
            o_ref[...] = r.astype(o_ref.dtype)

    a_spec = pl.BlockSpec((tk, tm), lambda i, j, k: (k, i)) if mode == "tn" else pl.BlockSpec((tm, tk), lambda i, j, k: (i, k))
    b_spec = pl.BlockSpec((tn, tk), lambda i, j, k: (j, k)) if mode == "nt" else pl.BlockSpec((tk, tn), lambda i, j, k: (k, j))
    o_spec = pl.BlockSpec((tm, tn), lambda i, j, k: (i, j))
    in_specs = [a_spec, b_spec] + ([o_spec] if res is not None else [])
    args = (a, b) + ((res,) if res is not None else ())
    return pl.pallas_call(
        body, name=name, grid=(M // tm, N // tn, nk),
        out_shape=jax.ShapeDtypeStruct((M, N), out_dtype),
        in_specs=in_specs, out_specs=o_spec,
        scratch_shapes=[pltpu.VMEM((tm, tn), f32)],
        compiler_params=_params(("parallel", "parallel", "arbitrary")),
    )(*args)


def _rms(x, g):
    return x * lax.rsqrt(jnp.mean(x * x, axis=-1, keepdims=True) + NORM_EPS) * g


def _rms_fwd(x, g, name):
    L, D = x.shape
    tm = _tile(L, 256, 8)

    def body(x_ref, g_ref, o_ref):
        o_ref[...] = _rms(x_ref[...], g_ref[...]).astype(bf16)

    return pl.pallas_call(
        body, name=name, grid=(L // tm,), out_shape=jax.ShapeDtypeStruct((L, D), bf16),
        in_specs=[pl.BlockSpec((tm, D), lambda i: (i, 0)), pl.BlockSpec((1, D), lambda i: (0, 0))],
        out_specs=pl.BlockSpec((tm, D), lambda i: (i, 0)),
        compiler_params=_params(("parallel",)),
    )(x, g)


def _rms_bwd(x, g, dh, dres, name):
    L, D = x.shape
    tm = _tile(L, 256, 8)

    def body(x_ref, g_ref, dh_ref, dres_ref, dx_ref, dg_ref):
        _, vjp = jax.vjp(_rms, x_ref[...], g_ref[...])
        dx, dg = vjp(dh_ref[...])
        dx_ref[...] = dres_ref[...] + dx

        @pl.when(pl.program_id(0) == 0)
        def _():
            dg_ref[...] = jnp.zeros_like(dg_ref)

        dg_ref[...] += dg

    row = pl.BlockSpec((tm, D), lambda i: (i, 0))
    vec = pl.BlockSpec((1, D), lambda i: (0, 0))
    return pl.pallas_call(
        body, name=name, grid=(L // tm,),
        out_shape=[jax.ShapeDtypeStruct((L, D), f32), jax.ShapeDtypeStruct((1, D), f32)],
        in_specs=[row, vec, row, row], out_specs=[row, vec],
        compiler_params=_params(("arbitrary",)),
    )(x, g, dh, dres)


def _final_loss(x, g, target, name):
    L, D = x.shape
    tm = _tile(L, 256, 8)

    def body(x_ref, g_ref, t_ref, loss_ref, dx_ref, dg_ref):
        tgt = t_ref[...]

        def f(xv, gv):
            err = _rms(xv, gv) - tgt
            return 0.5 * jnp.sum(jnp.mean(err * err, axis=-1, keepdims=True), axis=0, keepdims=True)

        val, vjp = jax.vjp(f, x_ref[...], g_ref[...])
        dx, dg = vjp(jnp.ones((1, 1), f32))
        dx_ref[...] = dx

        @pl.when(pl.program_id(0) == 0)
        def _():
            dg_ref[...] = jnp.zeros_like(dg_ref)
            loss_ref[...] = jnp.zeros_like(loss_ref)

        dg_ref[...] += dg
        loss_ref[...] += jnp.broadcast_to(val, loss_ref.shape)

    row = pl.BlockSpec((tm, D), lambda i: (i, 0))
    vec = pl.BlockSpec((1, D), lambda i: (0, 0))
    acc = pl.BlockSpec((8, 128), lambda i: (0, 0))
    loss, dx, dg = pl.pallas_call(
        body, name=name, grid=(L // tm,),
        out_shape=[jax.ShapeDtypeStruct((8, 128), f32), jax.ShapeDtypeStruct((L, D), f32), jax.ShapeDtypeStruct((1, D), f32)],
        in_specs=[row, vec, row], out_specs=[acc, row, vec],
        compiler_params=_params(("arbitrary",)),
    )(x, g, target)
    return loss[0, 0], dx, dg


def _swiglu_f(a, b):
    return jax.nn.silu(a) * b


def _swiglu_fwd(a, b, name):
    L, F = a.shape
    tm, tn = _tile(L, 512, 8), _tile(F, 1024)

    def body(a_ref, b_ref, o_ref):
        o_ref[...] = _swiglu_f(a_ref[...], b_ref[...]).astype(bf16)

    blk = pl.BlockSpec((tm, tn), lambda i, j: (i, j))
    return pl.pallas_call(
        body, name=name, grid=(L // tm, F // tn), out_shape=jax.ShapeDtypeStruct((L, F), bf16),
        in_specs=[blk, blk], out_specs=blk, compiler_params=_params(("parallel", "parallel")),
    )(a, b)


def _swiglu_bwd(a, b, dact, name):
    L, F = a.shape
    tm, tn = _tile(L, 512, 8), _tile(F, 1024)

    def body(a_ref, b_ref, d_ref, da_ref, db_ref):
        _, vjp = jax.vjp(_swiglu_f, a_ref[...], b_ref[...])
        da, db = vjp(d_ref[...])
        da_ref[...] = da.astype(bf16)
        db_ref[...] = db.astype(bf16)

    blk = pl.BlockSpec((tm, tn), lambda i, j: (i, j))
    return pl.pallas_call(
        body, name=name, grid=(L // tm, F // tn),
        out_shape=[jax.ShapeDtypeStruct((L, F), bf16)] * 2,
        in_specs=[blk, blk, blk], out_specs=[blk, blk], compiler_params=_params(("parallel", "parallel")),
    )(a, b, dact)


def _merge_f(g0, g1, g2, g3, y0, y1, y2, y3):
    s = jax.nn.sigmoid
    return s(g0) * y0 + s(g1) * y1 + s(g2) * y2 + s(g3) * y3


def _merge_fwd(proj, ys, D, name):
    L = proj.shape[0]
    tm, tn = _tile(L, 512, 8), _tile(D, 512)
    nj = D // tn

    def body(g0, g1, g2, g3, y0, y1, y2, y3, o_ref):
        o_ref[...] = _merge_f(g0[...], g1[...], g2[...], g3[...], y0[...], y1[...], y2[...], y3[...]).astype(bf16)

    gspecs = [pl.BlockSpec((tm, tn), functools.partial(lambda i, j, m: (i, m * nj + j), m=m)) for m in range(4)]
    blk = pl.BlockSpec((tm, tn), lambda i, j: (i, j))
    return pl.pallas_call(
        body, name=name, grid=(L // tm, nj), out_shape=jax.ShapeDtypeStruct((L, D), bf16),
        in_specs=gspecs + [blk] * 4, out_specs=blk, compiler_params=_params(("parallel", "parallel")),
    )(proj, proj, proj, proj, *ys)


def _merge_bwd(proj, ys, dmixed, D, name):
    L = proj.shape[0]
    tm, tn = _tile(L, 512, 8), _tile(D, 512)
    nj = D // tn

    def body(g0, g1, g2, g3, y0, y1, y2, y3, d_ref, dy0, dy1, dy2, dy3, dg_ref):
        _, vjp = jax.vjp(_merge_f, g0[...], g1[...], g2[...], g3[...], y0[...], y1[...], y2[...], y3[...])
        grads = vjp(d_ref[...])
        for m, r in enumerate((dy0, dy1, dy2, dy3)):
            r[...] = grads[4 + m].astype(bf16)
        for m in range(4):
            dg_ref[m] = grads[m].astype(bf16)

    gspecs = [pl.BlockSpec((tm, tn), functools.partial(lambda i, j, m: (i, m * nj + j), m=m)) for m in range(4)]
    blk = pl.BlockSpec((tm, tn), lambda i, j: (i, j))
    dgspec = pl.BlockSpec((4, tm, tn), lambda i, j: (0, i, j))
    outs = pl.pallas_call(
        body, name=name, grid=(L // tm, nj),
        out_shape=[jax.ShapeDtypeStruct((L, D), bf16)] * 4 + [jax.ShapeDtypeStruct((4, L, D), bf16)],
        in_specs=gspecs + [blk] * 5, out_specs=[blk] * 4 + [dgspec],
        compiler_params=_params(("parallel", "parallel")),
    )(proj, proj, proj, proj, *ys, dmixed)
    return outs[:4], outs[4]


def _gla_head(q, k, v, r, glr, st, a2, ab, ng):
    C, T = GLA_CHUNK, GLA_SUB
    row = lax.broadcasted_iota(jnp.int32, (C, C), 0)
    col = lax.broadcasted_iota(jnp.int32, (C, C), 1)
    tri = (col <= row).astype(f32)
    sel = (col == (row // T) * T).astype(f32)
    z = jnp.dot(glr, a2, precision=HI, preferred_element_type=f32) + ab
    g = jax.nn.log_sigmoid(z) / GLA_TAU
    cum = jnp.dot(tri, g, precision=HI, preferred_element_type=f32)
    excl = cum - g
    ref = jnp.dot(sel, excl, precision=HI, preferred_element_type=f32)
    qs = q * (GLA_DK ** -0.5)
    q_ref = qs * jnp.exp(cum - ref)
    rowk = lax.broadcasted_iota(jnp.int32, (C, GLA_DK), 0)
    a = jnp.zeros((C, C), f32)
    for s in range(1, C // T):
        ref_s = jnp.sum(jnp.where(rowk == s * T, excl, 0.0), axis=0, keepdims=True)
        k_ref = k * jnp.exp(jnp.where(rowk < s * T, ref_s - cum, -jnp.inf))
        a_s = lax.dot_general(q_ref, k_ref, (((1,), (1,)), ((), ())), preferred_element_type=f32)
        a = a + jnp.where(row // T == s, a_s, 0.0)
    o = jnp.dot(a, v, preferred_element_type=f32)
    sub = rowk % T
    for d in range(T):
        ks = _sroll(k, d) if d else k
        cs = _sroll(cum, d) if d else cum
        vs = _sroll(v, d) if d else v
        w = jnp.sum(qs * ks * jnp.exp(jnp.where(sub >= d, cum - cs, -jnp.inf)), axis=-1, keepdims=True)
        o = o + w * vs
    o = o + lax.dot_general(qs * jnp.exp(cum), st, (((1,), (1,)), ((), ())), preferred_element_type=f32)
    last = jnp.sum(jnp.where(rowk == C - 1, cum, 0.0), axis=0, keepdims=True)
    st_new = st * jnp.exp(last) + lax.dot_general(v, k * jnp.exp(last - cum), (((0,), (0,)), ((), ())),
                                                  preferred_element_type=f32)
    out = _rms(o, ng) * jax.nn.silu(r)
    return out, st_new


def _gla_specs(L, base128, rev):
    n = L // GLA_CHUNK
    ch = (lambda i: n - 1 - i) if rev else (lambda i: i)
    b = base128
    return n, ch, [
        pl.BlockSpec((GLA_CHUNK, 256), lambda i: (ch(i), (b + P_GQ // 128) // 2)),
        pl.BlockSpec((GLA_CHUNK, 256), lambda i: (ch(i), (b + P_GK // 128) // 2)),
        pl.BlockSpec((GLA_CHUNK, 512), lambda i: (ch(i), (b + P_GV // 128) // 4)),
        pl.BlockSpec((GLA_CHUNK, 512), lambda i: (ch(i), (b + P_GR // 128) // 4)),
        pl.BlockSpec((GLA_CHUNK, 128), lambda i: (ch(i), b + P_GLR // 128)),
    ]


def _gla_fwd(proj, a2p, ab, ng, base128, name):
    L = proj.shape[0]
    n, _, pspecs = _gla_specs(L, base128, False)
    H, DK, DV = GLA_HEADS, GLA_DK, GLA_DV

    def body(q_ref, k_ref, v_ref, r_ref, l_ref, a2_ref, ab_ref, ng_ref, o_ref, st_ref, state):
        @pl.when(pl.program_id(0) == 0)
        def _():
            state[...] = jnp.zeros_like(state)

        st_ref[0] = state[...]
        glr = l_ref[...]
        for h in range(H):
            kk, vv = slice(h * DK, (h + 1) * DK), slice(h * DV, (h + 1) * DV)
            out, st_new = _gla_head(q_ref[:, kk], k_ref[:, kk], v_ref[:, vv], r_ref[:, vv], glr, state[h],
                                    a2_ref[:, kk], ab_ref[:, kk], ng_ref[:, vv])
            o_ref[:, vv] = out.astype(bf16)
            state[h] = st_new

    full = lambda shape: pl.BlockSpec(shape, lambda i: (0,) * len(shape))
    return pl.pallas_call(
        body, name=name, grid=(n,),
        out_shape=[jax.ShapeDtypeStruct((L, H * DV), bf16), jax.ShapeDtypeStruct((n, H, DV, DK), f32)],
        in_specs=pspecs + [full((128, 256)), full((1, 256)), full((1, 512))],
        out_specs=[pl.BlockSpec((GLA_CHUNK, 512), lambda i: (i, 0)), pl.BlockSpec((1, H, DV, DK), lambda i: (i, 0, 0, 0))],
        scratch_shapes=[pltpu.VMEM((H, DV, DK), f32)],
        compiler_params=_params(("arbitrary",)),
    )(proj, proj, proj, proj, proj, a2p, ab, ng)


def _gla_bwd(proj, a2p, ab, ng, states, dout, base128, name):
    L = proj.shape[0]
    n, ch, pspecs = _gla_specs(L, base128, True)
    H, DK, DV = GLA_HEADS, GLA_DK, GLA_DV

    def body(q_ref, k_ref, v_ref, r_ref, l_ref, a2_ref, ab_ref, ng_ref, st_ref, do_ref,
             dq_ref, dk_ref, dv_ref, dr_ref, dl_ref, da2_ref, dab_ref, dng_ref, dstate):
        @pl.when(pl.program_id(0) == 0)
        def _():
            dstate[...] = jnp.zeros_like(dstate)
            da2_ref[...] = jnp.zeros_like(da2_ref)
            dab_ref[...] = jnp.zeros_like(dab_ref)
            dng_ref[...] = jnp.zeros_like(dng_ref)

        glr = l_ref[...]
        dglr = jnp.zeros(glr.shape, f32)
        for h in range(H):
            kk, vv = slice(h * DK, (h + 1) * DK), slice(h * DV, (h + 1) * DV)
            _, vjp = jax.vjp(_gla_head, q_ref[:, kk], k_ref[:, kk], v_ref[:, vv], r_ref[:, vv], glr, st_ref[0, h],
                             a2_ref[:, kk], ab_ref[:, kk], ng_ref[:, vv])
            dq, dk, dv, dr, dl, dst, da2, dab, dng = vjp((do_ref[:, vv].astype(f32), dstate[h]))
            dq_ref[:, kk] = dq.astype(bf16)
            dk_ref[:, kk] = dk.astype(bf16)
            dv_ref[:, vv] = dv.astype(bf16)
            dr_ref[:, vv] = dr.astype(bf16)
            dglr = dglr + dl
            dstate[h] = dst
            da2_ref[:, kk] += da2
            dab_ref[:, kk] += dab
            dng_ref[:, vv] += dng
        dl_ref[...] = dglr.astype(bf16)

    full = lambda shape: pl.BlockSpec(shape, lambda i: (0,) * len(shape))
    rowspec = lambda w: pl.BlockSpec((GLA_CHUNK, w), lambda i: (ch(i), 0))
    return pl.pallas_call(
        body, name=name, grid=(n,),
        out_shape=[jax.ShapeDtypeStruct((L, 256), bf16), jax.ShapeDtypeStruct((L, 256), bf16),
                   jax.ShapeDtypeStruct((L, 512), bf16), jax.ShapeDtypeStruct((L, 512), bf16),
                   jax.ShapeDtypeStruct((L, 128), bf16), jax.ShapeDtypeStruct((128, 256), f32),
                   jax.ShapeDtypeStruct((1, 256), f32), jax.ShapeDtypeStruct((1, 512), f32)],
        in_specs=pspecs + [full((128, 256)), full((1, 256)), full((1, 512)),
                           pl.BlockSpec((1, H, DV, DK), lambda i: (ch(i), 0, 0, 0)), rowspec(512)],
        out_specs=[rowspec(256), rowspec(256), rowspec(512), rowspec(512), rowspec(128),
                   full((128, 256)), full((1, 256)), full((1, 512))],
        scratch_shapes=[pltpu.VMEM((H, DV, DK), f32)],
        compiler_params=_params(("arbitrary",)),
    )(proj, proj, proj, proj, proj, a2p, ab, ng, states, dout)


def _s5_prep(lam_re, lam_im, log_dt, b_re, b_im, c_re, c_im, d):
    G, N, Cn = S5_GROUPS, S5_STATE, S5_GROUP
    J, GB = S5_LANE_BLOCKS, S5_GROUPS // S5_LANE_BLOCKS
    dt = jnp.exp(log_dt)[:, None]
    mag = jnp.exp(lam_re * dt)
    ab_re, ab_im = mag * jnp.cos(lam_im * dt), mag * jnp.sin(lam_im * dt)
    den = lam_re * lam_re + lam_im * lam_im
    z_re = ((ab_re - 1.0) * lam_re + ab_im * lam_im) / den
    z_im = (ab_im * lam_re - (ab_re - 1.0) * lam_im) / den
    bb_re = z_re[..., None] * b_re - z_im[..., None] * b_im
    bb_im = z_re[..., None] * b_im + z_im[..., None] * b_re
    eye = jnp.eye(GB, dtype=f32)

    def in_blocks(bb):
        return jnp.einsum("jgnc,gh->jgchn", bb.reshape(J, GB, N, Cn), eye).reshape(J, GB * Cn, GB * N)

    def out_blocks(cc):
        return jnp.einsum("jgcn,gh->jgnhc", cc.reshape(J, GB, Cn, N), eye).reshape(J, GB * N, GB * Cn)

    return (ab_re.reshape(1, G * N), ab_im.reshape(1, G * N), in_blocks(bb_re), in_blocks(bb_im),
            out_blocks(c_re), out_blocks(c_im), d.reshape(1, G * Cn))


def _s5_chunk(u, hin_r, hin_i, a_r, a_i, bb_r, bb_i, cc_r, cc_i, dvec):
    T = u.shape[0]
    hr = jnp.dot(u, bb_r, precision=HI, preferred_element_type=f32)
    hi = jnp.dot(u, bb_i, precision=HI, preferred_element_type=f32)
    row = lax.broadcasted_iota(jnp.int32, hr.shape, 0)
    hr = hr + jnp.where(row == 0, a_r * hin_r - a_i * hin_i, 0.0)
    hi = hi + jnp.where(row == 0, a_r * hin_i + a_i * hin_r, 0.0)
    pr, pi = a_r, a_i
    d = 1
    while d < T:
        sr = jnp.where(row >= d, _sroll(hr, d), 0.0)
        si = jnp.where(row >= d, _sroll(hi, d), 0.0)
        hr, hi = hr + pr * sr - pi * si, hi + pr * si + pi * sr
        pr, pi = pr * pr - pi * pi, 2.0 * pr * pi
        d *= 2
    y = (jnp.dot(hr, cc_r, precision=HI, preferred_element_type=f32)
         - jnp.dot(hi, cc_i, precision=HI, preferred_element_type=f32) + dvec * u)
    out_r = jnp.sum(jnp.where(row == T - 1, hr, 0.0), axis=0, keepdims=True)
    out_i = jnp.sum(jnp.where(row == T - 1, hi, 0.0), axis=0, keepdims=True)
    return y, out_r, out_i


def _s5_specs(L, base128, rev):
    T, J = S5_CHUNK, S5_LANE_BLOCKS
    n = L // T
    ch = (lambda c: n - 1 - c) if rev else (lambda c: c)
    ub = base128 + P_S5U // 128
    specs = [
        pl.BlockSpec((T, 128), lambda j, c: (ch(c), ub + j)),
        pl.BlockSpec((1, 512), lambda j, c: (0, j)), pl.BlockSpec((1, 512), lambda j, c: (0, j)),
        pl.BlockSpec((None, 128, 512), lambda j, c: (j, 0, 0)), pl.BlockSpec((None, 128, 512), lambda j, c: (j, 0, 0)),
        pl.BlockSpec((None, 512, 128), lambda j, c: (j, 0, 0)), pl.BlockSpec((None, 512, 128), lambda j, c: (j, 0, 0)),
        pl.BlockSpec((1, 128), lambda j, c: (0, j)),
    ]
    return n, ch, specs


def _s5_fwd(proj, prep, base128, name):
    L = proj.shape[0]
    T, J = S5_CHUNK, S5_LANE_BLOCKS
    n, _, specs = _s5_specs(L, base128, False)

    def body(u_ref, ar, ai, bbr, bbi, ccr, cci, dv, y_ref, sr_ref, si_ref, carry):
        @pl.when(pl.program_id(1) == 0)
        def _():
            carry[...] = jnp.zeros_like(carry)

        hin_r, hin_i = carry[0:1, :], carry[1:2, :]
        sr_ref[0] = jnp.broadcast_to(hin_r, (8, 512))
        si_ref[0] = jnp.broadcast_to(hin_i, (8, 512))
        y, out_r, out_i = _s5_chunk(u_ref[...], hin_r, hin_i, ar[...], ai[...], bbr[...], bbi[...], ccr[...], cci[...], dv[...])
        y_ref[...] = y
        carry[0:1, :] = out_r
        carry[1:2, :] = out_i

    st = pl.BlockSpec((1, 8, 512), lambda j, c: (c, 0, j))
    return pl.pallas_call(
        body, name=name, grid=(J, n),
        out_shape=[jax.ShapeDtypeStruct((L, 512), f32), jax.ShapeDtypeStruct((n, 8, 2048), f32), jax.ShapeDtypeStruct((n, 8, 2048), f32)],
        in_specs=specs, out_specs=[pl.BlockSpec((T, 128), lambda j, c: (c, j)), st, st],
        scratch_shapes=[pltpu.VMEM((8, 512), f32)],
        compiler_params=_params(("parallel", "arbitrary")),
    )(proj, *prep)


def _s5_bwd(proj, prep, st_r, st_i, dy, base128, name):
    L = proj.shape[0]
    T, J = S5_CHUNK, S5_LANE_BLOCKS
    n, ch, specs = _s5_specs(L, base128, True)

    def body(u_ref, ar, ai, bbr, bbi, ccr, cci, dv, sr_ref, si_ref, dy_ref,
             du_ref, dar, dai, dbbr, dbbi, dccr, dcci, ddv, dcarry):
        @pl.when(pl.program_id(1) == 0)
        def _():
            dcarry[...] = jnp.zeros_like(dcarry)
            for r in (dar, dai, dbbr, dbbi, dccr, dcci, ddv):
                r[...] = jnp.zeros_like(r)

        _, vjp = jax.vjp(_s5_chunk, u_ref[...], sr_ref[0, 0:1, :], si_ref[0, 0:1, :], ar[...], ai[...],
                         bbr[...], bbi[...], ccr[...], cci[...], dv[...])
        g = vjp((dy_ref[...], dcarry[0:1, :], dcarry[1:2, :]))
        du_ref[...] = g[0].astype(bf16)
        dcarry[0:1, :] = g[1]
        dcarry[1:2, :] = g[2]
        for r, val in zip((dar, dai, dbbr, dbbi, dccr, dcci, ddv), g[3:]):
            r[...] += val

    st = pl.BlockSpec((1, 8, 512), lambda j, c: (ch(c), 0, j))
    outs = pl.pallas_call(
        body, name=name, grid=(J, n),
        out_shape=[jax.ShapeDtypeStruct((L, 512), bf16),
                   jax.ShapeDtypeStruct((1, 2048), f32), jax.ShapeDtypeStruct((1, 2048), f32),
                   jax.ShapeDtypeStruct((J, 128, 512), f32), jax.ShapeDtypeStruct((J, 128, 512), f32),
                   jax.ShapeDtypeStruct((J, 512, 128), f32), jax.ShapeDtypeStruct((J, 512, 128), f32),
                   jax.ShapeDtypeStruct((1, 512), f32)],
        in_specs=specs + [st, st, pl.BlockSpec((T, 128), lambda j, c: (ch(c), j))],
        out_specs=[pl.BlockSpec((T, 128), lambda j, c: (ch(c), j))] + specs[1:],
        scratch_shapes=[pltpu.VMEM((8, 512), f32)],
        compiler_params=_params(("parallel", "arbitrary")),
    )(proj, *prep, st_r, st_i, dy)
    return outs[0], tuple(outs[1:])


def _glu_f(y, w, b):
    z = jax.nn.gelu(y)
    return z * jax.nn.sigmoid(jnp.dot(z.astype(bf16), w.astype(bf16), preferred_element_type=f32) + b)


def _glu_fwd(y, w, b, name):
    L = y.shape[0]
    tm = _tile(L, 512, 8)

    def body(y_ref, w_ref, b_ref, o_ref):
        o_ref[...] = _glu_f(y_ref[...], w_ref[...], b_ref[...]).astype(bf16)

    row = pl.BlockSpec((tm, 512), lambda i: (i, 0))
    return pl.pallas_call(
        body, name=name, grid=(L // tm,), out_shape=jax.ShapeDtypeStruct((L, 512), bf16),
        in_specs=[row, pl.BlockSpec((512, 512), lambda i: (0, 0)), pl.BlockSpec((1, 512), lambda i: (0, 0))],
        out_specs=row, compiler_params=_params(("parallel",)),
    )(y, w, b)


def _glu_bwd(y, w, b, dout, name):
    L = y.shape[0]
    tm = _tile(L, 512, 8)

    def body(y_ref, w_ref, b_ref, do_ref, dy_ref, dw_ref, db_ref):
        @pl.when(pl.program_id(0) == 0)
        def _():
            dw_ref[...] = jnp.zeros_like(dw_ref)
            db_ref[...] = jnp.zeros_like(db_ref)

        _, vjp = jax.vjp(_glu_f, y_ref[...], w_ref[...], b_ref[...])
        dy, dw, db = vjp(do_ref[...])
        dy_ref[...] = dy
        dw_ref[...] += dw
        db_ref[...] += db

    row = pl.BlockSpec((tm, 512), lambda i: (i, 0))
    wspec, bspec = pl.BlockSpec((512, 512), lambda i: (0, 0)), pl.BlockSpec((1, 512), lambda i: (0, 0))
    return pl.pallas_call(
        body, name=name, grid=(L // tm,),
        out_shape=[jax.ShapeDtypeStruct((L, 512), f32), jax.ShapeDtypeStruct((512, 512), f32), jax.ShapeDtypeStruct((1, 512), f32)],
        in_specs=[row, wspec, bspec, row], out_specs=[row, wspec, bspec],
        compiler_params=_params(("arbitrary",)),
    )(y, w, b, dout)


def _rope_tables(positions):
    half = ROPE_DIM // 2
    inv_freq = ROPE_THETA ** (-jnp.arange(half, dtype=f32) / half)
    ang = positions.astype(f32)[:, None] * inv_freq
    L = positions.shape[0]
    cos = jnp.concatenate([jnp.cos(ang), jnp.cos(ang), jnp.ones((L, HEAD_DIM - ROPE_DIM), f32)], axis=1)
    sin = jnp.concatenate([jnp.sin(ang), jnp.sin(ang), jnp.zeros((L, HEAD_DIM - ROPE_DIM), f32)], axis=1)
    return cos, sin


def _rot_matrix():
    half = ROPE_DIM // 2
    r = lax.broadcasted_iota(jnp.int32, (HEAD_DIM, HEAD_DIM), 0)
    c = lax.broadcasted_iota(jnp.int32, (HEAD_DIM, HEAD_DIM), 1)
    return jnp.where((c < half) & (r == c + half), -1.0, 0.0) + jnp.where((c >= half) & (c < ROPE_DIM) & (r == c - half), 1.0, 0.0)


def _attn_head(q, kp, kc, vp, vc, sink, *, cq, sq, ck, sk, lim, max_dist, use_rope):
    T = ATT_BLOCK
    k2 = jnp.concatenate([kp, kc], axis=0)
    v2 = jnp.concatenate([vp, vc], axis=0)
    if use_rope:
        rot = _rot_matrix()
        q = q * cq + jnp.dot(q, rot, precision=HI, preferred_element_type=f32) * sq
        k2 = k2 * ck + jnp.dot(k2, rot, precision=HI, preferred_element_type=f32) * sk
    s = lax.dot_general(q, k2, (((1,), (1,)), ((), ())), preferred_element_type=f32) * (HEAD_DIM ** -0.5)
    t = lax.broadcasted_iota(jnp.int32, (T, 2 * T), 0)
    j = lax.broadcasted_iota(jnp.int32, (T, 2 * T), 1)
    dist = T + t - j
    valid = (dist >= 0) & (dist <= max_dist) & (j >= lim)
    s = jnp.where(valid, s, -jnp.inf)
    m = lax.stop_gradient(jnp.max(s, axis=-1, keepdims=True))
    p = jnp.exp(s - m)
    den = jnp.sum(p, axis=-1, keepdims=True)
    o = jnp.dot(p, v2, preferred_element_type=f32) / den
    lse = jnp.broadcast_to(m + jnp.log(den), (T, HEAD_DIM))
    if sink is None:
        return o, lse
    return o * jax.nn.sigmoid(lse - sink)


def _attn_specs(L, q_col, k_col, v_col, wk, rev):
    T = ATT_BLOCK
    n = L // T
    blk = (lambda i: n - 1 - i) if rev else (lambda i: i)
    prev = lambda i: jnp.maximum(blk(i) - 1, 0)
    specs = [
        pl.BlockSpec((T, 512), lambda i: (blk(i), q_col)),
        pl.BlockSpec((T, wk), lambda i: (prev(i), k_col)), pl.BlockSpec((T, wk), lambda i: (blk(i), k_col)),
        pl.BlockSpec((T, wk), lambda i: (prev(i), v_col)), pl.BlockSpec((T, wk), lambda i: (blk(i), v_col)),
        pl.BlockSpec((T, 64), lambda i: (prev(i), 0)), pl.BlockSpec((T, 64), lambda i: (blk(i), 0)),
        pl.BlockSpec((T, 64), lambda i: (prev(i), 0)), pl.BlockSpec((T, 64), lambda i: (blk(i), 0)),
    ]
    return n, blk, specs


def _attn_fwd(qa, ka, va, cos, sin, sinks, *, q_col, k_col, v_col, hkv, nbc, max_dist, name):
    L = qa.shape[0]
    T, HQ, HD = ATT_BLOCK, 8, HEAD_DIM
    wk = hkv * HD
    n, _, specs = _attn_specs(L, q_col, k_col, v_col, wk, False)
    grp = HQ // hkv
    gated = sinks is not None

    def body(*refs):
        q_ref, kp_ref, kc_ref, vp_ref, vc_ref, cp_ref, cc_ref, sp_ref, sc_ref = refs[:9]
        rest = refs[9:]
        lim = jnp.where(pl.program_id(0) % nbc == 0, T, 0)
        ck = jnp.concatenate([cp_ref[...], cc_ref[...]], axis=0)
        sk = jnp.concatenate([sp_ref[...], sc_ref[...]], axis=0)
        for h in range(HQ):
            hs, ks = slice(h * HD, (h + 1) * HD), slice((h // grp) * HD, (h // grp + 1) * HD)
            res = _attn_head(q_ref[:, hs], kp_ref[:, ks], kc_ref[:, ks], vp_ref[:, ks], vc_ref[:, ks],
                             rest[0][:, hs] if gated else None,
                             cq=cc_ref[...], sq=sc_ref[...], ck=ck, sk=sk, lim=lim, max_dist=max_dist, use_rope=True)
            if gated:
                rest[1][:, hs] = res.astype(bf16)
            else:
                rest[0][:, hs] = res[0]
                rest[1][:, hs] = res[1]

    row = pl.BlockSpec((T, 512), lambda i: (i, 0))
    if gated:
        return pl.pallas_call(
            body, name=name, grid=(n,), out_shape=jax.ShapeDtypeStruct((L, 512), bf16),
            in_specs=specs + [pl.BlockSpec((1, 512), lambda i: (0, 0))], out_specs=row,
            compiler_params=_params(("parallel",)),
        )(qa, ka, ka, va, va, cos, cos, sin, sin, sinks)
    return pl.pallas_call(
        body, name=name, grid=(n,), out_shape=[jax.ShapeDtypeStruct((L, 512), f32)] * 2,
        in_specs=specs, out_specs=[row, row], compiler_params=_params(("parallel",)),
    )(qa, ka, ka, va, va, cos, cos, sin, sin)


def _attn_bwd(qa, ka, va, cos, sin, sinks, douts, *, q_col, k_col, v_col, hkv, nbc, max_dist, name):
    L = qa.shape[0]
    T, HQ, HD = ATT_BLOCK, 8, HEAD_DIM
    wk = hkv * HD
    n, blk, specs = _attn_specs(L, q_col, k_col, v_col, wk, True)
    grp = HQ // hkv
    gated = sinks is not None
    nd = len(douts)

    def body(*refs):
        q_ref, kp_ref, kc_ref, vp_ref, vc_ref, cp_ref, cc_ref, sp_ref, sc_ref = refs[:9]
        pos = 9
        sink_ref = None
        if gated:
            sink_ref = refs[pos]
            pos += 1
        d_refs = refs[pos:pos + nd]
        pos += nd
        dq_ref, dk_ref, dv_ref = refs[pos:pos + 3]
        pos += 3
        dsink_ref = None
        if gated:
            dsink_ref = refs[pos]
            pos += 1
        carry_k, carry_v = refs[pos:pos + 2]

        @pl.when(pl.program_id(0) == 0)
        def _():
            carry_k[...] = jnp.zeros_like(carry_k)
            carry_v[...] = jnp.zeros_like(carry_v)
            if gated:
                dsink_ref[...] = jnp.zeros_like(dsink_ref)

        lim = jnp.where(blk(pl.program_id(0)) % nbc == 0, T, 0)
        ck = jnp.concatenate([cp_ref[...], cc_ref[...]], axis=0)
        sk = jnp.concatenate([sp_ref[...], sc_ref[...]], axis=0)
        dkp = [jnp.zeros((T, HD), f32) for _ in range(hkv)]
        dkc = [jnp.zeros((T, HD), f32) for _ in range(hkv)]
        dvp = [jnp.zeros((T, HD), f32) for _ in range(hkv)]
        dvc = [jnp.zeros((T, HD), f32) for _ in range(hkv)]
        for h in range(HQ):
            g = h // grp
            hs, ks = slice(h * HD, (h + 1) * HD), slice(g * HD, (g + 1) * HD)
            fn = functools.partial(_attn_head, cq=cc_ref[...], sq=sc_ref[...], ck=ck, sk=sk, lim=lim,
                                   max_dist=max_dist, use_rope=True)
            prim = (q_ref[:, hs], kp_ref[:, ks], kc_ref[:, ks], vp_ref[:, ks], vc_ref[:, ks])
            if gated:
                _, vjp = jax.vjp(fn, *prim, sink_ref[:, hs])
                dq, a, b, c, d, ds = vjp(d_refs[0][:, hs].astype(f32))
                dsink_ref[:, hs] += ds
            else:
                _, vjp = jax.vjp(lambda *p: fn(*p, None), *prim)
                dq, a, b, c, d = vjp((d_refs[0][:, hs], d_refs[1][:, hs]))
            dq_ref[:, hs] = dq.astype(bf16)
            dkp[g], dkc[g], dvp[g], dvc[g] = dkp[g] + a, dkc[g] + b, dvp[g] + c, dvc[g] + d
        for g in range(hkv):
            ks = slice(g * HD, (g + 1) * HD)
            dk_ref[:, ks] = (dkc[g] + carry_k[:, ks]).astype(bf16)
            dv_ref[:, ks] = (dvc[g] + carry_v[:, ks]).astype(bf16)
            carry_k[:, ks] = dkp[g]
            carry_v[:, ks] = dvp[g]

    row = lambda w: pl.BlockSpec((T, w), lambda i: (blk(i), 0))
    vec = pl.BlockSpec((1, 512), lambda i: (0, 0))
    in_specs = specs + ([vec] if gated else []) + [row(512)] * nd
    out_shape = [jax.ShapeDtypeStruct((L, 512), bf16), jax.ShapeDtypeStruct((L, wk), bf16), jax.ShapeDtypeStruct((L, wk), bf16)]
    out_specs = [row(512), row(wk), row(wk)]
    if gated:
        out_shape.append(jax.ShapeDtypeStruct((1, 512), f32))
        out_specs.append(vec)
    args = (qa, ka, ka, va, va, cos, cos, sin, sin) + ((sinks,) if gated else ()) + tuple(douts)
    return pl.pallas_call(
        body, name=name, grid=(n,), out_shape=out_shape, in_specs=in_specs, out_specs=out_specs,
        scratch_shapes=[pltpu.VMEM((T, wk), f32), pltpu.VMEM((T, wk), f32)],
        compiler_params=_params(("arbitrary",)),
    )(*args)


def _dilmix_f(o0, o1, o2, l0, l1, l2):
    m = jnp.maximum(jnp.maximum(l0, l1), l2)
    e0, e1, e2 = jnp.exp(l0 - m), jnp.exp(l1 - m), jnp.exp(l2 - m)
    return (e0 * o0 + e1 * o1 + e2 * o2) / (e0 + e1 + e2)


def _dilmix_fwd(os_, ls, name):
    L = os_[0].shape[0]
    tm = _tile(L, 512, 8)

    def body(o0, o1, o2, l0, l1, l2, out):
        out[...] = _dilmix_f(o0[...], o1[...], o2[...], l0[...], l1[...], l2[...]).astype(bf16)

    row = pl.BlockSpec((tm, 512), lambda i: (i, 0))
    return pl.pallas_call(
        body, name=name, grid=(L // tm,), out_shape=jax.ShapeDtypeStruct((L, 512), bf16),
        in_specs=[row] * 6, out_specs=row, compiler_params=_params(("parallel",)),
    )(*os_, *ls)


def _dilmix_bwd(os_, ls, dout, name):
    L = os_[0].shape[0]
    tm = _tile(L, 512, 8)

    def body(o0, o1, o2, l0, l1, l2, d, *outs):
        _, vjp = jax.vjp(_dilmix_f, o0[...], o1[...], o2[...], l0[...], l1[...], l2[...])
        for r, val in zip(outs, vjp(d[...].astype(f32))):
            r[...] = val

    row = pl.BlockSpec((tm, 512), lambda i: (i, 0))
    outs = pl.pallas_call(
        body, name=name, grid=(L // tm,), out_shape=[jax.ShapeDtypeStruct((L, 512), f32)] * 6,
        in_specs=[row] * 7, out_specs=[row] * 6, compiler_params=_params(("parallel",)),
    )(*os_, *ls, dout)
    return outs[:3], outs[3:]


def _to_strided(z, dil):
    L, w = z.shape
    return z.reshape(L // dil, dil, w).transpose(1, 0, 2).reshape(L, w)


def _from_strided(z, dil):
    L, w = z.shape
    return z.reshape(dil, L // dil, w).transpose(1, 0, 2).reshape(L, w)


def _adamw_math(w, g, m, v):
    m = ADAM_B1 * m + (1.0 - ADAM_B1) * g
    v = ADAM_B2 * v + (1.0 - ADAM_B2) * (g * g)
    m_hat = m / (1.0 - ADAM_B1 ** ADAM_STEP)
    v_hat = v / (1.0 - ADAM_B2 ** ADAM_STEP)
    delta = -ADAM_LR * (m_hat / (jnp.sqrt(v_hat) + ADAM_EPS) + ADAM_WD * w)
    return delta, m, v


def _adamw(w, m, v, slots, name):
    _, R, C = w.shape
    tr = _tile(R, max(8, 131072 // C), 8)

    def body(w_ref, m_ref, v_ref, s_ref, g_ref, d_ref, nm_ref, nv_ref):
        g = s_ref[0, 0].astype(f32)
        for i in range(1, N_DEV):
            g = g + s_ref[i, 0].astype(f32)
        delta, nm, nv = _adamw_math(w_ref[0], g, m_ref[0], v_ref[0])
        g_ref[0], d_ref[0], nm_ref[0], nv_ref[0] = g, delta, nm, nv

    blk = pl.BlockSpec((1, tr, C), lambda l, i: (l, i, 0))
    return pl.pallas_call(
        body, name=name, grid=(2, R // tr), out_shape=[jax.ShapeDtypeStruct(w.shape, f32)] * 4,
        in_specs=[blk, blk, blk, pl.BlockSpec((N_DEV, 1, tr, C), lambda l, i: (0, l, i, 0))], out_specs=[blk] * 4,
        compiler_params=_params(("parallel", "parallel")),
    )(w, m, v, slots)


def _adamw_packed(w, m, v, slots, name):
    R = w.shape[0]
    tr = _tile(R, 512, 8)

    def body(w_ref, m_ref, v_ref, s_ref, g_ref, d_ref, nm_ref, nv_ref):
        g = s_ref[0]
        for i in range(1, N_DEV):
            g = g + s_ref[i]
        delta, nm, nv = _adamw_math(w_ref[...], g, m_ref[...], v_ref[...])
        g_ref[...], d_ref[...], nm_ref[...], nv_ref[...] = g, delta, nm, nv

    blk = pl.BlockSpec((tr, 128), lambda i: (i, 0))
    return pl.pallas_call(
        body, name=name, grid=(R // tr,), out_shape=[jax.ShapeDtypeStruct(w.shape, f32)] * 4,
        in_specs=[blk, blk, blk, pl.BlockSpec((N_DEV, tr, 128), lambda i: (0, i, 0))], out_specs=[blk] * 4,
        compiler_params=_params(("parallel",)),
    )(w, m, v, slots)


def _cols_gathered(g):
    nd = g.ndim
    perm = tuple(range(1, nd - 1)) + (0, nd - 1)
    t = g.transpose(perm)
    return t.reshape(t.shape[:-2] + (t.shape[-2] * t.shape[-1],))


def _cols_scatter(full):
    s = full.shape
    t = full.reshape(s[:-1] + (N_DEV, s[-1] // N_DEV))
    nd = t.ndim
    return t.transpose((nd - 2,) + tuple(range(nd - 2)) + (nd - 1,))


def _rows_gathered(g):
    t = g.transpose(1, 0, 2, 3)
    return t.reshape(t.shape[0], t.shape[1] * t.shape[2], t.shape[3])


def _rows_scatter(full):
    s = full.shape
    return full.reshape(s[0], N_DEV, s[1] // N_DEV, s[2]).transpose(1, 0, 2, 3)


def _win_reorder(w):
    pad = jnp.zeros(w.shape[:-1] + (128 - GLA_LOWRANK,), w.dtype)
    return jnp.concatenate([w[..., O_GATES:], w[..., :O_GLR], w[..., O_S5U:O_GATES], w[..., O_GLR:O_S5U], pad], axis=-1)


def _win_restore(wp, D):
    b = 4 * D
    return jnp.concatenate([wp[..., b:b + O_GLR], wp[..., b + P_GLR:b + P_GLR + GLA_LOWRANK],
                            wp[..., b + O_GLR:b + P_GLR], wp[..., :b]], axis=-1)


SMALL = ("norm1_g", "gla_a_b", "gla_norm_g", "s5_lambda_re", "s5_lambda_im", "s5_log_dt", "s5_b_re", "s5_b_im",
         "s5_c_re", "s5_c_im", "s5_d", "s5_glu_b", "swa_sinks", "norm2_g", "final_norm_g")
SHARDED = ("w_in", "gla_a2", "s5_glu_w", "w_branch", "w_out", "w_ffn_gate", "w_ffn_up", "w_ffn_down")
WEIGHTS = ("norm1_g", "w_in", "gla_a2", "gla_a_b", "gla_norm_g", "s5_lambda_re", "s5_lambda_im", "s5_log_dt", "s5_b_re",
           "s5_b_im", "s5_c_re", "s5_c_im", "s5_d", "s5_glu_w", "s5_glu_b", "swa_sinks", "w_branch", "w_out", "norm2_g",
           "w_ffn_gate", "w_ffn_up", "w_ffn_down", "final_norm_g")


def _pack(arrs):
    flat = jnp.concatenate([a.reshape(-1) for a in arrs])
    n = flat.shape[0]
    rows = -(-n // 1024) * 8
    return jnp.pad(flat, (0, rows * 128 - n)).reshape(rows, 128)


def _unpack(packed, like):
    flat = packed.reshape(-1)
    out, pos = [], 0
    for a in like:
        out.append(flat[pos:pos + a.size].reshape(a.shape))
        pos += a.size
    return out


def kernel(x, positions, norm1_g, w_in, gla_a2, gla_a_b, gla_norm_g, s5_lambda_re, s5_lambda_im, s5_log_dt, s5_b_re, s5_b_im, s5_c_re, s5_c_im, s5_d, s5_glu_w, s5_glu_b, swa_sinks, w_branch, w_out, norm2_g, w_ffn_gate, w_ffn_up, w_ffn_down, final_norm_g, loss_target, m_norm1_g, m_w_in, m_gla_a2, m_gla_a_b, m_gla_norm_g, m_s5_lambda_re, m_s5_lambda_im, m_s5_log_dt, m_s5_b_re, m_s5_b_im, m_s5_c_re, m_s5_c_im, m_s5_d, m_s5_glu_w, m_s5_glu_b, m_swa_sinks, m_w_branch, m_w_out, m_norm2_g, m_w_ffn_gate, m_w_ffn_up, m_w_ffn_down, m_final_norm_g, v_norm1_g, v_w_in, v_gla_a2, v_gla_a_b, v_gla_norm_g, v_s5_lambda_re, v_s5_lambda_im, v_s5_log_dt, v_s5_b_re, v_s5_b_im, v_s5_c_re, v_s5_c_im, v_s5_d, v_s5_glu_w, v_s5_glu_b, v_swa_sinks, v_w_branch, v_w_out, v_norm2_g, v_w_ffn_gate, v_w_ffn_up, v_w_ffn_down, v_final_norm_g):
    W = dict(norm1_g=norm1_g, w_in=w_in, gla_a2=gla_a2, gla_a_b=gla_a_b, gla_norm_g=gla_norm_g, s5_lambda_re=s5_lambda_re, s5_lambda_im=s5_lambda_im, s5_log_dt=s5_log_dt, s5_b_re=s5_b_re, s5_b_im=s5_b_im, s5_c_re=s5_c_re, s5_c_im=s5_c_im, s5_d=s5_d, s5_glu_w=s5_glu_w, s5_glu_b=s5_glu_b, swa_sinks=swa_sinks, w_branch=w_branch, w_out=w_out, norm2_g=norm2_g, w_ffn_gate=w_ffn_gate, w_ffn_up=w_ffn_up, w_ffn_down=w_ffn_down, final_norm_g=final_norm_g)
    Mo = dict(norm1_g=m_norm1_g, w_in=m_w_in, gla_a2=m_gla_a2, gla_a_b=m_gla_a_b, gla_norm_g=m_gla_norm_g, s5_lambda_re=m_s5_lambda_re, s5_lambda_im=m_s5_lambda_im, s5_log_dt=m_s5_log_dt, s5_b_re=m_s5_b_re, s5_b_im=m_s5_b_im, s5_c_re=m_s5_c_re, s5_c_im=m_s5_c_im, s5_d=m_s5_d, s5_glu_w=m_s5_glu_w, s5_glu_b=m_s5_glu_b, swa_sinks=m_swa_sinks, w_branch=m_w_branch, w_out=m_w_out, norm2_g=m_norm2_g, w_ffn_gate=m_w_ffn_gate, w_ffn_up=m_w_ffn_up, w_ffn_down=m_w_ffn_down, final_norm_g=m_final_norm_g)
    Vo = dict(norm1_g=v_norm1_g, w_in=v_w_in, gla_a2=v_gla_a2, gla_a_b=v_gla_a_b, gla_norm_g=v_gla_norm_g, s5_lambda_re=v_s5_lambda_re, s5_lambda_im=v_s5_lambda_im, s5_log_dt=v_s5_log_dt, s5_b_re=v_s5_b_re, s5_b_im=v_s5_b_im, s5_c_re=v_s5_c_re, s5_c_im=v_s5_c_im, s5_d=v_s5_d, s5_glu_w=v_s5_glu_w, s5_glu_b=v_s5_glu_b, swa_sinks=v_swa_sinks, w_branch=v_w_branch, w_out=v_w_out, norm2_g=v_norm2_g, w_ffn_gate=v_w_ffn_gate, w_ffn_up=v_w_ffn_up, w_ffn_down=v_w_ffn_down, final_norm_g=v_final_norm_g)

    L, D = x.shape[1], x.shape[2]
    depth = norm1_g.shape[0]
    xs = x.reshape(L, D)
    target = loss_target.reshape(L, D)
    base128 = 4 * D // 128

    big = ("w_in", "w_branch", "w_out", "w_ffn_gate", "w_ffn_up", "w_ffn_down")
    gathered = _all_gather([W[k].astype(bf16) for k in big] + [gla_a2, s5_glu_w], "gather_weights")
    G = dict(zip(big + ("gla_a2", "s5_glu_w"), gathered))
    win_p = _win_reorder(_cols_gathered(G["w_in"]))
    wb = _cols_gathered(G["w_branch"])
    wout = _rows_gathered(G["w_out"])
    wg, wu = _cols_gathered(G["w_ffn_gate"]), _cols_gathered(G["w_ffn_up"])
    wd = _rows_gathered(G["w_ffn_down"])
    a2_full = _cols_gathered(G["gla_a2"])
    a2p = jnp.pad(a2_full, ((0, 0), (0, 128 - GLA_LOWRANK), (0, 0)))
    glu_w = _rows_gathered(G["s5_glu_w"])

    cos, sin = _rope_tables(positions.reshape(L))
    strided_tabs = {dil: (_to_strided(cos, dil), _to_strided(sin, dil)) for _, dil in DIL_CONFIGS if dil > 1}

    saved = []
    cur = xs
    for l in range(depth):
        s = {"x": cur}
        h1 = _rms_fwd(cur, norm1_g[l][None], f"rms1_fwd{l}")
        proj = _matmul(h1, win_p[l], name=f"proj_in{l}")
        s["h1"], s["proj"] = h1, proj
        ab, ng = gla_a_b[l][None], gla_norm_g[l].reshape(1, 512)
        o_gla, s["gla_st"] = _gla_fwd(proj, a2p[l], ab, ng, base128, f"gla_fwd{l}")
        prep, s["prep_vjp"] = jax.vjp(_s5_prep, s5_lambda_re[l], s5_lambda_im[l], s5_log_dt[l], s5_b_re[l], s5_b_im[l],
                                      s5_c_re[l], s5_c_im[l], s5_d[l])
        s["prep"] = prep
        y_s5, s["s5_r"], s["s5_i"] = _s5_fwd(proj, prep, base128, f"s5_fwd{l}")
        s["y_s5"] = y_s5
        o_s5 = _glu_fwd(y_s5, glu_w[l], s5_glu_b[l][None], f"glu_fwd{l}")
        sinks_b = jnp.repeat(swa_sinks[l], HEAD_DIM)[None]
        s["sinks_b"] = sinks_b
        nb = L // ATT_BLOCK
        o_swa = _attn_fwd(proj, proj, proj, cos, sin, sinks_b, q_col=(base128 + P_SQ // 128) // 4,
                          k_col=base128 + P_SK // 128, v_col=base128 + P_SV // 128, hkv=SWA_KV_HEADS, nbc=nb,
                          max_dist=SWA_WINDOW - 1, name=f"swa_fwd{l}")
        cq, ck, cv = (base128 + P_CQ // 128) // 4, (base128 + P_CK // 128) // 4, (base128 + P_CV // 128) // 4
        dil_o, dil_l, s["dil_in"] = [], [], []
        for window, dil in DIL_CONFIGS:
            if dil == 1:
                o, lse = _attn_fwd(proj, proj, proj, cos, sin, None, q_col=cq, k_col=ck, v_col=cv, hkv=8, nbc=nb,
                                   max_dist=window // dil, name=f"dil{dil}_fwd{l}")
                s["dil_in"].append(None)
            else:
                qs_, ks_, vs_ = (_to_strided(proj[:, 4 * D + off:4 * D + off + 512], dil) for off in (P_CQ, P_CK, P_CV))
                o, lse = _attn_fwd(qs_, ks_, vs_, *strided_tabs[dil], None, q_col=0, k_col=0, v_col=0, hkv=8,
                                   nbc=nb // dil, max_dist=window // dil, name=f"dil{dil}_fwd{l}")
                o, lse = _from_strided(o, dil), _from_strided(lse, dil)
                s["dil_in"].append((qs_, ks_, vs_))
            dil_o.append(o)
            dil_l.append(lse)
        s["dil_o"], s["dil_l"] = dil_o, dil_l
        o_dil = _dilmix_fwd(dil_o, dil_l, f"dilmix_fwd{l}")
        branches = (o_gla, o_s5, o_dil, o_swa)
        s["branches"] = branches
        ys = [_matmul(br, wb[l, m], name=f"branch{m}_fwd{l}") for m, br in enumerate(branches)]
        s["ys"] = ys
        mixed = _merge_fwd(proj, ys, D, f"merge_fwd{l}")
        s["mixed"] = mixed
        x2 = _matmul(mixed, wout[l], res=cur, name=f"out_fwd{l}")
        s["x2"] = x2
        h2 = _rms_fwd(x2, norm2_g[l][None], f"rms2_fwd{l}")
        a = _matmul(h2, wg[l], name=f"ffn_gate_fwd{l}")
        b = _matmul(h2, wu[l], name=f"ffn_up_fwd{l}")
        act = _swiglu_fwd(a, b, f"swiglu_fwd{l}")
        s["h2"], s["a"], s["b"], s["act"] = h2, a, b, act
        cur = _matmul(act, wd[l], res=x2, name=f"ffn_down_fwd{l}")
        saved.append(s)

    loss_part, dcur, dgf = _final_loss(cur, final_norm_g[None], target, "final_loss")
    loss = lax.psum(loss_part, AXES)

    small_g = {k: [None] * depth for k in SMALL if k != "final_norm_g"}
    big_g = {k: [None] * depth for k in SHARDED}
    for l in reversed(range(depth)):
        s = saved[l]
        proj = s["proj"]
        dact = _matmul(dcur, wd[l], mode="nt", name=f"ffn_down_dx{l}")
        big_g["w_ffn_down"][l] = _matmul(s["act"], dcur, mode="tn", out_dtype=bf16, name=f"ffn_down_dw{l}")
        da, db = _swiglu_bwd(s["a"], s["b"], dact, f"swiglu_bwd{l}")
        dh2 = _matmul(da, wg[l], mode="nt", name=f"ffn_gate_dx{l}")
        dh2 = _matmul(db, wu[l], mode="nt", res=dh2, name=f"ffn_up_dx{l}")
        big_g["w_ffn_gate"][l] = _matmul(s["h2"], da, mode="tn", out_dtype=bf16, name=f"ffn_gate_dw{l}")
        big_g["w_ffn_up"][l] = _matmul(s["h2"], db, mode="tn", out_dtype=bf16, name=f"ffn_up_dw{l}")
        dx2, dg2 = _rms_bwd(s["x2"], norm2_g[l][None], dh2, dcur, f"rms2_bwd{l}")
        small_g["norm2_g"][l] = dg2[0]
        dmixed = _matmul(dx2, wout[l], mode="nt", name=f"out_dx{l}")
        big_g["w_out"][l] = _matmul(s["mixed"], dx2, mode="tn", out_dtype=bf16, name=f"out_dw{l}")
        dys, dgates = _merge_bwd(proj, s["ys"], dmixed, D, f"merge_bwd{l}")
        dbr = [_matmul(dys[m], wb[l, m], mode="nt", name=f"branch{m}_dx{l}") for m in range(4)]
        big_g["w_branch"][l] = jnp.stack([_matmul(s["branches"][m], dys[m], mode="tn", out_dtype=bf16, name=f"branch{m}_dw{l}")
                                          for m in range(4)])
        d_gla, d_s5, d_dil, d_swa = dbr
        ab, ng = gla_a_b[l][None], gla_norm_g[l].reshape(1, 512)
        dgq, dgk, dgv, dgr, dglr, da2, dab, dng = _gla_bwd(proj, a2p[l], ab, ng, s["gla_st"], d_gla, base128, f"gla_bwd{l}")
        big_g["gla_a2"][l] = da2[:GLA_LOWRANK]
        small_g["gla_a_b"][l] = dab[0]
        small_g["gla_norm_g"][l] = dng.reshape(GLA_HEADS, GLA_DV)
        dy_s5, dglu_w, dglu_b = _glu_bwd(s["y_s5"], glu_w[l], s5_glu_b[l][None], d_s5, f"glu_bwd{l}")
        big_g["s5_glu_w"][l] = dglu_w
        small_g["s5_glu_b"][l] = dglu_b[0]
        ds5u, dprep = _s5_bwd(proj, s["prep"], s["s5_r"], s["s5_i"], dy_s5, base128, f"s5_bwd{l}")
        draw = s["prep_vjp"](dprep)
        for k, val in zip(("s5_lambda_re", "s5_lambda_im", "s5_log_dt", "s5_b_re", "s5_b_im", "s5_c_re", "s5_c_im", "s5_d"), draw):
            small_g[k][l] = val
        nb = L // ATT_BLOCK
        dsq, dsk, dsv, dsinks = _attn_bwd(proj, proj, proj, cos, sin, s["sinks_b"], (d_swa,),
                                          q_col=(base128 + P_SQ // 128) // 4, k_col=base128 + P_SK // 128,
                                          v_col=base128 + P_SV // 128, hkv=SWA_KV_HEADS, nbc=nb, max_dist=SWA_WINDOW - 1,
                                          name=f"swa_bwd{l}")
        small_g["swa_sinks"][l] = dsinks.reshape(SWA_HEADS, HEAD_DIM).sum(axis=1)
        dos, dls = _dilmix_bwd(s["dil_o"], s["dil_l"], d_dil, f"dilmix_bwd{l}")
        cq, ck, cv = (base128 + P_CQ // 128) // 4, (base128 + P_CK // 128) // 4, (base128 + P_CV // 128) // 4
        dcq = dck = dcv = None
        for i, (window, dil) in enumerate(DIL_CONFIGS):
            if dil == 1:
                g3 = _attn_bwd(proj, proj, proj, cos, sin, None, (dos[i], dls[i]), q_col=cq, k_col=ck, v_col=cv, hkv=8,
                               nbc=nb, max_dist=window // dil, name=f"dil{dil}_bwd{l}")
            else:
                qs_, ks_, vs_ = s["dil_in"][i]
                g3 = _attn_bwd(qs_, ks_, vs_, *strided_tabs[dil], None, (_to_strided(dos[i], dil), _to_strided(dls[i], dil)),
                               q_col=0, k_col=0, v_col=0, hkv=8, nbc=nb // dil, max_dist=window // dil, name=f"dil{dil}_bwd{l}")
                g3 = [_from_strided(t, dil) for t in g3]
            g3 = [t.astype(f32) for t in g3]
            dcq, dck, dcv = (g3[0], g3[1], g3[2]) if dcq is None else (dcq + g3[0], dck + g3[1], dcv + g3[2])
        dproj = jnp.concatenate([dgates.transpose(1, 0, 2).reshape(L, 4 * D), dgq, dgk, dgv, dgr, ds5u,
                                 dcq.astype(bf16), dck.astype(bf16), dcv.astype(bf16), dsq, dsk, dsv, dglr], axis=1)
        dh1 = _matmul(dproj, win_p[l], mode="nt", name=f"proj_in_dx{l}")
        big_g["w_in"][l] = _matmul(s["h1"], dproj, mode="tn", out_dtype=bf16, name=f"proj_in_dw{l}")
        dcur, dg1 = _rms_bwd(s["x"], norm1_g[l][None], dh1, dx2, f"rms1_bwd{l}")
        small_g["norm1_g"][l] = dg1[0]
    grad_x = dcur.reshape(x.shape)

    sends = {
        "w_in": _cols_scatter(_win_restore(jnp.stack(big_g["w_in"]), D)),
        "gla_a2": _cols_scatter(jnp.stack(big_g["gla_a2"])),
        "s5_glu_w": _rows_scatter(jnp.stack(big_g["s5_glu_w"])),
        "w_branch": _cols_scatter(jnp.stack(big_g["w_branch"])),
        "w_out": _rows_scatter(jnp.stack(big_g["w_out"])),
        "w_ffn_gate": _cols_scatter(jnp.stack(big_g["w_ffn_gate"])),
        "w_ffn_up": _cols_scatter(jnp.stack(big_g["w_ffn_up"])),
        "w_ffn_down": _rows_scatter(jnp.stack(big_g["w_ffn_down"])),
    }
    recvs = dict(zip(SHARDED, _all_to_all([sends[k] for k in SHARDED], "exchange_grads")))
    out = {}
    for k in SHARDED:
        shp = W[k].shape
        as3 = lambda t: t.reshape((shp[0], -1, shp[-1]))
        slots = recvs[k].reshape((N_DEV, shp[0], -1, shp[-1]))
        res = _adamw(as3(W[k]), as3(Mo[k]), as3(Vo[k]), slots, f"adamw_{k}")
        out[k] = [t.reshape(shp) for t in res]

    small_list = [jnp.stack(small_g[k]) if k != "final_norm_g" else dgf[0] for k in SMALL]
    small_list = [t.reshape(W[k].shape) for t, k in zip(small_list, SMALL)]
    packed_parts = _all_gather([_pack(small_list)], "gather_small_grads")[0]
    res = _adamw_packed(_pack([W[k] for k in SMALL]), _pack([Mo[k] for k in SMALL]), _pack([Vo[k] for k in SMALL]),
                        packed_parts, "adamw_small")
    unpacked = [_unpack(t, [W[k] for k in SMALL]) for t in res]
    for i, k in enumerate(SMALL):
        out[k] = [unpacked[j][i] for j in range(4)]

    return (loss, grad_x, *[out[k][0] for k in WEIGHTS], *[out[k][1] for k in WEIGHTS],
            *[out[k][2] for k in WEIGHTS], *[out[k][3] for k in WEIGHTS])
```

```python
import functools
import math

import jax
import jax.numpy as jnp
from jax import lax
from jax.experimental import pallas as pl
from jax.experimental.pallas import tpu as pltpu

f32 = jnp.float32
bf16 = jnp.bfloat16
HI = lax.Precision.HIGHEST

N_DEV = 8
AXES = ("x", "y", "c")
NORM_EPS = 1e-6
ROPE_THETA = 500000.0
HEAD_DIM = 64
ROPE_DIM = 16
ATT_BLOCK = 128
BRANCH_WIDTH = 512
GLA_HEADS, GLA_DK, GLA_DV, GLA_LOWRANK, GLA_TAU, GLA_CHUNK, GLA_SUB = 4, 64, 128, 16, 16.0, 64, 16
S5_GROUPS, S5_GROUP, S5_STATE = 32, 16, 64
S5_CHUNK = 128
S5_LANE_BLOCKS = 4
DIL_CONFIGS = ((128, 1), (512, 4), (2048, 16))
SWA_HEADS, SWA_KV_HEADS, SWA_WINDOW = 8, 2, 128
ADAM_LR, ADAM_B1, ADAM_B2, ADAM_EPS, ADAM_WD, ADAM_STEP = 0.001, 0.9, 0.999, 1e-08, 0.01, 10
O_GLR, O_S5U, O_GATES = 1536, 1552, 4368
MIX_COLS = 4480
P_GQ, P_GK, P_GV, P_GR, P_S5U, P_CQ, P_CK, P_CV, P_SQ, P_SK, P_SV, P_GLR = (
    0, 256, 512, 1024, 1536, 2048, 2560, 3072, 3584, 4096, 4224, 4352)
VMEM_LIMIT = 56 * 1024 * 1024


def _tile(n, cap, q=128):
    if n <= cap:
        return n
    t = (cap // q) * q
    while t >= q:
        if n % t == 0:
            return t
        t -= q
    return n


def _params(sem=None):
    return pltpu.CompilerParams(dimension_semantics=sem, vmem_limit_bytes=VMEM_LIMIT)


@functools.partial(jax.custom_vjp, nondiff_argnums=(1,))
def _sroll(x, d):
    return pltpu.roll(x, d, 0)


def _sroll_fwd(x, d):
    return pltpu.roll(x, d, 0), None


def _sroll_bwd(d, _, g):
    n = g.shape[0]
    return (pltpu.roll(g, (n - d) % n, 0),)


_sroll.defvjp(_sroll_fwd, _sroll_bwd)


def _mesh_pos():
    return lax.axis_index("x"), lax.axis_index("y"), lax.axis_index("c")


def _all_gather(shards, name):
    n = len(shards)
    any_spec = pl.BlockSpec(memory_space=pl.ANY)

    def body(*refs):
        ins, outs = refs[:n], refs[n:2 * n]
        send_sems, recv_sems, local_sems = refs[2 * n:]
        x, y, c = _mesh_pos()
        me, sibling = (x, y, c), (x, y, 1 - c)
        chips = [(1 - x, y), (x, 1 - y), (1 - x, 1 - y)]

        def copy(a, k, block, to, src=None):
            slot = outs[a].at[4 * block[0] + 2 * block[1] + block[2]]
            return pltpu.make_async_remote_copy(
                src_ref=slot if src is None else src, dst_ref=slot,
                send_sem=send_sems.at[a, k], recv_sem=recv_sems.at[a, k],
                device_id=to, device_id_type=pl.DeviceIdType.MESH)

        started = []
        for a in range(n):
            mine = pltpu.make_async_copy(ins[a], outs[a].at[4 * x + 2 * y + c], local_sems.at[a])
            mine.start()
            first = [copy(a, 0, me, sibling, src=ins[a])]
            first += [copy(a, 1 + j, me, (*chip, c), src=ins[a]) for j, chip in enumerate(chips)]
            for cp in first:
                cp.start()
            started.append((mine, first))
        for a in range(n):
            mine, first = started[a]
            passed = [copy(a, 4 + j, (*chip, c), sibling) for j, chip in enumerate(chips)]
            for j, chip in enumerate(chips):
                copy(a, 1 + j, (*chip, c), me).wait_recv()
                passed[j].start()
            copy(a, 0, sibling, me).wait_recv()
            for j, chip in enumerate(chips):
                copy(a, 4 + j, (*chip, 1 - c), me).wait_recv()
            for cp in first + passed:
                cp.wait_send()
            mine.wait()

    outs = pl.pallas_call(
        body, name=name,
        out_shape=[jax.ShapeDtypeStruct((N_DEV,) + s.shape, s.dtype) for s in shards],
        in_specs=[any_spec] * n, out_specs=[any_spec] * n,
        scratch_shapes=[pltpu.SemaphoreType.DMA((n, 7)), pltpu.SemaphoreType.DMA((n, 7)),
                        pltpu.SemaphoreType.DMA((n,))],
    )(*shards)
    return list(outs)


def _all_to_all(sends, name):
    n = len(sends)
    any_spec = pl.BlockSpec(memory_space=pl.ANY)

    def body(*refs):
        ins, outs = refs[:n], refs[n:2 * n]
        send_sems, recv_sems, local_sems = refs[2 * n:]
        x, y, c = _mesh_pos()
        me = 4 * x + 2 * y + c
        copies = []
        for a in range(n):
            mine = pltpu.make_async_copy(ins[a].at[me], outs[a].at[me], local_sems.at[a])
            mine.start()
            copies.append(mine)
            for k in range(1, N_DEV):
                px = 1 - x if k & 4 else x
                py = 1 - y if k & 2 else y
                pc = 1 - c if k & 1 else c
                cp = pltpu.make_async_remote_copy(
                    src_ref=ins[a].at[4 * px + 2 * py + pc], dst_ref=outs[a].at[me],
                    send_sem=send_sems.at[a, k - 1], recv_sem=recv_sems.at[a, k - 1],
                    device_id=(px, py, pc), device_id_type=pl.DeviceIdType.MESH)
                cp.start()
                copies.append(cp)
        for cp in copies:
            cp.wait()

    outs = pl.pallas_call(
        body, name=name,
        out_shape=[jax.ShapeDtypeStruct(s.shape, s.dtype) for s in sends],
        in_specs=[any_spec] * n, out_specs=[any_spec] * n,
        scratch_shapes=[pltpu.SemaphoreType.DMA((n, 7)), pltpu.SemaphoreType.DMA((n, 7)),
                        pltpu.SemaphoreType.DMA((n,))],
    )(*sends)
    return list(outs)


def _matmul(a, b, *, mode="nn", out_dtype=f32, res=None, name):
    if mode == "tn":
        K, M = a.shape
    else:
        M, K = a.shape
    N = b.shape[0] if mode == "nt" else b.shape[1]
    k_cap = 2048 if (a.dtype == bf16 and b.dtype == bf16) else 1024
    tm, tn, tk = _tile(M, 1024), _tile(N, 1152), _tile(K, k_cap)
    nk = K // tk
    dims = {"nn": (((1,), (0,)), ((), ())), "nt": (((1,), (1,)), ((), ())), "tn": (((0,), (0,)), ((), ()))}[mode]

    def body(*refs):
        a_ref, b_ref = refs[:2]
        r_ref = refs[2] if res is not None else None
        o_ref = refs[3] if res is not None else refs[2]
        acc = refs[-1] if nk > 1 else None
        k = pl.program_id(2)
        part = lax.dot_general(a_ref[...].astype(bf16), b_ref[...].astype(bf16), dims, preferred_element_type=f32)

        def finish(r):
            if res is not None:
                r = r + r_ref[...]
            o_ref[...] = r.astype(o_ref.dtype)

        if nk == 1:
            finish(part)
            return

        @pl.when(k == 0)
        def _():
            acc[...] = part

        @pl.when((k > 0) & (k < nk - 1))
        def _():
            acc[...] += part

        @pl.when(k == nk - 1)
        def _():
            finish(acc[...] + part)

    a_spec = pl.BlockSpec((tk, tm), lambda i, j, k: (k, i)) if mode == "tn" else pl.BlockSpec((tm, tk), lambda i, j, k: (i, k))
    b_spec = pl.BlockSpec((tn, tk), lambda i, j, k: (j, k)) if mode == "nt" else pl.BlockSpec((tk, tn), lambda i, j, k: (k, j))
    o_spec = pl.BlockSpec((tm, tn), lambda i, j, k: (i, j))
    in_specs = [a_spec, b_spec] + ([o_spec] if res is not None else [])
    args = (a, b) + ((res,) if res is not None else ())
    return pl.pallas_call(
        body, name=name, grid=(M // tm, N // tn, nk),
        out_shape=jax.ShapeDtypeStruct((M, N), out_dtype),
        in_specs=in_specs, out_specs=o_spec,
        scratch_shapes=[pltpu.VMEM((tm, tn), f32)] if nk > 1 else [],
        compiler_params=_params(("parallel", "parallel", "arbitrary")),
    )(*args)


def _rms(x, g):
    return x * lax.rsqrt(jnp.mean(x * x, axis=-1, keepdims=True) + NORM_EPS) * g


def _rms_fwd(x, g, name):
    L, D = x.shape
    tm = _tile(L, 256, 8)

    def body(x_ref, g_ref, o_ref):
        o_ref[...] = _rms(x_ref[...], g_ref[...]).astype(bf16)

    return pl.pallas_call(
        body, name=name, grid=(L // tm,), out_shape=jax.ShapeDtypeStruct((L, D), bf16),
        in_specs=[pl.BlockSpec((tm, D), lambda i: (i, 0)), pl.BlockSpec((1, D), lambda i: (0, 0))],
        out_specs=pl.BlockSpec((tm, D), lambda i: (i, 0)),
        compiler_params=_params(("parallel",)),
    )(x, g)


def _rms_bwd(x, g, dh, dres, name):
    L, D = x.shape
    tm = _tile(L, 256, 8)

    def body(x_ref, g_ref, dh_ref, dres_ref, dx_ref, dxb_ref, dg_ref):
        _, vjp = jax.vjp(_rms, x_ref[...], g_ref[...])
        dx, dg = vjp(dh_ref[...])
        dx = dres_ref[...] + dx
        dx_ref[...] = dx
        dxb_ref[...] = dx.astype(bf16)

        @pl.when(pl.program_id(0) == 0)
        def _():
            dg_ref[...] = jnp.zeros_like(dg_ref)

        dg_ref[...] += dg

    row = pl.BlockSpec((tm, D), lambda i: (i, 0))
    vec = pl.BlockSpec((1, D), lambda i: (0, 0))
    return pl.pallas_call(
        body, name=name, grid=(L // tm,),
        out_shape=[jax.ShapeDtypeStruct((L, D), f32), jax.ShapeDtypeStruct((L, D), bf16), jax.ShapeDtypeStruct((1, D), f32)],
        in_specs=[row, vec, row, row], out_specs=[row, row, vec],
        compiler_params=_params(("arbitrary",)),
    )(x, g, dh, dres)


def _final_loss(x, g, target, name):
    L, D = x.shape
    tm = _tile(L, 256, 8)

    def body(x_ref, g_ref, t_ref, loss_ref, dx_ref, dxb_ref, dg_ref):
        tgt = t_ref[...]

        def f(xv, gv):
            err = _rms(xv, gv) - tgt
            return 0.5 * jnp.sum(jnp.mean(err * err, axis=-1, keepdims=True), axis=0, keepdims=True)

        val, vjp = jax.vjp(f, x_ref[...], g_ref[...])
        dx, dg = vjp(jnp.ones((1, 1), f32))
        dx_ref[...] = dx
        dxb_ref[...] = dx.astype(bf16)

        @pl.when(pl.program_id(0) == 0)
        def _():
            dg_ref[...] = jnp.zeros_like(dg_ref)
            loss_ref[...] = jnp.zeros_like(loss_ref)

        dg_ref[...] += dg
        loss_ref[...] += jnp.broadcast_to(val, loss_ref.shape)

    row = pl.BlockSpec((tm, D), lambda i: (i, 0))
    vec = pl.BlockSpec((1, D), lambda i: (0, 0))
    acc = pl.BlockSpec((8, 128), lambda i: (0, 0))
    loss, dx, dxb, dg = pl.pallas_call(
        body, name=name, grid=(L // tm,),
        out_shape=[jax.ShapeDtypeStruct((8, 128), f32), jax.ShapeDtypeStruct((L, D), f32), jax.ShapeDtypeStruct((L, D), bf16),
                   jax.ShapeDtypeStruct((1, D), f32)],
        in_specs=[row, vec, row], out_specs=[acc, row, row, vec],
        compiler_params=_params(("arbitrary",)),
    )(x, g, target)
    return loss[0, 0], dx, dxb, dg


def _swiglu_f(a, b):
    return jax.nn.silu(a) * b


def _swiglu_fwd(a, b, name):
    L, F = a.shape
    tm, tn = _tile(L, 512, 8), _tile(F, 1024)

    def body(a_ref, b_ref, o_ref):
        o_ref[...] = _swiglu_f(a_ref[...], b_ref[...]).astype(bf16)

    blk = pl.BlockSpec((tm, tn), lambda i, j: (i, j))
    return pl.pallas_call(
        body, name=name, grid=(L // tm, F // tn), out_shape=jax.ShapeDtypeStruct((L, F), bf16),
        in_specs=[blk, blk], out_specs=blk, compiler_params=_params(("parallel", "parallel")),
    )(a, b)


def _swiglu_bwd(a, b, dact, name):
    L, F = a.shape
    tm, tn = _tile(L, 512, 8), _tile(F, 1024)

    def body(a_ref, b_ref, d_ref, da_ref, db_ref):
        _, vjp = jax.vjp(_swiglu_f, a_ref[...], b_ref[...])
        da, db = vjp(d_ref[...])
        da_ref[...] = da.astype(bf16)
        db_ref[...] = db.astype(bf16)

    blk = pl.BlockSpec((tm, tn), lambda i, j: (i, j))
    return pl.pallas_call(
        body, name=name, grid=(L // tm, F // tn),
        out_shape=[jax.ShapeDtypeStruct((L, F), bf16)] * 2,
        in_specs=[blk, blk, blk], out_specs=[blk, blk], compiler_params=_params(("parallel", "parallel")),
    )(a, b, dact)


def _merge_f(g0, g1, g2, g3, y0, y1, y2, y3):
    s = jax.nn.sigmoid
    return s(g0) * y0 + s(g1) * y1 + s(g2) * y2 + s(g3) * y3


def _merge_fwd(proj, ys, D, name):
    L = proj.shape[0]
    tm, tn = _tile(L, 512, 8), _tile(D, 512)
    nj = D // tn

    def body(g0, g1, g2, g3, y0, y1, y2, y3, o_ref):
        o_ref[...] = _merge_f(g0[...], g1[...], g2[...], g3[...], y0[...], y1[...], y2[...], y3[...]).astype(bf16)

    gspecs = [pl.BlockSpec((tm, tn), functools.partial(lambda i, j, m: (i, m * nj + j), m=m)) for m in range(4)]
    blk = pl.BlockSpec((tm, tn), lambda i, j: (i, j))
    return pl.pallas_call(
        body, name=name, grid=(L // tm, nj), out_shape=jax.ShapeDtypeStruct((L, D), bf16),
        in_specs=gspecs + [blk] * 4, out_specs=blk, compiler_params=_params(("parallel", "parallel")),
    )(proj, proj, proj, proj, *ys)


def _merge_bwd(proj, ys, dmixed, D, name):
    L = proj.shape[0]
    tm, tn = _tile(L, 512, 8), _tile(D, 512)
    nj = D // tn

    def body(g0, g1, g2, g3, y0, y1, y2, y3, d_ref, dy0, dy1, dy2, dy3, dg_ref):
        _, vjp = jax.vjp(_merge_f, g0[...], g1[...], g2[...], g3[...], y0[...], y1[...], y2[...], y3[...])
        grads = vjp(d_ref[...])
        for m, r in enumerate((dy0, dy1, dy2, dy3)):
            r[...] = grads[4 + m].astype(bf16)
        for m in range(4):
            dg_ref[m] = grads[m].astype(bf16)

    gspecs = [pl.BlockSpec((tm, tn), functools.partial(lambda i, j, m: (i, m * nj + j), m=m)) for m in range(4)]
    blk = pl.BlockSpec((tm, tn), lambda i, j: (i, j))
    dgspec = pl.BlockSpec((4, tm, tn), lambda i, j: (0, i, j))
    outs = pl.pallas_call(
        body, name=name, grid=(L // tm, nj),
        out_shape=[jax.ShapeDtypeStruct((L, D), bf16)] * 4 + [jax.ShapeDtypeStruct((4, L, D), bf16)],
        in_specs=gspecs + [blk] * 5, out_specs=[blk] * 4 + [dgspec],
        compiler_params=_params(("parallel", "parallel")),
    )(proj, proj, proj, proj, *ys, dmixed)
    return outs[:4], outs[4]


def _gla_head(q, k, v, r, glr, st, a2, ab, ng):
    C, T = GLA_CHUNK, GLA_SUB
    row = lax.broadcasted_iota(jnp.int32, (C, C), 0)
    col = lax.broadcasted_iota(jnp.int32, (C, C), 1)
    tri = (col <= row).astype(f32)
    sel = (col == (row // T) * T).astype(f32)
    z = jnp.dot(glr, a2, preferred_element_type=f32) + ab
    g = jax.nn.log_sigmoid(z) / GLA_TAU
    cum = jnp.dot(tri, g, precision=HI, preferred_element_type=f32)
    excl = cum - g
    ref = jnp.dot(sel, excl, precision=HI, preferred_element_type=f32)
    qs = q * (GLA_DK ** -0.5)
    q_ref = qs * jnp.exp(cum - ref)
    rowk = lax.broadcasted_iota(jnp.int32, (C, GLA_DK), 0)
    a = jnp.zeros((C, C), f32)
    for s in range(1, C // T):
        ref_s = jnp.sum(jnp.where(rowk == s * T, excl, 0.0), axis=0, keepdims=True)
        k_ref = k * jnp.exp(jnp.where(rowk < s * T, ref_s - cum, -jnp.inf))
        a_s = lax.dot_general(q_ref, k_ref, (((1,), (1,)), ((), ())), preferred_element_type=f32)
        a = a + jnp.where(row // T == s, a_s, 0.0)
    o = jnp.dot(a, v, preferred_element_type=f32)
    sub = rowk % T
    for d in range(T):
        ks = _sroll(k, d) if d else k
        cs = _sroll(cum, d) if d else cum
        vs = _sroll(v, d) if d else v
        w = jnp.sum(qs * ks * jnp.exp(jnp.where(sub >= d, cum - cs, -jnp.inf)), axis=-1, keepdims=True)
        o = o + w * vs
    o = o + lax.dot_general(qs * jnp.exp(cum), st, (((1,), (1,)), ((), ())), preferred_element_type=f32)
    last = jnp.sum(jnp.where(rowk == C - 1, cum, 0.0), axis=0, keepdims=True)
    st_new = st * jnp.exp(last) + lax.dot_general(v, k * jnp.exp(last - cum), (((0,), (0,)), ((), ())),
                                                  preferred_element_type=f32)
    out = _rms(o, ng) * jax.nn.silu(r)
    return out, st_new


def _gla_specs(L, base128, rev):
    n = L // GLA_CHUNK
    ch = (lambda i: n - 1 - i) if rev else (lambda i: i)
    b = base128
    return n, ch, [
        pl.BlockSpec((GLA_CHUNK, 256), lambda i: (ch(i), (b + P_GQ // 128) // 2)),
        pl.BlockSpec((GLA_CHUNK, 256), lambda i: (ch(i), (b + P_GK // 128) // 2)),
        pl.BlockSpec((GLA_CHUNK, 512), lambda i: (ch(i), (b + P_GV // 128) // 4)),
        pl.BlockSpec((GLA_CHUNK, 512), lambda i: (ch(i), (b + P_GR // 128) // 4)),
        pl.BlockSpec((GLA_CHUNK, 128), lambda i: (ch(i), b + P_GLR // 128)),
    ]


def _gla_fwd(proj, a2p, ab, ng, base128, name):
    L = proj.shape[0]
    n, _, pspecs = _gla_specs(L, base128, False)
    H, DK, DV = GLA_HEADS, GLA_DK, GLA_DV

    def body(q_ref, k_ref, v_ref, r_ref, l_ref, a2_ref, ab_ref, ng_ref, o_ref, st_ref, state):
        @pl.when(pl.program_id(0) == 0)
        def _():
            state[...] = jnp.zeros_like(state)

        st_ref[0] = state[...]
        glr = l_ref[...]
        for h in range(H):
            kk, vv = slice(h * DK, (h + 1) * DK), slice(h * DV, (h + 1) * DV)
            out, st_new = _gla_head(q_ref[:, kk], k_ref[:, kk], v_ref[:, vv], r_ref[:, vv], glr, state[h],
                                    a2_ref[:, kk], ab_ref[:, kk], ng_ref[:, vv])
            o_ref[:, vv] = out.astype(bf16)
            state[h] = st_new

    full = lambda shape: pl.BlockSpec(shape, lambda i: (0,) * len(shape))
    return pl.pallas_call(
        body, name=name, grid=(n,),
        out_shape=[jax.ShapeDtypeStruct((L, H * DV), bf16), jax.ShapeDtypeStruct((n, H, DV, DK), f32)],
        in_specs=pspecs + [full((128, 256)), full((1, 256)), full((1, 512))],
        out_specs=[pl.BlockSpec((GLA_CHUNK, 512), lambda i: (i, 0)), pl.BlockSpec((1, H, DV, DK), lambda i: (i, 0, 0, 0))],
        scratch_shapes=[pltpu.VMEM((H, DV, DK), f32)],
        compiler_params=_params(("arbitrary",)),
    )(proj, proj, proj, proj, proj, a2p, ab, ng)


def _gla_bwd(proj, a2p, ab, ng, states, dout, base128, name):
    L = proj.shape[0]
    n, ch, pspecs = _gla_specs(L, base128, True)
    H, DK, DV = GLA_HEADS, GLA_DK, GLA_DV

    def body(q_ref, k_ref, v_ref, r_ref, l_ref, a2_ref, ab_ref, ng_ref, st_ref, do_ref,
             dq_ref, dk_ref, dv_ref, dr_ref, dl_ref, da2_ref, dab_ref, dng_ref, dstate):
        @pl.when(pl.program_id(0) == 0)
        def _():
            dstate[...] = jnp.zeros_like(dstate)
            da2_ref[...] = jnp.zeros_like(da2_ref)
            dab_ref[...] = jnp.zeros_like(dab_ref)
            dng_ref[...] = jnp.zeros_like(dng_ref)

        glr = l_ref[...]
        dglr = jnp.zeros(glr.shape, f32)
        for h in range(H):
            kk, vv = slice(h * DK, (h + 1) * DK), slice(h * DV, (h + 1) * DV)
            _, vjp = jax.vjp(_gla_head, q_ref[:, kk], k_ref[:, kk], v_ref[:, vv], r_ref[:, vv], glr, st_ref[0, h],
                             a2_ref[:, kk], ab_ref[:, kk], ng_ref[:, vv])
            dq, dk, dv, dr, dl, dst, da2, dab, dng = vjp((do_ref[:, vv].astype(f32), dstate[h]))
            dq_ref[:, kk] = dq.astype(bf16)
            dk_ref[:, kk] = dk.astype(bf16)
            dv_ref[:, vv] = dv.astype(bf16)
            dr_ref[:, vv] = dr.astype(bf16)
            dglr = dglr + dl
            dstate[h] = dst
            da2_ref[:, kk] += da2
            dab_ref[:, kk] += dab
            dng_ref[:, vv] += dng
        dl_ref[...] = dglr.astype(bf16)

    full = lambda shape: pl.BlockSpec(shape, lambda i: (0,) * len(shape))
    rowspec = lambda w: pl.BlockSpec((GLA_CHUNK, w), lambda i: (ch(i), 0))
    return pl.pallas_call(
        body, name=name, grid=(n,),
        out_shape=[jax.ShapeDtypeStruct((L, 256), bf16), jax.ShapeDtypeStruct((L, 256), bf16),
                   jax.ShapeDtypeStruct((L, 512), bf16), jax.ShapeDtypeStruct((L, 512), bf16),
                   jax.ShapeDtypeStruct((L, 128), bf16), jax.ShapeDtypeStruct((128, 256), f32),
                   jax.ShapeDtypeStruct((1, 256), f32), jax.ShapeDtypeStruct((1, 512), f32)],
        in_specs=pspecs + [full((128, 256)), full((1, 256)), full((1, 512)),
                           pl.BlockSpec((1, H, DV, DK), lambda i: (ch(i), 0, 0, 0)), rowspec(512)],
        out_specs=[rowspec(256), rowspec(256), rowspec(512), rowspec(512), rowspec(128),
                   full((128, 256)), full((1, 256)), full((1, 512))],
        scratch_shapes=[pltpu.VMEM((H, DV, DK), f32)],
        compiler_params=_params(("arbitrary",)),
    )(proj, proj, proj, proj, proj, a2p, ab, ng, states, dout)


def _s5_prep(lam_re, lam_im, log_dt, b_re, b_im, c_re, c_im, d):
    G, N, Cn = S5_GROUPS, S5_STATE, S5_GROUP
    J, GB = S5_LANE_BLOCKS, S5_GROUPS // S5_LANE_BLOCKS
    dt = jnp.exp(log_dt)[:, None]
    mag = jnp.exp(lam_re * dt)
    ab_re, ab_im = mag * jnp.cos(lam_im * dt), mag * jnp.sin(lam_im * dt)
    den = lam_re * lam_re + lam_im * lam_im
    z_re = ((ab_re - 1.0) * lam_re + ab_im * lam_im) / den
    z_im = (ab_im * lam_re - (ab_re - 1.0) * lam_im) / den
    bb_re = z_re[..., None] * b_re - z_im[..., None] * b_im
    bb_im = z_re[..., None] * b_im + z_im[..., None] * b_re
    eye = jnp.eye(GB, dtype=f32)

    def in_blocks(bb):
        return jnp.einsum("jgnc,gh->jgchn", bb.reshape(J, GB, N, Cn), eye).reshape(J, GB * Cn, GB * N)

    def out_blocks(cc):
        return jnp.einsum("jgcn,gh->jgnhc", cc.reshape(J, GB, Cn, N), eye).reshape(J, GB * N, GB * Cn)

    return (ab_re.reshape(1, G * N), ab_im.reshape(1, G * N), in_blocks(bb_re), in_blocks(bb_im),
            out_blocks(c_re), out_blocks(c_im), d.reshape(1, G * Cn))


def _s5_chunk(u, hin_r, hin_i, a_r, a_i, bb_r, bb_i, cc_r, cc_i, dvec):
    T = u.shape[0]
    hr = jnp.dot(u, bb_r, preferred_element_type=f32)
    hi = jnp.dot(u, bb_i, preferred_element_type=f32)
    row = lax.broadcasted_iota(jnp.int32, hr.shape, 0)
    hr = hr + jnp.where(row == 0, a_r * hin_r - a_i * hin_i, 0.0)
    hi = hi + jnp.where(row == 0, a_r * hin_i + a_i * hin_r, 0.0)
    pr, pi = a_r, a_i
    d = 1
    while d < T:
        sr = jnp.where(row >= d, _sroll(hr, d), 0.0)
        si = jnp.where(row >= d, _sroll(hi, d), 0.0)
        hr, hi = hr + pr * sr - pi * si, hi + pr * si + pi * sr
        pr, pi = pr * pr - pi * pi, 2.0 * pr * pi
        d *= 2
    y = (jnp.dot(hr, cc_r, preferred_element_type=f32)
         - jnp.dot(hi, cc_i, preferred_element_type=f32) + dvec * u)
    out_r = jnp.sum(jnp.where(row == T - 1, hr, 0.0), axis=0, keepdims=True)
    out_i = jnp.sum(jnp.where(row == T - 1, hi, 0.0), axis=0, keepdims=True)
    return y, out_r, out_i


def _s5_specs(L, base128, rev):
    T, J = S5_CHUNK, S5_LANE_BLOCKS
    n = L // T
    ch = (lambda c: n - 1 - c) if rev else (lambda c: c)
    ub = base128 + P_S5U // 128
    specs = [
        pl.BlockSpec((T, 128), lambda j, c: (ch(c), ub + j)),
        pl.BlockSpec((1, 512), lambda j, c: (0, j)), pl.BlockSpec((1, 512), lambda j, c: (0, j)),
        pl.BlockSpec((None, 128, 512), lambda j, c: (j, 0, 0)), pl.BlockSpec((None, 128, 512), lambda j, c: (j, 0, 0)),
        pl.BlockSpec((None, 512, 128), lambda j, c: (j, 0, 0)), pl.BlockSpec((None, 512, 128), lambda j, c: (j, 0, 0)),
        pl.BlockSpec((1, 128), lambda j, c: (0, j)),
    ]
    return n, ch, specs


def _s5_fwd(proj, prep, base128, name):
    L = proj.shape[0]
    T, J = S5_CHUNK, S5_LANE_BLOCKS
    n, _, specs = _s5_specs(L, base128, False)

    def body(u_ref, ar, ai, bbr, bbi, ccr, cci, dv, y_ref, sr_ref, si_ref, carry):
        @pl.when(pl.program_id(1) == 0)
        def _():
            carry[...] = jnp.zeros_like(carry)

        hin_r, hin_i = carry[0:1, :], carry[1:2, :]
        sr_ref[0] = jnp.broadcast_to(hin_r, (8, 512))
        si_ref[0] = jnp.broadcast_to(hin_i, (8, 512))
        y, out_r, out_i = _s5_chunk(u_ref[...], hin_r, hin_i, ar[...], ai[...], bbr[...], bbi[...], ccr[...], cci[...], dv[...])
        y_ref[...] = y
        carry[0:1, :] = out_r
        carry[1:2, :] = out_i

    st = pl.BlockSpec((1, 8, 512), lambda j, c: (c, 0, j))
    return pl.pallas_call(
        body, name=name, grid=(J, n),
        out_shape=[jax.ShapeDtypeStruct((L, 512), f32), jax.ShapeDtypeStruct((n, 8, 2048), f32), jax.ShapeDtypeStruct((n, 8, 2048), f32)],
        in_specs=specs, out_specs=[pl.BlockSpec((T, 128), lambda j, c: (c, j)), st, st],
        scratch_shapes=[pltpu.VMEM((8, 512), f32)],
        compiler_params=_params(("parallel", "arbitrary")),
    )(proj, *prep)


def _s5_bwd(proj, prep, st_r, st_i, dy, base128, name):
    L = proj.shape[0]
    T, J = S5_CHUNK, S5_LANE_BLOCKS
    n, ch, specs = _s5_specs(L, base128, True)

    def body(u_ref, ar, ai, bbr, bbi, ccr, cci, dv, sr_ref, si_ref, dy_ref,
             du_ref, dar, dai, dbbr, dbbi, dccr, dcci, ddv, dcarry):
        @pl.when(pl.program_id(1) == 0)
        def _():
            dcarry[...] = jnp.zeros_like(dcarry)
            for r in (dar, dai, dbbr, dbbi, dccr, dcci, ddv):
                r[...] = jnp.zeros_like(r)

        _, vjp = jax.vjp(_s5_chunk, u_ref[...], sr_ref[0, 0:1, :], si_ref[0, 0:1, :], ar[...], ai[...],
                         bbr[...], bbi[...], ccr[...], cci[...], dv[...])
        g = vjp((dy_ref[...], dcarry[0:1, :], dcarry[1:2, :]))
        du_ref[...] = g[0].astype(bf16)
        dcarry[0:1, :] = g[1]
        dcarry[1:2, :] = g[2]
        for r, val in zip((dar, dai, dbbr, dbbi, dccr, dcci, ddv), g[3:]):
            r[...] += val

    st = pl.BlockSpec((1, 8, 512), lambda j, c: (ch(c), 0, j))
    outs = pl.pallas_call(
        body, name=name, grid=(J, n),
        out_shape=[jax.ShapeDtypeStruct((L, 512), bf16),
                   jax.ShapeDtypeStruct((1, 2048), f32), jax.ShapeDtypeStruct((1, 2048), f32),
                   jax.ShapeDtypeStruct((J, 128, 512), f32), jax.ShapeDtypeStruct((J, 128, 512), f32),
                   jax.ShapeDtypeStruct((J, 512, 128), f32), jax.ShapeDtypeStruct((J, 512, 128), f32),
                   jax.ShapeDtypeStruct((1, 512), f32)],
        in_specs=specs + [st, st, pl.BlockSpec((T, 128), lambda j, c: (ch(c), j))],
        out_specs=[pl.BlockSpec((T, 128), lambda j, c: (ch(c), j))] + specs[1:],
        scratch_shapes=[pltpu.VMEM((8, 512), f32)],
        compiler_params=_params(("parallel", "arbitrary")),
    )(proj, *prep, st_r, st_i, dy)
    return outs[0], tuple(outs[1:])


def _glu_f(y, w, b):
    z = jax.nn.gelu(y)
    return z * jax.nn.sigmoid(jnp.dot(z.astype(bf16), w.astype(bf16), preferred_element_type=f32) + b)


def _glu_fwd(y, w, b, name):
    L = y.shape[0]
    tm = _tile(L, 512, 8)

    def body(y_ref, w_ref, b_ref, o_ref):
        o_ref[...] = _glu_f(y_ref[...], w_ref[...], b_ref[...]).astype(bf16)

    row = pl.BlockSpec((tm, 512), lambda i: (i, 0))
    return pl.pallas_call(
        body, name=name, grid=(L // tm,), out_shape=jax.ShapeDtypeStruct((L, 512), bf16),
        in_specs=[row, pl.BlockSpec((512, 512), lambda i: (0, 0)), pl.BlockSpec((1, 512), lambda i: (0, 0))],
        out_specs=row, compiler_params=_params(("parallel",)),
    )(y, w, b)


def _glu_bwd(y, w, b, dout, name):
    L = y.shape[0]
    tm = _tile(L, 512, 8)

    def body(y_ref, w_ref, b_ref, do_ref, dy_ref, dw_ref, db_ref):
        @pl.when(pl.program_id(0) == 0)
        def _():
            dw_ref[...] = jnp.zeros_like(dw_ref)
            db_ref[...] = jnp.zeros_like(db_ref)

        _, vjp = jax.vjp(_glu_f, y_ref[...], w_ref[...], b_ref[...])
        dy, dw, db = vjp(do_ref[...])
        dy_ref[...] = dy
        dw_ref[...] += dw
        db_ref[...] += db

    row = pl.BlockSpec((tm, 512), lambda i: (i, 0))
    wspec, bspec = pl.BlockSpec((512, 512), lambda i: (0, 0)), pl.BlockSpec((1, 512), lambda i: (0, 0))
    return pl.pallas_call(
        body, name=name, grid=(L // tm,),
        out_shape=[jax.ShapeDtypeStruct((L, 512), f32), jax.ShapeDtypeStruct((512, 512), f32), jax.ShapeDtypeStruct((1, 512), f32)],
        in_specs=[row, wspec, bspec, row], out_specs=[row, wspec, bspec],
        compiler_params=_params(("arbitrary",)),
    )(y, w, b, dout)


def _rope_tables(positions):
    half = ROPE_DIM // 2
    inv_freq = ROPE_THETA ** (-jnp.arange(half, dtype=f32) / half)
    ang = positions.astype(f32)[:, None] * inv_freq
    L = positions.shape[0]
    cos = jnp.concatenate([jnp.cos(ang), jnp.cos(ang), jnp.ones((L, HEAD_DIM - ROPE_DIM), f32)], axis=1)
    sin = jnp.concatenate([jnp.sin(ang), jnp.sin(ang), jnp.zeros((L, HEAD_DIM - ROPE_DIM), f32)], axis=1)
    return cos, sin


def _rot_matrix():
    half = ROPE_DIM // 2
    r = lax.broadcasted_iota(jnp.int32, (HEAD_DIM, HEAD_DIM), 0)
    c = lax.broadcasted_iota(jnp.int32, (HEAD_DIM, HEAD_DIM), 1)
    return jnp.where((c < half) & (r == c + half), -1.0, 0.0) + jnp.where((c >= half) & (c < ROPE_DIM) & (r == c - half), 1.0, 0.0)


def _attn_head(q, kp, kc, vp, vc, sink, *, cq, sq, ck, sk, lim, max_dist, use_rope):
    T = ATT_BLOCK
    k2 = jnp.concatenate([kp, kc], axis=0)
    v2 = jnp.concatenate([vp, vc], axis=0)
    if use_rope:
        rot = _rot_matrix()
        q = q * cq + jnp.dot(q, rot, preferred_element_type=f32) * sq
        k2 = k2 * ck + jnp.dot(k2, rot, preferred_element_type=f32) * sk
    s = lax.dot_general(q, k2, (((1,), (1,)), ((), ())), preferred_element_type=f32) * (HEAD_DIM ** -0.5)
    t = lax.broadcasted_iota(jnp.int32, (T, 2 * T), 0)
    j = lax.broadcasted_iota(jnp.int32, (T, 2 * T), 1)
    dist = T + t - j
    valid = (dist >= 0) & (dist <= max_dist) & (j >= lim)
    s = jnp.where(valid, s, -jnp.inf)
    m = lax.stop_gradient(jnp.max(s, axis=-1, keepdims=True))
    p = jnp.exp(s - m)
    den = jnp.sum(p, axis=-1, keepdims=True)
    o = jnp.dot(p, v2, preferred_element_type=f32) / den
    lse = jnp.broadcast_to(m + jnp.log(den), (T, HEAD_DIM))
    if sink is None:
        return o, lse
    return o * jax.nn.sigmoid(lse - sink)


def _attn_specs(L, q_col, k_col, v_col, wk, rev):
    T = ATT_BLOCK
    n = L // T
    blk = (lambda i: n - 1 - i) if rev else (lambda i: i)
    prev = lambda i: jnp.maximum(blk(i) - 1, 0)
    specs = [
        pl.BlockSpec((T, 512), lambda i: (blk(i), q_col)),
        pl.BlockSpec((T, wk), lambda i: (prev(i), k_col)), pl.BlockSpec((T, wk), lambda i: (blk(i), k_col)),
        pl.BlockSpec((T, wk), lambda i: (prev(i), v_col)), pl.BlockSpec((T, wk), lambda i: (blk(i), v_col)),
        pl.BlockSpec((T, 64), lambda i: (prev(i), 0)), pl.BlockSpec((T, 64), lambda i: (blk(i), 0)),
        pl.BlockSpec((T, 64), lambda i: (prev(i), 0)), pl.BlockSpec((T, 64), lambda i: (blk(i), 0)),
    ]
    return n, blk, specs


def _attn_fwd(qa, ka, va, cos, sin, sinks, *, q_col, k_col, v_col, hkv, nbc, max_dist, name):
    L = qa.shape[0]
    T, HQ, HD = ATT_BLOCK, 8, HEAD_DIM
    wk = hkv * HD
    n, _, specs = _attn_specs(L, q_col, k_col, v_col, wk, False)
    grp = HQ // hkv
    gated = sinks is not None

    def body(*refs):
        q_ref, kp_ref, kc_ref, vp_ref, vc_ref, cp_ref, cc_ref, sp_ref, sc_ref = refs[:9]
        rest = refs[9:]
        lim = jnp.where(pl.program_id(0) % nbc == 0, T, 0)
        ck = jnp.concatenate([cp_ref[...], cc_ref[...]], axis=0)
        sk = jnp.concatenate([sp_ref[...], sc_ref[...]], axis=0)
        for h in range(HQ):
            hs, ks = slice(h * HD, (h + 1) * HD), slice((h // grp) * HD, (h // grp + 1) * HD)
            res = _attn_head(q_ref[:, hs], kp_ref[:, ks], kc_ref[:, ks], vp_ref[:, ks], vc_ref[:, ks],
                             rest[0][:, hs] if gated else None,
                             cq=cc_ref[...], sq=sc_ref[...], ck=ck, sk=sk, lim=lim, max_dist=max_dist, use_rope=True)
            if gated:
                rest[1][:, hs] = res.astype(bf16)
            else:
                rest[0][:, hs] = res[0]
                rest[1][:, hs] = res[1]

    row = pl.BlockSpec((T, 512), lambda i: (i, 0))
    if gated:
        return pl.pallas_call(
            body, name=name, grid=(n,), out_shape=jax.ShapeDtypeStruct((L, 512), bf16),
            in_specs=specs + [pl.BlockSpec((1, 512), lambda i: (0, 0))], out_specs=row,
            compiler_params=_params(("parallel",)),
        )(qa, ka, ka, va, va, cos, cos, sin, sin, sinks)
    return pl.pallas_call(
        body, name=name, grid=(n,), out_shape=[jax.ShapeDtypeStruct((L, 512), f32)] * 2,
        in_specs=specs, out_specs=[row, row], compiler_params=_params(("parallel",)),
    )(qa, ka, ka, va, va, cos, cos, sin, sin)


def _attn_bwd(qa, ka, va, cos, sin, sinks, douts, *, q_col, k_col, v_col, hkv, nbc, max_dist, name):
    L = qa.shape[0]
    T, HQ, HD = ATT_BLOCK, 8, HEAD_DIM
    wk = hkv * HD
    n, blk, specs = _attn_specs(L, q_col, k_col, v_col, wk, True)
    grp = HQ // hkv
    gated = sinks is not None
    nd = len(douts)

    def body(*refs):
        q_ref, kp_ref, kc_ref, vp_ref, vc_ref, cp_ref, cc_ref, sp_ref, sc_ref = refs[:9]
        pos = 9
        sink_ref = None
        if gated:
            sink_ref = refs[pos]
            pos += 1
        d_refs = refs[pos:pos + nd]
        pos += nd
        dq_ref, dk_ref, dv_ref = refs[pos:pos + 3]
        pos += 3
        dsink_ref = None
        if gated:
            dsink_ref = refs[pos]
            pos += 1
        carry_k, carry_v = refs[pos:pos + 2]

        @pl.when(pl.program_id(0) == 0)
        def _():
            carry_k[...] = jnp.zeros_like(carry_k)
            carry_v[...] = jnp.zeros_like(carry_v)
            if gated:
                dsink_ref[...] = jnp.zeros_like(dsink_ref)

        lim = jnp.where(blk(pl.program_id(0)) % nbc == 0, T, 0)
        ck = jnp.concatenate([cp_ref[...], cc_ref[...]], axis=0)
        sk = jnp.concatenate([sp_ref[...], sc_ref[...]], axis=0)
        dkp = [jnp.zeros((T, HD), f32) for _ in range(hkv)]
        dkc = [jnp.zeros((T, HD), f32) for _ in range(hkv)]
        dvp = [jnp.zeros((T, HD), f32) for _ in range(hkv)]
        dvc = [jnp.zeros((T, HD), f32) for _ in range(hkv)]
        for h in range(HQ):
            g = h // grp
            hs, ks = slice(h * HD, (h + 1) * HD), slice(g * HD, (g + 1) * HD)
            fn = functools.partial(_attn_head, cq=cc_ref[...], sq=sc_ref[...], ck=ck, sk=sk, lim=lim,
                                   max_dist=max_dist, use_rope=True)
            prim = (q_ref[:, hs], kp_ref[:, ks], kc_ref[:, ks], vp_ref[:, ks], vc_ref[:, ks])
            if gated:
                _, vjp = jax.vjp(fn, *prim, sink_ref[:, hs])
                dq, a, b, c, d, ds = vjp(d_refs[0][:, hs].astype(f32))
                dsink_ref[:, hs] += ds
            else:
                _, vjp = jax.vjp(lambda *p: fn(*p, None), *prim)
                dq, a, b, c, d = vjp((d_refs[0][:, hs], d_refs[1][:, hs]))
            dq_ref[:, hs] = dq.astype(bf16)
            dkp[g], dkc[g], dvp[g], dvc[g] = dkp[g] + a, dkc[g] + b, dvp[g] + c, dvc[g] + d
        for g in range(hkv):
            ks = slice(g * HD, (g + 1) * HD)
            dk_ref[:, ks] = (dkc[g] + carry_k[:, ks]).astype(bf16)
            dv_ref[:, ks] = (dvc[g] + carry_v[:, ks]).astype(bf16)
            carry_k[:, ks] = dkp[g]
            carry_v[:, ks] = dvp[g]

    row = lambda w: pl.BlockSpec((T, w), lambda i: (blk(i), 0))
    vec = pl.BlockSpec((1, 512), lambda i: (0, 0))
    in_specs = specs + ([vec] if gated else []) + [row(512)] * nd
    out_shape = [jax.ShapeDtypeStruct((L, 512), bf16), jax.ShapeDtypeStruct((L, wk), bf16), jax.ShapeDtypeStruct((L, wk), bf16)]
    out_specs = [row(512), row(wk), row(wk)]
    if gated:
        out_shape.append(jax.ShapeDtypeStruct((1, 512), f32))
        out_specs.append(vec)
    args = (qa, ka, ka, va, va, cos, cos, sin, sin) + ((sinks,) if gated else ()) + tuple(douts)
    return pl.pallas_call(
        body, name=name, grid=(n,), out_shape=out_shape, in_specs=in_specs, out_specs=out_specs,
        scratch_shapes=[pltpu.VMEM((T, wk), f32), pltpu.VMEM((T, wk), f32)],
        compiler_params=_params(("arbitrary",)),
    )(*args)


def _dilmix_f(o0, o1, o2, l0, l1, l2):
    m = jnp.maximum(jnp.maximum(l0, l1), l2)
    e0, e1, e2 = jnp.exp(l0 - m), jnp.exp(l1 - m), jnp.exp(l2 - m)
    return (e0 * o0 + e1 * o1 + e2 * o2) / (e0 + e1 + e2)


def _dilmix_fwd(os_, ls, name):
    L = os_[0].shape[0]
    tm = _tile(L, 512, 8)

    def body(o0, o1, o2, l0, l1, l2, out):
        out[...] = _dilmix_f(o0[...], o1[...], o2[...], l0[...], l1[...], l2[...]).astype(bf16)

    row = pl.BlockSpec((tm, 512), lambda i: (i, 0))
    return pl.pallas_call(
        body, name=name, grid=(L // tm,), out_shape=jax.ShapeDtypeStruct((L, 512), bf16),
        in_specs=[row] * 6, out_specs=row, compiler_params=_params(("parallel",)),
    )(*os_, *ls)


def _dilmix_bwd(os_, ls, dout, name):
    L = os_[0].shape[0]
    tm = _tile(L, 512, 8)

    def body(o0, o1, o2, l0, l1, l2, d, *outs):
        _, vjp = jax.vjp(_dilmix_f, o0[...], o1[...], o2[...], l0[...], l1[...], l2[...])
        for r, val in zip(outs, vjp(d[...].astype(f32))):
            r[...] = val

    row = pl.BlockSpec((tm, 512), lambda i: (i, 0))
    outs = pl.pallas_call(
        body, name=name, grid=(L // tm,), out_shape=[jax.ShapeDtypeStruct((L, 512), f32)] * 6,
        in_specs=[row] * 7, out_specs=[row] * 6, compiler_params=_params(("parallel",)),
    )(*os_, *ls, dout)
    return outs[:3], outs[3:]


def _to_strided(z, dil):
    L, w = z.shape
    return z.reshape(L // dil, dil, w).transpose(1, 0, 2).reshape(L, w)


def _from_strided(z, dil):
    L, w = z.shape
    return z.reshape(dil, L // dil, w).transpose(1, 0, 2).reshape(L, w)


def _adamw_math(w, g, m, v):
    m = ADAM_B1 * m + (1.0 - ADAM_B1) * g
    v = ADAM_B2 * v + (1.0 - ADAM_B2) * (g * g)
    m_hat = m / (1.0 - ADAM_B1 ** ADAM_STEP)
    v_hat = v / (1.0 - ADAM_B2 ** ADAM_STEP)
    delta = -ADAM_LR * (m_hat / (jnp.sqrt(v_hat) + ADAM_EPS) + ADAM_WD * w)
    return delta, m, v


def _adamw(w, m, v, slots, name):
    _, R, C = w.shape
    tr = _tile(R, max(8, 131072 // C), 8)

    def body(w_ref, m_ref, v_ref, s_ref, g_ref, d_ref, nm_ref, nv_ref):
        g = s_ref[0, 0].astype(f32)
        for i in range(1, N_DEV):
            g = g + s_ref[i, 0].astype(f32)
        delta, nm, nv = _adamw_math(w_ref[0], g, m_ref[0], v_ref[0])
        g_ref[0], d_ref[0], nm_ref[0], nv_ref[0] = g, delta, nm, nv

    blk = pl.BlockSpec((1, tr, C), lambda l, i: (l, i, 0))
    return pl.pallas_call(
        body, name=name, grid=(2, R // tr), out_shape=[jax.ShapeDtypeStruct(w.shape, f32)] * 4,
        in_specs=[blk, blk, blk, pl.BlockSpec((N_DEV, 1, tr, C), lambda l, i: (0, l, i, 0))], out_specs=[blk] * 4,
        compiler_params=_params(("parallel", "parallel")),
    )(w, m, v, slots)


def _adamw_packed(w, m, v, slots, name):
    R = w.shape[0]
    tr = _tile(R, 512, 8)

    def body(w_ref, m_ref, v_ref, s_ref, g_ref, d_ref, nm_ref, nv_ref):
        g = s_ref[0]
        for i in range(1, N_DEV):
            g = g + s_ref[i]
        delta, nm, nv = _adamw_math(w_ref[...], g, m_ref[...], v_ref[...])
        g_ref[...], d_ref[...], nm_ref[...], nv_ref[...] = g, delta, nm, nv

    blk = pl.BlockSpec((tr, 128), lambda i: (i, 0))
    return pl.pallas_call(
        body, name=name, grid=(R // tr,), out_shape=[jax.ShapeDtypeStruct(w.shape, f32)] * 4,
        in_specs=[blk, blk, blk, pl.BlockSpec((N_DEV, tr, 128), lambda i: (0, i, 0))], out_specs=[blk] * 4,
        compiler_params=_params(("parallel",)),
    )(w, m, v, slots)


def _cols_gathered(g):
    nd = g.ndim
    perm = tuple(range(1, nd - 1)) + (0, nd - 1)
    t = g.transpose(perm)
    return t.reshape(t.shape[:-2] + (t.shape[-2] * t.shape[-1],))


def _cols_scatter(full):
    s = full.shape
    t = full.reshape(s[:-1] + (N_DEV, s[-1] // N_DEV))
    nd = t.ndim
    return t.transpose((nd - 2,) + tuple(range(nd - 2)) + (nd - 1,))


def _rows_gathered(g):
    t = g.transpose(1, 0, 2, 3)
    return t.reshape(t.shape[0], t.shape[1] * t.shape[2], t.shape[3])


def _rows_scatter(full):
    s = full.shape
    return full.reshape(s[0], N_DEV, s[1] // N_DEV, s[2]).transpose(1, 0, 2, 3)


def _win_reorder(w):
    pad = jnp.zeros(w.shape[:-1] + (128 - GLA_LOWRANK,), w.dtype)
    return jnp.concatenate([w[..., O_GATES:], w[..., :O_GLR], w[..., O_S5U:O_GATES], w[..., O_GLR:O_S5U], pad], axis=-1)


def _win_restore(wp, D):
    b = 4 * D
    return jnp.concatenate([wp[..., b:b + O_GLR], wp[..., b + P_GLR:b + P_GLR + GLA_LOWRANK],
                            wp[..., b + O_GLR:b + P_GLR], wp[..., :b]], axis=-1)


SMALL = ("norm1_g", "gla_a_b", "gla_norm_g", "s5_lambda_re", "s5_lambda_im", "s5_log_dt", "s5_b_re", "s5_b_im",
         "s5_c_re", "s5_c_im", "s5_d", "s5_glu_b", "swa_sinks", "norm2_g", "final_norm_g")
SHARDED = ("w_in", "gla_a2", "s5_glu_w", "w_branch", "w_out", "w_ffn_gate", "w_ffn_up", "w_ffn_down")
WEIGHTS = ("norm1_g", "w_in", "gla_a2", "gla_a_b", "gla_norm_g", "s5_lambda_re", "s5_lambda_im", "s5_log_dt", "s5_b_re",
           "s5_b_im", "s5_c_re", "s5_c_im", "s5_d", "s5_glu_w", "s5_glu_b", "swa_sinks", "w_branch", "w_out", "norm2_g",
           "w_ffn_gate", "w_ffn_up", "w_ffn_down", "final_norm_g")


def _pack(arrs):
    flat = jnp.concatenate([a.reshape(-1) for a in arrs])
    n = flat.shape[0]
    rows = -(-n // 1024) * 8
    return jnp.pad(flat, (0, rows * 128 - n)).reshape(rows, 128)


def _unpack(packed, like):
    flat = packed.reshape(-1)
    out, pos = [], 0
    for a in like:
        out.append(flat[pos:pos + a.size].reshape(a.shape))
        pos += a.size
    return out


def kernel(x, positions, norm1_g, w_in, gla_a2, gla_a_b, gla_norm_g, s5_lambda_re, s5_lambda_im, s5_log_dt, s5_b_re, s5_b_im, s5_c_re, s5_c_im, s5_d, s5_glu_w, s5_glu_b, swa_sinks, w_branch, w_out, norm2_g, w_ffn_gate, w_ffn_up, w_ffn_down, final_norm_g, loss_target, m_norm1_g, m_w_in, m_gla_a2, m_gla_a_b, m_gla_norm_g, m_s5_lambda_re, m_s5_lambda_im, m_s5_log_dt, m_s5_b_re, m_s5_b_im, m_s5_c_re, m_s5_c_im, m_s5_d, m_s5_glu_w, m_s5_glu_b, m_swa_sinks, m_w_branch, m_w_out, m_norm2_g, m_w_ffn_gate, m_w_ffn_up, m_w_ffn_down, m_final_norm_g, v_norm1_g, v_w_in, v_gla_a2, v_gla_a_b, v_gla_norm_g, v_s5_lambda_re, v_s5_lambda_im, v_s5_log_dt, v_s5_b_re, v_s5_b_im, v_s5_c_re, v_s5_c_im, v_s5_d, v_s5_glu_w, v_s5_glu_b, v_swa_sinks, v_w_branch, v_w_out, v_norm2_g, v_w_ffn_gate, v_w_ffn_up, v_w_ffn_down, v_final_norm_g):
    W = dict(norm1_g=norm1_g, w_in=w_in, gla_a2=gla_a2, gla_a_b=gla_a_b, gla_norm_g=gla_norm_g, s5_lambda_re=s5_lambda_re, s5_lambda_im=s5_lambda_im, s5_log_dt=s5_log_dt, s5_b_re=s5_b_re, s5_b_im=s5_b_im, s5_c_re=s5_c_re, s5_c_im=s5_c_im, s5_d=s5_d, s5_glu_w=s5_glu_w, s5_glu_b=s5_glu_b, swa_sinks=swa_sinks, w_branch=w_branch, w_out=w_out, norm2_g=norm2_g, w_ffn_gate=w_ffn_gate, w_ffn_up=w_ffn_up, w_ffn_down=w_ffn_down, final_norm_g=final_norm_g)
    Mo = dict(norm1_g=m_norm1_g, w_in=m_w_in, gla_a2=m_gla_a2, gla_a_b=m_gla_a_b, gla_norm_g=m_gla_norm_g, s5_lambda_re=m_s5_lambda_re, s5_lambda_im=m_s5_lambda_im, s5_log_dt=m_s5_log_dt, s5_b_re=m_s5_b_re, s5_b_im=m_s5_b_im, s5_c_re=m_s5_c_re, s5_c_im=m_s5_c_im, s5_d=m_s5_d, s5_glu_w=m_s5_glu_w, s5_glu_b=m_s5_glu_b, swa_sinks=m_swa_sinks, w_branch=m_w_branch, w_out=m_w_out, norm2_g=m_norm2_g, w_ffn_gate=m_w_ffn_gate, w_ffn_up=m_w_ffn_up, w_ffn_down=m_w_ffn_down, final_norm_g=m_final_norm_g)
    Vo = dict(norm1_g=v_norm1_g, w_in=v_w_in, gla_a2=v_gla_a2, gla_a_b=v_gla_a_b, gla_norm_g=v_gla_norm_g, s5_lambda_re=v_s5_lambda_re, s5_lambda_im=v_s5_lambda_im, s5_log_dt=v_s5_log_dt, s5_b_re=v_s5_b_re, s5_b_im=v_s5_b_im, s5_c_re=v_s5_c_re, s5_c_im=v_s5_c_im, s5_d=v_s5_d, s5_glu_w=v_s5_glu_w, s5_glu_b=v_s5_glu_b, swa_sinks=v_swa_sinks, w_branch=v_w_branch, w_out=v_w_out, norm2_g=v_norm2_g, w_ffn_gate=v_w_ffn_gate, w_ffn_up=v_w_ffn_up, w_ffn_down=v_w_ffn_down, final_norm_g=v_final_norm_g)

    L, D = x.shape[1], x.shape[2]
    depth = norm1_g.shape[0]
    xs = x.reshape(L, D)
    target = loss_target.reshape(L, D)
    base128 = 4 * D // 128

    big = ("w_in", "w_branch", "w_out", "w_ffn_gate", "w_ffn_up", "w_ffn_down")
    gathered = _all_gather([W[k].astype(bf16) for k in big] + [gla_a2, s5_glu_w], "gather_weights")
    G = dict(zip(big + ("gla_a2", "s5_glu_w"), gathered))
    win_p = _win_reorder(_cols_gathered(G["w_in"]))
    wb = _cols_gathered(G["w_branch"])
    wout = _rows_gathered(G["w_out"])
    wg, wu = _cols_gathered(G["w_ffn_gate"]), _cols_gathered(G["w_ffn_up"])
    wd = _rows_gathered(G["w_ffn_down"])
    a2_full = _cols_gathered(G["gla_a2"])
    a2p = jnp.pad(a2_full, ((0, 0), (0, 128 - GLA_LOWRANK), (0, 0)))
    glu_w = _rows_gathered(G["s5_glu_w"])

    cos, sin = _rope_tables(positions.reshape(L))
    strided_tabs = {dil: (_to_strided(cos, dil), _to_strided(sin, dil)) for _, dil in DIL_CONFIGS if dil > 1}

    saved = []
    cur = xs
    for l in range(depth):
        s = {"x": cur}
        h1 = _rms_fwd(cur, norm1_g[l][None], f"rms1_fwd{l}")
        proj = _matmul(h1, win_p[l], name=f"proj_in{l}")
        s["h1"], s["proj"] = h1, proj
        ab, ng = gla_a_b[l][None], gla_norm_g[l].reshape(1, 512)
        o_gla, s["gla_st"] = _gla_fwd(proj, a2p[l], ab, ng, base128, f"gla_fwd{l}")
        prep, s["prep_vjp"] = jax.vjp(_s5_prep, s5_lambda_re[l], s5_lambda_im[l], s5_log_dt[l], s5_b_re[l], s5_b_im[l],
                                      s5_c_re[l], s5_c_im[l], s5_d[l])
        s["prep"] = prep
        y_s5, s["s5_r"], s["s5_i"] = _s5_fwd(proj, prep, base128, f"s5_fwd{l}")
        s["y_s5"] = y_s5
        o_s5 = _glu_fwd(y_s5, glu_w[l], s5_glu_b[l][None], f"glu_fwd{l}")
        sinks_b = jnp.repeat(swa_sinks[l], HEAD_DIM)[None]
        s["sinks_b"] = sinks_b
        nb = L // ATT_BLOCK
        o_swa = _attn_fwd(proj, proj, proj, cos, sin, sinks_b, q_col=(base128 + P_SQ // 128) // 4,
                          k_col=base128 + P_SK // 128, v_col=base128 + P_SV // 128, hkv=SWA_KV_HEADS, nbc=nb,
                          max_dist=SWA_WINDOW - 1, name=f"swa_fwd{l}")
        cq, ck, cv = (base128 + P_CQ // 128) // 4, (base128 + P_CK // 128) // 4, (base128 + P_CV // 128) // 4
        dil_o, dil_l, s["dil_in"] = [], [], []
        for window, dil in DIL_CONFIGS:
            if dil == 1:
                o, lse = _attn_fwd(proj, proj, proj, cos, sin, None, q_col=cq, k_col=ck, v_col=cv, hkv=8, nbc=nb,
                                   max_dist=window // dil, name=f"dil{dil}_fwd{l}")
                s["dil_in"].append(None)
            else:
                qs_, ks_, vs_ = (_to_strided(proj[:, 4 * D + off:4 * D + off + 512], dil) for off in (P_CQ, P_CK, P_CV))
                o, lse = _attn_fwd(qs_, ks_, vs_, *strided_tabs[dil], None, q_col=0, k_col=0, v_col=0, hkv=8,
                                   nbc=nb // dil, max_dist=window // dil, name=f"dil{dil}_fwd{l}")
                o, lse = _from_strided(o, dil), _from_strided(lse, dil)
                s["dil_in"].append((qs_, ks_, vs_))
            dil_o.append(o)
            dil_l.append(lse)
        s["dil_o"], s["dil_l"] = dil_o, dil_l
        o_dil = _dilmix_fwd(dil_o, dil_l, f"dilmix_fwd{l}")
        branches = (o_gla, o_s5, o_dil, o_swa)
        s["branches"] = branches
        ys = [_matmul(br, wb[l, m], name=f"branch{m}_fwd{l}") for m, br in enumerate(branches)]
        s["ys"] = ys
        mixed = _merge_fwd(proj, ys, D, f"merge_fwd{l}")
        s["mixed"] = mixed
        x2 = _matmul(mixed, wout[l], res=cur, name=f"out_fwd{l}")
        s["x2"] = x2
        h2 = _rms_fwd(x2, norm2_g[l][None], f"rms2_fwd{l}")
        a = _matmul(h2, wg[l], name=f"ffn_gate_fwd{l}")
        b = _matmul(h2, wu[l], name=f"ffn_up_fwd{l}")
        act = _swiglu_fwd(a, b, f"swiglu_fwd{l}")
        s["h2"], s["a"], s["b"], s["act"] = h2, a, b, act
        cur = _matmul(act, wd[l], res=x2, name=f"ffn_down_fwd{l}")
        saved.append(s)

    loss_part, dcur, dcur_b, dgf = _final_loss(cur, final_norm_g[None], target, "final_loss")
    loss = lax.psum(loss_part, AXES)

    small_g = {k: [None] * depth for k in SMALL if k != "final_norm_g"}
    big_g = {k: [None] * depth for k in SHARDED}
    for l in reversed(range(depth)):
        s = saved[l]
        proj = s["proj"]
        dact = _matmul(dcur_b, wd[l], mode="nt", name=f"ffn_down_dx{l}")
        big_g["w_ffn_down"][l] = _matmul(s["act"], dcur_b, mode="tn", out_dtype=bf16, name=f"ffn_down_dw{l}")
        da, db = _swiglu_bwd(s["a"], s["b"], dact, f"swiglu_bwd{l}")
        dh2 = _matmul(da, wg[l], mode="nt", name=f"ffn_gate_dx{l}")
        dh2 = _matmul(db, wu[l], mode="nt", res=dh2, name=f"ffn_up_dx{l}")
        big_g["w_ffn_gate"][l] = _matmul(s["h2"], da, mode="tn", out_dtype=bf16, name=f"ffn_gate_dw{l}")
        big_g["w_ffn_up"][l] = _matmul(s["h2"], db, mode="tn", out_dtype=bf16, name=f"ffn_up_dw{l}")
        dx2, dx2_b, dg2 = _rms_bwd(s["x2"], norm2_g[l][None], dh2, dcur, f"rms2_bwd{l}")
        small_g["norm2_g"][l] = dg2[0]
        dmixed = _matmul(dx2_b, wout[l], mode="nt", name=f"out_dx{l}")
        big_g["w_out"][l] = _matmul(s["mixed"], dx2_b, mode="tn", out_dtype=bf16, name=f"out_dw{l}")
        dys, dgates = _merge_bwd(proj, s["ys"], dmixed, D, f"merge_bwd{l}")
        dbr = [_matmul(dys[m], wb[l, m], mode="nt", name=f"branch{m}_dx{l}") for m in range(4)]
        big_g["w_branch"][l] = jnp.stack([_matmul(s["branches"][m], dys[m], mode="tn", out_dtype=bf16, name=f"branch{m}_dw{l}")
                                          for m in range(4)])
        d_gla, d_s5, d_dil, d_swa = dbr
        ab, ng = gla_a_b[l][None], gla_norm_g[l].reshape(1, 512)
        dgq, dgk, dgv, dgr, dglr, da2, dab, dng = _gla_bwd(proj, a2p[l], ab, ng, s["gla_st"], d_gla, base128, f"gla_bwd{l}")
        big_g["gla_a2"][l] = da2[:GLA_LOWRANK]
        small_g["gla_a_b"][l] = dab[0]
        small_g["gla_norm_g"][l] = dng.reshape(GLA_HEADS, GLA_DV)
        dy_s5, dglu_w, dglu_b = _glu_bwd(s["y_s5"], glu_w[l], s5_glu_b[l][None], d_s5, f"glu_bwd{l}")
        big_g["s5_glu_w"][l] = dglu_w
        small_g["s5_glu_b"][l] = dglu_b[0]
        ds5u, dprep = _s5_bwd(proj, s["prep"], s["s5_r"], s["s5_i"], dy_s5, base128, f"s5_bwd{l}")
        draw = s["prep_vjp"](dprep)
        for k, val in zip(("s5_lambda_re", "s5_lambda_im", "s5_log_dt", "s5_b_re", "s5_b_im", "s5_c_re", "s5_c_im", "s5_d"), draw):
            small_g[k][l] = val
        nb = L // ATT_BLOCK
        dsq, dsk, dsv, dsinks = _attn_bwd(proj, proj, proj, cos, sin, s["sinks_b"], (d_swa,),
                                          q_col=(base128 + P_SQ // 128) // 4, k_col=base128 + P_SK // 128,
                                          v_col=base128 + P_SV // 128, hkv=SWA_KV_HEADS, nbc=nb, max_dist=SWA_WINDOW - 1,
                                          name=f"swa_bwd{l}")
        small_g["swa_sinks"][l] = dsinks.reshape(SWA_HEADS, HEAD_DIM).sum(axis=1)
        dos, dls = _dilmix_bwd(s["dil_o"], s["dil_l"], d_dil, f"dilmix_bwd{l}")
        cq, ck, cv = (base128 + P_CQ // 128) // 4, (base128 + P_CK // 128) // 4, (base128 + P_CV // 128) // 4
        dcq = dck = dcv = None
        for i, (window, dil) in enumerate(DIL_CONFIGS):
            if dil == 1:
                g3 = _attn_bwd(proj, proj, proj, cos, sin, None, (dos[i], dls[i]), q_col=cq, k_col=ck, v_col=cv, hkv=8,
                               nbc=nb, max_dist=window // dil, name=f"dil{dil}_bwd{l}")
            else:
                qs_, ks_, vs_ = s["dil_in"][i]
                g3 = _attn_bwd(qs_, ks_, vs_, *strided_tabs[dil], None, (_to_strided(dos[i], dil), _to_strided(dls[i], dil)),
                               q_col=0, k_col=0, v_col=0, hkv=8, nbc=nb // dil, max_dist=window // dil, name=f"dil{dil}_bwd{l}")
                g3 = [_from_strided(t, dil) for t in g3]
            g3 = [t.astype(f32) for t in g3]
            dcq, dck, dcv = (g3[0], g3[1], g3[2]) if dcq is None else (dcq + g3[0], dck + g3[1], dcv + g3[2])
        dproj = jnp.concatenate([dgates.transpose(1, 0, 2).reshape(L, 4 * D), dgq, dgk, dgv, dgr, ds5u,
                                 dcq.astype(bf16), dck.astype(bf16), dcv.astype(bf16), dsq, dsk, dsv, dglr], axis=1)
        dh1 = _matmul(dproj, win_p[l], mode="nt", name=f"proj_in_dx{l}")
        big_g["w_in"][l] = _matmul(s["h1"], dproj, mode="tn", out_dtype=bf16, name=f"proj_in_dw{l}")
        dcur, dcur_b, dg1 = _rms_bwd(s["x"], norm1_g[l][None], dh1, dx2, f"rms1_bwd{l}")
        small_g["norm1_g"][l] = dg1[0]
    grad_x = dcur.reshape(x.shape)

    sends = {
        "w_in": _cols_scatter(_win_restore(jnp.stack(big_g["w_in"]), D)),
        "gla_a2": _cols_scatter(jnp.stack(big_g["gla_a2"])),
        "s5_glu_w": _rows_scatter(jnp.stack(big_g["s5_glu_w"])),
        "w_branch": _cols_scatter(jnp.stack(big_g["w_branch"])),
        "w_out": _rows_scatter(jnp.stack(big_g["w_out"])),
        "w_ffn_gate": _cols_scatter(jnp.stack(big_g["w_ffn_gate"])),
        "w_ffn_up": _cols_scatter(jnp.stack(big_g["w_ffn_up"])),
        "w_ffn_down": _rows_scatter(jnp.stack(big_g["w_ffn_down"])),
    }
    recvs = dict(zip(SHARDED, _all_to_all([sends[k] for k in SHARDED], "exchange_grads")))
    out = {}
    for k in SHARDED:
        shp = W[k].shape
        as3 = lambda t: t.reshape((shp[0], -1, shp[-1]))
        slots = recvs[k].reshape((N_DEV, shp[0], -1, shp[-1]))
        res = _adamw(as3(W[k]), as3(Mo[k]), as3(Vo[k]), slots, f"adamw_{k}")
        out[k] = [t.reshape(shp) for t in res]

    small_list = [jnp.stack(small_g[k]) if k != "final_norm_g" else dgf[0] for k in SMALL]
    small_list = [t.reshape(W[k].shape) for t, k in zip(small_list, SMALL)]
    packed_parts = _all_gather([_pack(small_list)], "gather_small_grads")[0]
    res = _adamw_packed(_pack([W[k] for k in SMALL]), _pack([Mo[k] for k in SMALL]), _pack([Vo[k] for k in SMALL]),
                        packed_parts, "adamw_small")
    unpacked = [_unpack(t, [W[k] for k in SMALL]) for t in res]
    for i, k in enumerate(SMALL):
        out[k] = [unpacked[j][i] for j in range(4)]

    return (loss, grad_x, *[out[k][0] for k in WEIGHTS], *[out[k][1] for k in WEIGHTS],
            *[out[k][2] for k in WEIGHTS], *[out[k][3] for k in WEIGHTS])
```

```python
import functools
import math

import jax
import jax.numpy as jnp
from jax import lax
from jax.experimental import pallas as pl
from jax.experimental.pallas import tpu as pltpu

f32 = jnp.float32
bf16 = jnp.bfloat16
HI = lax.Precision.HIGHEST

N_DEV = 8
AXES = ("x", "y", "c")
NORM_EPS = 1e-6
ROPE_THETA = 500000.0
HEAD_DIM = 64
ROPE_DIM = 16
ATT_BLOCK = 128
BRANCH_WIDTH = 512
GLA_HEADS, GLA_DK, GLA_DV, GLA_LOWRANK, GLA_TAU, GLA_CHUNK, GLA_SUB = 4, 64, 128, 16, 16.0, 64, 16
S5_GROUPS, S5_GROUP, S5_STATE = 32, 16, 64
S5_CHUNK = 128
S5_LANE_BLOCKS = 4
DIL_CONFIGS = ((128, 1), (512, 4), (2048, 16))
SWA_HEADS, SWA_KV_HEADS, SWA_WINDOW = 8, 2, 128
ADAM_LR, ADAM_B1, ADAM_B2, ADAM_EPS, ADAM_WD, ADAM_STEP = 0.001, 0.9, 0.999, 1e-08, 0.01, 10
O_GLR, O_S5U, O_GATES = 1536, 1552, 4368
MIX_COLS = 4480
P_GQ, P_GK, P_GV, P_GR, P_S5U, P_CQ, P_CK, P_CV, P_SQ, P_SK, P_SV, P_GLR = (
    0, 256, 512, 1024, 1536, 2048, 2560, 3072, 3584, 4096, 4224, 4352)
VMEM_LIMIT = 56 * 1024 * 1024


def _tile(n, cap, q=128):
    if n <= cap:
        return n
    t = (cap // q) * q
    while t >= q:
        if n % t == 0:
            return t
        t -= q
    return n


def _params(sem=None):
    return pltpu.CompilerParams(dimension_semantics=sem, vmem_limit_bytes=VMEM_LIMIT)


@functools.partial(jax.custom_vjp, nondiff_argnums=(1,))
def _sroll(x, d):
    return pltpu.roll(x, d, 0)


def _sroll_fwd(x, d):
    return pltpu.roll(x, d, 0), None


def _sroll_bwd(d, _, g):
    n = g.shape[0]
    return (pltpu.roll(g, (n - d) % n, 0),)


_sroll.defvjp(_sroll_fwd, _sroll_bwd)


def _mesh_pos():
    return lax.axis_index("x"), lax.axis_index("y"), lax.axis_index("c")


def _all_gather(shards, name):
    n = len(shards)
    any_spec = pl.BlockSpec(memory_space=pl.ANY)

    def body(*refs):
        ins, outs = refs[:n], refs[n:2 * n]
        send_sems, recv_sems, local_sems = refs[2 * n:]
        x, y, c = _mesh_pos()
        me, sibling = (x, y, c), (x, y, 1 - c)
        chips = [(1 - x, y), (x, 1 - y), (1 - x, 1 - y)]

        def copy(a, k, block, to, src=None):
            slot = outs[a].at[4 * block[0] + 2 * block[1] + block[2]]
            return pltpu.make_async_remote_copy(
                src_ref=slot if src is None else src, dst_ref=slot,
                send_sem=send_sems.at[a, k], recv_sem=recv_sems.at[a, k],
                device_id=to, device_id_type=pl.DeviceIdType.MESH)

        started = []
        for a in range(n):
            mine = pltpu.make_async_copy(ins[a], outs[a].at[4 * x + 2 * y + c], local_sems.at[a])
            mine.start()
            first = [copy(a, 0, me, sibling, src=ins[a])]
            first += [copy(a, 1 + j, me, (*chip, c), src=ins[a]) for j, chip in enumerate(chips)]
            for cp in first:
                cp.start()
            started.append((mine, first))
        for a in range(n):
            mine, first = started[a]
            passed = [copy(a, 4 + j, (*chip, c), sibling) for j, chip in enumerate(chips)]
            for j, chip in enumerate(chips):
                copy(a, 1 + j, (*chip, c), me).wait_recv()
                passed[j].start()
            copy(a, 0, sibling, me).wait_recv()
            for j, chip in enumerate(chips):
                copy(a, 4 + j, (*chip, 1 - c), me).wait_recv()
            for cp in first + passed:
                cp.wait_send()
            mine.wait()

    outs = pl.pallas_call(
        body, name=name,
        out_shape=[jax.ShapeDtypeStruct((N_DEV,) + s.shape, s.dtype) for s in shards],
        in_specs=[any_spec] * n, out_specs=[any_spec] * n,
        scratch_shapes=[pltpu.SemaphoreType.DMA((n, 7)), pltpu.SemaphoreType.DMA((n, 7)),
                        pltpu.SemaphoreType.DMA((n,))],
    )(*shards)
    return list(outs)


def _a2a_copies(ins, outs, send_sems, recv_sems, local_sems):
    x, y, c = _mesh_pos()
    me = 4 * x + 2 * y + c
    copies = []
    for a in range(len(ins)):
        copies.append(pltpu.make_async_copy(ins[a].at[me], outs[a].at[me], local_sems.at[a]))
        for k in range(1, N_DEV):
            px = 1 - x if k & 4 else x
            py = 1 - y if k & 2 else y
            pc = 1 - c if k & 1 else c
            copies.append(pltpu.make_async_remote_copy(
                src_ref=ins[a].at[4 * px + 2 * py + pc], dst_ref=outs[a].at[me],
                send_sem=send_sems.at[a, k - 1], recv_sem=recv_sems.at[a, k - 1],
                device_id=(px, py, pc), device_id_type=pl.DeviceIdType.MESH))
    return copies


def _host_call(body, sends, args, *, name, grid, out_shape, in_specs, out_specs, scratch_shapes, compiler_params):
    single = not isinstance(out_shape, (list, tuple))
    out_shape = [out_shape] if single else list(out_shape)
    out_specs = [out_specs] if single else list(out_specs)
    sends = list(sends or ())
    n, n_in, n_out, n_scr = len(sends), len(args), len(out_shape), len(scratch_shapes)
    if n == 0:
        outs = pl.pallas_call(body, name=name, grid=grid, out_shape=out_shape, in_specs=in_specs, out_specs=out_specs,
                              scratch_shapes=list(scratch_shapes), compiler_params=compiler_params)(*args)
        return (outs[0] if single else list(outs)), []
    any_spec = pl.BlockSpec(memory_space=pl.ANY)

    def hosted(*refs):
        ins, s_in = refs[:n_in], refs[n_in:n_in + n]
        pos = n_in + n
        outs, s_out = refs[pos:pos + n_out], refs[pos + n_out:pos + n_out + n]
        pos += n_out + n
        scr, sems = refs[pos:pos + n_scr], refs[pos + n_scr:]
        ids = [pl.program_id(i) for i in range(len(grid))]
        first = functools.reduce(lambda p, q: p & q, [i == 0 for i in ids])
        last = functools.reduce(lambda p, q: p & q, [i == g - 1 for i, g in zip(ids, grid)])

        @pl.when(first)
        def _():
            for cp in _a2a_copies(s_in, s_out, *sems):
                cp.start()

        body(*ins, *outs, *scr)

        @pl.when(last)
        def _():
            for cp in _a2a_copies(s_in, s_out, *sems):
                cp.wait()

    outs = pl.pallas_call(
        hosted, name=name, grid=grid,
        out_shape=out_shape + [jax.ShapeDtypeStruct(s.shape, s.dtype) for s in sends],
        in_specs=list(in_specs) + [any_spec] * n, out_specs=out_specs + [any_spec] * n,
        scratch_shapes=list(scratch_shapes) + [pltpu.SemaphoreType.DMA((n, 7)), pltpu.SemaphoreType.DMA((n, 7)),
                                               pltpu.SemaphoreType.DMA((n,))],
        compiler_params=compiler_params,
    )(*args, *sends)
    main = list(outs[:n_out])
    return (main[0] if single else main), list(outs[n_out:])


def _matmul(a, b, *, mode="nn", out_dtype=f32, res=None, sends=None, name):
    if mode == "tn":
        K, M = a.shape
    else:
        M, K = a.shape
    N = b.shape[0] if mode == "nt" else b.shape[1]
    k_cap = 2048 if (a.dtype == bf16 and b.dtype == bf16) else 1024
    tm, tn, tk = _tile(M, 1024), _tile(N, 1152), _tile(K, k_cap)
    nk = K // tk
    dims = {"nn": (((1,), (0,)), ((), ())), "nt": (((1,), (1,)), ((), ())), "tn": (((0,), (0,)), ((), ()))}[mode]

    def body(*refs):
        a_ref, b_ref = refs[:2]
        r_ref = refs[2] if res is not None else None
        o_ref = refs[3] if res is not None else refs[2]
        acc = refs[-1] if nk > 1 else None
        k = pl.program_id(2)
        part = lax.dot_general(a_ref[...].astype(bf16), b_ref[...].astype(bf16), dims, preferred_element_type=f32)

        def finish(r):
            if res is not None:
                r = r + r_ref[...]
            o_ref[...] = r.astype(o_ref.dtype)

        if nk == 1:
            finish(part)
            return

        @pl.when(k == 0)
        def _():
            acc[...] = part

        @pl.when((k > 0) & (k < nk - 1))
        def _():
            acc[...] += part

        @pl.when(k == nk - 1)
        def _():
            finish(acc[...] + part)

    a_spec = pl.BlockSpec((tk, tm), lambda i, j, k: (k, i)) if mode == "tn" else pl.BlockSpec((tm, tk), lambda i, j, k: (i, k))
    b_spec = pl.BlockSpec((tn, tk), lambda i, j, k: (j, k)) if mode == "nt" else pl.BlockSpec((tk, tn), lambda i, j, k: (k, j))
    o_spec = pl.BlockSpec((tm, tn), lambda i, j, k: (i, j))
    in_specs = [a_spec, b_spec] + ([o_spec] if res is not None else [])
    args = (a, b) + ((res,) if res is not None else ())
    out, recvs = _host_call(
        body, sends, args, name=name, grid=(M // tm, N // tn, nk),
        out_shape=jax.ShapeDtypeStruct((M, N), out_dtype),
        in_specs=in_specs, out_specs=o_spec,
        scratch_shapes=[pltpu.VMEM((tm, tn), f32)] if nk > 1 else [],
        compiler_params=_params(("parallel", "parallel", "arbitrary")))
    return out if sends is None else (out, recvs)


def _rms(x, g):
    return x * lax.rsqrt(jnp.mean(x * x, axis=-1, keepdims=True) + NORM_EPS) * g


def _rms_fwd(x, g, name):
    L, D = x.shape
    tm = _tile(L, 256, 8)

    def body(x_ref, g_ref, o_ref):
        o_ref[...] = _rms(x_ref[...], g_ref[...]).astype(bf16)

    return pl.pallas_call(
        body, name=name, grid=(L // tm,), out_shape=jax.ShapeDtypeStruct((L, D), bf16),
        in_specs=[pl.BlockSpec((tm, D), lambda i: (i, 0)), pl.BlockSpec((1, D), lambda i: (0, 0))],
        out_specs=pl.BlockSpec((tm, D), lambda i: (i, 0)),
        compiler_params=_params(("parallel",)),
    )(x, g)


def _rms_bwd(x, g, dh, dres, name):
    L, D = x.shape
    tm = _tile(L, 256, 8)

    def body(x_ref, g_ref, dh_ref, dres_ref, dx_ref, dxb_ref, dg_ref):
        _, vjp = jax.vjp(_rms, x_ref[...], g_ref[...])
        dx, dg = vjp(dh_ref[...])
        dx = dres_ref[...] + dx
        dx_ref[...] = dx
        dxb_ref[...] = dx.astype(bf16)

        @pl.when(pl.program_id(0) == 0)
        def _():
            dg_ref[...] = jnp.zeros_like(dg_ref)

        dg_ref[...] += dg

    row = pl.BlockSpec((tm, D), lambda i: (i, 0))
    vec = pl.BlockSpec((1, D), lambda i: (0, 0))
    return pl.pallas_call(
        body, name=name, grid=(L // tm,),
        out_shape=[jax.ShapeDtypeStruct((L, D), f32), jax.ShapeDtypeStruct((L, D), bf16), jax.ShapeDtypeStruct((1, D), f32)],
        in_specs=[row, vec, row, row], out_specs=[row, row, vec],
        compiler_params=_params(("arbitrary",)),
    )(x, g, dh, dres)


def _final_loss(x, g, target, name):
    L, D = x.shape
    tm = _tile(L, 256, 8)

    def body(x_ref, g_ref, t_ref, loss_ref, dx_ref, dxb_ref, dg_ref):
        tgt = t_ref[...]

        def f(xv, gv):
            err = _rms(xv, gv) - tgt
            return 0.5 * jnp.sum(jnp.mean(err * err, axis=-1, keepdims=True), axis=0, keepdims=True)

        val, vjp = jax.vjp(f, x_ref[...], g_ref[...])
        dx, dg = vjp(jnp.ones((1, 1), f32))
        dx_ref[...] = dx
        dxb_ref[...] = dx.astype(bf16)

        @pl.when(pl.program_id(0) == 0)
        def _():
            dg_ref[...] = jnp.zeros_like(dg_ref)
            loss_ref[...] = jnp.zeros_like(loss_ref)

        dg_ref[...] += dg
        loss_ref[...] += jnp.broadcast_to(val, loss_ref.shape)

    row = pl.BlockSpec((tm, D), lambda i: (i, 0))
    vec = pl.BlockSpec((1, D), lambda i: (0, 0))
    acc = pl.BlockSpec((8, 128), lambda i: (0, 0))
    loss, dx, dxb, dg = pl.pallas_call(
        body, name=name, grid=(L // tm,),
        out_shape=[jax.ShapeDtypeStruct((8, 128), f32), jax.ShapeDtypeStruct((L, D), f32), jax.ShapeDtypeStruct((L, D), bf16),
                   jax.ShapeDtypeStruct((1, D), f32)],
        in_specs=[row, vec, row], out_specs=[acc, row, row, vec],
        compiler_params=_params(("arbitrary",)),
    )(x, g, target)
    return loss[0, 0], dx, dxb, dg


def _swiglu_f(a, b):
    return jax.nn.silu(a) * b


def _swiglu_fwd(a, b, name):
    L, F = a.shape
    tm, tn = _tile(L, 512, 8), _tile(F, 1024)

    def body(a_ref, b_ref, o_ref):
        o_ref[...] = _swiglu_f(a_ref[...], b_ref[...]).astype(bf16)

    blk = pl.BlockSpec((tm, tn), lambda i, j: (i, j))
    return pl.pallas_call(
        body, name=name, grid=(L // tm, F // tn), out_shape=jax.ShapeDtypeStruct((L, F), bf16),
        in_specs=[blk, blk], out_specs=blk, compiler_params=_params(("parallel", "parallel")),
    )(a, b)


def _swiglu_bwd(a, b, dact, name):
    L, F = a.shape
    tm, tn = _tile(L, 512, 8), _tile(F, 1024)

    def body(a_ref, b_ref, d_ref, da_ref, db_ref):
        _, vjp = jax.vjp(_swiglu_f, a_ref[...], b_ref[...])
        da, db = vjp(d_ref[...])
        da_ref[...] = da.astype(bf16)
        db_ref[...] = db.astype(bf16)

    blk = pl.BlockSpec((tm, tn), lambda i, j: (i, j))
    return pl.pallas_call(
        body, name=name, grid=(L // tm, F // tn),
        out_shape=[jax.ShapeDtypeStruct((L, F), bf16)] * 2,
        in_specs=[blk, blk, blk], out_specs=[blk, blk], compiler_params=_params(("parallel", "parallel")),
    )(a, b, dact)


def _merge_f(g0, g1, g2, g3, y0, y1, y2, y3):
    s = jax.nn.sigmoid
    return s(g0) * y0 + s(g1) * y1 + s(g2) * y2 + s(g3) * y3


def _merge_fwd(proj, ys, D, name):
    L = proj.shape[0]
    tm, tn = _tile(L, 512, 8), _tile(D, 512)
    nj = D // tn

    def body(g0, g1, g2, g3, y0, y1, y2, y3, o_ref):
        o_ref[...] = _merge_f(g0[...], g1[...], g2[...], g3[...], y0[...], y1[...], y2[...], y3[...]).astype(bf16)

    gspecs = [pl.BlockSpec((tm, tn), functools.partial(lambda i, j, m: (i, m * nj + j), m=m)) for m in range(4)]
    blk = pl.BlockSpec((tm, tn), lambda i, j: (i, j))
    return pl.pallas_call(
        body, name=name, grid=(L // tm, nj), out_shape=jax.ShapeDtypeStruct((L, D), bf16),
        in_specs=gspecs + [blk] * 4, out_specs=blk, compiler_params=_params(("parallel", "parallel")),
    )(proj, proj, proj, proj, *ys)


def _merge_bwd(proj, ys, dmixed, D, name):
    L = proj.shape[0]
    tm, tn = _tile(L, 512, 8), _tile(D, 512)
    nj = D // tn

    def body(g0, g1, g2, g3, y0, y1, y2, y3, d_ref, dy0, dy1, dy2, dy3, dg_ref):
        _, vjp = jax.vjp(_merge_f, g0[...], g1[...], g2[...], g3[...], y0[...], y1[...], y2[...], y3[...])
        grads = vjp(d_ref[...])
        for m, r in enumerate((dy0, dy1, dy2, dy3)):
            r[...] = grads[4 + m].astype(bf16)
        for m in range(4):
            dg_ref[m] = grads[m].astype(bf16)

    gspecs = [pl.BlockSpec((tm, tn), functools.partial(lambda i, j, m: (i, m * nj + j), m=m)) for m in range(4)]
    blk = pl.BlockSpec((tm, tn), lambda i, j: (i, j))
    dgspec = pl.BlockSpec((4, tm, tn), lambda i, j: (0, i, j))
    outs = pl.pallas_call(
        body, name=name, grid=(L // tm, nj),
        out_shape=[jax.ShapeDtypeStruct((L, D), bf16)] * 4 + [jax.ShapeDtypeStruct((4, L, D), bf16)],
        in_specs=gspecs + [blk] * 5, out_specs=[blk] * 4 + [dgspec],
        compiler_params=_params(("parallel", "parallel")),
    )(proj, proj, proj, proj, *ys, dmixed)
    return outs[:4], outs[4]


def _gla_head(q, k, v, r, glr, st, a2, ab, ng):
    C, T = GLA_CHUNK, GLA_SUB
    row = lax.broadcasted_iota(jnp.int32, (C, C), 0)
    col = lax.broadcasted_iota(jnp.int32, (C, C), 1)
    tri = (col <= row).astype(f32)
    sel = (col == (row // T) * T).astype(f32)
    z = jnp.dot(glr, a2, preferred_element_type=f32) + ab
    g = jax.nn.log_sigmoid(z) / GLA_TAU
    cum = jnp.dot(tri, g, precision=HI, preferred_element_type=f32)
    excl = cum - g
    ref = jnp.dot(sel, excl, precision=HI, preferred_element_type=f32)
    qs = q * (GLA_DK ** -0.5)
    q_ref = qs * jnp.exp(cum - ref)
    rowk = lax.broadcasted_iota(jnp.int32, (C, GLA_DK), 0)
    a = jnp.zeros((C, C), f32)
    for s in range(1, C // T):
        ref_s = jnp.sum(jnp.where(rowk == s * T, excl, 0.0), axis=0, keepdims=True)
        k_ref = k * jnp.exp(jnp.where(rowk < s * T, ref_s - cum, -jnp.inf))
        a_s = lax.dot_general(q_ref, k_ref, (((1,), (1,)), ((), ())), preferred_element_type=f32)
        a = a + jnp.where(row // T == s, a_s, 0.0)
    o = jnp.dot(a, v, preferred_element_type=f32)
    sub = rowk % T
    for d in range(T):
        ks = _sroll(k, d) if d else k
        cs = _sroll(cum, d) if d else cum
        vs = _sroll(v, d) if d else v
        w = jnp.sum(qs * ks * jnp.exp(jnp.where(sub >= d, cum - cs, -jnp.inf)), axis=-1, keepdims=True)
        o = o + w * vs
    o = o + lax.dot_general(qs * jnp.exp(cum), st, (((1,), (1,)), ((), ())), preferred_element_type=f32)
    last = jnp.sum(jnp.where(rowk == C - 1, cum, 0.0), axis=0, keepdims=True)
    st_new = st * jnp.exp(last) + lax.dot_general(v, k * jnp.exp(last - cum), (((0,), (0,)), ((), ())),
                                                  preferred_element_type=f32)
    out = _rms(o, ng) * jax.nn.silu(r)
    return out, st_new


def _gla_specs(L, base128, rev):
    n = L // GLA_CHUNK
    ch = (lambda i: n - 1 - i) if rev else (lambda i: i)
    b = base128
    return n, ch, [
        pl.BlockSpec((GLA_CHUNK, 256), lambda i: (ch(i), (b + P_GQ // 128) // 2)),
        pl.BlockSpec((GLA_CHUNK, 256), lambda i: (ch(i), (b + P_GK // 128) // 2)),
        pl.BlockSpec((GLA_CHUNK, 512), lambda i: (ch(i), (b + P_GV // 128) // 4)),
        pl.BlockSpec((GLA_CHUNK, 512), lambda i: (ch(i), (b + P_GR // 128) // 4)),
        pl.BlockSpec((GLA_CHUNK, 128), lambda i: (ch(i), b + P_GLR // 128)),
    ]


def _gla_fwd(proj, a2p, ab, ng, base128, name):
    L = proj.shape[0]
    n, _, pspecs = _gla_specs(L, base128, False)
    H, DK, DV = GLA_HEADS, GLA_DK, GLA_DV

    def body(q_ref, k_ref, v_ref, r_ref, l_ref, a2_ref, ab_ref, ng_ref, o_ref, st_ref, state):
        @pl.when(pl.program_id(0) == 0)
        def _():
            state[...] = jnp.zeros_like(state)

        st_ref[0] = state[...]
        glr = l_ref[...]
        for h in range(H):
            kk, vv = slice(h * DK, (h + 1) * DK), slice(h * DV, (h + 1) * DV)
            out, st_new = _gla_head(q_ref[:, kk], k_ref[:, kk], v_ref[:, vv], r_ref[:, vv], glr, state[h],
                                    a2_ref[:, kk], ab_ref[:, kk], ng_ref[:, vv])
            o_ref[:, vv] = out.astype(bf16)
            state[h] = st_new

    full = lambda shape: pl.BlockSpec(shape, lambda i: (0,) * len(shape))
    return pl.pallas_call(
        body, name=name, grid=(n,),
        out_shape=[jax.ShapeDtypeStruct((L, H * DV), bf16), jax.ShapeDtypeStruct((n, H, DV, DK), f32)],
        in_specs=pspecs + [full((128, 256)), full((1, 256)), full((1, 512))],
        out_specs=[pl.BlockSpec((GLA_CHUNK, 512), lambda i: (i, 0)), pl.BlockSpec((1, H, DV, DK), lambda i: (i, 0, 0, 0))],
        scratch_shapes=[pltpu.VMEM((H, DV, DK), f32)],
        compiler_params=_params(("arbitrary",)),
    )(proj, proj, proj, proj, proj, a2p, ab, ng)


def _gla_bwd(proj, a2p, ab, ng, states, dout, base128, name, sends=None):
    L = proj.shape[0]
    n, ch, pspecs = _gla_specs(L, base128, True)
    H, DK, DV = GLA_HEADS, GLA_DK, GLA_DV

    def body(q_ref, k_ref, v_ref, r_ref, l_ref, a2_ref, ab_ref, ng_ref, st_ref, do_ref,
             dq_ref, dk_ref, dv_ref, dr_ref, dl_ref, da2_ref, dab_ref, dng_ref, dstate):
        @pl.when(pl.program_id(0) == 0)
        def _():
            dstate[...] = jnp.zeros_like(dstate)
            da2_ref[...] = jnp.zeros_like(da2_ref)
            dab_ref[...] = jnp.zeros_like(dab_ref)
            dng_ref[...] = jnp.zeros_like(dng_ref)

        glr = l_ref[...]
        dglr = jnp.zeros(glr.shape, f32)
        for h in range(H):
            kk, vv = slice(h * DK, (h + 1) * DK), slice(h * DV, (h + 1) * DV)
            _, vjp = jax.vjp(_gla_head, q_ref[:, kk], k_ref[:, kk], v_ref[:, vv], r_ref[:, vv], glr, st_ref[0, h],
                             a2_ref[:, kk], ab_ref[:, kk], ng_ref[:, vv])
            dq, dk, dv, dr, dl, dst, da2, dab, dng = vjp((do_ref[:, vv].astype(f32), dstate[h]))
            dq_ref[:, kk] = dq.astype(bf16)
            dk_ref[:, kk] = dk.astype(bf16)
            dv_ref[:, vv] = dv.astype(bf16)
            dr_ref[:, vv] = dr.astype(bf16)
            dglr = dglr + dl
            dstate[h] = dst
            da2_ref[:, kk] += da2
            dab_ref[:, kk] += dab
            dng_ref[:, vv] += dng
        dl_ref[...] = dglr.astype(bf16)

    full = lambda shape: pl.BlockSpec(shape, lambda i: (0,) * len(shape))
    rowspec = lambda w: pl.BlockSpec((GLA_CHUNK, w), lambda i: (ch(i), 0))
    return _host_call(
        body, sends, (proj, proj, proj, proj, proj, a2p, ab, ng, states, dout), name=name, grid=(n,),
        out_shape=[jax.ShapeDtypeStruct((L, 256), bf16), jax.ShapeDtypeStruct((L, 256), bf16),
                   jax.ShapeDtypeStruct((L, 512), bf16), jax.ShapeDtypeStruct((L, 512), bf16),
                   jax.ShapeDtypeStruct((L, 128), bf16), jax.ShapeDtypeStruct((128, 256), f32),
                   jax.ShapeDtypeStruct((1, 256), f32), jax.ShapeDtypeStruct((1, 512), f32)],
        in_specs=pspecs + [full((128, 256)), full((1, 256)), full((1, 512)),
                           pl.BlockSpec((1, H, DV, DK), lambda i: (ch(i), 0, 0, 0)), rowspec(512)],
        out_specs=[rowspec(256), rowspec(256), rowspec(512), rowspec(512), rowspec(128),
                   full((128, 256)), full((1, 256)), full((1, 512))],
        scratch_shapes=[pltpu.VMEM((H, DV, DK), f32)],
        compiler_params=_params(("arbitrary",)))


def _s5_prep(lam_re, lam_im, log_dt, b_re, b_im, c_re, c_im, d):
    G, N, Cn = S5_GROUPS, S5_STATE, S5_GROUP
    J, GB = S5_LANE_BLOCKS, S5_GROUPS // S5_LANE_BLOCKS
    dt = jnp.exp(log_dt)[:, None]
    mag = jnp.exp(lam_re * dt)
    ab_re, ab_im = mag * jnp.cos(lam_im * dt), mag * jnp.sin(lam_im * dt)
    den = lam_re * lam_re + lam_im * lam_im
    z_re = ((ab_re - 1.0) * lam_re + ab_im * lam_im) / den
    z_im = (ab_im * lam_re - (ab_re - 1.0) * lam_im) / den
    bb_re = z_re[..., None] * b_re - z_im[..., None] * b_im
    bb_im = z_re[..., None] * b_im + z_im[..., None] * b_re
    eye = jnp.eye(GB, dtype=f32)

    def in_blocks(bb):
        return jnp.einsum("jgnc,gh->jgchn", bb.reshape(J, GB, N, Cn), eye).reshape(J, GB * Cn, GB * N)

    def out_blocks(cc):
        return jnp.einsum("jgcn,gh->jgnhc", cc.reshape(J, GB, Cn, N), eye).reshape(J, GB * N, GB * Cn)

    return (ab_re.reshape(1, G * N), ab_im.reshape(1, G * N), in_blocks(bb_re), in_blocks(bb_im),
            out_blocks(c_re), out_blocks(c_im), d.reshape(1, G * Cn))


def _s5_chunk(u, hin_r, hin_i, a_r, a_i, bb_r, bb_i, cc_r, cc_i, dvec):
    T = u.shape[0]
    hr = jnp.dot(u, bb_r, preferred_element_type=f32)
    hi = jnp.dot(u, bb_i, preferred_element_type=f32)
    row = lax.broadcasted_iota(jnp.int32, hr.shape, 0)
    hr = hr + jnp.where(row == 0, a_r * hin_r - a_i * hin_i, 0.0)
    hi = hi + jnp.where(row == 0, a_r * hin_i + a_i * hin_r, 0.0)
    pr, pi = a_r, a_i
    d = 1
    while d < T:
        sr = jnp.where(row >= d, _sroll(hr, d), 0.0)
        si = jnp.where(row >= d, _sroll(hi, d), 0.0)
        hr, hi = hr + pr * sr - pi * si, hi + pr * si + pi * sr
        pr, pi = pr * pr - pi * pi, 2.0 * pr * pi
        d *= 2
    y = (jnp.dot(hr, cc_r, preferred_element_type=f32)
         - jnp.dot(hi, cc_i, preferred_element_type=f32) + dvec * u)
    out_r = jnp.sum(jnp.where(row == T - 1, hr, 0.0), axis=0, keepdims=True)
    out_i = jnp.sum(jnp.where(row == T - 1, hi, 0.0), axis=0, keepdims=True)
    return y, out_r, out_i


def _s5_specs(L, base128, rev):
    T, J = S5_CHUNK, S5_LANE_BLOCKS
    n = L // T
    ch = (lambda c: n - 1 - c) if rev else (lambda c: c)
    ub = base128 + P_S5U // 128
    specs = [
        pl.BlockSpec((T, 128), lambda j, c: (ch(c), ub + j)),
        pl.BlockSpec((1, 512), lambda j, c: (0, j)), pl.BlockSpec((1, 512), lambda j, c: (0, j)),
        pl.BlockSpec((None, 128, 512), lambda j, c: (j, 0, 0)), pl.BlockSpec((None, 128, 512), lambda j, c: (j, 0, 0)),
        pl.BlockSpec((None, 512, 128), lambda j, c: (j, 0, 0)), pl.BlockSpec((None, 512, 128), lambda j, c: (j, 0, 0)),
        pl.BlockSpec((1, 128), lambda j, c: (0, j)),
    ]
    return n, ch, specs


def _s5_fwd(proj, prep, base128, name):
    L = proj.shape[0]
    T, J = S5_CHUNK, S5_LANE_BLOCKS
    n, _, specs = _s5_specs(L, base128, False)

    def body(u_ref, ar, ai, bbr, bbi, ccr, cci, dv, y_ref, sr_ref, si_ref, carry):
        @pl.when(pl.program_id(1) == 0)
        def _():
            carry[...] = jnp.zeros_like(carry)

        hin_r, hin_i = carry[0:1, :], carry[1:2, :]
        sr_ref[0] = jnp.broadcast_to(hin_r, (8, 512))
        si_ref[0] = jnp.broadcast_to(hin_i, (8, 512))
        y, out_r, out_i = _s5_chunk(u_ref[...], hin_r, hin_i, ar[...], ai[...], bbr[...], bbi[...], ccr[...], cci[...], dv[...])
        y_ref[...] = y
        carry[0:1, :] = out_r
        carry[1:2, :] = out_i

    st = pl.BlockSpec((1, 8, 512), lambda j, c: (c, 0, j))
    return pl.pallas_call(
        body, name=name, grid=(J, n),
        out_shape=[jax.ShapeDtypeStruct((L, 512), f32), jax.ShapeDtypeStruct((n, 8, 2048), f32), jax.ShapeDtypeStruct((n, 8, 2048), f32)],
        in_specs=specs, out_specs=[pl.BlockSpec((T, 128), lambda j, c: (c, j)), st, st],
        scratch_shapes=[pltpu.VMEM((8, 512), f32)],
        compiler_params=_params(("parallel", "arbitrary")),
    )(proj, *prep)


def _s5_bwd(proj, prep, st_r, st_i, dy, base128, name, sends=None):
    L = proj.shape[0]
    T, J = S5_CHUNK, S5_LANE_BLOCKS
    n, ch, specs = _s5_specs(L, base128, True)

    def body(u_ref, ar, ai, bbr, bbi, ccr, cci, dv, sr_ref, si_ref, dy_ref,
             du_ref, dar, dai, dbbr, dbbi, dccr, dcci, ddv, dcarry):
        @pl.when(pl.program_id(1) == 0)
        def _():
            dcarry[...] = jnp.zeros_like(dcarry)
            for r in (dar, dai, dbbr, dbbi, dccr, dcci, ddv):
                r[...] = jnp.zeros_like(r)

        _, vjp = jax.vjp(_s5_chunk, u_ref[...], sr_ref[0, 0:1, :], si_ref[0, 0:1, :], ar[...], ai[...],
                         bbr[...], bbi[...], ccr[...], cci[...], dv[...])
        g = vjp((dy_ref[...], dcarry[0:1, :], dcarry[1:2, :]))
        du_ref[...] = g[0].astype(bf16)
        dcarry[0:1, :] = g[1]
        dcarry[1:2, :] = g[2]
        for r, val in zip((dar, dai, dbbr, dbbi, dccr, dcci, ddv), g[3:]):
            r[...] += val

    st = pl.BlockSpec((1, 8, 512), lambda j, c: (ch(c), 0, j))
    outs, recvs = _host_call(
        body, sends, (proj, *prep, st_r, st_i, dy), name=name, grid=(J, n),
        out_shape=[jax.ShapeDtypeStruct((L, 512), bf16),
                   jax.ShapeDtypeStruct((1, 2048), f32), jax.ShapeDtypeStruct((1, 2048), f32),
                   jax.ShapeDtypeStruct((J, 128, 512), f32), jax.ShapeDtypeStruct((J, 128, 512), f32),
                   jax.ShapeDtypeStruct((J, 512, 128), f32), jax.ShapeDtypeStruct((J, 512, 128), f32),
                   jax.ShapeDtypeStruct((1, 512), f32)],
        in_specs=specs + [st, st, pl.BlockSpec((T, 128), lambda j, c: (ch(c), j))],
        out_specs=[pl.BlockSpec((T, 128), lambda j, c: (ch(c), j))] + specs[1:],
        scratch_shapes=[pltpu.VMEM((8, 512), f32)],
        compiler_params=_params(("parallel", "arbitrary")))
    return outs[0], tuple(outs[1:]), recvs


def _glu_f(y, w, b):
    z = jax.nn.gelu(y)
    return z * jax.nn.sigmoid(jnp.dot(z.astype(bf16), w.astype(bf16), preferred_element_type=f32) + b)


def _glu_fwd(y, w, b, name):
    L = y.shape[0]
    tm = _tile(L, 512, 8)

    def body(y_ref, w_ref, b_ref, o_ref):
        o_ref[...] = _glu_f(y_ref[...], w_ref[...], b_ref[...]).astype(bf16)

    row = pl.BlockSpec((tm, 512), lambda i: (i, 0))
    return pl.pallas_call(
        body, name=name, grid=(L // tm,), out_shape=jax.ShapeDtypeStruct((L, 512), bf16),
        in_specs=[row, pl.BlockSpec((512, 512), lambda i: (0, 0)), pl.BlockSpec((1, 512), lambda i: (0, 0))],
        out_specs=row, compiler_params=_params(("parallel",)),
    )(y, w, b)


def _glu_bwd(y, w, b, dout, name):
    L = y.shape[0]
    tm = _tile(L, 512, 8)

    def body(y_ref, w_ref, b_ref, do_ref, dy_ref, dw_ref, db_ref):
        @pl.when(pl.program_id(0) == 0)
        def _():
            dw_ref[...] = jnp.zeros_like(dw_ref)
            db_ref[...] = jnp.zeros_like(db_ref)

        _, vjp = jax.vjp(_glu_f, y_ref[...], w_ref[...], b_ref[...])
        dy, dw, db = vjp(do_ref[...])
        dy_ref[...] = dy
        dw_ref[...] += dw
        db_ref[...] += db

    row = pl.BlockSpec((tm, 512), lambda i: (i, 0))
    wspec, bspec = pl.BlockSpec((512, 512), lambda i: (0, 0)), pl.BlockSpec((1, 512), lambda i: (0, 0))
    return pl.pallas_call(
        body, name=name, grid=(L // tm,),
        out_shape=[jax.ShapeDtypeStruct((L, 512), f32), jax.ShapeDtypeStruct((512, 512), f32), jax.ShapeDtypeStruct((1, 512), f32)],
        in_specs=[row, wspec, bspec, row], out_specs=[row, wspec, bspec],
        compiler_params=_params(("arbitrary",)),
    )(y, w, b, dout)


def _rope_tables(positions):
    half = ROPE_DIM // 2
    inv_freq = ROPE_THETA ** (-jnp.arange(half, dtype=f32) / half)
    ang = positions.astype(f32)[:, None] * inv_freq
    L = positions.shape[0]
    cos = jnp.concatenate([jnp.cos(ang), jnp.cos(ang), jnp.ones((L, HEAD_DIM - ROPE_DIM), f32)], axis=1)
    sin = jnp.concatenate([jnp.sin(ang), jnp.sin(ang), jnp.zeros((L, HEAD_DIM - ROPE_DIM), f32)], axis=1)
    return cos, sin


def _rot_matrix():
    half = ROPE_DIM // 2
    r = lax.broadcasted_iota(jnp.int32, (HEAD_DIM, HEAD_DIM), 0)
    c = lax.broadcasted_iota(jnp.int32, (HEAD_DIM, HEAD_DIM), 1)
    return jnp.where((c < half) & (r == c + half), -1.0, 0.0) + jnp.where((c >= half) & (c < ROPE_DIM) & (r == c - half), 1.0, 0.0)


def _attn_head(q, kp, kc, vp, vc, sink, *, cq, sq, ck, sk, lim, max_dist, use_rope):
    T = ATT_BLOCK
    k2 = jnp.concatenate([kp, kc], axis=0)
    v2 = jnp.concatenate([vp, vc], axis=0)
    if use_rope:
        rot = _rot_matrix()
        q = q * cq + jnp.dot(q, rot, preferred_element_type=f32) * sq
        k2 = k2 * ck + jnp.dot(k2, rot, preferred_element_type=f32) * sk
    s = lax.dot_general(q, k2, (((1,), (1,)), ((), ())), preferred_element_type=f32) * (HEAD_DIM ** -0.5)
    t = lax.broadcasted_iota(jnp.int32, (T, 2 * T), 0)
    j = lax.broadcasted_iota(jnp.int32, (T, 2 * T), 1)
    dist = T + t - j
    valid = (dist >= 0) & (dist <= max_dist) & (j >= lim)
    s = jnp.where(valid, s, -jnp.inf)
    m = lax.stop_gradient(jnp.max(s, axis=-1, keepdims=True))
    p = jnp.exp(s - m)
    den = jnp.sum(p, axis=-1, keepdims=True)
    o = jnp.dot(p, v2, preferred_element_type=f32) / den
    lse = jnp.broadcast_to(m + jnp.log(den), (T, HEAD_DIM))
    if sink is None:
        return o, lse
    return o * jax.nn.sigmoid(lse - sink)


def _attn_specs(L, q_col, k_col, v_col, wk, rev):
    T = ATT_BLOCK
    n = L // T
    blk = (lambda i: n - 1 - i) if rev else (lambda i: i)
    prev = lambda i: jnp.maximum(blk(i) - 1, 0)
    specs = [
        pl.BlockSpec((T, 512), lambda i: (blk(i), q_col)),
        pl.BlockSpec((T, wk), lambda i: (prev(i), k_col)), pl.BlockSpec((T, wk), lambda i: (blk(i), k_col)),
        pl.BlockSpec((T, wk), lambda i: (prev(i), v_col)), pl.BlockSpec((T, wk), lambda i: (blk(i), v_col)),
        pl.BlockSpec((T, 64), lambda i: (prev(i), 0)), pl.BlockSpec((T, 64), lambda i: (blk(i), 0)),
        pl.BlockSpec((T, 64), lambda i: (prev(i), 0)), pl.BlockSpec((T, 64), lambda i: (blk(i), 0)),
    ]
    return n, blk, specs


def _attn_fwd(qa, ka, va, cos, sin, sinks, *, q_col, k_col, v_col, hkv, nbc, max_dist, name):
    L = qa.shape[0]
    T, HQ, HD = ATT_BLOCK, 8, HEAD_DIM
    wk = hkv * HD
    n, _, specs = _attn_specs(L, q_col, k_col, v_col, wk, False)
    grp = HQ // hkv
    gated = sinks is not None

    def body(*refs):
        q_ref, kp_ref, kc_ref, vp_ref, vc_ref, cp_ref, cc_ref, sp_ref, sc_ref = refs[:9]
        rest = refs[9:]
        lim = jnp.where(pl.program_id(0) % nbc == 0, T, 0)
        ck = jnp.concatenate([cp_ref[...], cc_ref[...]], axis=0)
        sk = jnp.concatenate([sp_ref[...], sc_ref[...]], axis=0)
        for h in range(HQ):
            hs, ks = slice(h * HD, (h + 1) * HD), slice((h // grp) * HD, (h // grp + 1) * HD)
            res = _attn_head(q_ref[:, hs], kp_ref[:, ks], kc_ref[:, ks], vp_ref[:, ks], vc_ref[:, ks],
                             rest[0][:, hs] if gated else None,
                             cq=cc_ref[...], sq=sc_ref[...], ck=ck, sk=sk, lim=lim, max_dist=max_dist, use_rope=True)
            if gated:
                rest[1][:, hs] = res.astype(bf16)
            else:
                rest[0][:, hs] = res[0]
                rest[1][:, hs] = res[1]

    row = pl.BlockSpec((T, 512), lambda i: (i, 0))
    if gated:
        return pl.pallas_call(
            body, name=name, grid=(n,), out_shape=jax.ShapeDtypeStruct((L, 512), bf16),
            in_specs=specs + [pl.BlockSpec((1, 512), lambda i: (0, 0))], out_specs=row,
            compiler_params=_params(("parallel",)),
        )(qa, ka, ka, va, va, cos, cos, sin, sin, sinks)
    return pl.pallas_call(
        body, name=name, grid=(n,), out_shape=[jax.ShapeDtypeStruct((L, 512), f32)] * 2,
        in_specs=specs, out_specs=[row, row], compiler_params=_params(("parallel",)),
    )(qa, ka, ka, va, va, cos, cos, sin, sin)


def _attn_bwd(qa, ka, va, cos, sin, sinks, douts, *, q_col, k_col, v_col, hkv, nbc, max_dist, name, sends=None):
    L = qa.shape[0]
    T, HQ, HD = ATT_BLOCK, 8, HEAD_DIM
    wk = hkv * HD
    n, blk, specs = _attn_specs(L, q_col, k_col, v_col, wk, True)
    grp = HQ // hkv
    gated = sinks is not None
    nd = len(douts)

    def body(*refs):
        q_ref, kp_ref, kc_ref, vp_ref, vc_ref, cp_ref, cc_ref, sp_ref, sc_ref = refs[:9]
        pos = 9
        sink_ref = None
        if gated:
            sink_ref = refs[pos]
            pos += 1
        d_refs = refs[pos:pos + nd]
        pos += nd
        dq_ref, dk_ref, dv_ref = refs[pos:pos + 3]
        pos += 3
        dsink_ref = None
        if gated:
            dsink_ref = refs[pos]
            pos += 1
        carry_k, carry_v = refs[pos:pos + 2]

        @pl.when(pl.program_id(0) == 0)
        def _():
            carry_k[...] = jnp.zeros_like(carry_k)
            carry_v[...] = jnp.zeros_like(carry_v)
            if gated:
                dsink_ref[...] = jnp.zeros_like(dsink_ref)

        lim = jnp.where(blk(pl.program_id(0)) % nbc == 0, T, 0)
        ck = jnp.concatenate([cp_ref[...], cc_ref[...]], axis=0)
        sk = jnp.concatenate([sp_ref[...], sc_ref[...]], axis=0)
        dkp = [jnp.zeros((T, HD), f32) for _ in range(hkv)]
        dkc = [jnp.zeros((T, HD), f32) for _ in range(hkv)]
        dvp = [jnp.zeros((T, HD), f32) for _ in range(hkv)]
        dvc = [jnp.zeros((T, HD), f32) for _ in range(hkv)]
        for h in range(HQ):
            g = h // grp
            hs, ks = slice(h * HD, (h + 1) * HD), slice(g * HD, (g + 1) * HD)
            fn = functools.partial(_attn_head, cq=cc_ref[...], sq=sc_ref[...], ck=ck, sk=sk, lim=lim,
                                   max_dist=max_dist, use_rope=True)
            prim = (q_ref[:, hs], kp_ref[:, ks], kc_ref[:, ks], vp_ref[:, ks], vc_ref[:, ks])
            if gated:
                _, vjp = jax.vjp(fn, *prim, sink_ref[:, hs])
                dq, a, b, c, d, ds = vjp(d_refs[0][:, hs].astype(f32))
                dsink_ref[:, hs] += ds
            else:
                _, vjp = jax.vjp(lambda *p: fn(*p, None), *prim)
                dq, a, b, c, d = vjp((d_refs[0][:, hs], d_refs[1][:, hs]))
            dq_ref[:, hs] = dq.astype(bf16)
            dkp[g], dkc[g], dvp[g], dvc[g] = dkp[g] + a, dkc[g] + b, dvp[g] + c, dvc[g] + d
        for g in range(hkv):
            ks = slice(g * HD, (g + 1) * HD)
            dk_ref[:, ks] = (dkc[g] + carry_k[:, ks]).astype(bf16)
            dv_ref[:, ks] = (dvc[g] + carry_v[:, ks]).astype(bf16)
            carry_k[:, ks] = dkp[g]
            carry_v[:, ks] = dvp[g]

    row = lambda w: pl.BlockSpec((T, w), lambda i: (blk(i), 0))
    vec = pl.BlockSpec((1, 512), lambda i: (0, 0))
    in_specs = specs + ([vec] if gated else []) + [row(512)] * nd
    out_shape = [jax.ShapeDtypeStruct((L, 512), bf16), jax.ShapeDtypeStruct((L, wk), bf16), jax.ShapeDtypeStruct((L, wk), bf16)]
    out_specs = [row(512), row(wk), row(wk)]
    if gated:
        out_shape.append(jax.ShapeDtypeStruct((1, 512), f32))
        out_specs.append(vec)
    args = (qa, ka, ka, va, va, cos, cos, sin, sin) + ((sinks,) if gated else ()) + tuple(douts)
    outs, recvs = _host_call(
        body, sends, args, name=name, grid=(n,), out_shape=out_shape, in_specs=in_specs, out_specs=out_specs,
        scratch_shapes=[pltpu.VMEM((T, wk), f32), pltpu.VMEM((T, wk), f32)],
        compiler_params=_params(("arbitrary",)))
    return outs if sends is None else (outs, recvs)


def _dilmix_f(o0, o1, o2, l0, l1, l2):
    m = jnp.maximum(jnp.maximum(l0, l1), l2)
    e0, e1, e2 = jnp.exp(l0 - m), jnp.exp(l1 - m), jnp.exp(l2 - m)
    return (e0 * o0 + e1 * o1 + e2 * o2) / (e0 + e1 + e2)


def _dilmix_fwd(os_, ls, name):
    L = os_[0].shape[0]
    tm = _tile(L, 512, 8)

    def body(o0, o1, o2, l0, l1, l2, out):
        out[...] = _dilmix_f(o0[...], o1[...], o2[...], l0[...], l1[...], l2[...]).astype(bf16)

    row = pl.BlockSpec((tm, 512), lambda i: (i, 0))
    return pl.pallas_call(
        body, name=name, grid=(L // tm,), out_shape=jax.ShapeDtypeStruct((L, 512), bf16),
        in_specs=[row] * 6, out_specs=row, compiler_params=_params(("parallel",)),
    )(*os_, *ls)


def _dilmix_bwd(os_, ls, dout, name):
    L = os_[0].shape[0]
    tm = _tile(L, 512, 8)

    def body(o0, o1, o2, l0, l1, l2, d, *outs):
        _, vjp = jax.vjp(_dilmix_f, o0[...], o1[...], o2[...], l0[...], l1[...], l2[...])
        for r, val in zip(outs, vjp(d[...].astype(f32))):
            r[...] = val

    row = pl.BlockSpec((tm, 512), lambda i: (i, 0))
    outs = pl.pallas_call(
        body, name=name, grid=(L // tm,), out_shape=[jax.ShapeDtypeStruct((L, 512), f32)] * 6,
        in_specs=[row] * 7, out_specs=[row] * 6, compiler_params=_params(("parallel",)),
    )(*os_, *ls, dout)
    return outs[:3], outs[3:]


def _to_strided(z, dil):
    L, w = z.shape
    return z.reshape(L // dil, dil, w).transpose(1, 0, 2).reshape(L, w)


def _from_strided(z, dil):
    L, w = z.shape
    return z.reshape(dil, L // dil, w).transpose(1, 0, 2).reshape(L, w)


def _adamw_math(w, g, m, v):
    m = ADAM_B1 * m + (1.0 - ADAM_B1) * g
    v = ADAM_B2 * v + (1.0 - ADAM_B2) * (g * g)
    m_hat = m / (1.0 - ADAM_B1 ** ADAM_STEP)
    v_hat = v / (1.0 - ADAM_B2 ** ADAM_STEP)
    delta = -ADAM_LR * (m_hat / (jnp.sqrt(v_hat) + ADAM_EPS) + ADAM_WD * w)
    return delta, m, v


def _adamw(w, m, v, slots, name):
    depth, R, C = w.shape
    tr = _tile(R, max(8, 131072 // C), 8)
    outs = None
    for l in range(depth):
        def body(w_ref, m_ref, v_ref, s_ref, *rest):
            g_ref, d_ref, nm_ref, nv_ref = rest[-4:]
            g = s_ref[0].astype(f32)
            for i in range(1, N_DEV):
                g = g + s_ref[i].astype(f32)
            delta, nm, nv = _adamw_math(w_ref[0], g, m_ref[0], v_ref[0])
            g_ref[0], d_ref[0], nm_ref[0], nv_ref[0] = g, delta, nm, nv

        blk = pl.BlockSpec((1, tr, C), functools.partial(lambda i, l: (l, i, 0), l=l))
        carried = [] if outs is None else list(outs)
        outs = pl.pallas_call(
            body, name=f"{name}_{l}", grid=(R // tr,), out_shape=[jax.ShapeDtypeStruct(w.shape, f32)] * 4,
            in_specs=[blk, blk, blk, pl.BlockSpec((N_DEV, tr, C), lambda i: (0, i, 0))]
            + [pl.BlockSpec(memory_space=pl.ANY)] * len(carried),
            out_specs=[blk] * 4, input_output_aliases={4 + j: j for j in range(len(carried))},
            compiler_params=_params(("parallel",)),
        )(w, m, v, slots[l], *carried)
    return outs


def _adamw_packed(w, m, v, slots, name):
    R = w.shape[0]
    tr = _tile(R, 512, 8)

    def body(w_ref, m_ref, v_ref, s_ref, g_ref, d_ref, nm_ref, nv_ref):
        g = s_ref[0]
        for i in range(1, N_DEV):
            g = g + s_ref[i]
        delta, nm, nv = _adamw_math(w_ref[...], g, m_ref[...], v_ref[...])
        g_ref[...], d_ref[...], nm_ref[...], nv_ref[...] = g, delta, nm, nv

    blk = pl.BlockSpec((tr, 128), lambda i: (i, 0))
    return pl.pallas_call(
        body, name=name, grid=(R // tr,), out_shape=[jax.ShapeDtypeStruct(w.shape, f32)] * 4,
        in_specs=[blk, blk, blk, pl.BlockSpec((N_DEV, tr, 128), lambda i: (0, i, 0))], out_specs=[blk] * 4,
        compiler_params=_params(("parallel",)),
    )(w, m, v, slots)


def _cols_gathered(g):
    nd = g.ndim
    perm = tuple(range(1, nd - 1)) + (0, nd - 1)
    t = g.transpose(perm)
    return t.reshape(t.shape[:-2] + (t.shape[-2] * t.shape[-1],))


def _cols_scatter(full):
    s = full.shape
    t = full.reshape(s[:-1] + (N_DEV, s[-1] // N_DEV))
    nd = t.ndim
    return t.transpose((nd - 2,) + tuple(range(nd - 2)) + (nd - 1,))


def _rows_gathered(g):
    t = g.transpose(1, 0, 2, 3)
    return t.reshape(t.shape[0], t.shape[1] * t.shape[2], t.shape[3])


def _win_reorder(w):
    pad = jnp.zeros(w.shape[:-1] + (128 - GLA_LOWRANK,), w.dtype)
    return jnp.concatenate([w[..., O_GATES:], w[..., :O_GLR], w[..., O_S5U:O_GATES], w[..., O_GLR:O_S5U], pad], axis=-1)


def _win_restore(wp, D):
    b = 4 * D
    return jnp.concatenate([wp[..., b:b + O_GLR], wp[..., b + P_GLR:b + P_GLR + GLA_LOWRANK],
                            wp[..., b + O_GLR:b + P_GLR], wp[..., :b]], axis=-1)


SMALL = ("norm1_g", "gla_a_b", "gla_norm_g", "s5_lambda_re", "s5_lambda_im", "s5_log_dt", "s5_b_re", "s5_b_im",
         "s5_c_re", "s5_c_im", "s5_d", "s5_glu_b", "swa_sinks", "norm2_g", "final_norm_g")
SHARDED = ("w_in", "gla_a2", "s5_glu_w", "w_branch", "w_out", "w_ffn_gate", "w_ffn_up", "w_ffn_down")
WEIGHTS = ("norm1_g", "w_in", "gla_a2", "gla_a_b", "gla_norm_g", "s5_lambda_re", "s5_lambda_im", "s5_log_dt", "s5_b_re",
           "s5_b_im", "s5_c_re", "s5_c_im", "s5_d", "s5_glu_w", "s5_glu_b", "swa_sinks", "w_branch", "w_out", "norm2_g",
           "w_ffn_gate", "w_ffn_up", "w_ffn_down", "final_norm_g")


def _pack(arrs):
    flat = jnp.concatenate([a.reshape(-1) for a in arrs])
    n = flat.shape[0]
    rows = -(-n // 1024) * 8
    return jnp.pad(flat, (0, rows * 128 - n)).reshape(rows, 128)


def _unpack(packed, like):
    flat = packed.reshape(-1)
    out, pos = [], 0
    for a in like:
        out.append(flat[pos:pos + a.size].reshape(a.shape))
        pos += a.size
    return out


def kernel(x, positions, norm1_g, w_in, gla_a2, gla_a_b, gla_norm_g, s5_lambda_re, s5_lambda_im, s5_log_dt, s5_b_re, s5_b_im, s5_c_re, s5_c_im, s5_d, s5_glu_w, s5_glu_b, swa_sinks, w_branch, w_out, norm2_g, w_ffn_gate, w_ffn_up, w_ffn_down, final_norm_g, loss_target, m_norm1_g, m_w_in, m_gla_a2, m_gla_a_b, m_gla_norm_g, m_s5_lambda_re, m_s5_lambda_im, m_s5_log_dt, m_s5_b_re, m_s5_b_im, m_s5_c_re, m_s5_c_im, m_s5_d, m_s5_glu_w, m_s5_glu_b, m_swa_sinks, m_w_branch, m_w_out, m_norm2_g, m_w_ffn_gate, m_w_ffn_up, m_w_ffn_down, m_final_norm_g, v_norm1_g, v_w_in, v_gla_a2, v_gla_a_b, v_gla_norm_g, v_s5_lambda_re, v_s5_lambda_im, v_s5_log_dt, v_s5_b_re, v_s5_b_im, v_s5_c_re, v_s5_c_im, v_s5_d, v_s5_glu_w, v_s5_glu_b, v_swa_sinks, v_w_branch, v_w_out, v_norm2_g, v_w_ffn_gate, v_w_ffn_up, v_w_ffn_down, v_final_norm_g):
    W = dict(norm1_g=norm1_g, w_in=w_in, gla_a2=gla_a2, gla_a_b=gla_a_b, gla_norm_g=gla_norm_g, s5_lambda_re=s5_lambda_re, s5_lambda_im=s5_lambda_im, s5_log_dt=s5_log_dt, s5_b_re=s5_b_re, s5_b_im=s5_b_im, s5_c_re=s5_c_re, s5_c_im=s5_c_im, s5_d=s5_d, s5_glu_w=s5_glu_w, s5_glu_b=s5_glu_b, swa_sinks=swa_sinks, w_branch=w_branch, w_out=w_out, norm2_g=norm2_g, w_ffn_gate=w_ffn_gate, w_ffn_up=w_ffn_up, w_ffn_down=w_ffn_down, final_norm_g=final_norm_g)
    Mo = dict(norm1_g=m_norm1_g, w_in=m_w_in, gla_a2=m_gla_a2, gla_a_b=m_gla_a_b, gla_norm_g=m_gla_norm_g, s5_lambda_re=m_s5_lambda_re, s5_lambda_im=m_s5_lambda_im, s5_log_dt=m_s5_log_dt, s5_b_re=m_s5_b_re, s5_b_im=m_s5_b_im, s5_c_re=m_s5_c_re, s5_c_im=m_s5_c_im, s5_d=m_s5_d, s5_glu_w=m_s5_glu_w, s5_glu_b=m_s5_glu_b, swa_sinks=m_swa_sinks, w_branch=m_w_branch, w_out=m_w_out, norm2_g=m_norm2_g, w_ffn_gate=m_w_ffn_gate, w_ffn_up=m_w_ffn_up, w_ffn_down=m_w_ffn_down, final_norm_g=m_final_norm_g)
    Vo = dict(norm1_g=v_norm1_g, w_in=v_w_in, gla_a2=v_gla_a2, gla_a_b=v_gla_a_b, gla_norm_g=v_gla_norm_g, s5_lambda_re=v_s5_lambda_re, s5_lambda_im=v_s5_lambda_im, s5_log_dt=v_s5_log_dt, s5_b_re=v_s5_b_re, s5_b_im=v_s5_b_im, s5_c_re=v_s5_c_re, s5_c_im=v_s5_c_im, s5_d=v_s5_d, s5_glu_w=v_s5_glu_w, s5_glu_b=v_s5_glu_b, swa_sinks=v_swa_sinks, w_branch=v_w_branch, w_out=v_w_out, norm2_g=v_norm2_g, w_ffn_gate=v_w_ffn_gate, w_ffn_up=v_w_ffn_up, w_ffn_down=v_w_ffn_down, final_norm_g=v_final_norm_g)

    L, D = x.shape[1], x.shape[2]
    depth = norm1_g.shape[0]
    xs = x.reshape(L, D)
    target = loss_target.reshape(L, D)
    base128 = 4 * D // 128

    big = ("w_in", "w_branch", "w_out", "w_ffn_gate", "w_ffn_up", "w_ffn_down")
    gathered = _all_gather([W[k].astype(bf16) for k in big] + [gla_a2, s5_glu_w], "gather_weights")
    G = dict(zip(big + ("gla_a2", "s5_glu_w"), gathered))
    win_p = _win_reorder(_cols_gathered(G["w_in"]))
    wb = _cols_gathered(G["w_branch"])
    wout = _rows_gathered(G["w_out"])
    wg, wu = _cols_gathered(G["w_ffn_gate"]), _cols_gathered(G["w_ffn_up"])
    wd = _rows_gathered(G["w_ffn_down"])
    a2_full = _cols_gathered(G["gla_a2"])
    a2p = jnp.pad(a2_full, ((0, 0), (0, 128 - GLA_LOWRANK), (0, 0)))
    glu_w = _rows_gathered(G["s5_glu_w"])

    cos, sin = _rope_tables(positions.reshape(L))
    strided_tabs = {dil: (_to_strided(cos, dil), _to_strided(sin, dil)) for _, dil in DIL_CONFIGS if dil > 1}

    saved = []
    cur = xs
    for l in range(depth):
        s = {"x": cur}
        h1 = _rms_fwd(cur, norm1_g[l][None], f"rms1_fwd{l}")
        proj = _matmul(h1, win_p[l], name=f"proj_in{l}")
        s["h1"], s["proj"] = h1, proj
        ab, ng = gla_a_b[l][None], gla_norm_g[l].reshape(1, 512)
        o_gla, s["gla_st"] = _gla_fwd(proj, a2p[l], ab, ng, base128, f"gla_fwd{l}")
        prep, s["prep_vjp"] = jax.vjp(_s5_prep, s5_lambda_re[l], s5_lambda_im[l], s5_log_dt[l], s5_b_re[l], s5_b_im[l],
                                      s5_c_re[l], s5_c_im[l], s5_d[l])
        s["prep"] = prep
        y_s5, s["s5_r"], s["s5_i"] = _s5_fwd(proj, prep, base128, f"s5_fwd{l}")
        s["y_s5"] = y_s5
        o_s5 = _glu_fwd(y_s5, glu_w[l], s5_glu_b[l][None], f"glu_fwd{l}")
        sinks_b = jnp.repeat(swa_sinks[l], HEAD_DIM)[None]
        s["sinks_b"] = sinks_b
        nb = L // ATT_BLOCK
        o_swa = _attn_fwd(proj, proj, proj, cos, sin, sinks_b, q_col=(base128 + P_SQ // 128) // 4,
                          k_col=base128 + P_SK // 128, v_col=base128 + P_SV // 128, hkv=SWA_KV_HEADS, nbc=nb,
                          max_dist=SWA_WINDOW - 1, name=f"swa_fwd{l}")
        cq, ck, cv = (base128 + P_CQ // 128) // 4, (base128 + P_CK // 128) // 4, (base128 + P_CV // 128) // 4
        dil_o, dil_l, s["dil_in"] = [], [], []
        for window, dil in DIL_CONFIGS:
            if dil == 1:
                o, lse = _attn_fwd(proj, proj, proj, cos, sin, None, q_col=cq, k_col=ck, v_col=cv, hkv=8, nbc=nb,
                                   max_dist=window // dil, name=f"dil{dil}_fwd{l}")
                s["dil_in"].append(None)
            else:
                qs_, ks_, vs_ = (_to_strided(proj[:, 4 * D + off:4 * D + off + 512], dil) for off in (P_CQ, P_CK, P_CV))
                o, lse = _attn_fwd(qs_, ks_, vs_, *strided_tabs[dil], None, q_col=0, k_col=0, v_col=0, hkv=8,
                                   nbc=nb // dil, max_dist=window // dil, name=f"dil{dil}_fwd{l}")
                o, lse = _from_strided(o, dil), _from_strided(lse, dil)
                s["dil_in"].append((qs_, ks_, vs_))
            dil_o.append(o)
            dil_l.append(lse)
        s["dil_o"], s["dil_l"] = dil_o, dil_l
        o_dil = _dilmix_fwd(dil_o, dil_l, f"dilmix_fwd{l}")
        branches = (o_gla, o_s5, o_dil, o_swa)
        s["branches"] = branches
        ys = [_matmul(br, wb[l, m], name=f"branch{m}_fwd{l}") for m, br in enumerate(branches)]
        s["ys"] = ys
        mixed = _merge_fwd(proj, ys, D, f"merge_fwd{l}")
        s["mixed"] = mixed
        x2 = _matmul(mixed, wout[l], res=cur, name=f"out_fwd{l}")
        s["x2"] = x2
        h2 = _rms_fwd(x2, norm2_g[l][None], f"rms2_fwd{l}")
        a = _matmul(h2, wg[l], name=f"ffn_gate_fwd{l}")
        b = _matmul(h2, wu[l], name=f"ffn_up_fwd{l}")
        act = _swiglu_fwd(a, b, f"swiglu_fwd{l}")
        s["h2"], s["a"], s["b"], s["act"] = h2, a, b, act
        cur = _matmul(act, wd[l], res=x2, name=f"ffn_down_fwd{l}")
        saved.append(s)

    loss_part, dcur, dcur_b, dgf = _final_loss(cur, final_norm_g[None], target, "final_loss")
    loss = lax.psum(loss_part, AXES)

    small_g = {k: [None] * depth for k in SMALL if k != "final_norm_g"}
    recv = {k: [None] * depth for k in SHARDED}
    in_group = ("w_in", "gla_a2", "s5_glu_w")
    pending = None
    for l in reversed(range(depth)):
        s = saved[l]
        proj = s["proj"]
        dact = _matmul(dcur_b, wd[l], mode="nt", name=f"ffn_down_dx{l}")
        g_down = _matmul(s["act"], dcur_b, mode="tn", out_dtype=bf16, name=f"ffn_down_dw{l}")
        da, db = _swiglu_bwd(s["a"], s["b"], dact, f"swiglu_bwd{l}")
        dh2 = _matmul(da, wg[l], mode="nt", name=f"ffn_gate_dx{l}")
        dh2 = _matmul(db, wu[l], mode="nt", res=dh2, name=f"ffn_up_dx{l}")
        g_gate = _matmul(s["h2"], da, mode="tn", out_dtype=bf16, name=f"ffn_gate_dw{l}")
        g_up = _matmul(s["h2"], db, mode="tn", out_dtype=bf16, name=f"ffn_up_dw{l}")
        ffn_sends = (("w_ffn_down", g_down.reshape((N_DEV, -1, D))), ("w_ffn_gate", _cols_scatter(g_gate)),
                     ("w_ffn_up", _cols_scatter(g_up)))
        dx2, dx2_b, dg2 = _rms_bwd(s["x2"], norm2_g[l][None], dh2, dcur, f"rms2_bwd{l}")
        small_g["norm2_g"][l] = dg2[0]
        dmixed = _matmul(dx2_b, wout[l], mode="nt", name=f"out_dx{l}")
        g_out = _matmul(s["mixed"], dx2_b, mode="tn", out_dtype=bf16, name=f"out_dw{l}")
        dys, dgates = _merge_bwd(proj, s["ys"], dmixed, D, f"merge_bwd{l}")
        dbr = [_matmul(dys[m], wb[l, m], mode="nt", name=f"branch{m}_dx{l}") for m in range(4)]
        g_branch = jnp.stack([_matmul(s["branches"][m], dys[m], mode="tn", out_dtype=bf16, name=f"branch{m}_dw{l}")
                              for m in range(4)])
        d_gla, d_s5, d_dil, d_swa = dbr
        ab, ng = gla_a_b[l][None], gla_norm_g[l].reshape(1, 512)
        (dgq, dgk, dgv, dgr, dglr, da2, dab, dng), got = _gla_bwd(proj, a2p[l], ab, ng, s["gla_st"], d_gla, base128,
                                                                   f"gla_bwd{l}", sends=pending)
        for k, r in zip(in_group, got):
            recv[k][l + 1] = r
        small_g["gla_a_b"][l] = dab[0]
        small_g["gla_norm_g"][l] = dng.reshape(GLA_HEADS, GLA_DV)
        dy_s5, dglu_w, dglu_b = _glu_bwd(s["y_s5"], glu_w[l], s5_glu_b[l][None], d_s5, f"glu_bwd{l}")
        small_g["s5_glu_b"][l] = dglu_b[0]
        ds5u, dprep, got = _s5_bwd(proj, s["prep"], s["s5_r"], s["s5_i"], dy_s5, base128, f"s5_bwd{l}",
                                   sends=[g_out.reshape((N_DEV, -1, D)), _cols_scatter(g_branch)])
        recv["w_out"][l], recv["w_branch"][l] = got
        draw = s["prep_vjp"](dprep)
        for k, val in zip(("s5_lambda_re", "s5_lambda_im", "s5_log_dt", "s5_b_re", "s5_b_im", "s5_c_re", "s5_c_im", "s5_d"), draw):
            small_g[k][l] = val
        nb = L // ATT_BLOCK
        dsq, dsk, dsv, dsinks = _attn_bwd(proj, proj, proj, cos, sin, s["sinks_b"], (d_swa,),
                                          q_col=(base128 + P_SQ // 128) // 4, k_col=base128 + P_SK // 128,
                                          v_col=base128 + P_SV // 128, hkv=SWA_KV_HEADS, nbc=nb, max_dist=SWA_WINDOW - 1,
                                          name=f"swa_bwd{l}")
        small_g["swa_sinks"][l] = dsinks.reshape(SWA_HEADS, HEAD_DIM).sum(axis=1)
        dos, dls = _dilmix_bwd(s["dil_o"], s["dil_l"], d_dil, f"dilmix_bwd{l}")
        cq, ck, cv = (base128 + P_CQ // 128) // 4, (base128 + P_CK // 128) // 4, (base128 + P_CV // 128) // 4
        dcq = dck = dcv = None
        for i, (window, dil) in enumerate(DIL_CONFIGS):
            key, send = ffn_sends[i]
            if dil == 1:
                g3, got = _attn_bwd(proj, proj, proj, cos, sin, None, (dos[i], dls[i]), q_col=cq, k_col=ck, v_col=cv, hkv=8,
                                    nbc=nb, max_dist=window // dil, name=f"dil{dil}_bwd{l}", sends=[send])
            else:
                qs_, ks_, vs_ = s["dil_in"][i]
                g3, got = _attn_bwd(qs_, ks_, vs_, *strided_tabs[dil], None, (_to_strided(dos[i], dil), _to_strided(dls[i], dil)),
                                    q_col=0, k_col=0, v_col=0, hkv=8, nbc=nb // dil, max_dist=window // dil,
                                    name=f"dil{dil}_bwd{l}", sends=[send])
                g3 = [_from_strided(t, dil) for t in g3]
            recv[key][l] = got[0]
            g3 = [t.astype(f32) for t in g3]
            dcq, dck, dcv = (g3[0], g3[1], g3[2]) if dcq is None else (dcq + g3[0], dck + g3[1], dcv + g3[2])
        dproj = jnp.concatenate([dgates.transpose(1, 0, 2).reshape(L, 4 * D), dgq, dgk, dgv, dgr, ds5u,
                                 dcq.astype(bf16), dck.astype(bf16), dcv.astype(bf16), dsq, dsk, dsv, dglr], axis=1)
        g_in = _matmul(s["h1"], dproj, mode="tn", out_dtype=bf16, name=f"proj_in_dw{l}")
        in_sends = [_cols_scatter(_win_restore(g_in, D)), _cols_scatter(da2[:GLA_LOWRANK]), dglu_w.reshape((N_DEV, -1, 512))]
        if l > 0:
            dh1 = _matmul(dproj, win_p[l], mode="nt", name=f"proj_in_dx{l}")
            pending = in_sends
        else:
            dh1, got = _matmul(dproj, win_p[l], mode="nt", sends=in_sends, name=f"proj_in_dx{l}")
            for k, r in zip(in_group, got):
                recv[k][l] = r
        dcur, dcur_b, dg1 = _rms_bwd(s["x"], norm1_g[l][None], dh1, dx2, f"rms1_bwd{l}")
        small_g["norm1_g"][l] = dg1[0]
    grad_x = dcur.reshape(x.shape)

    out = {}
    for k in SHARDED:
        shp = W[k].shape
        as3 = lambda t: t.reshape((shp[0], -1, shp[-1]))
        slots = [r.reshape((N_DEV, -1, shp[-1])) for r in recv[k]]
        res = _adamw(as3(W[k]), as3(Mo[k]), as3(Vo[k]), slots, f"adamw_{k}")
        out[k] = [t.reshape(shp) for t in res]

    small_list = [jnp.stack(small_g[k]) if k != "final_norm_g" else dgf[0] for k in SMALL]
    small_list = [t.reshape(W[k].shape) for t, k in zip(small_list, SMALL)]
    packed_parts = _all_gather([_pack(small_list)], "gather_small_grads")[0]
    res = _adamw_packed(_pack([W[k] for k in SMALL]), _pack([Mo[k] for k in SMALL]), _pack([Vo[k] for k in SMALL]),
                        packed_parts, "adamw_small")
    unpacked = [_unpack(t, [W[k] for k in SMALL]) for t in res]
    for i, k in enumerate(SMALL):
        out[k] = [unpacked[j][i] for j in range(4)]

    return (loss, grad_x, *[out[k][0] for k in WEIGHTS], *[out[k][1] for k in WEIGHTS],
            *[out[k][2] for k in WEIGHTS], *[out[k][3] for k in WEIGHTS])
```

```python
import functools
import math

import jax
import jax.numpy as jnp
from jax import lax
from jax.experimental import pallas as pl
from jax.experimental.pallas import tpu as pltpu

f32 = jnp.float32
bf16 = jnp.bfloat16
HI = lax.Precision.HIGHEST

N_DEV = 8
AXES = ("x", "y", "c")
NORM_EPS = 1e-6
ROPE_THETA = 500000.0
HEAD_DIM = 64
ROPE_DIM = 16
ATT_BLOCK = 128
BRANCH_WIDTH = 512
GLA_HEADS, GLA_DK, GLA_DV, GLA_LOWRANK, GLA_TAU, GLA_CHUNK, GLA_SUB = 4, 64, 128, 16, 16.0, 64, 16
S5_GROUPS, S5_GROUP, S5_STATE = 32, 16, 64
S5_CHUNK = 128
S5_LANE_BLOCKS = 4
DIL_CONFIGS = ((128, 1), (512, 4), (2048, 16))
SWA_HEADS, SWA_KV_HEADS, SWA_WINDOW = 8, 2, 128
ADAM_LR, ADAM_B1, ADAM_B2, ADAM_EPS, ADAM_WD, ADAM_STEP = 0.001, 0.9, 0.999, 1e-08, 0.01, 10
O_GLR, O_S5U, O_GATES = 1536, 1552, 4368
MIX_COLS = 4480
P_GQ, P_GK, P_GV, P_GR, P_S5U, P_CQ, P_CK, P_CV, P_SQ, P_SK, P_SV, P_GLR = (
    0, 256, 512, 1024, 1536, 2048, 2560, 3072, 3584, 4096, 4224, 4352)
VMEM_LIMIT = 56 * 1024 * 1024


def _tile(n, cap, q=128):
    if n <= cap:
        return n
    t = (cap // q) * q
    while t >= q:
        if n % t == 0:
            return t
        t -= q
    return n


def _params(sem=None):
    return pltpu.CompilerParams(dimension_semantics=sem, vmem_limit_bytes=VMEM_LIMIT)


@functools.partial(jax.custom_vjp, nondiff_argnums=(1,))
def _sroll(x, d):
    return pltpu.roll(x, d, 0)


def _sroll_fwd(x, d):
    return pltpu.roll(x, d, 0), None


def _sroll_bwd(d, _, g):
    n = g.shape[0]
    return (pltpu.roll(g, (n - d) % n, 0),)


_sroll.defvjp(_sroll_fwd, _sroll_bwd)


def _mesh_pos():
    return lax.axis_index("x"), lax.axis_index("y"), lax.axis_index("c")


class _Gather:
    def __init__(self, ins, outs, send_sems, recv_sems, local_sems):
        self.ins, self.outs = ins, outs
        self.send_sems, self.recv_sems, self.local_sems = send_sems, recv_sems, local_sems
        x, y, c = _mesh_pos()
        self.x, self.y, self.c = x, y, c
        self.me, self.sibling = (x, y, c), (x, y, 1 - c)
        self.chips = [(1 - x, y), (x, 1 - y), (1 - x, 1 - y)]

    def copy(self, a, k, block, to, src=None):
        slot = self.outs[a].at[4 * block[0] + 2 * block[1] + block[2]]
        return pltpu.make_async_remote_copy(
            src_ref=slot if src is None else src, dst_ref=slot,
            send_sem=self.send_sems.at[a, k], recv_sem=self.recv_sems.at[a, k],
            device_id=to, device_id_type=pl.DeviceIdType.MESH)

    def mine(self, a):
        return pltpu.make_async_copy(self.ins[a], self.outs[a].at[4 * self.x + 2 * self.y + self.c], self.local_sems.at[a])

    def first(self, a):
        return [self.copy(a, 0, self.me, self.sibling, src=self.ins[a])] + [
            self.copy(a, 1 + j, self.me, (*chip, self.c), src=self.ins[a]) for j, chip in enumerate(self.chips)]

    def start(self):
        for a in range(len(self.ins)):
            self.mine(a).start()
            for cp in self.first(a):
                cp.start()

    def finish(self):
        c = self.c
        for a in range(len(self.ins)):
            passed = [self.copy(a, 4 + j, (*chip, c), self.sibling) for j, chip in enumerate(self.chips)]
            for j, chip in enumerate(self.chips):
                self.copy(a, 1 + j, (*chip, c), self.me).wait_recv()
                passed[j].start()
            self.copy(a, 0, self.sibling, self.me).wait_recv()
            for j, chip in enumerate(self.chips):
                self.copy(a, 4 + j, (*chip, 1 - c), self.me).wait_recv()
            for cp in self.first(a) + passed:
                cp.wait_send()
            self.mine(a).wait()


def _all_gather(shards, name):
    n = len(shards)
    any_spec = pl.BlockSpec(memory_space=pl.ANY)

    def body(*refs):
        g = _Gather(refs[:n], refs[n:2 * n], *refs[2 * n:])
        g.start()
        g.finish()

    outs = pl.pallas_call(
        body, name=name,
        out_shape=[jax.ShapeDtypeStruct((N_DEV,) + s.shape, s.dtype) for s in shards],
        in_specs=[any_spec] * n, out_specs=[any_spec] * n,
        scratch_shapes=[pltpu.SemaphoreType.DMA((n, 7)), pltpu.SemaphoreType.DMA((n, 7)),
                        pltpu.SemaphoreType.DMA((n,))],
    )(*shards)
    return list(outs)


def _a2a_copies(ins, outs, send_sems, recv_sems, local_sems):
    x, y, c = _mesh_pos()
    me = 4 * x + 2 * y + c
    copies = []
    for a in range(len(ins)):
        copies.append(pltpu.make_async_copy(ins[a].at[me], outs[a].at[me], local_sems.at[a]))
        for k in range(1, N_DEV):
            px = 1 - x if k & 4 else x
            py = 1 - y if k & 2 else y
            pc = 1 - c if k & 1 else c
            copies.append(pltpu.make_async_remote_copy(
                src_ref=ins[a].at[4 * px + 2 * py + pc], dst_ref=outs[a].at[me],
                send_sem=send_sems.at[a, k - 1], recv_sem=recv_sems.at[a, k - 1],
                device_id=(px, py, pc), device_id_type=pl.DeviceIdType.MESH))
    return copies


def _host_call(body, sends, args, *, name, grid, out_shape, in_specs, out_specs, scratch_shapes, compiler_params,
               gather=False):
    single = not isinstance(out_shape, (list, tuple))
    out_shape = [out_shape] if single else list(out_shape)
    out_specs = [out_specs] if single else list(out_specs)
    sends = list(sends or ())
    n, n_in, n_out, n_scr = len(sends), len(args), len(out_shape), len(scratch_shapes)
    if n == 0:
        outs = pl.pallas_call(body, name=name, grid=grid, out_shape=out_shape, in_specs=in_specs, out_specs=out_specs,
                              scratch_shapes=list(scratch_shapes), compiler_params=compiler_params)(*args)
        return (outs[0] if single else list(outs)), []
    any_spec = pl.BlockSpec(memory_space=pl.ANY)

    def hosted(*refs):
        ins, s_in = refs[:n_in], refs[n_in:n_in + n]
        pos = n_in + n
        outs, s_out = refs[pos:pos + n_out], refs[pos + n_out:pos + n_out + n]
        pos += n_out + n
        scr, sems = refs[pos:pos + n_scr], refs[pos + n_scr:]
        ids = [pl.program_id(i) for i in range(len(grid))]
        first = functools.reduce(lambda p, q: p & q, [i == 0 for i in ids])
        last = functools.reduce(lambda p, q: p & q, [i == g - 1 for i, g in zip(ids, grid)])

        @pl.when(first)
        def _():
            if gather:
                _Gather(s_in, s_out, *sems).start()
            else:
                for cp in _a2a_copies(s_in, s_out, *sems):
                    cp.start()

        body(*ins, *outs, *scr)

        @pl.when(last)
        def _():
            if gather:
                _Gather(s_in, s_out, *sems).finish()
            else:
                for cp in _a2a_copies(s_in, s_out, *sems):
                    cp.wait()

    lead = (N_DEV,) if gather else ()
    outs = pl.pallas_call(
        hosted, name=name, grid=grid,
        out_shape=out_shape + [jax.ShapeDtypeStruct(lead + s.shape, s.dtype) for s in sends],
        in_specs=list(in_specs) + [any_spec] * n, out_specs=out_specs + [any_spec] * n,
        scratch_shapes=list(scratch_shapes) + [pltpu.SemaphoreType.DMA((n, 7)), pltpu.SemaphoreType.DMA((n, 7)),
                                               pltpu.SemaphoreType.DMA((n,))],
        compiler_params=compiler_params,
    )(*args, *sends)
    main = list(outs[:n_out])
    return (main[0] if single else main), list(outs[n_out:])


def _matmul(a, b, *, mode="nn", out_dtype=f32, res=None, sends=None, gather=False, name):
    if mode == "tn":
        K, M = a.shape
    else:
        M, K = a.shape
    N = b.shape[0] if mode == "nt" else b.shape[1]
    k_cap = 2048 if (a.dtype == bf16 and b.dtype == bf16) else 1024
    tm, tn, tk = _tile(M, 1024), _tile(N, 1152), _tile(K, k_cap)
    nk = K // tk
    dims = {"nn": (((1,), (0,)), ((), ())), "nt": (((1,), (1,)), ((), ())), "tn": (((0,), (0,)), ((), ()))}[mode]

    def body(*refs):
        a_ref, b_ref = refs[:2]
        r_ref = refs[2] if res is not None else None
        o_ref = refs[3] if res is not None else refs[2]
        acc = refs[-1] if nk > 1 else None
        k = pl.program_id(2)
        part = lax.dot_general(a_ref[...].astype(bf16), b_ref[...].astype(bf16), dims, preferred_element_type=f32)

        def finish(r):
            if res is not None:
                r = r + r_ref[...]
            o_ref[...] = r.astype(o_ref.dtype)

        if nk == 1:
            finish(part)
            return

        @pl.when(k == 0)
        def _():
            acc[...] = part

        @pl.when((k > 0) & (k < nk - 1))
        def _():
            acc[...] += part

        @pl.when(k == nk - 1)
        def _():
            finish(acc[...] + part)

    a_spec = pl.BlockSpec((tk, tm), lambda i, j, k: (k, i)) if mode == "tn" else pl.BlockSpec((tm, tk), lambda i, j, k: (i, k))
    b_spec = pl.BlockSpec((tn, tk), lambda i, j, k: (j, k)) if mode == "nt" else pl.BlockSpec((tk, tn), lambda i, j, k: (k, j))
    o_spec = pl.BlockSpec((tm, tn), lambda i, j, k: (i, j))
    in_specs = [a_spec, b_spec] + ([o_spec] if res is not None else [])
    args = (a, b) + ((res,) if res is not None else ())
    out, recvs = _host_call(
        body, sends, args, name=name, grid=(M // tm, N // tn, nk),
        out_shape=jax.ShapeDtypeStruct((M, N), out_dtype),
        in_specs=in_specs, out_specs=o_spec,
        scratch_shapes=[pltpu.VMEM((tm, tn), f32)] if nk > 1 else [],
        compiler_params=_params(("parallel", "parallel", "arbitrary")), gather=gather)
    return out if sends is None else (out, recvs)


def _rms(x, g):
    return x * lax.rsqrt(jnp.mean(x * x, axis=-1, keepdims=True) + NORM_EPS) * g


def _rms_fwd(x, g, name):
    L, D = x.shape
    tm = _tile(L, 256, 8)

    def body(x_ref, g_ref, o_ref):
        o_ref[...] = _rms(x_ref[...], g_ref[...]).astype(bf16)

    return pl.pallas_call(
        body, name=name, grid=(L // tm,), out_shape=jax.ShapeDtypeStruct((L, D), bf16),
        in_specs=[pl.BlockSpec((tm, D), lambda i: (i, 0)), pl.BlockSpec((1, D), lambda i: (0, 0))],
        out_specs=pl.BlockSpec((tm, D), lambda i: (i, 0)),
        compiler_params=_params(("parallel",)),
    )(x, g)


def _rms_bwd(x, g, dh, dres, name):
    L, D = x.shape
    tm = _tile(L, 256, 8)

    def body(x_ref, g_ref, dh_ref, dres_ref, dx_ref, dxb_ref, dg_ref):
        _, vjp = jax.vjp(_rms, x_ref[...], g_ref[...])
        dx, dg = vjp(dh_ref[...])
        dx = dres_ref[...] + dx
        dx_ref[...] = dx
        dxb_ref[...] = dx.astype(bf16)

        @pl.when(pl.program_id(0) == 0)
        def _():
            dg_ref[...] = jnp.zeros_like(dg_ref)

        dg_ref[...] += dg

    row = pl.BlockSpec((tm, D), lambda i: (i, 0))
    vec = pl.BlockSpec((1, D), lambda i: (0, 0))
    return pl.pallas_call(
        body, name=name, grid=(L // tm,),
        out_shape=[jax.ShapeDtypeStruct((L, D), f32), jax.ShapeDtypeStruct((L, D), bf16), jax.ShapeDtypeStruct((1, D), f32)],
        in_specs=[row, vec, row, row], out_specs=[row, row, vec],
        compiler_params=_params(("arbitrary",)),
    )(x, g, dh, dres)


def _final_loss(x, g, target, name):
    L, D = x.shape
    tm = _tile(L, 256, 8)

    def body(x_ref, g_ref, t_ref, loss_ref, dx_ref, dxb_ref, dg_ref):
        tgt = t_ref[...]

        def f(xv, gv):
            err = _rms(xv, gv) - tgt
            return 0.5 * jnp.sum(jnp.mean(err * err, axis=-1, keepdims=True), axis=0, keepdims=True)

        val, vjp = jax.vjp(f, x_ref[...], g_ref[...])
        dx, dg = vjp(jnp.ones((1, 1), f32))
        dx_ref[...] = dx
        dxb_ref[...] = dx.astype(bf16)

        @pl.when(pl.program_id(0) == 0)
        def _():
            dg_ref[...] = jnp.zeros_like(dg_ref)
            loss_ref[...] = jnp.zeros_like(loss_ref)

        dg_ref[...] += dg
        loss_ref[...] += jnp.broadcast_to(val, loss_ref.shape)

    row = pl.BlockSpec((tm, D), lambda i: (i, 0))
    vec = pl.BlockSpec((1, D), lambda i: (0, 0))
    acc = pl.BlockSpec((8, 128), lambda i: (0, 0))
    loss, dx, dxb, dg = pl.pallas_call(
        body, name=name, grid=(L // tm,),
        out_shape=[jax.ShapeDtypeStruct((8, 128), f32), jax.ShapeDtypeStruct((L, D), f32), jax.ShapeDtypeStruct((L, D), bf16),
                   jax.ShapeDtypeStruct((1, D), f32)],
        in_specs=[row, vec, row], out_specs=[acc, row, row, vec],
        compiler_params=_params(("arbitrary",)),
    )(x, g, target)
    return loss[0, 0], dx, dxb, dg


def _swiglu_f(a, b):
    return jax.nn.silu(a) * b


def _swiglu_fwd(a, b, name):
    L, F = a.shape
    tm, tn = _tile(L, 512, 8), _tile(F, 1024)

    def body(a_ref, b_ref, o_ref):
        o_ref[...] = _swiglu_f(a_ref[...], b_ref[...]).astype(bf16)

    blk = pl.BlockSpec((tm, tn), lambda i, j: (i, j))
    return pl.pallas_call(
        body, name=name, grid=(L // tm, F // tn), out_shape=jax.ShapeDtypeStruct((L, F), bf16),
        in_specs=[blk, blk], out_specs=blk, compiler_params=_params(("parallel", "parallel")),
    )(a, b)


def _swiglu_bwd(a, b, dact, name):
    L, F = a.shape
    tm, tn = _tile(L, 512, 8), _tile(F, 1024)

    def body(a_ref, b_ref, d_ref, da_ref, db_ref):
        _, vjp = jax.vjp(_swiglu_f, a_ref[...], b_ref[...])
        da, db = vjp(d_ref[...])
        da_ref[...] = da.astype(bf16)
        db_ref[...] = db.astype(bf16)

    blk = pl.BlockSpec((tm, tn), lambda i, j: (i, j))
    return pl.pallas_call(
        body, name=name, grid=(L // tm, F // tn),
        out_shape=[jax.ShapeDtypeStruct((L, F), bf16)] * 2,
        in_specs=[blk, blk, blk], out_specs=[blk, blk], compiler_params=_params(("parallel", "parallel")),
    )(a, b, dact)


def _merge_f(g0, g1, g2, g3, y0, y1, y2, y3):
    s = jax.nn.sigmoid
    return s(g0) * y0 + s(g1) * y1 + s(g2) * y2 + s(g3) * y3


def _merge_fwd(proj, ys, D, name):
    L = proj.shape[0]
    tm, tn = _tile(L, 512, 8), _tile(D, 512)
    nj = D // tn

    def body(g0, g1, g2, g3, y0, y1, y2, y3, o_ref):
        o_ref[...] = _merge_f(g0[...], g1[...], g2[...], g3[...], y0[...], y1[...], y2[...], y3[...]).astype(bf16)

    gspecs = [pl.BlockSpec((tm, tn), functools.partial(lambda i, j, m: (i, m * nj + j), m=m)) for m in range(4)]
    blk = pl.BlockSpec((tm, tn), lambda i, j: (i, j))
    return pl.pallas_call(
        body, name=name, grid=(L // tm, nj), out_shape=jax.ShapeDtypeStruct((L, D), bf16),
        in_specs=gspecs + [blk] * 4, out_specs=blk, compiler_params=_params(("parallel", "parallel")),
    )(proj, proj, proj, proj, *ys)


def _merge_bwd(proj, ys, dmixed, D, name):
    L = proj.shape[0]
    tm, tn = _tile(L, 512, 8), _tile(D, 512)
    nj = D // tn

    def body(g0, g1, g2, g3, y0, y1, y2, y3, d_ref, dy0, dy1, dy2, dy3, dg_ref):
        _, vjp = jax.vjp(_merge_f, g0[...], g1[...], g2[...], g3[...], y0[...], y1[...], y2[...], y3[...])
        grads = vjp(d_ref[...])
        for m, r in enumerate((dy0, dy1, dy2, dy3)):
            r[...] = grads[4 + m].astype(bf16)
        for m in range(4):
            dg_ref[m] = grads[m].astype(bf16)

    gspecs = [pl.BlockSpec((tm, tn), functools.partial(lambda i, j, m: (i, m * nj + j), m=m)) for m in range(4)]
    blk = pl.BlockSpec((tm, tn), lambda i, j: (i, j))
    dgspec = pl.BlockSpec((4, tm, tn), lambda i, j: (0, i, j))
    outs = pl.pallas_call(
        body, name=name, grid=(L // tm, nj),
        out_shape=[jax.ShapeDtypeStruct((L, D), bf16)] * 4 + [jax.ShapeDtypeStruct((4, L, D), bf16)],
        in_specs=gspecs + [blk] * 5, out_specs=[blk] * 4 + [dgspec],
        compiler_params=_params(("parallel", "parallel")),
    )(proj, proj, proj, proj, *ys, dmixed)
    return outs[:4], outs[4]


def _gla_head(q, k, v, r, glr, st, a2, ab, ng):
    C, T = GLA_CHUNK, GLA_SUB
    row = lax.broadcasted_iota(jnp.int32, (C, C), 0)
    col = lax.broadcasted_iota(jnp.int32, (C, C), 1)
    tri = (col <= row).astype(f32)
    sel = (col == (row // T) * T).astype(f32)
    z = jnp.dot(glr, a2, preferred_element_type=f32) + ab
    g = jax.nn.log_sigmoid(z) / GLA_TAU
    cum = jnp.dot(tri, g, precision=HI, preferred_element_type=f32)
    excl = cum - g
    ref = jnp.dot(sel, excl, precision=HI, preferred_element_type=f32)
    qs = q * (GLA_DK ** -0.5)
    q_ref = qs * jnp.exp(cum - ref)
    rowk = lax.broadcasted_iota(jnp.int32, (C, GLA_DK), 0)
    a = jnp.zeros((C, C), f32)
    for s in range(1, C // T):
        ref_s = jnp.sum(jnp.where(rowk == s * T, excl, 0.0), axis=0, keepdims=True)
        k_ref = k * jnp.exp(jnp.where(rowk < s * T, ref_s - cum, -jnp.inf))
        a_s = lax.dot_general(q_ref, k_ref, (((1,), (1,)), ((), ())), preferred_element_type=f32)
        a = a + jnp.where(row // T == s, a_s, 0.0)
    o = jnp.dot(a, v, preferred_element_type=f32)
    sub = rowk % T
    for d in range(T):
        ks = _sroll(k, d) if d else k
        cs = _sroll(cum, d) if d else cum
        vs = _sroll(v, d) if d else v
        w = jnp.sum(qs * ks * jnp.exp(jnp.where(sub >= d, cum - cs, -jnp.inf)), axis=-1, keepdims=True)
        o = o + w * vs
    o = o + lax.dot_general(qs * jnp.exp(cum), st, (((1,), (1,)), ((), ())), preferred_element_type=f32)
    last = jnp.sum(jnp.where(rowk == C - 1, cum, 0.0), axis=0, keepdims=True)
    st_new = st * jnp.exp(last) + lax.dot_general(v, k * jnp.exp(last - cum), (((0,), (0,)), ((), ())),
                                                  preferred_element_type=f32)
    out = _rms(o, ng) * jax.nn.silu(r)
    return out, st_new


def _gla_specs(L, base128, rev):
    n = L // GLA_CHUNK
    ch = (lambda i: n - 1 - i) if rev else (lambda i: i)
    b = base128
    return n, ch, [
        pl.BlockSpec((GLA_CHUNK, 256), lambda i: (ch(i), (b + P_GQ // 128) // 2)),
        pl.BlockSpec((GLA_CHUNK, 256), lambda i: (ch(i), (b + P_GK // 128) // 2)),
        pl.BlockSpec((GLA_CHUNK, 512), lambda i: (ch(i), (b + P_GV // 128) // 4)),
        pl.BlockSpec((GLA_CHUNK, 512), lambda i: (ch(i), (b + P_GR // 128) // 4)),
        pl.BlockSpec((GLA_CHUNK, 128), lambda i: (ch(i), b + P_GLR // 128)),
    ]


def _gla_fwd(proj, a2p, ab, ng, base128, name, shards=None):
    L = proj.shape[0]
    n, _, pspecs = _gla_specs(L, base128, False)
    H, DK, DV = GLA_HEADS, GLA_DK, GLA_DV

    def body(q_ref, k_ref, v_ref, r_ref, l_ref, a2_ref, ab_ref, ng_ref, o_ref, st_ref, state):
        @pl.when(pl.program_id(0) == 0)
        def _():
            state[...] = jnp.zeros_like(state)

        st_ref[0] = state[...]
        glr = l_ref[...]
        for h in range(H):
            kk, vv = slice(h * DK, (h + 1) * DK), slice(h * DV, (h + 1) * DV)
            out, st_new = _gla_head(q_ref[:, kk], k_ref[:, kk], v_ref[:, vv], r_ref[:, vv], glr, state[h],
                                    a2_ref[:, kk], ab_ref[:, kk], ng_ref[:, vv])
            o_ref[:, vv] = out.astype(bf16)
            state[h] = st_new

    full = lambda shape: pl.BlockSpec(shape, lambda i: (0,) * len(shape))
    return _host_call(
        body, shards, (proj, proj, proj, proj, proj, a2p, ab, ng), name=name, grid=(n,),
        out_shape=[jax.ShapeDtypeStruct((L, H * DV), bf16), jax.ShapeDtypeStruct((n, H, DV, DK), f32)],
        in_specs=pspecs + [full((128, 256)), full((1, 256)), full((1, 512))],
        out_specs=[pl.BlockSpec((GLA_CHUNK, 512), lambda i: (i, 0)), pl.BlockSpec((1, H, DV, DK), lambda i: (i, 0, 0, 0))],
        scratch_shapes=[pltpu.VMEM((H, DV, DK), f32)],
        compiler_params=_params(("arbitrary",)), gather=True)


def _gla_bwd(proj, a2p, ab, ng, states, dout, base128, name, sends=None):
    L = proj.shape[0]
    n, ch, pspecs = _gla_specs(L, base128, True)
    H, DK, DV = GLA_HEADS, GLA_DK, GLA_DV

    def body(q_ref, k_ref, v_ref, r_ref, l_ref, a2_ref, ab_ref, ng_ref, st_ref, do_ref,
             dq_ref, dk_ref, dv_ref, dr_ref, dl_ref, da2_ref, dab_ref, dng_ref, dstate):
        @pl.when(pl.program_id(0) == 0)
        def _():
            dstate[...] = jnp.zeros_like(dstate)
            da2_ref[...] = jnp.zeros_like(da2_ref)
            dab_ref[...] = jnp.zeros_like(dab_ref)
            dng_ref[...] = jnp.zeros_like(dng_ref)

        glr = l_ref[...]
        dglr = jnp.zeros(glr.shape, f32)
        for h in range(H):
            kk, vv = slice(h * DK, (h + 1) * DK), slice(h * DV, (h + 1) * DV)
            _, vjp = jax.vjp(_gla_head, q_ref[:, kk], k_ref[:, kk], v_ref[:, vv], r_ref[:, vv], glr, st_ref[0, h],
                             a2_ref[:, kk], ab_ref[:, kk], ng_ref[:, vv])
            dq, dk, dv, dr, dl, dst, da2, dab, dng = vjp((do_ref[:, vv].astype(f32), dstate[h]))
            dq_ref[:, kk] = dq.astype(bf16)
            dk_ref[:, kk] = dk.astype(bf16)
            dv_ref[:, vv] = dv.astype(bf16)
            dr_ref[:, vv] = dr.astype(bf16)
            dglr = dglr + dl
            dstate[h] = dst
            da2_ref[:, kk] += da2
            dab_ref[:, kk] += dab
            dng_ref[:, vv] += dng
        dl_ref[...] = dglr.astype(bf16)

    full = lambda shape: pl.BlockSpec(shape, lambda i: (0,) * len(shape))
    rowspec = lambda w: pl.BlockSpec((GLA_CHUNK, w), lambda i: (ch(i), 0))
    return _host_call(
        body, sends, (proj, proj, proj, proj, proj, a2p, ab, ng, states, dout), name=name, grid=(n,),
        out_shape=[jax.ShapeDtypeStruct((L, 256), bf16), jax.ShapeDtypeStruct((L, 256), bf16),
                   jax.ShapeDtypeStruct((L, 512), bf16), jax.ShapeDtypeStruct((L, 512), bf16),
                   jax.ShapeDtypeStruct((L, 128), bf16), jax.ShapeDtypeStruct((128, 256), f32),
                   jax.ShapeDtypeStruct((1, 256), f32), jax.ShapeDtypeStruct((1, 512), f32)],
        in_specs=pspecs + [full((128, 256)), full((1, 256)), full((1, 512)),
                           pl.BlockSpec((1, H, DV, DK), lambda i: (ch(i), 0, 0, 0)), rowspec(512)],
        out_specs=[rowspec(256), rowspec(256), rowspec(512), rowspec(512), rowspec(128),
                   full((128, 256)), full((1, 256)), full((1, 512))],
        scratch_shapes=[pltpu.VMEM((H, DV, DK), f32)],
        compiler_params=_params(("arbitrary",)))


def _s5_prep(lam_re, lam_im, log_dt, b_re, b_im, c_re, c_im, d):
    G, N, Cn = S5_GROUPS, S5_STATE, S5_GROUP
    J, GB = S5_LANE_BLOCKS, S5_GROUPS // S5_LANE_BLOCKS
    dt = jnp.exp(log_dt)[:, None]
    mag = jnp.exp(lam_re * dt)
    ab_re, ab_im = mag * jnp.cos(lam_im * dt), mag * jnp.sin(lam_im * dt)
    den = lam_re * lam_re + lam_im * lam_im
    z_re = ((ab_re - 1.0) * lam_re + ab_im * lam_im) / den
    z_im = (ab_im * lam_re - (ab_re - 1.0) * lam_im) / den
    bb_re = z_re[..., None] * b_re - z_im[..., None] * b_im
    bb_im = z_re[..., None] * b_im + z_im[..., None] * b_re
    eye = jnp.eye(GB, dtype=f32)

    def in_blocks(bb):
        return jnp.einsum("jgnc,gh->jgchn", bb.reshape(J, GB, N, Cn), eye).reshape(J, GB * Cn, GB * N)

    def out_blocks(cc):
        return jnp.einsum("jgcn,gh->jgnhc", cc.reshape(J, GB, Cn, N), eye).reshape(J, GB * N, GB * Cn)

    return (ab_re.reshape(1, G * N), ab_im.reshape(1, G * N), in_blocks(bb_re), in_blocks(bb_im),
            out_blocks(c_re), out_blocks(c_im), d.reshape(1, G * Cn))


def _s5_chunk(u, hin_r, hin_i, a_r, a_i, bb_r, bb_i, cc_r, cc_i, dvec):
    T = u.shape[0]
    hr = jnp.dot(u, bb_r, preferred_element_type=f32)
    hi = jnp.dot(u, bb_i, preferred_element_type=f32)
    row = lax.broadcasted_iota(jnp.int32, hr.shape, 0)
    hr = hr + jnp.where(row == 0, a_r * hin_r - a_i * hin_i, 0.0)
    hi = hi + jnp.where(row == 0, a_r * hin_i + a_i * hin_r, 0.0)
    pr, pi = a_r, a_i
    d = 1
    while d < T:
        sr = jnp.where(row >= d, _sroll(hr, d), 0.0)
        si = jnp.where(row >= d, _sroll(hi, d), 0.0)
        hr, hi = hr + pr * sr - pi * si, hi + pr * si + pi * sr
        pr, pi = pr * pr - pi * pi, 2.0 * pr * pi
        d *= 2
    y = (jnp.dot(hr, cc_r, preferred_element_type=f32)
         - jnp.dot(hi, cc_i, preferred_element_type=f32) + dvec * u)
    out_r = jnp.sum(jnp.where(row == T - 1, hr, 0.0), axis=0, keepdims=True)
    out_i = jnp.sum(jnp.where(row == T - 1, hi, 0.0), axis=0, keepdims=True)
    return y, out_r, out_i


def _s5_specs(L, base128, rev):
    T, J = S5_CHUNK, S5_LANE_BLOCKS
    n = L // T
    ch = (lambda c: n - 1 - c) if rev else (lambda c: c)
    ub = base128 + P_S5U // 128
    specs = [
        pl.BlockSpec((T, 128), lambda j, c: (ch(c), ub + j)),
        pl.BlockSpec((1, 512), lambda j, c: (0, j)), pl.BlockSpec((1, 512), lambda j, c: (0, j)),
        pl.BlockSpec((None, 128, 512), lambda j, c: (j, 0, 0)), pl.BlockSpec((None, 128, 512), lambda j, c: (j, 0, 0)),
        pl.BlockSpec((None, 512, 128), lambda j, c: (j, 0, 0)), pl.BlockSpec((None, 512, 128), lambda j, c: (j, 0, 0)),
        pl.BlockSpec((1, 128), lambda j, c: (0, j)),
    ]
    return n, ch, specs


def _s5_fwd(proj, prep, base128, name, shards=None):
    L = proj.shape[0]
    T, J = S5_CHUNK, S5_LANE_BLOCKS
    n, _, specs = _s5_specs(L, base128, False)

    def body(u_ref, ar, ai, bbr, bbi, ccr, cci, dv, y_ref, sr_ref, si_ref, carry):
        @pl.when(pl.program_id(1) == 0)
        def _():
            carry[...] = jnp.zeros_like(carry)

        hin_r, hin_i = carry[0:1, :], carry[1:2, :]
        sr_ref[0] = jnp.broadcast_to(hin_r, (8, 512))
        si_ref[0] = jnp.broadcast_to(hin_i, (8, 512))
        y, out_r, out_i = _s5_chunk(u_ref[...], hin_r, hin_i, ar[...], ai[...], bbr[...], bbi[...], ccr[...], cci[...], dv[...])
        y_ref[...] = y
        carry[0:1, :] = out_r
        carry[1:2, :] = out_i

    st = pl.BlockSpec((1, 8, 512), lambda j, c: (c, 0, j))
    return _host_call(
        body, shards, (proj, *prep), name=name, grid=(J, n),
        out_shape=[jax.ShapeDtypeStruct((L, 512), f32), jax.ShapeDtypeStruct((n, 8, 2048), f32), jax.ShapeDtypeStruct((n, 8, 2048), f32)],
        in_specs=specs, out_specs=[pl.BlockSpec((T, 128), lambda j, c: (c, j)), st, st],
        scratch_shapes=[pltpu.VMEM((8, 512), f32)],
        compiler_params=_params(("parallel", "arbitrary")), gather=True)


def _s5_bwd(proj, prep, st_r, st_i, dy, base128, name, sends=None):
    L = proj.shape[0]
    T, J = S5_CHUNK, S5_LANE_BLOCKS
    n, ch, specs = _s5_specs(L, base128, True)

    def body(u_ref, ar, ai, bbr, bbi, ccr, cci, dv, sr_ref, si_ref, dy_ref,
             du_ref, dar, dai, dbbr, dbbi, dccr, dcci, ddv, dcarry):
        @pl.when(pl.program_id(1) == 0)
        def _():
            dcarry[...] = jnp.zeros_like(dcarry)
            for r in (dar, dai, dbbr, dbbi, dccr, dcci, ddv):
                r[...] = jnp.zeros_like(r)

        _, vjp = jax.vjp(_s5_chunk, u_ref[...], sr_ref[0, 0:1, :], si_ref[0, 0:1, :], ar[...], ai[...],
                         bbr[...], bbi[...], ccr[...], cci[...], dv[...])
        g = vjp((dy_ref[...], dcarry[0:1, :], dcarry[1:2, :]))
        du_ref[...] = g[0].astype(bf16)
        dcarry[0:1, :] = g[1]
        dcarry[1:2, :] = g[2]
        for r, val in zip((dar, dai, dbbr, dbbi, dccr, dcci, ddv), g[3:]):
            r[...] += val

    st = pl.BlockSpec((1, 8, 512), lambda j, c: (ch(c), 0, j))
    outs, recvs = _host_call(
        body, sends, (proj, *prep, st_r, st_i, dy), name=name, grid=(J, n),
        out_shape=[jax.ShapeDtypeStruct((L, 512), bf16),
                   jax.ShapeDtypeStruct((1, 2048), f32), jax.ShapeDtypeStruct((1, 2048), f32),
                   jax.ShapeDtypeStruct((J, 128, 512), f32), jax.ShapeDtypeStruct((J, 128, 512), f32),
                   jax.ShapeDtypeStruct((J, 512, 128), f32), jax.ShapeDtypeStruct((J, 512, 128), f32),
                   jax.ShapeDtypeStruct((1, 512), f32)],
        in_specs=specs + [st, st, pl.BlockSpec((T, 128), lambda j, c: (ch(c), j))],
        out_specs=[pl.BlockSpec((T, 128), lambda j, c: (ch(c), j))] + specs[1:],
        scratch_shapes=[pltpu.VMEM((8, 512), f32)],
        compiler_params=_params(("parallel", "arbitrary")))
    return outs[0], tuple(outs[1:]), recvs


def _glu_f(y, w, b):
    z = jax.nn.gelu(y)
    return z * jax.nn.sigmoid(jnp.dot(z.astype(bf16), w.astype(bf16), preferred_element_type=f32) + b)


def _glu_fwd(y, w, b, name):
    L = y.shape[0]
    tm = _tile(L, 512, 8)

    def body(y_ref, w_ref, b_ref, o_ref):
        o_ref[...] = _glu_f(y_ref[...], w_ref[...], b_ref[...]).astype(bf16)

    row = pl.BlockSpec((tm, 512), lambda i: (i, 0))
    return pl.pallas_call(
        body, name=name, grid=(L // tm,), out_shape=jax.ShapeDtypeStruct((L, 512), bf16),
        in_specs=[row, pl.BlockSpec((512, 512), lambda i: (0, 0)), pl.BlockSpec((1, 512), lambda i: (0, 0))],
        out_specs=row, compiler_params=_params(("parallel",)),
    )(y, w, b)


def _glu_bwd(y, w, b, dout, name):
    L = y.shape[0]
    tm = _tile(L, 512, 8)

    def body(y_ref, w_ref, b_ref, do_ref, dy_ref, dw_ref, db_ref):
        @pl.when(pl.program_id(0) == 0)
        def _():
            dw_ref[...] = jnp.zeros_like(dw_ref)
            db_ref[...] = jnp.zeros_like(db_ref)

        _, vjp = jax.vjp(_glu_f, y_ref[...], w_ref[...], b_ref[...])
        dy, dw, db = vjp(do_ref[...])
        dy_ref[...] = dy
        dw_ref[...] += dw
        db_ref[...] += db

    row = pl.BlockSpec((tm, 512), lambda i: (i, 0))
    wspec, bspec = pl.BlockSpec((512, 512), lambda i: (0, 0)), pl.BlockSpec((1, 512), lambda i: (0, 0))
    return pl.pallas_call(
        body, name=name, grid=(L // tm,),
        out_shape=[jax.ShapeDtypeStruct((L, 512), f32), jax.ShapeDtypeStruct((512, 512), f32), jax.ShapeDtypeStruct((1, 512), f32)],
        in_specs=[row, wspec, bspec, row], out_specs=[row, wspec, bspec],
        compiler_params=_params(("arbitrary",)),
    )(y, w, b, dout)


def _rope_tables(positions):
    half = ROPE_DIM // 2
    inv_freq = ROPE_THETA ** (-jnp.arange(half, dtype=f32) / half)
    ang = positions.astype(f32)[:, None] * inv_freq
    L = positions.shape[0]
    cos = jnp.concatenate([jnp.cos(ang), jnp.cos(ang), jnp.ones((L, HEAD_DIM - ROPE_DIM), f32)], axis=1)
    sin = jnp.concatenate([jnp.sin(ang), jnp.sin(ang), jnp.zeros((L, HEAD_DIM - ROPE_DIM), f32)], axis=1)
    return cos, sin


def _rot_matrix():
    half = ROPE_DIM // 2
    r = lax.broadcasted_iota(jnp.int32, (HEAD_DIM, HEAD_DIM), 0)
    c = lax.broadcasted_iota(jnp.int32, (HEAD_DIM, HEAD_DIM), 1)
    return jnp.where((c < half) & (r == c + half), -1.0, 0.0) + jnp.where((c >= half) & (c < ROPE_DIM) & (r == c - half), 1.0, 0.0)


def _attn_head(q, kp, kc, vp, vc, sink, *, cq, sq, ck, sk, lim, max_dist, use_rope):
    T = ATT_BLOCK
    k2 = jnp.concatenate([kp, kc], axis=0)
    v2 = jnp.concatenate([vp, vc], axis=0)
    if use_rope:
        rot = _rot_matrix()
        q = q * cq + jnp.dot(q, rot, preferred_element_type=f32) * sq
        k2 = k2 * ck + jnp.dot(k2, rot, preferred_element_type=f32) * sk
    s = lax.dot_general(q, k2, (((1,), (1,)), ((), ())), preferred_element_type=f32) * (HEAD_DIM ** -0.5)
    t = lax.broadcasted_iota(jnp.int32, (T, 2 * T), 0)
    j = lax.broadcasted_iota(jnp.int32, (T, 2 * T), 1)
    dist = T + t - j
    valid = (dist >= 0) & (dist <= max_dist) & (j >= lim)
    s = jnp.where(valid, s, -jnp.inf)
    m = lax.stop_gradient(jnp.max(s, axis=-1, keepdims=True))
    p = jnp.exp(s - m)
    den = jnp.sum(p, axis=-1, keepdims=True)
    o = jnp.dot(p, v2, preferred_element_type=f32) / den
    lse = jnp.broadcast_to(m + jnp.log(den), (T, HEAD_DIM))
    if sink is None:
        return o, lse
    return o * jax.nn.sigmoid(lse - sink)


def _attn_specs(L, q_col, k_col, v_col, wk, rev):
    T = ATT_BLOCK
    n = L // T
    blk = (lambda i: n - 1 - i) if rev else (lambda i: i)
    prev = lambda i: jnp.maximum(blk(i) - 1, 0)
    specs = [
        pl.BlockSpec((T, 512), lambda i: (blk(i), q_col)),
        pl.BlockSpec((T, wk), lambda i: (prev(i), k_col)), pl.BlockSpec((T, wk), lambda i: (blk(i), k_col)),
        pl.BlockSpec((T, wk), lambda i: (prev(i), v_col)), pl.BlockSpec((T, wk), lambda i: (blk(i), v_col)),
        pl.BlockSpec((T, 64), lambda i: (prev(i), 0)), pl.BlockSpec((T, 64), lambda i: (blk(i), 0)),
        pl.BlockSpec((T, 64), lambda i: (prev(i), 0)), pl.BlockSpec((T, 64), lambda i: (blk(i), 0)),
    ]
    return n, blk, specs


def _attn_fwd(qa, ka, va, cos, sin, sinks, *, q_col, k_col, v_col, hkv, nbc, max_dist, name, shards=None):
    L = qa.shape[0]
    T, HQ, HD = ATT_BLOCK, 8, HEAD_DIM
    wk = hkv * HD
    n, _, specs = _attn_specs(L, q_col, k_col, v_col, wk, False)
    grp = HQ // hkv
    gated = sinks is not None

    def body(*refs):
        q_ref, kp_ref, kc_ref, vp_ref, vc_ref, cp_ref, cc_ref, sp_ref, sc_ref = refs[:9]
        rest = refs[9:]
        lim = jnp.where(pl.program_id(0) % nbc == 0, T, 0)
        ck = jnp.concatenate([cp_ref[...], cc_ref[...]], axis=0)
        sk = jnp.concatenate([sp_ref[...], sc_ref[...]], axis=0)
        for h in range(HQ):
            hs, ks = slice(h * HD, (h + 1) * HD), slice((h // grp) * HD, (h // grp + 1) * HD)
            res = _attn_head(q_ref[:, hs], kp_ref[:, ks], kc_ref[:, ks], vp_ref[:, ks], vc_ref[:, ks],
                             rest[0][:, hs] if gated else None,
                             cq=cc_ref[...], sq=sc_ref[...], ck=ck, sk=sk, lim=lim, max_dist=max_dist, use_rope=True)
            if gated:
                rest[1][:, hs] = res.astype(bf16)
            else:
                rest[0][:, hs] = res[0]
                rest[1][:, hs] = res[1]

    row = pl.BlockSpec((T, 512), lambda i: (i, 0))
    if gated:
        return _host_call(
            body, shards, (qa, ka, ka, va, va, cos, cos, sin, sin, sinks), name=name, grid=(n,),
            out_shape=jax.ShapeDtypeStruct((L, 512), bf16),
            in_specs=specs + [pl.BlockSpec((1, 512), lambda i: (0, 0))], out_specs=row, scratch_shapes=[],
            compiler_params=_params(("parallel",)), gather=True)
    return _host_call(
        body, shards, (qa, ka, ka, va, va, cos, cos, sin, sin), name=name, grid=(n,),
        out_shape=[jax.ShapeDtypeStruct((L, 512), f32)] * 2,
        in_specs=specs, out_specs=[row, row], scratch_shapes=[], compiler_params=_params(("parallel",)), gather=True)


def _attn_bwd(qa, ka, va, cos, sin, sinks, douts, *, q_col, k_col, v_col, hkv, nbc, max_dist, name, sends=None):
    L = qa.shape[0]
    T, HQ, HD = ATT_BLOCK, 8, HEAD_DIM
    wk = hkv * HD
    n, blk, specs = _attn_specs(L, q_col, k_col, v_col, wk, True)
    grp = HQ // hkv
    gated = sinks is not None
    nd = len(douts)

    def body(*refs):
        q_ref, kp_ref, kc_ref, vp_ref, vc_ref, cp_ref, cc_ref, sp_ref, sc_ref = refs[:9]
        pos = 9
        sink_ref = None
        if gated:
            sink_ref = refs[pos]
            pos += 1
        d_refs = refs[pos:pos + nd]
        pos += nd
        dq_ref, dk_ref, dv_ref = refs[pos:pos + 3]
        pos += 3
        dsink_ref = None
        if gated:
            dsink_ref = refs[pos]
            pos += 1
        carry_k, carry_v = refs[pos:pos + 2]

        @pl.when(pl.program_id(0) == 0)
        def _():
            carry_k[...] = jnp.zeros_like(carry_k)
            carry_v[...] = jnp.zeros_like(carry_v)
            if gated:
                dsink_ref[...] = jnp.zeros_like(dsink_ref)

        lim = jnp.where(blk(pl.program_id(0)) % nbc == 0, T, 0)
        ck = jnp.concatenate([cp_ref[...], cc_ref[...]], axis=0)
        sk = jnp.concatenate([sp_ref[...], sc_ref[...]], axis=0)
        dkp = [jnp.zeros((T, HD), f32) for _ in range(hkv)]
        dkc = [jnp.zeros((T, HD), f32) for _ in range(hkv)]
        dvp = [jnp.zeros((T, HD), f32) for _ in range(hkv)]
        dvc = [jnp.zeros((T, HD), f32) for _ in range(hkv)]
        for h in range(HQ):
            g = h // grp
            hs, ks = slice(h * HD, (h + 1) * HD), slice(g * HD, (g + 1) * HD)
            fn = functools.partial(_attn_head, cq=cc_ref[...], sq=sc_ref[...], ck=ck, sk=sk, lim=lim,
                                   max_dist=max_dist, use_rope=True)
            prim = (q_ref[:, hs], kp_ref[:, ks], kc_ref[:, ks], vp_ref[:, ks], vc_ref[:, ks])
            if gated:
                _, vjp = jax.vjp(fn, *prim, sink_ref[:, hs])
                dq, a, b, c, d, ds = vjp(d_refs[0][:, hs].astype(f32))
                dsink_ref[:, hs] += ds
            else:
                _, vjp = jax.vjp(lambda *p: fn(*p, None), *prim)
                dq, a, b, c, d = vjp((d_refs[0][:, hs], d_refs[1][:, hs]))
            dq_ref[:, hs] = dq.astype(bf16)
            dkp[g], dkc[g], dvp[g], dvc[g] = dkp[g] + a, dkc[g] + b, dvp[g] + c, dvc[g] + d
        for g in range(hkv):
            ks = slice(g * HD, (g + 1) * HD)
            dk_ref[:, ks] = (dkc[g] + carry_k[:, ks]).astype(bf16)
            dv_ref[:, ks] = (dvc[g] + carry_v[:, ks]).astype(bf16)
            carry_k[:, ks] = dkp[g]
            carry_v[:, ks] = dvp[g]

    row = lambda w: pl.BlockSpec((T, w), lambda i: (blk(i), 0))
    vec = pl.BlockSpec((1, 512), lambda i: (0, 0))
    in_specs = specs + ([vec] if gated else []) + [row(512)] * nd
    out_shape = [jax.ShapeDtypeStruct((L, 512), bf16), jax.ShapeDtypeStruct((L, wk), bf16), jax.ShapeDtypeStruct((L, wk), bf16)]
    out_specs = [row(512), row(wk), row(wk)]
    if gated:
        out_shape.append(jax.ShapeDtypeStruct((1, 512), f32))
        out_specs.append(vec)
    args = (qa, ka, ka, va, va, cos, cos, sin, sin) + ((sinks,) if gated else ()) + tuple(douts)
    outs, recvs = _host_call(
        body, sends, args, name=name, grid=(n,), out_shape=out_shape, in_specs=in_specs, out_specs=out_specs,
        scratch_shapes=[pltpu.VMEM((T, wk), f32), pltpu.VMEM((T, wk), f32)],
        compiler_params=_params(("arbitrary",)))
    return outs if sends is None else (outs, recvs)


def _dilmix_f(o0, o1, o2, l0, l1, l2):
    m = jnp.maximum(jnp.maximum(l0, l1), l2)
    e0, e1, e2 = jnp.exp(l0 - m), jnp.exp(l1 - m), jnp.exp(l2 - m)
    return (e0 * o0 + e1 * o1 + e2 * o2) / (e0 + e1 + e2)


def _dilmix_fwd(os_, ls, name):
    L = os_[0].shape[0]
    tm = _tile(L, 512, 8)

    def body(o0, o1, o2, l0, l1, l2, out):
        out[...] = _dilmix_f(o0[...], o1[...], o2[...], l0[...], l1[...], l2[...]).astype(bf16)

    row = pl.BlockSpec((tm, 512), lambda i: (i, 0))
    return pl.pallas_call(
        body, name=name, grid=(L // tm,), out_shape=jax.ShapeDtypeStruct((L, 512), bf16),
        in_specs=[row] * 6, out_specs=row, compiler_params=_params(("parallel",)),
    )(*os_, *ls)


def _dilmix_bwd(os_, ls, dout, name):
    L = os_[0].shape[0]
    tm = _tile(L, 512, 8)

    def body(o0, o1, o2, l0, l1, l2, d, *outs):
        _, vjp = jax.vjp(_dilmix_f, o0[...], o1[...], o2[...], l0[...], l1[...], l2[...])
        for r, val in zip(outs, vjp(d[...].astype(f32))):
            r[...] = val

    row = pl.BlockSpec((tm, 512), lambda i: (i, 0))
    outs = pl.pallas_call(
        body, name=name, grid=(L // tm,), out_shape=[jax.ShapeDtypeStruct((L, 512), f32)] * 6,
        in_specs=[row] * 7, out_specs=[row] * 6, compiler_params=_params(("parallel",)),
    )(*os_, *ls, dout)
    return outs[:3], outs[3:]


def _to_strided(z, dil):
    L, w = z.shape
    return z.reshape(L // dil, dil, w).transpose(1, 0, 2).reshape(L, w)


def _from_strided(z, dil):
    L, w = z.shape
    return z.reshape(dil, L // dil, w).transpose(1, 0, 2).reshape(L, w)


def _adamw_math(w, g, m, v):
    m = ADAM_B1 * m + (1.0 - ADAM_B1) * g
    v = ADAM_B2 * v + (1.0 - ADAM_B2) * (g * g)
    m_hat = m / (1.0 - ADAM_B1 ** ADAM_STEP)
    v_hat = v / (1.0 - ADAM_B2 ** ADAM_STEP)
    delta = -ADAM_LR * (m_hat / (jnp.sqrt(v_hat) + ADAM_EPS) + ADAM_WD * w)
    return delta, m, v


def _adamw(w, m, v, slots, name):
    depth, R, C = w.shape
    tr = _tile(R, max(8, 131072 // C), 8)
    outs = None
    for l in range(depth):
        def body(w_ref, m_ref, v_ref, s_ref, *rest):
            g_ref, d_ref, nm_ref, nv_ref = rest[-4:]
            g = s_ref[0].astype(f32)
            for i in range(1, N_DEV):
                g = g + s_ref[i].astype(f32)
            delta, nm, nv = _adamw_math(w_ref[0], g, m_ref[0], v_ref[0])
            g_ref[0], d_ref[0], nm_ref[0], nv_ref[0] = g, delta, nm, nv

        blk = pl.BlockSpec((1, tr, C), functools.partial(lambda i, l: (l, i, 0), l=l))
        carried = [] if outs is None else list(outs)
        outs = pl.pallas_call(
            body, name=f"{name}_{l}", grid=(R // tr,), out_shape=[jax.ShapeDtypeStruct(w.shape, f32)] * 4,
            in_specs=[blk, blk, blk, pl.BlockSpec((N_DEV, tr, C), lambda i: (0, i, 0))]
            + [pl.BlockSpec(memory_space=pl.ANY)] * len(carried),
            out_specs=[blk] * 4, input_output_aliases={4 + j: j for j in range(len(carried))},
            compiler_params=_params(("parallel",)),
        )(w, m, v, slots[l], *carried)
    return outs


def _adamw_packed(w, m, v, slots, name):
    R = w.shape[0]
    tr = _tile(R, 512, 8)

    def body(w_ref, m_ref, v_ref, s_ref, g_ref, d_ref, nm_ref, nv_ref):
        g = s_ref[0]
        for i in range(1, N_DEV):
            g = g + s_ref[i]
        delta, nm, nv = _adamw_math(w_ref[...], g, m_ref[...], v_ref[...])
        g_ref[...], d_ref[...], nm_ref[...], nv_ref[...] = g, delta, nm, nv

    blk = pl.BlockSpec((tr, 128), lambda i: (i, 0))
    return pl.pallas_call(
        body, name=name, grid=(R // tr,), out_shape=[jax.ShapeDtypeStruct(w.shape, f32)] * 4,
        in_specs=[blk, blk, blk, pl.BlockSpec((N_DEV, tr, 128), lambda i: (0, i, 0))], out_specs=[blk] * 4,
        compiler_params=_params(("parallel",)),
    )(w, m, v, slots)


def _cols_gathered(g):
    nd = g.ndim
    perm = tuple(range(1, nd - 1)) + (0, nd - 1)
    t = g.transpose(perm)
    return t.reshape(t.shape[:-2] + (t.shape[-2] * t.shape[-1],))


def _cols_scatter(full):
    s = full.shape
    t = full.reshape(s[:-1] + (N_DEV, s[-1] // N_DEV))
    nd = t.ndim
    return t.transpose((nd - 2,) + tuple(range(nd - 2)) + (nd - 1,))


def _win_reorder(w):
    pad = jnp.zeros(w.shape[:-1] + (128 - GLA_LOWRANK,), w.dtype)
    return jnp.concatenate([w[..., O_GATES:], w[..., :O_GLR], w[..., O_S5U:O_GATES], w[..., O_GLR:O_S5U], pad], axis=-1)


def _win_restore(wp, D):
    b = 4 * D
    return jnp.concatenate([wp[..., b:b + O_GLR], wp[..., b + P_GLR:b + P_GLR + GLA_LOWRANK],
                            wp[..., b + O_GLR:b + P_GLR], wp[..., :b]], axis=-1)


SMALL = ("norm1_g", "gla_a_b", "gla_norm_g", "s5_lambda_re", "s5_lambda_im", "s5_log_dt", "s5_b_re", "s5_b_im",
         "s5_c_re", "s5_c_im", "s5_d", "s5_glu_b", "swa_sinks", "norm2_g", "final_norm_g")
SHARDED = ("w_in", "gla_a2", "s5_glu_w", "w_branch", "w_out", "w_ffn_gate", "w_ffn_up", "w_ffn_down")
WEIGHTS = ("norm1_g", "w_in", "gla_a2", "gla_a_b", "gla_norm_g", "s5_lambda_re", "s5_lambda_im", "s5_log_dt", "s5_b_re",
           "s5_b_im", "s5_c_re", "s5_c_im", "s5_d", "s5_glu_w", "s5_glu_b", "swa_sinks", "w_branch", "w_out", "norm2_g",
           "w_ffn_gate", "w_ffn_up", "w_ffn_down", "final_norm_g")


def _pack(arrs):
    flat = jnp.concatenate([a.reshape(-1) for a in arrs])
    n = flat.shape[0]
    rows = -(-n // 1024) * 8
    return jnp.pad(flat, (0, rows * 128 - n)).reshape(rows, 128)


def _unpack(packed, like):
    flat = packed.reshape(-1)
    out, pos = [], 0
    for a in like:
        out.append(flat[pos:pos + a.size].reshape(a.shape))
        pos += a.size
    return out


def kernel(x, positions, norm1_g, w_in, gla_a2, gla_a_b, gla_norm_g, s5_lambda_re, s5_lambda_im, s5_log_dt, s5_b_re, s5_b_im, s5_c_re, s5_c_im, s5_d, s5_glu_w, s5_glu_b, swa_sinks, w_branch, w_out, norm2_g, w_ffn_gate, w_ffn_up, w_ffn_down, final_norm_g, loss_target, m_norm1_g, m_w_in, m_gla_a2, m_gla_a_b, m_gla_norm_g, m_s5_lambda_re, m_s5_lambda_im, m_s5_log_dt, m_s5_b_re, m_s5_b_im, m_s5_c_re, m_s5_c_im, m_s5_d, m_s5_glu_w, m_s5_glu_b, m_swa_sinks, m_w_branch, m_w_out, m_norm2_g, m_w_ffn_gate, m_w_ffn_up, m_w_ffn_down, m_final_norm_g, v_norm1_g, v_w_in, v_gla_a2, v_gla_a_b, v_gla_norm_g, v_s5_lambda_re, v_s5_lambda_im, v_s5_log_dt, v_s5_b_re, v_s5_b_im, v_s5_c_re, v_s5_c_im, v_s5_d, v_s5_glu_w, v_s5_glu_b, v_swa_sinks, v_w_branch, v_w_out, v_norm2_g, v_w_ffn_gate, v_w_ffn_up, v_w_ffn_down, v_final_norm_g):
    W = dict(norm1_g=norm1_g, w_in=w_in, gla_a2=gla_a2, gla_a_b=gla_a_b, gla_norm_g=gla_norm_g, s5_lambda_re=s5_lambda_re, s5_lambda_im=s5_lambda_im, s5_log_dt=s5_log_dt, s5_b_re=s5_b_re, s5_b_im=s5_b_im, s5_c_re=s5_c_re, s5_c_im=s5_c_im, s5_d=s5_d, s5_glu_w=s5_glu_w, s5_glu_b=s5_glu_b, swa_sinks=swa_sinks, w_branch=w_branch, w_out=w_out, norm2_g=norm2_g, w_ffn_gate=w_ffn_gate, w_ffn_up=w_ffn_up, w_ffn_down=w_ffn_down, final_norm_g=final_norm_g)
    Mo = dict(norm1_g=m_norm1_g, w_in=m_w_in, gla_a2=m_gla_a2, gla_a_b=m_gla_a_b, gla_norm_g=m_gla_norm_g, s5_lambda_re=m_s5_lambda_re, s5_lambda_im=m_s5_lambda_im, s5_log_dt=m_s5_log_dt, s5_b_re=m_s5_b_re, s5_b_im=m_s5_b_im, s5_c_re=m_s5_c_re, s5_c_im=m_s5_c_im, s5_d=m_s5_d, s5_glu_w=m_s5_glu_w, s5_glu_b=m_s5_glu_b, swa_sinks=m_swa_sinks, w_branch=m_w_branch, w_out=m_w_out, norm2_g=m_norm2_g, w_ffn_gate=m_w_ffn_gate, w_ffn_up=m_w_ffn_up, w_ffn_down=m_w_ffn_down, final_norm_g=m_final_norm_g)
    Vo = dict(norm1_g=v_norm1_g, w_in=v_w_in, gla_a2=v_gla_a2, gla_a_b=v_gla_a_b, gla_norm_g=v_gla_norm_g, s5_lambda_re=v_s5_lambda_re, s5_lambda_im=v_s5_lambda_im, s5_log_dt=v_s5_log_dt, s5_b_re=v_s5_b_re, s5_b_im=v_s5_b_im, s5_c_re=v_s5_c_re, s5_c_im=v_s5_c_im, s5_d=v_s5_d, s5_glu_w=v_s5_glu_w, s5_glu_b=v_s5_glu_b, swa_sinks=v_swa_sinks, w_branch=v_w_branch, w_out=v_w_out, norm2_g=v_norm2_g, w_ffn_gate=v_w_ffn_gate, w_ffn_up=v_w_ffn_up, w_ffn_down=v_w_ffn_down, final_norm_g=v_final_norm_g)

    L, D = x.shape[1], x.shape[2]
    depth = norm1_g.shape[0]
    xs = x.reshape(L, D)
    target = loss_target.reshape(L, D)
    base128 = 4 * D // 128

    in_group = ("w_in", "gla_a2", "s5_glu_w")
    full = {}

    def shards(keys, l):
        if l >= depth:
            return None
        return [W[k][l] if k in ("gla_a2", "s5_glu_w") else W[k][l].astype(bf16) for k in keys]

    def landed(keys, l, gathered):
        for k, g in zip(keys, gathered):
            if k == "w_in":
                full[k, l] = _win_reorder(_cols_gathered(g))
            elif k == "gla_a2":
                full[k, l] = jnp.pad(_cols_gathered(g), ((0, 128 - GLA_LOWRANK), (0, 0)))
            elif k in ("w_branch", "w_ffn_gate", "w_ffn_up"):
                full[k, l] = _cols_gathered(g)
            else:
                full[k, l] = g.reshape((-1, g.shape[-1]))

    landed(in_group, 0, _all_gather(shards(in_group, 0), "gather_w_in0"))

    cos, sin = _rope_tables(positions.reshape(L))
    strided_tabs = {dil: (_to_strided(cos, dil), _to_strided(sin, dil)) for _, dil in DIL_CONFIGS if dil > 1}

    saved = []
    cur = xs
    for l in range(depth):
        s = {"x": cur}
        nxt = l + 1
        h1 = _rms_fwd(cur, norm1_g[l][None], f"rms1_fwd{l}")
        if l == 0:
            keys = ("w_branch", "w_out", "w_ffn_gate")
            proj, got = _matmul(h1, full["w_in", l], sends=shards(keys, 0), gather=True, name=f"proj_in{l}")
            landed(keys, 0, got)
        else:
            proj = _matmul(h1, full["w_in", l], name=f"proj_in{l}")
        s["h1"], s["proj"] = h1, proj
        ab, ng = gla_a_b[l][None], gla_norm_g[l].reshape(1, 512)
        (o_gla, s["gla_st"]), got = _gla_fwd(proj, full["gla_a2", l], ab, ng, base128, f"gla_fwd{l}", shards=shards(in_group, nxt))
        landed(in_group, nxt, got)
        prep, s["prep_vjp"] = jax.vjp(_s5_prep, s5_lambda_re[l], s5_lambda_im[l], s5_log_dt[l], s5_b_re[l], s5_b_im[l],
                                      s5_c_re[l], s5_c_im[l], s5_d[l])
        s["prep"] = prep
        (y_s5, s["s5_r"], s["s5_i"]), got = _s5_fwd(proj, prep, base128, f"s5_fwd{l}", shards=shards(("w_branch", "w_out"), nxt))
        landed(("w_branch", "w_out"), nxt, got)
        s["y_s5"] = y_s5
        o_s5 = _glu_fwd(y_s5, full["s5_glu_w", l], s5_glu_b[l][None], f"glu_fwd{l}")
        sinks_b = jnp.repeat(swa_sinks[l], HEAD_DIM)[None]
        s["sinks_b"] = sinks_b
        nb = L // ATT_BLOCK
        o_swa, got = _attn_fwd(proj, proj, proj, cos, sin, sinks_b, q_col=(base128 + P_SQ // 128) // 4,
                               k_col=base128 + P_SK // 128, v_col=base128 + P_SV // 128, hkv=SWA_KV_HEADS, nbc=nb,
                               max_dist=SWA_WINDOW - 1, name=f"swa_fwd{l}", shards=shards(("w_ffn_up",), 0) if l == 0 else None)
        landed(("w_ffn_up",), 0, got)
        cq, ck, cv = (base128 + P_CQ // 128) // 4, (base128 + P_CK // 128) // 4, (base128 + P_CV // 128) // 4
        dil_o, dil_l, s["dil_in"] = [], [], []
        riders = ((("w_ffn_down",), 0 if l == 0 else depth), (("w_ffn_gate",), nxt), (("w_ffn_up",), nxt))
        for (window, dil), (keys, kl) in zip(DIL_CONFIGS, riders):
            if dil == 1:
                (o, lse), got = _attn_fwd(proj, proj, proj, cos, sin, None, q_col=cq, k_col=ck, v_col=cv, hkv=8, nbc=nb,
                                          max_dist=window // dil, name=f"dil{dil}_fwd{l}", shards=shards(keys, kl))
                s["dil_in"].append(None)
            else:
                qs_, ks_, vs_ = (_to_strided(proj[:, 4 * D + off:4 * D + off + 512], dil) for off in (P_CQ, P_CK, P_CV))
                (o, lse), got = _attn_fwd(qs_, ks_, vs_, *strided_tabs[dil], None, q_col=0, k_col=0, v_col=0, hkv=8,
                                          nbc=nb // dil, max_dist=window // dil, name=f"dil{dil}_fwd{l}", shards=shards(keys, kl))
                o, lse = _from_strided(o, dil), _from_strided(lse, dil)
                s["dil_in"].append((qs_, ks_, vs_))
            landed(keys, kl, got)
            dil_o.append(o)
            dil_l.append(lse)
        s["dil_o"], s["dil_l"] = dil_o, dil_l
        o_dil = _dilmix_fwd(dil_o, dil_l, f"dilmix_fwd{l}")
        branches = (o_gla, o_s5, o_dil, o_swa)
        s["branches"] = branches
        ys = [_matmul(br, full["w_branch", l][m], name=f"branch{m}_fwd{l}") for m, br in enumerate(branches)]
        s["ys"] = ys
        mixed = _merge_fwd(proj, ys, D, f"merge_fwd{l}")
        s["mixed"] = mixed
        x2 = _matmul(mixed, full["w_out", l], res=cur, name=f"out_fwd{l}")
        s["x2"] = x2
        h2 = _rms_fwd(x2, norm2_g[l][None], f"rms2_fwd{l}")
        rider = shards(("w_ffn_down",), nxt)
        if rider:
            a, got = _matmul(h2, full["w_ffn_gate", l], sends=rider, gather=True, name=f"ffn_gate_fwd{l}")
            landed(("w_ffn_down",), nxt, got)
        else:
            a = _matmul(h2, full["w_ffn_gate", l], name=f"ffn_gate_fwd{l}")
        b = _matmul(h2, full["w_ffn_up", l], name=f"ffn_up_fwd{l}")
        act = _swiglu_fwd(a, b, f"swiglu_fwd{l}")
        s["h2"], s["a"], s["b"], s["act"] = h2, a, b, act
        cur = _matmul(act, full["w_ffn_down", l], res=x2, name=f"ffn_down_fwd{l}")
        saved.append(s)
    win_p, a2p, glu_w, wb, wout, wg, wu, wd = (
        [full[k, l] for l in range(depth)]
        for k in ("w_in", "gla_a2", "s5_glu_w", "w_branch", "w_out", "w_ffn_gate", "w_ffn_up", "w_ffn_down"))

    loss_part, dcur, dcur_b, dgf = _final_loss(cur, final_norm_g[None], target, "final_loss")
    loss = lax.psum(loss_part, AXES)

    small_g = {k: [None] * depth for k in SMALL if k != "final_norm_g"}
    recv = {k: [None] * depth for k in SHARDED}
    in_group = ("w_in", "gla_a2", "s5_glu_w")
    pending = None
    for l in reversed(range(depth)):
        s = saved[l]
        proj = s["proj"]
        dact = _matmul(dcur_b, wd[l], mode="nt", name=f"ffn_down_dx{l}")
        g_down = _matmul(s["act"], dcur_b, mode="tn", out_dtype=bf16, name=f"ffn_down_dw{l}")
        da, db = _swiglu_bwd(s["a"], s["b"], dact, f"swiglu_bwd{l}")
        dh2 = _matmul(da, wg[l], mode="nt", name=f"ffn_gate_dx{l}")
        dh2 = _matmul(db, wu[l], mode="nt", res=dh2, name=f"ffn_up_dx{l}")
        g_gate = _matmul(s["h2"], da, mode="tn", out_dtype=bf16, name=f"ffn_gate_dw{l}")
        g_up = _matmul(s["h2"], db, mode="tn", out_dtype=bf16, name=f"ffn_up_dw{l}")
        ffn_sends = (("w_ffn_down", g_down.reshape((N_DEV, -1, D))), ("w_ffn_gate", _cols_scatter(g_gate)),
                     ("w_ffn_up", _cols_scatter(g_up)))
        dx2, dx2_b, dg2 = _rms_bwd(s["x2"], norm2_g[l][None], dh2, dcur, f"rms2_bwd{l}")
        small_g["norm2_g"][l] = dg2[0]
        dmixed = _matmul(dx2_b, wout[l], mode="nt", name=f"out_dx{l}")
        g_out = _matmul(s["mixed"], dx2_b, mode="tn", out_dtype=bf16, name=f"out_dw{l}")
        dys, dgates = _merge_bwd(proj, s["ys"], dmixed, D, f"merge_bwd{l}")
        dbr = [_matmul(dys[m], wb[l][m], mode="nt", name=f"branch{m}_dx{l}") for m in range(4)]
        g_branch = jnp.stack([_matmul(s["branches"][m], dys[m], mode="tn", out_dtype=bf16, name=f"branch{m}_dw{l}")
                              for m in range(4)])
        d_gla, d_s5, d_dil, d_swa = dbr
        ab, ng = gla_a_b[l][None], gla_norm_g[l].reshape(1, 512)
        (dgq, dgk, dgv, dgr, dglr, da2, dab, dng), got = _gla_bwd(proj, a2p[l], ab, ng, s["gla_st"], d_gla, base128,
                                                                   f"gla_bwd{l}", sends=pending)
        for k, r in zip(in_group, got):
            recv[k][l + 1] = r
        small_g["gla_a_b"][l] = dab[0]
        small_g["gla_norm_g"][l] = dng.reshape(GLA_HEADS, GLA_DV)
        dy_s5, dglu_w, dglu_b = _glu_bwd(s["y_s5"], glu_w[l], s5_glu_b[l][None], d_s5, f"glu_bwd{l}")
        small_g["s5_glu_b"][l] = dglu_b[0]
        ds5u, dprep, got = _s5_bwd(proj, s["prep"], s["s5_r"], s["s5_i"], dy_s5, base128, f"s5_bwd{l}",
                                   sends=[g_out.reshape((N_DEV, -1, D)), _cols_scatter(g_branch)])
        recv["w_out"][l], recv["w_branch"][l] = got
        draw = s["prep_vjp"](dprep)
        for k, val in zip(("s5_lambda_re", "s5_lambda_im", "s5_log_dt", "s5_b_re", "s5_b_im", "s5_c_re", "s5_c_im", "s5_d"), draw):
            small_g[k][l] = val
        nb = L // ATT_BLOCK
        dsq, dsk, dsv, dsinks = _attn_bwd(proj, proj, proj, cos, sin, s["sinks_b"], (d_swa,),
                                          q_col=(base128 + P_SQ // 128) // 4, k_col=base128 + P_SK // 128,
                                          v_col=base128 + P_SV // 128, hkv=SWA_KV_HEADS, nbc=nb, max_dist=SWA_WINDOW - 1,
                                          name=f"swa_bwd{l}")
        small_g["swa_sinks"][l] = dsinks.reshape(SWA_HEADS, HEAD_DIM).sum(axis=1)
        dos, dls = _dilmix_bwd(s["dil_o"], s["dil_l"], d_dil, f"dilmix_bwd{l}")
        cq, ck, cv = (base128 + P_CQ // 128) // 4, (base128 + P_CK // 128) // 4, (base128 + P_CV // 128) // 4
        dcq = dck = dcv = None
        for i, (window, dil) in enumerate(DIL_CONFIGS):
            key, send = ffn_sends[i]
            if dil == 1:
                g3, got = _attn_bwd(proj, proj, proj, cos, sin, None, (dos[i], dls[i]), q_col=cq, k_col=ck, v_col=cv, hkv=8,
                                    nbc=nb, max_dist=window // dil, name=f"dil{dil}_bwd{l}", sends=[send])
            else:
                qs_, ks_, vs_ = s["dil_in"][i]
                g3, got = _attn_bwd(qs_, ks_, vs_, *strided_tabs[dil], None, (_to_strided(dos[i], dil), _to_strided(dls[i], dil)),
                                    q_col=0, k_col=0, v_col=0, hkv=8, nbc=nb // dil, max_dist=window // dil,
                                    name=f"dil{dil}_bwd{l}", sends=[send])
                g3 = [_from_strided(t, dil) for t in g3]
            recv[key][l] = got[0]
            g3 = [t.astype(f32) for t in g3]
            dcq, dck, dcv = (g3[0], g3[1], g3[2]) if dcq is None else (dcq + g3[0], dck + g3[1], dcv + g3[2])
        dproj = jnp.concatenate([dgates.transpose(1, 0, 2).reshape(L, 4 * D), dgq, dgk, dgv, dgr, ds5u,
                                 dcq.astype(bf16), dck.astype(bf16), dcv.astype(bf16), dsq, dsk, dsv, dglr], axis=1)
        g_in = _matmul(s["h1"], dproj, mode="tn", out_dtype=bf16, name=f"proj_in_dw{l}")
        in_sends = [_cols_scatter(_win_restore(g_in, D)), _cols_scatter(da2[:GLA_LOWRANK]), dglu_w.reshape((N_DEV, -1, 512))]
        if l > 0:
            dh1 = _matmul(dproj, win_p[l], mode="nt", name=f"proj_in_dx{l}")
            pending = in_sends
        else:
            dh1, got = _matmul(dproj, win_p[l], mode="nt", sends=in_sends, name=f"proj_in_dx{l}")
            for k, r in zip(in_group, got):
                recv[k][l] = r
        dcur, dcur_b, dg1 = _rms_bwd(s["x"], norm1_g[l][None], dh1, dx2, f"rms1_bwd{l}")
        small_g["norm1_g"][l] = dg1[0]
    grad_x = dcur.reshape(x.shape)

    out = {}
    for k in SHARDED:
        shp = W[k].shape
        as3 = lambda t: t.reshape((shp[0], -1, shp[-1]))
        slots = [r.reshape((N_DEV, -1, shp[-1])) for r in recv[k]]
        res = _adamw(as3(W[k]), as3(Mo[k]), as3(Vo[k]), slots, f"adamw_{k}")
        out[k] = [t.reshape(shp) for t in res]

    small_list = [jnp.stack(small_g[k]) if k != "final_norm_g" else dgf[0] for k in SMALL]
    small_list = [t.reshape(W[k].shape) for t, k in zip(small_list, SMALL)]
    packed_parts = _all_gather([_pack(small_list)], "gather_small_grads")[0]
    res = _adamw_packed(_pack([W[k] for k in SMALL]), _pack([Mo[k] for k in SMALL]), _pack([Vo[k] for k in SMALL]),
                        packed_parts, "adamw_small")
    unpacked = [_unpack(t, [W[k] for k in SMALL]) for t in res]
    for i, k in enumerate(SMALL):
        out[k] = [unpacked[j][i] for j in range(4)]

    return (loss, grad_x, *[out[k][0] for k in WEIGHTS], *[out[k][1] for k in WEIGHTS],
            *[out[k][2] for k in WEIGHTS], *[out[k][3] for k in WEIGHTS])
```

```python
import functools
import math

import jax
import jax.numpy as jnp
from jax import lax
from jax.experimental import pallas as pl
from jax.experimental.pallas import tpu as pltpu

f32 = jnp.float32
bf16 = jnp.bfloat16
HI = lax.Precision.HIGHEST

N_DEV = 8
AXES = ("x", "y", "c")
NORM_EPS = 1e-6
ROPE_THETA = 500000.0
HEAD_DIM = 64
ROPE_DIM = 16
ATT_BLOCK = 128
BRANCH_WIDTH = 512
GLA_HEADS, GLA_DK, GLA_DV, GLA_LOWRANK, GLA_TAU, GLA_CHUNK, GLA_SUB = 4, 64, 128, 16, 16.0, 64, 16
S5_GROUPS, S5_GROUP, S5_STATE = 32, 16, 64
S5_CHUNK = 128
S5_LANE_BLOCKS = 4
DIL_CONFIGS = ((128, 1), (512, 4), (2048, 16))
SWA_HEADS, SWA_KV_HEADS, SWA_WINDOW = 8, 2, 128
ADAM_LR, ADAM_B1, ADAM_B2, ADAM_EPS, ADAM_WD, ADAM_STEP = 0.001, 0.9, 0.999, 1e-08, 0.01, 10
O_GLR, O_S5U, O_GATES = 1536, 1552, 4368
MIX_COLS = 4480
P_GQ, P_GK, P_GV, P_GR, P_S5U, P_CQ, P_CK, P_CV, P_SQ, P_SK, P_SV, P_GLR = (
    0, 256, 512, 1024, 1536, 2048, 2560, 3072, 3584, 4096, 4224, 4352)
VMEM_LIMIT = 56 * 1024 * 1024


def _tile(n, cap, q=128):
    if n <= cap:
        return n
    t = (cap // q) * q
    while t >= q:
        if n % t == 0:
            return t
        t -= q
    return n


def _params(sem=None):
    return pltpu.CompilerParams(dimension_semantics=sem, vmem_limit_bytes=VMEM_LIMIT)


@functools.partial(jax.custom_vjp, nondiff_argnums=(1,))
def _sroll(x, d):
    return pltpu.roll(x, d, 0)


def _sroll_fwd(x, d):
    return pltpu.roll(x, d, 0), None


def _sroll_bwd(d, _, g):
    n = g.shape[0]
    return (pltpu.roll(g, (n - d) % n, 0),)


_sroll.defvjp(_sroll_fwd, _sroll_bwd)


def _mesh_pos():
    return lax.axis_index("x"), lax.axis_index("y"), lax.axis_index("c")


class _Gather:
    def __init__(self, ins, outs, send_sems, recv_sems, local_sems):
        self.ins, self.outs = ins, outs
        self.send_sems, self.recv_sems, self.local_sems = send_sems, recv_sems, local_sems
        x, y, c = _mesh_pos()
        self.x, self.y, self.c = x, y, c
        self.me, self.sibling = (x, y, c), (x, y, 1 - c)
        self.chips = [(1 - x, y), (x, 1 - y), (1 - x, 1 - y)]

    def copy(self, a, k, block, to, src=None):
        slot = self.outs[a].at[4 * block[0] + 2 * block[1] + block[2]]
        return pltpu.make_async_remote_copy(
            src_ref=slot if src is None else src, dst_ref=slot,
            send_sem=self.send_sems.at[a, k], recv_sem=self.recv_sems.at[a, k],
            device_id=to, device_id_type=pl.DeviceIdType.MESH)

    def mine(self, a):
        return pltpu.make_async_copy(self.ins[a], self.outs[a].at[4 * self.x + 2 * self.y + self.c], self.local_sems.at[a])

    def first(self, a):
        return [self.copy(a, 0, self.me, self.sibling, src=self.ins[a])] + [
            self.copy(a, 1 + j, self.me, (*chip, self.c), src=self.ins[a]) for j, chip in enumerate(self.chips)]

    def start(self):
        for a in range(len(self.ins)):
            self.mine(a).start()
            for cp in self.first(a):
                cp.start()

    def finish(self):
        c = self.c
        for a in range(len(self.ins)):
            passed = [self.copy(a, 4 + j, (*chip, c), self.sibling) for j, chip in enumerate(self.chips)]
            for j, chip in enumerate(self.chips):
                self.copy(a, 1 + j, (*chip, c), self.me).wait_recv()
                passed[j].start()
            self.copy(a, 0, self.sibling, self.me).wait_recv()
            for j, chip in enumerate(self.chips):
                self.copy(a, 4 + j, (*chip, 1 - c), self.me).wait_recv()
            for cp in self.first(a) + passed:
                cp.wait_send()
            self.mine(a).wait()


def _all_gather(shards, name):
    n = len(shards)
    any_spec = pl.BlockSpec(memory_space=pl.ANY)

    def body(*refs):
        g = _Gather(refs[:n], refs[n:2 * n], *refs[2 * n:])
        g.start()
        g.finish()

    outs = pl.pallas_call(
        body, name=name,
        out_shape=[jax.ShapeDtypeStruct((N_DEV,) + s.shape, s.dtype) for s in shards],
        in_specs=[any_spec] * n, out_specs=[any_spec] * n,
        scratch_shapes=[pltpu.SemaphoreType.DMA((n, 7)), pltpu.SemaphoreType.DMA((n, 7)),
                        pltpu.SemaphoreType.DMA((n,))],
    )(*shards)
    return list(outs)


def _a2a_copies(ins, outs, send_sems, recv_sems, local_sems):
    x, y, c = _mesh_pos()
    me = 4 * x + 2 * y + c
    copies = []
    for a in range(len(ins)):
        copies.append(pltpu.make_async_copy(ins[a].at[me], outs[a].at[me], local_sems.at[a]))
        for k in range(1, N_DEV):
            px = 1 - x if k & 4 else x
            py = 1 - y if k & 2 else y
            pc = 1 - c if k & 1 else c
            copies.append(pltpu.make_async_remote_copy(
                src_ref=ins[a].at[4 * px + 2 * py + pc], dst_ref=outs[a].at[me],
                send_sem=send_sems.at[a, k - 1], recv_sem=recv_sems.at[a, k - 1],
                device_id=(px, py, pc), device_id_type=pl.DeviceIdType.MESH))
    return copies


def _host_call(body, sends, args, *, name, grid, out_shape, in_specs, out_specs, scratch_shapes, compiler_params,
               gather=False):
    single = not isinstance(out_shape, (list, tuple))
    out_shape = [out_shape] if single else list(out_shape)
    out_specs = [out_specs] if single else list(out_specs)
    sends = list(sends or ())
    n, n_in, n_out, n_scr = len(sends), len(args), len(out_shape), len(scratch_shapes)
    if n == 0:
        outs = pl.pallas_call(body, name=name, grid=grid, out_shape=out_shape, in_specs=in_specs, out_specs=out_specs,
                              scratch_shapes=list(scratch_shapes), compiler_params=compiler_params)(*args)
        return (outs[0] if single else list(outs)), []
    any_spec = pl.BlockSpec(memory_space=pl.ANY)

    def hosted(*refs):
        ins, s_in = refs[:n_in], refs[n_in:n_in + n]
        pos = n_in + n
        outs, s_out = refs[pos:pos + n_out], refs[pos + n_out:pos + n_out + n]
        pos += n_out + n
        scr, sems = refs[pos:pos + n_scr], refs[pos + n_scr:]
        ids = [pl.program_id(i) for i in range(len(grid))]
        first = functools.reduce(lambda p, q: p & q, [i == 0 for i in ids])
        last = functools.reduce(lambda p, q: p & q, [i == g - 1 for i, g in zip(ids, grid)])

        @pl.when(first)
        def _():
            if gather:
                _Gather(s_in, s_out, *sems).start()
            else:
                for cp in _a2a_copies(s_in, s_out, *sems):
                    cp.start()

        body(*ins, *outs, *scr)

        @pl.when(last)
        def _():
            if gather:
                _Gather(s_in, s_out, *sems).finish()
            else:
                for cp in _a2a_copies(s_in, s_out, *sems):
                    cp.wait()

    lead = (N_DEV,) if gather else ()
    outs = pl.pallas_call(
        hosted, name=name, grid=grid,
        out_shape=out_shape + [jax.ShapeDtypeStruct(lead + s.shape, s.dtype) for s in sends],
        in_specs=list(in_specs) + [any_spec] * n, out_specs=out_specs + [any_spec] * n,
        scratch_shapes=list(scratch_shapes) + [pltpu.SemaphoreType.DMA((n, 7)), pltpu.SemaphoreType.DMA((n, 7)),
                                               pltpu.SemaphoreType.DMA((n,))],
        compiler_params=compiler_params,
    )(*args, *sends)
    main = list(outs[:n_out])
    return (main[0] if single else main), list(outs[n_out:])


def _matmul(a, b, *, mode="nn", out_dtype=f32, res=None, sends=None, gather=False, name):
    if mode == "tn":
        K, M = a.shape
    else:
        M, K = a.shape
    N = b.shape[0] if mode == "nt" else b.shape[1]
    k_cap = 2048 if (a.dtype == bf16 and b.dtype == bf16) else 1024
    tm, tn, tk = _tile(M, 1024), _tile(N, 1152), _tile(K, k_cap)
    nk = K // tk
    dims = {"nn": (((1,), (0,)), ((), ())), "nt": (((1,), (1,)), ((), ())), "tn": (((0,), (0,)), ((), ()))}[mode]

    def body(*refs):
        a_ref, b_ref = refs[:2]
        r_ref = refs[2] if res is not None else None
        o_ref = refs[3] if res is not None else refs[2]
        acc = refs[-1] if nk > 1 else None
        k = pl.program_id(2)
        part = lax.dot_general(a_ref[...].astype(bf16), b_ref[...].astype(bf16), dims, preferred_element_type=f32)

        def finish(r):
            if res is not None:
                r = r + r_ref[...]
            o_ref[...] = r.astype(o_ref.dtype)

        if nk == 1:
            finish(part)
            return

        @pl.when(k == 0)
        def _():
            acc[...] = part

        @pl.when((k > 0) & (k < nk - 1))
        def _():
            acc[...] += part

        @pl.when(k == nk - 1)
        def _():
            finish(acc[...] + part)

    a_spec = pl.BlockSpec((tk, tm), lambda i, j, k: (k, i)) if mode == "tn" else pl.BlockSpec((tm, tk), lambda i, j, k: (i, k))
    b_spec = pl.BlockSpec((tn, tk), lambda i, j, k: (j, k)) if mode == "nt" else pl.BlockSpec((tk, tn), lambda i, j, k: (k, j))
    o_spec = pl.BlockSpec((tm, tn), lambda i, j, k: (i, j))
    in_specs = [a_spec, b_spec] + ([o_spec] if res is not None else [])
    args = (a, b) + ((res,) if res is not None else ())
    out, recvs = _host_call(
        body, sends, args, name=name, grid=(M // tm, N // tn, nk),
        out_shape=jax.ShapeDtypeStruct((M, N), out_dtype),
        in_specs=in_specs, out_specs=o_spec,
        scratch_shapes=[pltpu.VMEM((tm, tn), f32)] if nk > 1 else [],
        compiler_params=_params(("parallel", "parallel", "arbitrary")), gather=gather)
    return out if sends is None else (out, recvs)


def _rms(x, g):
    return x * lax.rsqrt(jnp.mean(x * x, axis=-1, keepdims=True) + NORM_EPS) * g


def _rms_fwd(x, g, name):
    L, D = x.shape
    tm = _tile(L, 256, 8)

    def body(x_ref, g_ref, o_ref):
        o_ref[...] = _rms(x_ref[...], g_ref[...]).astype(bf16)

    return pl.pallas_call(
        body, name=name, grid=(L // tm,), out_shape=jax.ShapeDtypeStruct((L, D), bf16),
        in_specs=[pl.BlockSpec((tm, D), lambda i: (i, 0)), pl.BlockSpec((1, D), lambda i: (0, 0))],
        out_specs=pl.BlockSpec((tm, D), lambda i: (i, 0)),
        compiler_params=_params(("parallel",)),
    )(x, g)


def _rms_bwd(x, g, dh, dres, name):
    L, D = x.shape
    tm = _tile(L, 256, 8)

    def body(x_ref, g_ref, dh_ref, dres_ref, dx_ref, dxb_ref, dg_ref):
        _, vjp = jax.vjp(_rms, x_ref[...], g_ref[...])
        dx, dg = vjp(dh_ref[...])
        dx = dres_ref[...] + dx
        dx_ref[...] = dx
        dxb_ref[...] = dx.astype(bf16)

        @pl.when(pl.program_id(0) == 0)
        def _():
            dg_ref[...] = jnp.zeros_like(dg_ref)

        dg_ref[...] += dg

    row = pl.BlockSpec((tm, D), lambda i: (i, 0))
    vec = pl.BlockSpec((1, D), lambda i: (0, 0))
    return pl.pallas_call(
        body, name=name, grid=(L // tm,),
        out_shape=[jax.ShapeDtypeStruct((L, D), f32), jax.ShapeDtypeStruct((L, D), bf16), jax.ShapeDtypeStruct((1, D), f32)],
        in_specs=[row, vec, row, row], out_specs=[row, row, vec],
        compiler_params=_params(("arbitrary",)),
    )(x, g, dh, dres)


def _final_loss(x, g, target, name):
    L, D = x.shape
    tm = _tile(L, 256, 8)

    def body(x_ref, g_ref, t_ref, loss_ref, dx_ref, dxb_ref, dg_ref):
        tgt = t_ref[...]

        def f(xv, gv):
            err = _rms(xv, gv) - tgt
            return 0.5 * jnp.sum(jnp.mean(err * err, axis=-1, keepdims=True), axis=0, keepdims=True)

        val, vjp = jax.vjp(f, x_ref[...], g_ref[...])
        dx, dg = vjp(jnp.ones((1, 1), f32))
        dx_ref[...] = dx
        dxb_ref[...] = dx.astype(bf16)

        @pl.when(pl.program_id(0) == 0)
        def _():
            dg_ref[...] = jnp.zeros_like(dg_ref)
            loss_ref[...] = jnp.zeros_like(loss_ref)

        dg_ref[...] += dg
        loss_ref[...] += jnp.broadcast_to(val, loss_ref.shape)

    row = pl.BlockSpec((tm, D), lambda i: (i, 0))
    vec = pl.BlockSpec((1, D), lambda i: (0, 0))
    acc = pl.BlockSpec((8, 128), lambda i: (0, 0))
    loss, dx, dxb, dg = pl.pallas_call(
        body, name=name, grid=(L // tm,),
        out_shape=[jax.ShapeDtypeStruct((8, 128), f32), jax.ShapeDtypeStruct((L, D), f32), jax.ShapeDtypeStruct((L, D), bf16),
                   jax.ShapeDtypeStruct((1, D), f32)],
        in_specs=[row, vec, row], out_specs=[acc, row, row, vec],
        compiler_params=_params(("arbitrary",)),
    )(x, g, target)
    return loss[0, 0], dx, dxb, dg


def _swiglu_f(a, b):
    return jax.nn.silu(a) * b


def _swiglu_fwd(a, b, name):
    L, F = a.shape
    tm, tn = _tile(L, 512, 8), _tile(F, 1024)

    def body(a_ref, b_ref, o_ref):
        o_ref[...] = _swiglu_f(a_ref[...], b_ref[...]).astype(bf16)

    blk = pl.BlockSpec((tm, tn), lambda i, j: (i, j))
    return pl.pallas_call(
        body, name=name, grid=(L // tm, F // tn), out_shape=jax.ShapeDtypeStruct((L, F), bf16),
        in_specs=[blk, blk], out_specs=blk, compiler_params=_params(("parallel", "parallel")),
    )(a, b)


def _swiglu_bwd(a, b, dact, name):
    L, F = a.shape
    tm, tn = _tile(L, 512, 8), _tile(F, 1024)

    def body(a_ref, b_ref, d_ref, da_ref, db_ref):
        _, vjp = jax.vjp(_swiglu_f, a_ref[...], b_ref[...])
        da, db = vjp(d_ref[...])
        da_ref[...] = da.astype(bf16)
        db_ref[...] = db.astype(bf16)

    blk = pl.BlockSpec((tm, tn), lambda i, j: (i, j))
    return pl.pallas_call(
        body, name=name, grid=(L // tm, F // tn),
        out_shape=[jax.ShapeDtypeStruct((L, F), bf16)] * 2,
        in_specs=[blk, blk, blk], out_specs=[blk, blk], compiler_params=_params(("parallel", "parallel")),
    )(a, b, dact)


def _merge_f(g0, g1, g2, g3, y0, y1, y2, y3):
    s = jax.nn.sigmoid
    return s(g0) * y0 + s(g1) * y1 + s(g2) * y2 + s(g3) * y3


def _merge_fwd(proj, ys, D, name):
    L = proj.shape[0]
    tm, tn = _tile(L, 512, 8), _tile(D, 512)
    nj = D // tn

    def body(g0, g1, g2, g3, y0, y1, y2, y3, o_ref):
        o_ref[...] = _merge_f(g0[...], g1[...], g2[...], g3[...], y0[...], y1[...], y2[...], y3[...]).astype(bf16)

    gspecs = [pl.BlockSpec((tm, tn), functools.partial(lambda i, j, m: (i, m * nj + j), m=m)) for m in range(4)]
    blk = pl.BlockSpec((tm, tn), lambda i, j: (i, j))
    return pl.pallas_call(
        body, name=name, grid=(L // tm, nj), out_shape=jax.ShapeDtypeStruct((L, D), bf16),
        in_specs=gspecs + [blk] * 4, out_specs=blk, compiler_params=_params(("parallel", "parallel")),
    )(proj, proj, proj, proj, *ys)


def _merge_bwd(proj, ys, dmixed, D, name):
    L = proj.shape[0]
    tm, tn = _tile(L, 512, 8), _tile(D, 512)
    nj = D // tn

    def body(g0, g1, g2, g3, y0, y1, y2, y3, d_ref, dy0, dy1, dy2, dy3, dg_ref):
        _, vjp = jax.vjp(_merge_f, g0[...], g1[...], g2[...], g3[...], y0[...], y1[...], y2[...], y3[...])
        grads = vjp(d_ref[...])
        for m, r in enumerate((dy0, dy1, dy2, dy3)):
            r[...] = grads[4 + m].astype(bf16)
        for m in range(4):
            dg_ref[m] = grads[m].astype(bf16)

    gspecs = [pl.BlockSpec((tm, tn), functools.partial(lambda i, j, m: (i, m * nj + j), m=m)) for m in range(4)]
    blk = pl.BlockSpec((tm, tn), lambda i, j: (i, j))
    dgspec = pl.BlockSpec((4, tm, tn), lambda i, j: (0, i, j))
    outs = pl.pallas_call(
        body, name=name, grid=(L // tm, nj),
        out_shape=[jax.ShapeDtypeStruct((L, D), bf16)] * 4 + [jax.ShapeDtypeStruct((4, L, D), bf16)],
        in_specs=gspecs + [blk] * 5, out_specs=[blk] * 4 + [dgspec],
        compiler_params=_params(("parallel", "parallel")),
    )(proj, proj, proj, proj, *ys, dmixed)
    return outs[:4], outs[4]


def _gla_head(q, k, v, r, glr, st, a2, ab, ng):
    C, T = GLA_CHUNK, GLA_SUB
    row = lax.broadcasted_iota(jnp.int32, (C, C), 0)
    col = lax.broadcasted_iota(jnp.int32, (C, C), 1)
    tri = (col <= row).astype(f32)
    sel = (col == (row // T) * T).astype(f32)
    z = jnp.dot(glr, a2, preferred_element_type=f32) + ab
    g = jax.nn.log_sigmoid(z) / GLA_TAU
    cum = jnp.dot(tri, g, precision=HI, preferred_element_type=f32)
    excl = cum - g
    ref = jnp.dot(sel, excl, precision=HI, preferred_element_type=f32)
    qs = q * (GLA_DK ** -0.5)
    q_ref = qs * jnp.exp(cum - ref)
    rowk = lax.broadcasted_iota(jnp.int32, (C, GLA_DK), 0)
    a = jnp.zeros((C, C), f32)
    for s in range(1, C // T):
        ref_s = jnp.sum(jnp.where(rowk == s * T, excl, 0.0), axis=0, keepdims=True)
        k_ref = k * jnp.exp(jnp.where(rowk < s * T, ref_s - cum, -jnp.inf))
        a_s = lax.dot_general(q_ref, k_ref, (((1,), (1,)), ((), ())), preferred_element_type=f32)
        a = a + jnp.where(row // T == s, a_s, 0.0)
    o = jnp.dot(a, v, preferred_element_type=f32)
    sub = rowk % T
    for d in range(T):
        ks = _sroll(k, d) if d else k
        cs = _sroll(cum, d) if d else cum
        vs = _sroll(v, d) if d else v
        w = jnp.sum(qs * ks * jnp.exp(jnp.where(sub >= d, cum - cs, -jnp.inf)), axis=-1, keepdims=True)
        o = o + w * vs
    o = o + lax.dot_general(qs * jnp.exp(cum), st, (((1,), (1,)), ((), ())), preferred_element_type=f32)
    last = jnp.sum(jnp.where(rowk == C - 1, cum, 0.0), axis=0, keepdims=True)
    st_new = st * jnp.exp(last) + lax.dot_general(v, k * jnp.exp(last - cum), (((0,), (0,)), ((), ())),
                                                  preferred_element_type=f32)
    out = _rms(o, ng) * jax.nn.silu(r)
    return out, st_new


def _gla_specs(L, base128, rev):
    n = L // GLA_CHUNK
    ch = (lambda i: n - 1 - i) if rev else (lambda i: i)
    b = base128
    return n, ch, [
        pl.BlockSpec((GLA_CHUNK, 256), lambda i: (ch(i), (b + P_GQ // 128) // 2)),
        pl.BlockSpec((GLA_CHUNK, 256), lambda i: (ch(i), (b + P_GK // 128) // 2)),
        pl.BlockSpec((GLA_CHUNK, 512), lambda i: (ch(i), (b + P_GV // 128) // 4)),
        pl.BlockSpec((GLA_CHUNK, 512), lambda i: (ch(i), (b + P_GR // 128) // 4)),
        pl.BlockSpec((GLA_CHUNK, 128), lambda i: (ch(i), b + P_GLR // 128)),
    ]


def _gla_fwd(proj, a2p, ab, ng, base128, name, shards=None):
    L = proj.shape[0]
    n, _, pspecs = _gla_specs(L, base128, False)
    H, DK, DV = GLA_HEADS, GLA_DK, GLA_DV

    def body(q_ref, k_ref, v_ref, r_ref, l_ref, a2_ref, ab_ref, ng_ref, o_ref, st_ref, state):
        @pl.when(pl.program_id(0) == 0)
        def _():
            state[...] = jnp.zeros_like(state)

        st_ref[0] = state[...]
        glr = l_ref[...]
        for h in range(H):
            kk, vv = slice(h * DK, (h + 1) * DK), slice(h * DV, (h + 1) * DV)
            out, st_new = _gla_head(q_ref[:, kk], k_ref[:, kk], v_ref[:, vv], r_ref[:, vv], glr, state[h],
                                    a2_ref[:, kk], ab_ref[:, kk], ng_ref[:, vv])
            o_ref[:, vv] = out.astype(bf16)
            state[h] = st_new

    full = lambda shape: pl.BlockSpec(shape, lambda i: (0,) * len(shape))
    return _host_call(
        body, shards, (proj, proj, proj, proj, proj, a2p, ab, ng), name=name, grid=(n,),
        out_shape=[jax.ShapeDtypeStruct((L, H * DV), bf16), jax.ShapeDtypeStruct((n, H, DV, DK), f32)],
        in_specs=pspecs + [full((128, 256)), full((1, 256)), full((1, 512))],
        out_specs=[pl.BlockSpec((GLA_CHUNK, 512), lambda i: (i, 0)), pl.BlockSpec((1, H, DV, DK), lambda i: (i, 0, 0, 0))],
        scratch_shapes=[pltpu.VMEM((H, DV, DK), f32)],
        compiler_params=_params(("arbitrary",)), gather=True)


def _gla_bwd(proj, a2p, ab, ng, states, dout, base128, name, sends=None):
    L = proj.shape[0]
    n, ch, pspecs = _gla_specs(L, base128, True)
    H, DK, DV = GLA_HEADS, GLA_DK, GLA_DV

    def body(q_ref, k_ref, v_ref, r_ref, l_ref, a2_ref, ab_ref, ng_ref, st_ref, do_ref,
             dq_ref, dk_ref, dv_ref, dr_ref, dl_ref, da2_ref, dab_ref, dng_ref, dstate):
        @pl.when(pl.program_id(0) == 0)
        def _():
            dstate[...] = jnp.zeros_like(dstate)
            da2_ref[...] = jnp.zeros_like(da2_ref)
            dab_ref[...] = jnp.zeros_like(dab_ref)
            dng_ref[...] = jnp.zeros_like(dng_ref)

        glr = l_ref[...]
        dglr = jnp.zeros(glr.shape, f32)
        for h in range(H):
            kk, vv = slice(h * DK, (h + 1) * DK), slice(h * DV, (h + 1) * DV)
            _, vjp = jax.vjp(_gla_head, q_ref[:, kk], k_ref[:, kk], v_ref[:, vv], r_ref[:, vv], glr, st_ref[0, h],
                             a2_ref[:, kk], ab_ref[:, kk], ng_ref[:, vv])
            dq, dk, dv, dr, dl, dst, da2, dab, dng = vjp((do_ref[:, vv].astype(f32), dstate[h]))
            dq_ref[:, kk] = dq.astype(bf16)
            dk_ref[:, kk] = dk.astype(bf16)
            dv_ref[:, vv] = dv.astype(bf16)
            dr_ref[:, vv] = dr.astype(bf16)
            dglr = dglr + dl
            dstate[h] = dst
            da2_ref[:, kk] += da2
            dab_ref[:, kk] += dab
            dng_ref[:, vv] += dng
        dl_ref[...] = dglr.astype(bf16)

    full = lambda shape: pl.BlockSpec(shape, lambda i: (0,) * len(shape))
    rowspec = lambda w: pl.BlockSpec((GLA_CHUNK, w), lambda i: (ch(i), 0))
    return _host_call(
        body, sends, (proj, proj, proj, proj, proj, a2p, ab, ng, states, dout), name=name, grid=(n,),
        out_shape=[jax.ShapeDtypeStruct((L, 256), bf16), jax.ShapeDtypeStruct((L, 256), bf16),
                   jax.ShapeDtypeStruct((L, 512), bf16), jax.ShapeDtypeStruct((L, 512), bf16),
                   jax.ShapeDtypeStruct((L, 128), bf16), jax.ShapeDtypeStruct((128, 256), f32),
                   jax.ShapeDtypeStruct((1, 256), f32), jax.ShapeDtypeStruct((1, 512), f32)],
        in_specs=pspecs + [full((128, 256)), full((1, 256)), full((1, 512)),
                           pl.BlockSpec((1, H, DV, DK), lambda i: (ch(i), 0, 0, 0)), rowspec(512)],
        out_specs=[rowspec(256), rowspec(256), rowspec(512), rowspec(512), rowspec(128),
                   full((128, 256)), full((1, 256)), full((1, 512))],
        scratch_shapes=[pltpu.VMEM((H, DV, DK), f32)],
        compiler_params=_params(("arbitrary",)))


def _s5_prep(lam_re, lam_im, log_dt, b_re, b_im, c_re, c_im, d):
    G, N, Cn = S5_GROUPS, S5_STATE, S5_GROUP
    J, GB = S5_LANE_BLOCKS, S5_GROUPS // S5_LANE_BLOCKS
    dt = jnp.exp(log_dt)[:, None]
    mag = jnp.exp(lam_re * dt)
    ab_re, ab_im = mag * jnp.cos(lam_im * dt), mag * jnp.sin(lam_im * dt)
    den = lam_re * lam_re + lam_im * lam_im
    z_re = ((ab_re - 1.0) * lam_re + ab_im * lam_im) / den
    z_im = (ab_im * lam_re - (ab_re - 1.0) * lam_im) / den
    bb_re = z_re[..., None] * b_re - z_im[..., None] * b_im
    bb_im = z_re[..., None] * b_im + z_im[..., None] * b_re
    eye = jnp.eye(GB, dtype=f32)

    def in_blocks(bb):
        return jnp.einsum("jgnc,gh->jgchn", bb.reshape(J, GB, N, Cn), eye).reshape(J, GB * Cn, GB * N)

    def out_blocks(cc):
        return jnp.einsum("jgcn,gh->jgnhc", cc.reshape(J, GB, Cn, N), eye).reshape(J, GB * N, GB * Cn)

    return (ab_re.reshape(1, G * N), ab_im.reshape(1, G * N), in_blocks(bb_re), in_blocks(bb_im),
            out_blocks(c_re), out_blocks(c_im), d.reshape(1, G * Cn))


def _s5_chunk(u, hin_r, hin_i, a_r, a_i, bb_r, bb_i, cc_r, cc_i, dvec):
    T = u.shape[0]
    hr = jnp.dot(u, bb_r, preferred_element_type=f32)
    hi = jnp.dot(u, bb_i, preferred_element_type=f32)
    row = lax.broadcasted_iota(jnp.int32, hr.shape, 0)
    hr = hr + jnp.where(row == 0, a_r * hin_r - a_i * hin_i, 0.0)
    hi = hi + jnp.where(row == 0, a_r * hin_i + a_i * hin_r, 0.0)
    pr, pi = a_r, a_i
    d = 1
    while d < T:
        sr = jnp.where(row >= d, _sroll(hr, d), 0.0)
        si = jnp.where(row >= d, _sroll(hi, d), 0.0)
        hr, hi = hr + pr * sr - pi * si, hi + pr * si + pi * sr
        pr, pi = pr * pr - pi * pi, 2.0 * pr * pi
        d *= 2
    y = (jnp.dot(hr, cc_r, preferred_element_type=f32)
         - jnp.dot(hi, cc_i, preferred_element_type=f32) + dvec * u)
    out_r = jnp.sum(jnp.where(row == T - 1, hr, 0.0), axis=0, keepdims=True)
    out_i = jnp.sum(jnp.where(row == T - 1, hi, 0.0), axis=0, keepdims=True)
    return y, out_r, out_i


def _s5_specs(L, base128, rev):
    T, J = S5_CHUNK, S5_LANE_BLOCKS
    n = L // T
    ch = (lambda c: n - 1 - c) if rev else (lambda c: c)
    ub = base128 + P_S5U // 128
    specs = [
        pl.BlockSpec((T, 128), lambda j, c: (ch(c), ub + j)),
        pl.BlockSpec((1, 512), lambda j, c: (0, j)), pl.BlockSpec((1, 512), lambda j, c: (0, j)),
        pl.BlockSpec((None, 128, 512), lambda j, c: (j, 0, 0)), pl.BlockSpec((None, 128, 512), lambda j, c: (j, 0, 0)),
        pl.BlockSpec((None, 512, 128), lambda j, c: (j, 0, 0)), pl.BlockSpec((None, 512, 128), lambda j, c: (j, 0, 0)),
        pl.BlockSpec((1, 128), lambda j, c: (0, j)),
    ]
    return n, ch, specs


def _s5_fwd(proj, prep, base128, name, shards=None):
    L = proj.shape[0]
    T, J = S5_CHUNK, S5_LANE_BLOCKS
    n, _, specs = _s5_specs(L, base128, False)

    def body(u_ref, ar, ai, bbr, bbi, ccr, cci, dv, y_ref, sr_ref, si_ref, carry):
        @pl.when(pl.program_id(1) == 0)
        def _():
            carry[...] = jnp.zeros_like(carry)

        hin_r, hin_i = carry[0:1, :], carry[1:2, :]
        sr_ref[0] = jnp.broadcast_to(hin_r, (8, 512))
        si_ref[0] = jnp.broadcast_to(hin_i, (8, 512))
        y, out_r, out_i = _s5_chunk(u_ref[...], hin_r, hin_i, ar[...], ai[...], bbr[...], bbi[...], ccr[...], cci[...], dv[...])
        y_ref[...] = y
        carry[0:1, :] = out_r
        carry[1:2, :] = out_i

    st = pl.BlockSpec((1, 8, 512), lambda j, c: (c, 0, j))
    return _host_call(
        body, shards, (proj, *prep), name=name, grid=(J, n),
        out_shape=[jax.ShapeDtypeStruct((L, 512), f32), jax.ShapeDtypeStruct((n, 8, 2048), f32), jax.ShapeDtypeStruct((n, 8, 2048), f32)],
        in_specs=specs, out_specs=[pl.BlockSpec((T, 128), lambda j, c: (c, j)), st, st],
        scratch_shapes=[pltpu.VMEM((8, 512), f32)],
        compiler_params=_params(("parallel", "arbitrary")), gather=True)


def _s5_bwd(proj, prep, st_r, st_i, dy, base128, name, sends=None):
    L = proj.shape[0]
    T, J = S5_CHUNK, S5_LANE_BLOCKS
    n, ch, specs = _s5_specs(L, base128, True)

    def body(u_ref, ar, ai, bbr, bbi, ccr, cci, dv, sr_ref, si_ref, dy_ref,
             du_ref, dar, dai, dbbr, dbbi, dccr, dcci, ddv, dcarry):
        @pl.when(pl.program_id(1) == 0)
        def _():
            dcarry[...] = jnp.zeros_like(dcarry)
            for r in (dar, dai, dbbr, dbbi, dccr, dcci, ddv):
                r[...] = jnp.zeros_like(r)

        _, vjp = jax.vjp(_s5_chunk, u_ref[...], sr_ref[0, 0:1, :], si_ref[0, 0:1, :], ar[...], ai[...],
                         bbr[...], bbi[...], ccr[...], cci[...], dv[...])
        g = vjp((dy_ref[...], dcarry[0:1, :], dcarry[1:2, :]))
        du_ref[...] = g[0].astype(bf16)
        dcarry[0:1, :] = g[1]
        dcarry[1:2, :] = g[2]
        for r, val in zip((dar, dai, dbbr, dbbi, dccr, dcci, ddv), g[3:]):
            r[...] += val

    st = pl.BlockSpec((1, 8, 512), lambda j, c: (ch(c), 0, j))
    outs, recvs = _host_call(
        body, sends, (proj, *prep, st_r, st_i, dy), name=name, grid=(J, n),
        out_shape=[jax.ShapeDtypeStruct((L, 512), bf16),
                   jax.ShapeDtypeStruct((1, 2048), f32), jax.ShapeDtypeStruct((1, 2048), f32),
                   jax.ShapeDtypeStruct((J, 128, 512), f32), jax.ShapeDtypeStruct((J, 128, 512), f32),
                   jax.ShapeDtypeStruct((J, 512, 128), f32), jax.ShapeDtypeStruct((J, 512, 128), f32),
                   jax.ShapeDtypeStruct((1, 512), f32)],
        in_specs=specs + [st, st, pl.BlockSpec((T, 128), lambda j, c: (ch(c), j))],
        out_specs=[pl.BlockSpec((T, 128), lambda j, c: (ch(c), j))] + specs[1:],
        scratch_shapes=[pltpu.VMEM((8, 512), f32)],
        compiler_params=_params(("parallel", "arbitrary")))
    return outs[0], tuple(outs[1:]), recvs


def _glu_f(y, w, b):
    z = jax.nn.gelu(y)
    return z * jax.nn.sigmoid(jnp.dot(z.astype(bf16), w.astype(bf16), preferred_element_type=f32) + b)


def _glu_fwd(y, w, b, name):
    L = y.shape[0]
    tm = _tile(L, 512, 8)

    def body(y_ref, w_ref, b_ref, o_ref):
        o_ref[...] = _glu_f(y_ref[...], w_ref[...], b_ref[...]).astype(bf16)

    row = pl.BlockSpec((tm, 512), lambda i: (i, 0))
    return pl.pallas_call(
        body, name=name, grid=(L // tm,), out_shape=jax.ShapeDtypeStruct((L, 512), bf16),
        in_specs=[row, pl.BlockSpec((512, 512), lambda i: (0, 0)), pl.BlockSpec((1, 512), lambda i: (0, 0))],
        out_specs=row, compiler_params=_params(("parallel",)),
    )(y, w, b)


def _glu_bwd(y, w, b, dout, name):
    L = y.shape[0]
    tm = _tile(L, 512, 8)

    def body(y_ref, w_ref, b_ref, do_ref, dy_ref, dw_ref, db_ref):
        @pl.when(pl.program_id(0) == 0)
        def _():
            dw_ref[...] = jnp.zeros_like(dw_ref)
            db_ref[...] = jnp.zeros_like(db_ref)

        _, vjp = jax.vjp(_glu_f, y_ref[...], w_ref[...], b_ref[...])
        dy, dw, db = vjp(do_ref[...])
        dy_ref[...] = dy
        dw_ref[...] += dw
        db_ref[...] += db

    row = pl.BlockSpec((tm, 512), lambda i: (i, 0))
    wspec, bspec = pl.BlockSpec((512, 512), lambda i: (0, 0)), pl.BlockSpec((1, 512), lambda i: (0, 0))
    return pl.pallas_call(
        body, name=name, grid=(L // tm,),
        out_shape=[jax.ShapeDtypeStruct((L, 512), f32), jax.ShapeDtypeStruct((512, 512), f32), jax.ShapeDtypeStruct((1, 512), f32)],
        in_specs=[row, wspec, bspec, row], out_specs=[row, wspec, bspec],
        compiler_params=_params(("arbitrary",)),
    )(y, w, b, dout)


def _rope_tables(positions):
    half = ROPE_DIM // 2
    inv_freq = ROPE_THETA ** (-jnp.arange(half, dtype=f32) / half)
    ang = positions.astype(f32)[:, None] * inv_freq
    L = positions.shape[0]
    cos = jnp.concatenate([jnp.cos(ang), jnp.cos(ang), jnp.ones((L, HEAD_DIM - ROPE_DIM), f32)], axis=1)
    sin = jnp.concatenate([jnp.sin(ang), jnp.sin(ang), jnp.zeros((L, HEAD_DIM - ROPE_DIM), f32)], axis=1)
    return jnp.tile(cos, (1, 8)), jnp.tile(sin, (1, 8))


def _rope_matrix(w):
    half = ROPE_DIM // 2
    r = lax.broadcasted_iota(jnp.int32, (w, w), 0)
    c = lax.broadcasted_iota(jnp.int32, (w, w), 1)
    same = (r // HEAD_DIM) == (c // HEAD_DIM)
    rr, cc = r % HEAD_DIM, c % HEAD_DIM
    return (jnp.where(same & (cc < half) & (rr == cc + half), -1.0, 0.0)
            + jnp.where(same & (cc >= half) & (cc < ROPE_DIM) & (rr == cc - half), 1.0, 0.0))


def _rope(items, cos, sin, *, transpose, out_dtype, name):
    L = cos.shape[0]
    tm = _tile(L, 512, 8)
    n = len(items)

    def body(*refs):
        xs, c_ref, s_ref, outs = refs[:n], refs[n], refs[n + 1], refs[n + 2:]
        for x_ref, o_ref in zip(xs, outs):
            w = x_ref.shape[1]
            x = x_ref[...].astype(f32)
            c, s = c_ref[:, :w], s_ref[:, :w]
            rot = _rope_matrix(w)
            if transpose:
                y = x * c + lax.dot_general(x * s, rot, (((1,), (1,)), ((), ())), preferred_element_type=f32)
            else:
                y = x * c + jnp.dot(x, rot, preferred_element_type=f32) * s
            o_ref[...] = y.astype(o_ref.dtype)

    in_specs = [pl.BlockSpec((tm, w), functools.partial(lambda i, col: (i, col), col=col)) for _, col, w in items]
    tab = pl.BlockSpec((tm, 512), lambda i: (i, 0))
    outs = pl.pallas_call(
        body, name=name, grid=(L // tm,),
        out_shape=[jax.ShapeDtypeStruct((L, w), out_dtype) for _, _, w in items],
        in_specs=in_specs + [tab, tab], out_specs=[pl.BlockSpec((tm, w), lambda i: (i, 0)) for _, _, w in items],
        compiler_params=_params(("parallel",)),
    )(*[a for a, _, _ in items], cos, sin)
    return list(outs)


def _attn_head(q, kp, kc, vp, vc, sink, *, lim, max_dist):
    T = ATT_BLOCK
    k2 = jnp.concatenate([kp, kc], axis=0)
    v2 = jnp.concatenate([vp, vc], axis=0)
    s = lax.dot_general(q, k2, (((1,), (1,)), ((), ())), preferred_element_type=f32) * (HEAD_DIM ** -0.5)
    t = lax.broadcasted_iota(jnp.int32, (T, 2 * T), 0)
    j = lax.broadcasted_iota(jnp.int32, (T, 2 * T), 1)
    dist = T + t - j
    valid = (dist >= 0) & (dist <= max_dist) & (j >= lim)
    s = jnp.where(valid, s, -jnp.inf)
    m = lax.stop_gradient(jnp.max(s, axis=-1, keepdims=True))
    p = jnp.exp(s - m)
    den = jnp.sum(p, axis=-1, keepdims=True)
    o = jnp.dot(p, v2, preferred_element_type=f32) / den
    lse = jnp.broadcast_to(m + jnp.log(den), (T, HEAD_DIM))
    if sink is None:
        return o, lse
    return o * jax.nn.sigmoid(lse - sink)


def _attn_specs(L, q_col, k_col, v_col, wk, rev):
    T = ATT_BLOCK
    n = L // T
    blk = (lambda i: n - 1 - i) if rev else (lambda i: i)
    prev = lambda i: jnp.maximum(blk(i) - 1, 0)
    specs = [
        pl.BlockSpec((T, 512), lambda i: (blk(i), q_col)),
        pl.BlockSpec((T, wk), lambda i: (prev(i), k_col)), pl.BlockSpec((T, wk), lambda i: (blk(i), k_col)),
        pl.BlockSpec((T, wk), lambda i: (prev(i), v_col)), pl.BlockSpec((T, wk), lambda i: (blk(i), v_col)),
    ]
    return n, blk, specs


def _attn_fwd(qa, ka, va, sinks, *, q_col, k_col, v_col, hkv, nbc, max_dist, name, shards=None):
    L = qa.shape[0]
    T, HQ, HD = ATT_BLOCK, 8, HEAD_DIM
    wk = hkv * HD
    n, _, specs = _attn_specs(L, q_col, k_col, v_col, wk, False)
    grp = HQ // hkv
    gated = sinks is not None

    def body(*refs):
        q_ref, kp_ref, kc_ref, vp_ref, vc_ref = refs[:5]
        rest = refs[5:]
        lim = jnp.where(pl.program_id(0) % nbc == 0, T, 0)
        for h in range(HQ):
            hs, ks = slice(h * HD, (h + 1) * HD), slice((h // grp) * HD, (h // grp + 1) * HD)
            res = _attn_head(q_ref[:, hs], kp_ref[:, ks], kc_ref[:, ks], vp_ref[:, ks], vc_ref[:, ks],
                             rest[0][:, hs] if gated else None, lim=lim, max_dist=max_dist)
            if gated:
                rest[1][:, hs] = res.astype(bf16)
            else:
                rest[0][:, hs] = res[0]
                rest[1][:, hs] = res[1]

    row = pl.BlockSpec((T, 512), lambda i: (i, 0))
    if gated:
        return _host_call(
            body, shards, (qa, ka, ka, va, va, sinks), name=name, grid=(n,),
            out_shape=jax.ShapeDtypeStruct((L, 512), bf16),
            in_specs=specs + [pl.BlockSpec((1, 512), lambda i: (0, 0))], out_specs=row, scratch_shapes=[],
            compiler_params=_params(("parallel",)), gather=True)
    return _host_call(
        body, shards, (qa, ka, ka, va, va), name=name, grid=(n,),
        out_shape=[jax.ShapeDtypeStruct((L, 512), f32)] * 2,
        in_specs=specs, out_specs=[row, row], scratch_shapes=[], compiler_params=_params(("parallel",)), gather=True)


def _attn_bwd(qa, ka, va, sinks, douts, *, q_col, k_col, v_col, hkv, nbc, max_dist, name, sends=None):
    L = qa.shape[0]
    T, HQ, HD = ATT_BLOCK, 8, HEAD_DIM
    wk = hkv * HD
    n, blk, specs = _attn_specs(L, q_col, k_col, v_col, wk, True)
    grp = HQ // hkv
    gated = sinks is not None
    nd = len(douts)

    def body(*refs):
        q_ref, kp_ref, kc_ref, vp_ref, vc_ref = refs[:5]
        pos = 5
        sink_ref = None
        if gated:
            sink_ref = refs[pos]
            pos += 1
        d_refs = refs[pos:pos + nd]
        pos += nd
        dq_ref, dk_ref, dv_ref = refs[pos:pos + 3]
        pos += 3
        dsink_ref = None
        if gated:
            dsink_ref = refs[pos]
            pos += 1
        carry_k, carry_v = refs[pos:pos + 2]

        @pl.when(pl.program_id(0) == 0)
        def _():
            carry_k[...] = jnp.zeros_like(carry_k)
            carry_v[...] = jnp.zeros_like(carry_v)
            if gated:
                dsink_ref[...] = jnp.zeros_like(dsink_ref)

        lim = jnp.where(blk(pl.program_id(0)) % nbc == 0, T, 0)
        dkp = [jnp.zeros((T, HD), f32) for _ in range(hkv)]
        dkc = [jnp.zeros((T, HD), f32) for _ in range(hkv)]
        dvp = [jnp.zeros((T, HD), f32) for _ in range(hkv)]
        dvc = [jnp.zeros((T, HD), f32) for _ in range(hkv)]
        for h in range(HQ):
            g = h // grp
            hs, ks = slice(h * HD, (h + 1) * HD), slice(g * HD, (g + 1) * HD)
            fn = functools.partial(_attn_head, lim=lim, max_dist=max_dist)
            prim = (q_ref[:, hs], kp_ref[:, ks], kc_ref[:, ks], vp_ref[:, ks], vc_ref[:, ks])
            if gated:
                _, vjp = jax.vjp(fn, *prim, sink_ref[:, hs])
                dq, a, b, c, d, ds = vjp(d_refs[0][:, hs].astype(f32))
                dsink_ref[:, hs] += ds
            else:
                _, vjp = jax.vjp(lambda *p: fn(*p, None), *prim)
                dq, a, b, c, d = vjp((d_refs[0][:, hs], d_refs[1][:, hs]))
            dq_ref[:, hs] = dq.astype(bf16)
            dkp[g], dkc[g], dvp[g], dvc[g] = dkp[g] + a, dkc[g] + b, dvp[g] + c, dvc[g] + d
        for g in range(hkv):
            ks = slice(g * HD, (g + 1) * HD)
            dk_ref[:, ks] = (dkc[g] + carry_k[:, ks]).astype(bf16)
            dv_ref[:, ks] = (dvc[g] + carry_v[:, ks]).astype(bf16)
            carry_k[:, ks] = dkp[g]
            carry_v[:, ks] = dvp[g]

    row = lambda w: pl.BlockSpec((T, w), lambda i: (blk(i), 0))
    vec = pl.BlockSpec((1, 512), lambda i: (0, 0))
    in_specs = specs + ([vec] if gated else []) + [row(512)] * nd
    out_shape = [jax.ShapeDtypeStruct((L, 512), bf16), jax.ShapeDtypeStruct((L, wk), bf16), jax.ShapeDtypeStruct((L, wk), bf16)]
    out_specs = [row(512), row(wk), row(wk)]
    if gated:
        out_shape.append(jax.ShapeDtypeStruct((1, 512), f32))
        out_specs.append(vec)
    args = (qa, ka, ka, va, va) + ((sinks,) if gated else ()) + tuple(douts)
    outs, recvs = _host_call(
        body, sends, args, name=name, grid=(n,), out_shape=out_shape, in_specs=in_specs, out_specs=out_specs,
        scratch_shapes=[pltpu.VMEM((T, wk), f32), pltpu.VMEM((T, wk), f32)],
        compiler_params=_params(("arbitrary",)))
    return outs if sends is None else (outs, recvs)


def _dilmix_f(o0, o1, o2, l0, l1, l2):
    m = jnp.maximum(jnp.maximum(l0, l1), l2)
    e0, e1, e2 = jnp.exp(l0 - m), jnp.exp(l1 - m), jnp.exp(l2 - m)
    return (e0 * o0 + e1 * o1 + e2 * o2) / (e0 + e1 + e2)


def _dilmix_fwd(os_, ls, name):
    L = os_[0].shape[0]
    tm = _tile(L, 512, 8)

    def body(o0, o1, o2, l0, l1, l2, out):
        out[...] = _dilmix_f(o0[...], o1[...], o2[...], l0[...], l1[...], l2[...]).astype(bf16)

    row = pl.BlockSpec((tm, 512), lambda i: (i, 0))
    return pl.pallas_call(
        body, name=name, grid=(L // tm,), out_shape=jax.ShapeDtypeStruct((L, 512), bf16),
        in_specs=[row] * 6, out_specs=row, compiler_params=_params(("parallel",)),
    )(*os_, *ls)


def _dilmix_bwd(os_, ls, dout, name):
    L = os_[0].shape[0]
    tm = _tile(L, 512, 8)

    def body(o0, o1, o2, l0, l1, l2, d, *outs):
        _, vjp = jax.vjp(_dilmix_f, o0[...], o1[...], o2[...], l0[...], l1[...], l2[...])
        for r, val in zip(outs, vjp(d[...].astype(f32))):
            r[...] = val

    row = pl.BlockSpec((tm, 512), lambda i: (i, 0))
    outs = pl.pallas_call(
        body, name=name, grid=(L // tm,), out_shape=[jax.ShapeDtypeStruct((L, 512), f32)] * 6,
        in_specs=[row] * 7, out_specs=[row] * 6, compiler_params=_params(("parallel",)),
    )(*os_, *ls, dout)
    return outs[:3], outs[3:]


def _to_strided(z, dil):
    L, w = z.shape
    return z.reshape(L // dil, dil, w).transpose(1, 0, 2).reshape(L, w)


def _from_strided(z, dil):
    L, w = z.shape
    return z.reshape(dil, L // dil, w).transpose(1, 0, 2).reshape(L, w)


def _adamw_math(w, g, m, v):
    m = ADAM_B1 * m + (1.0 - ADAM_B1) * g
    v = ADAM_B2 * v + (1.0 - ADAM_B2) * (g * g)
    m_hat = m / (1.0 - ADAM_B1 ** ADAM_STEP)
    v_hat = v / (1.0 - ADAM_B2 ** ADAM_STEP)
    delta = -ADAM_LR * (m_hat / (jnp.sqrt(v_hat) + ADAM_EPS) + ADAM_WD * w)
    return delta, m, v


def _adamw(w, m, v, slots, name):
    depth, R, C = w.shape
    tr = _tile(R, max(8, 131072 // C), 8)
    outs = None
    for l in range(depth):
        def body(w_ref, m_ref, v_ref, s_ref, *rest):
            g_ref, d_ref, nm_ref, nv_ref = rest[-4:]
            g = s_ref[0].astype(f32)
            for i in range(1, N_DEV):
                g = g + s_ref[i].astype(f32)
            delta, nm, nv = _adamw_math(w_ref[0], g, m_ref[0], v_ref[0])
            g_ref[0], d_ref[0], nm_ref[0], nv_ref[0] = g, delta, nm, nv

        blk = pl.BlockSpec((1, tr, C), functools.partial(lambda i, l: (l, i, 0), l=l))
        carried = [] if outs is None else list(outs)
        outs = pl.pallas_call(
            body, name=f"{name}_{l}", grid=(R // tr,), out_shape=[jax.ShapeDtypeStruct(w.shape, f32)] * 4,
            in_specs=[blk, blk, blk, pl.BlockSpec((N_DEV, tr, C), lambda i: (0, i, 0))]
            + [pl.BlockSpec(memory_space=pl.ANY)] * len(carried),
            out_specs=[blk] * 4, input_output_aliases={4 + j: j for j in range(len(carried))},
            compiler_params=_params(("parallel",)),
        )(w, m, v, slots[l], *carried)
    return outs


def _adamw_packed(w, m, v, slots, name):
    R = w.shape[0]
    tr = _tile(R, 512, 8)

    def body(w_ref, m_ref, v_ref, s_ref, g_ref, d_ref, nm_ref, nv_ref):
        g = s_ref[0]
        for i in range(1, N_DEV):
            g = g + s_ref[i]
        delta, nm, nv = _adamw_math(w_ref[...], g, m_ref[...], v_ref[...])
        g_ref[...], d_ref[...], nm_ref[...], nv_ref[...] = g, delta, nm, nv

    blk = pl.BlockSpec((tr, 128), lambda i: (i, 0))
    return pl.pallas_call(
        body, name=name, grid=(R // tr,), out_shape=[jax.ShapeDtypeStruct(w.shape, f32)] * 4,
        in_specs=[blk, blk, blk, pl.BlockSpec((N_DEV, tr, 128), lambda i: (0, i, 0))], out_specs=[blk] * 4,
        compiler_params=_params(("parallel",)),
    )(w, m, v, slots)


def _cols_gathered(g):
    return jnp.concatenate([g[d] for d in range(N_DEV)], axis=-1)


def _cols_scatter(full):
    c = full.shape[-1] // N_DEV
    return jnp.stack([full[..., d * c:(d + 1) * c] for d in range(N_DEV)])


def _win_segments(D):
    b = 4 * D
    return (((O_GATES, O_GATES + b), 0), ((0, O_GLR), b), ((O_S5U, O_GATES), b + O_GLR), ((O_GLR, O_S5U), b + P_GLR))


def _win_from_shards(g):
    rows, c = g.shape[1], g.shape[2]
    D = (N_DEV * c - O_GATES) // 4
    pieces = []
    for (lo, hi), _ in sorted(_win_segments(D), key=lambda t: t[1]):
        for d in range(N_DEV):
            a, b = max(lo, d * c), min(hi, (d + 1) * c)
            if a < b:
                pieces.append(g[d][:, a - d * c:b - d * c])
    pieces.append(jnp.zeros((rows, 128 - GLA_LOWRANK), g.dtype))
    return jnp.concatenate(pieces, axis=1)


def _win_to_shards(wp, D):
    c = (O_GATES + 4 * D) // N_DEV
    segs = sorted(_win_segments(D), key=lambda t: t[0][0])
    out = []
    for d in range(N_DEV):
        pieces = []
        for (lo, hi), off in segs:
            a, b = max(lo, d * c), min(hi, (d + 1) * c)
            if a < b:
                pieces.append(wp[:, off + a - lo:off + b - lo])
        out.append(jnp.concatenate(pieces, axis=1))
    return jnp.stack(out)


SMALL = ("norm1_g", "gla_a_b", "gla_norm_g", "s5_lambda_re", "s5_lambda_im", "s5_log_dt", "s5_b_re", "s5_b_im",
         "s5_c_re", "s5_c_im", "s5_d", "s5_glu_b", "swa_sinks", "norm2_g", "final_norm_g")
SHARDED = ("w_in", "gla_a2", "s5_glu_w", "w_branch", "w_out", "w_ffn_gate", "w_ffn_up", "w_ffn_down")
WEIGHTS = ("norm1_g", "w_in", "gla_a2", "gla_a_b", "gla_norm_g", "s5_lambda_re", "s5_lambda_im", "s5_log_dt", "s5_b_re",
           "s5_b_im", "s5_c_re", "s5_c_im", "s5_d", "s5_glu_w", "s5_glu_b", "swa_sinks", "w_branch", "w_out", "norm2_g",
           "w_ffn_gate", "w_ffn_up", "w_ffn_down", "final_norm_g")


def _pack(arrs):
    flat = jnp.concatenate([a.reshape(-1) for a in arrs])
    n = flat.shape[0]
    rows = -(-n // 1024) * 8
    return jnp.pad(flat, (0, rows * 128 - n)).reshape(rows, 128)


def _unpack(packed, like):
    flat = packed.reshape(-1)
    out, pos = [], 0
    for a in like:
        out.append(flat[pos:pos + a.size].reshape(a.shape))
        pos += a.size
    return out


def kernel(x, positions, norm1_g, w_in, gla_a2, gla_a_b, gla_norm_g, s5_lambda_re, s5_lambda_im, s5_log_dt, s5_b_re, s5_b_im, s5_c_re, s5_c_im, s5_d, s5_glu_w, s5_glu_b, swa_sinks, w_branch, w_out, norm2_g, w_ffn_gate, w_ffn_up, w_ffn_down, final_norm_g, loss_target, m_norm1_g, m_w_in, m_gla_a2, m_gla_a_b, m_gla_norm_g, m_s5_lambda_re, m_s5_lambda_im, m_s5_log_dt, m_s5_b_re, m_s5_b_im, m_s5_c_re, m_s5_c_im, m_s5_d, m_s5_glu_w, m_s5_glu_b, m_swa_sinks, m_w_branch, m_w_out, m_norm2_g, m_w_ffn_gate, m_w_ffn_up, m_w_ffn_down, m_final_norm_g, v_norm1_g, v_w_in, v_gla_a2, v_gla_a_b, v_gla_norm_g, v_s5_lambda_re, v_s5_lambda_im, v_s5_log_dt, v_s5_b_re, v_s5_b_im, v_s5_c_re, v_s5_c_im, v_s5_d, v_s5_glu_w, v_s5_glu_b, v_swa_sinks, v_w_branch, v_w_out, v_norm2_g, v_w_ffn_gate, v_w_ffn_up, v_w_ffn_down, v_final_norm_g):
    W = dict(norm1_g=norm1_g, w_in=w_in, gla_a2=gla_a2, gla_a_b=gla_a_b, gla_norm_g=gla_norm_g, s5_lambda_re=s5_lambda_re, s5_lambda_im=s5_lambda_im, s5_log_dt=s5_log_dt, s5_b_re=s5_b_re, s5_b_im=s5_b_im, s5_c_re=s5_c_re, s5_c_im=s5_c_im, s5_d=s5_d, s5_glu_w=s5_glu_w, s5_glu_b=s5_glu_b, swa_sinks=swa_sinks, w_branch=w_branch, w_out=w_out, norm2_g=norm2_g, w_ffn_gate=w_ffn_gate, w_ffn_up=w_ffn_up, w_ffn_down=w_ffn_down, final_norm_g=final_norm_g)
    Mo = dict(norm1_g=m_norm1_g, w_in=m_w_in, gla_a2=m_gla_a2, gla_a_b=m_gla_a_b, gla_norm_g=m_gla_norm_g, s5_lambda_re=m_s5_lambda_re, s5_lambda_im=m_s5_lambda_im, s5_log_dt=m_s5_log_dt, s5_b_re=m_s5_b_re, s5_b_im=m_s5_b_im, s5_c_re=m_s5_c_re, s5_c_im=m_s5_c_im, s5_d=m_s5_d, s5_glu_w=m_s5_glu_w, s5_glu_b=m_s5_glu_b, swa_sinks=m_swa_sinks, w_branch=m_w_branch, w_out=m_w_out, norm2_g=m_norm2_g, w_ffn_gate=m_w_ffn_gate, w_ffn_up=m_w_ffn_up, w_ffn_down=m_w_ffn_down, final_norm_g=m_final_norm_g)
    Vo = dict(norm1_g=v_norm1_g, w_in=v_w_in, gla_a2=v_gla_a2, gla_a_b=v_gla_a_b, gla_norm_g=v_gla_norm_g, s5_lambda_re=v_s5_lambda_re, s5_lambda_im=v_s5_lambda_im, s5_log_dt=v_s5_log_dt, s5_b_re=v_s5_b_re, s5_b_im=v_s5_b_im, s5_c_re=v_s5_c_re, s5_c_im=v_s5_c_im, s5_d=v_s5_d, s5_glu_w=v_s5_glu_w, s5_glu_b=v_s5_glu_b, swa_sinks=v_swa_sinks, w_branch=v_w_branch, w_out=v_w_out, norm2_g=v_norm2_g, w_ffn_gate=v_w_ffn_gate, w_ffn_up=v_w_ffn_up, w_ffn_down=v_w_ffn_down, final_norm_g=v_final_norm_g)

    L, D = x.shape[1], x.shape[2]
    depth = norm1_g.shape[0]
    xs = x.reshape(L, D)
    target = loss_target.reshape(L, D)
    base128 = 4 * D // 128

    in_group = ("w_in", "gla_a2", "s5_glu_w")
    full = {}

    def shards(keys, l):
        if l >= depth:
            return None
        return [W[k][l] if k in ("gla_a2", "s5_glu_w") else W[k][l].astype(bf16) for k in keys]

    def landed(keys, l, gathered):
        for k, g in zip(keys, gathered):
            if k == "w_in":
                full[k, l] = _win_from_shards(g)
            elif k == "gla_a2":
                full[k, l] = jnp.pad(_cols_gathered(g), ((0, 128 - GLA_LOWRANK), (0, 0)))
            elif k in ("w_branch", "w_ffn_gate", "w_ffn_up"):
                full[k, l] = _cols_gathered(g)
            else:
                full[k, l] = g.reshape((-1, g.shape[-1]))

    landed(in_group, 0, _all_gather(shards(in_group, 0), "gather_w_in0"))

    cos, sin = _rope_tables(positions.reshape(L))

    saved = []
    cur = xs
    for l in range(depth):
        s = {"x": cur}
        nxt = l + 1
        h1 = _rms_fwd(cur, norm1_g[l][None], f"rms1_fwd{l}")
        if l == 0:
            keys = ("w_branch", "w_out", "w_ffn_gate")
            proj, got = _matmul(h1, full["w_in", l], sends=shards(keys, 0), gather=True, name=f"proj_in{l}")
            landed(keys, 0, got)
        else:
            proj = _matmul(h1, full["w_in", l], name=f"proj_in{l}")
        s["h1"], s["proj"] = h1, proj
        ab, ng = gla_a_b[l][None], gla_norm_g[l].reshape(1, 512)
        (o_gla, s["gla_st"]), got = _gla_fwd(proj, full["gla_a2", l], ab, ng, base128, f"gla_fwd{l}", shards=shards(in_group, nxt))
        landed(in_group, nxt, got)
        prep, s["prep_vjp"] = jax.vjp(_s5_prep, s5_lambda_re[l], s5_lambda_im[l], s5_log_dt[l], s5_b_re[l], s5_b_im[l],
                                      s5_c_re[l], s5_c_im[l], s5_d[l])
        s["prep"] = prep
        (y_s5, s["s5_r"], s["s5_i"]), got = _s5_fwd(proj, prep, base128, f"s5_fwd{l}", shards=shards(("w_branch", "w_out"), nxt))
        landed(("w_branch", "w_out"), nxt, got)
        s["y_s5"] = y_s5
        o_s5 = _glu_fwd(y_s5, full["s5_glu_w", l], s5_glu_b[l][None], f"glu_fwd{l}")
        sinks_b = jnp.repeat(swa_sinks[l], HEAD_DIM)[None]
        s["sinks_b"] = sinks_b
        nb = L // ATT_BLOCK
        cq, ck, cv = (base128 + P_CQ // 128) // 4, (base128 + P_CK // 128) // 4, (base128 + P_CV // 128) // 4
        sq_col, sk_col, sv_col = (base128 + P_SQ // 128) // 4, base128 + P_SK // 128, base128 + P_SV // 128
        cq_r, ck_r, sq_r, sk_r = _rope([(proj, cq, 512), (proj, ck, 512), (proj, sq_col, 512), (proj, sk_col, 128)], cos, sin,
                                       transpose=False, out_dtype=f32, name=f"rope_fwd{l}")
        s["rot"] = (cq_r, ck_r, sq_r, sk_r)
        o_swa, got = _attn_fwd(sq_r, sk_r, proj, sinks_b, q_col=0, k_col=0, v_col=sv_col, hkv=SWA_KV_HEADS, nbc=nb,
                               max_dist=SWA_WINDOW - 1, name=f"swa_fwd{l}", shards=shards(("w_ffn_up",), 0) if l == 0 else None)
        landed(("w_ffn_up",), 0, got)
        dil_o, dil_l, s["dil_in"] = [], [], []
        riders = ((("w_ffn_down",), 0 if l == 0 else depth), (("w_ffn_gate",), nxt), (("w_ffn_up",), nxt))
        for (window, dil), (keys, kl) in zip(DIL_CONFIGS, riders):
            if dil == 1:
                (o, lse), got = _attn_fwd(cq_r, ck_r, proj, None, q_col=0, k_col=0, v_col=cv, hkv=8, nbc=nb,
                                          max_dist=window // dil, name=f"dil{dil}_fwd{l}", shards=shards(keys, kl))
                s["dil_in"].append(None)
            else:
                qs_, ks_ = _to_strided(cq_r, dil), _to_strided(ck_r, dil)
                vs_ = _to_strided(proj[:, 4 * D + P_CV:4 * D + P_CV + 512], dil)
                (o, lse), got = _attn_fwd(qs_, ks_, vs_, None, q_col=0, k_col=0, v_col=0, hkv=8,
                                          nbc=nb // dil, max_dist=window // dil, name=f"dil{dil}_fwd{l}", shards=shards(keys, kl))
                o, lse = _from_strided(o, dil), _from_strided(lse, dil)
                s["dil_in"].append((qs_, ks_, vs_))
            landed(keys, kl, got)
            dil_o.append(o)
            dil_l.append(lse)
        s["dil_o"], s["dil_l"] = dil_o, dil_l
        o_dil = _dilmix_fwd(dil_o, dil_l, f"dilmix_fwd{l}")
        branches = (o_gla, o_s5, o_dil, o_swa)
        s["branches"] = branches
        ys = [_matmul(br, full["w_branch", l][m], name=f"branch{m}_fwd{l}") for m, br in enumerate(branches)]
        s["ys"] = ys
        mixed = _merge_fwd(proj, ys, D, f"merge_fwd{l}")
        s["mixed"] = mixed
        x2 = _matmul(mixed, full["w_out", l], res=cur, name=f"out_fwd{l}")
        s["x2"] = x2
        h2 = _rms_fwd(x2, norm2_g[l][None], f"rms2_fwd{l}")
        rider = shards(("w_ffn_down",), nxt)
        if rider:
            a, got = _matmul(h2, full["w_ffn_gate", l], sends=rider, gather=True, name=f"ffn_gate_fwd{l}")
            landed(("w_ffn_down",), nxt, got)
        else:
            a = _matmul(h2, full["w_ffn_gate", l], name=f"ffn_gate_fwd{l}")
        b = _matmul(h2, full["w_ffn_up", l], name=f"ffn_up_fwd{l}")
        act = _swiglu_fwd(a, b, f"swiglu_fwd{l}")
        s["h2"], s["a"], s["b"], s["act"] = h2, a, b, act
        cur = _matmul(act, full["w_ffn_down", l], res=x2, name=f"ffn_down_fwd{l}")
        saved.append(s)
    win_p, a2p, glu_w, wb, wout, wg, wu, wd = (
        [full[k, l] for l in range(depth)]
        for k in ("w_in", "gla_a2", "s5_glu_w", "w_branch", "w_out", "w_ffn_gate", "w_ffn_up", "w_ffn_down"))

    loss_part, dcur, dcur_b, dgf = _final_loss(cur, final_norm_g[None], target, "final_loss")
    loss = lax.psum(loss_part, AXES)

    small_g = {k: [None] * depth for k in SMALL if k != "final_norm_g"}
    recv = {k: [None] * depth for k in SHARDED}
    in_group = ("w_in", "gla_a2", "s5_glu_w")
    pending = None
    for l in reversed(range(depth)):
        s = saved[l]
        proj = s["proj"]
        dact = _matmul(dcur_b, wd[l], mode="nt", name=f"ffn_down_dx{l}")
        g_down = _matmul(s["act"], dcur_b, mode="tn", out_dtype=bf16, name=f"ffn_down_dw{l}")
        da, db = _swiglu_bwd(s["a"], s["b"], dact, f"swiglu_bwd{l}")
        dh2 = _matmul(da, wg[l], mode="nt", name=f"ffn_gate_dx{l}")
        dh2 = _matmul(db, wu[l], mode="nt", res=dh2, name=f"ffn_up_dx{l}")
        g_gate = _matmul(s["h2"], da, mode="tn", out_dtype=bf16, name=f"ffn_gate_dw{l}")
        g_up = _matmul(s["h2"], db, mode="tn", out_dtype=bf16, name=f"ffn_up_dw{l}")
        ffn_sends = (("w_ffn_down", g_down.reshape((N_DEV, -1, D))), ("w_ffn_gate", _cols_scatter(g_gate)),
                     ("w_ffn_up", _cols_scatter(g_up)))
        dx2, dx2_b, dg2 = _rms_bwd(s["x2"], norm2_g[l][None], dh2, dcur, f"rms2_bwd{l}")
        small_g["norm2_g"][l] = dg2[0]
        dmixed = _matmul(dx2_b, wout[l], mode="nt", name=f"out_dx{l}")
        g_out = _matmul(s["mixed"], dx2_b, mode="tn", out_dtype=bf16, name=f"out_dw{l}")
        dys, dgates = _merge_bwd(proj, s["ys"], dmixed, D, f"merge_bwd{l}")
        dbr = [_matmul(dys[m], wb[l][m], mode="nt", name=f"branch{m}_dx{l}") for m in range(4)]
        g_branch = jnp.stack([_matmul(s["branches"][m], dys[m], mode="tn", out_dtype=bf16, name=f"branch{m}_dw{l}")
                              for m in range(4)])
        d_gla, d_s5, d_dil, d_swa = dbr
        ab, ng = gla_a_b[l][None], gla_norm_g[l].reshape(1, 512)
        (dgq, dgk, dgv, dgr, dglr, da2, dab, dng), got = _gla_bwd(proj, a2p[l], ab, ng, s["gla_st"], d_gla, base128,
                                                                   f"gla_bwd{l}", sends=pending)
        for k, r in zip(in_group, got):
            recv[k][l + 1] = r
        small_g["gla_a_b"][l] = dab[0]
        small_g["gla_norm_g"][l] = dng.reshape(GLA_HEADS, GLA_DV)
        dy_s5, dglu_w, dglu_b = _glu_bwd(s["y_s5"], glu_w[l], s5_glu_b[l][None], d_s5, f"glu_bwd{l}")
        small_g["s5_glu_b"][l] = dglu_b[0]
        ds5u, dprep, got = _s5_bwd(proj, s["prep"], s["s5_r"], s["s5_i"], dy_s5, base128, f"s5_bwd{l}",
                                   sends=[g_out.reshape((N_DEV, -1, D)), _cols_scatter(g_branch)])
        recv["w_out"][l], recv["w_branch"][l] = got
        draw = s["prep_vjp"](dprep)
        for k, val in zip(("s5_lambda_re", "s5_lambda_im", "s5_log_dt", "s5_b_re", "s5_b_im", "s5_c_re", "s5_c_im", "s5_d"), draw):
            small_g[k][l] = val
        nb = L // ATT_BLOCK
        cq_r, ck_r, sq_r, sk_r = s["rot"]
        dsq, dsk, dsv, dsinks = _attn_bwd(sq_r, sk_r, proj, s["sinks_b"], (d_swa,), q_col=0, k_col=0,
                                          v_col=base128 + P_SV // 128, hkv=SWA_KV_HEADS, nbc=nb, max_dist=SWA_WINDOW - 1,
                                          name=f"swa_bwd{l}")
        small_g["swa_sinks"][l] = dsinks.reshape(SWA_HEADS, HEAD_DIM).sum(axis=1)
        dos, dls = _dilmix_bwd(s["dil_o"], s["dil_l"], d_dil, f"dilmix_bwd{l}")
        cv = (base128 + P_CV // 128) // 4
        dcq = dck = dcv = None
        for i, (window, dil) in enumerate(DIL_CONFIGS):
            key, send = ffn_sends[i]
            if dil == 1:
                g3, got = _attn_bwd(cq_r, ck_r, proj, None, (dos[i], dls[i]), q_col=0, k_col=0, v_col=cv, hkv=8,
                                    nbc=nb, max_dist=window // dil, name=f"dil{dil}_bwd{l}", sends=[send])
            else:
                qs_, ks_, vs_ = s["dil_in"][i]
                g3, got = _attn_bwd(qs_, ks_, vs_, None, (_to_strided(dos[i], dil), _to_strided(dls[i], dil)),
                                    q_col=0, k_col=0, v_col=0, hkv=8, nbc=nb // dil, max_dist=window // dil,
                                    name=f"dil{dil}_bwd{l}", sends=[send])
                g3 = [_from_strided(t, dil) for t in g3]
            recv[key][l] = got[0]
            g3 = [t.astype(f32) for t in g3]
            dcq, dck, dcv = (g3[0], g3[1], g3[2]) if dcq is None else (dcq + g3[0], dck + g3[1], dcv + g3[2])
        dcq, dck, dsq, dsk = _rope([(dcq, 0, 512), (dck, 0, 512), (dsq, 0, 512), (dsk, 0, 128)], cos, sin,
                                   transpose=True, out_dtype=bf16, name=f"rope_bwd{l}")
        dproj = jnp.concatenate([dgates.transpose(1, 0, 2).reshape(L, 4 * D), dgq, dgk, dgv, dgr, ds5u,
                                 dcq, dck, dcv.astype(bf16), dsq, dsk, dsv, dglr], axis=1)
        g_in = _matmul(s["h1"], dproj, mode="tn", out_dtype=bf16, name=f"proj_in_dw{l}")
        in_sends = [_win_to_shards(g_in, D),_cols_scatter(da2[:GLA_LOWRANK]), dglu_w.reshape((N_DEV, -1, 512))]
        if l > 0:
            dh1 = _matmul(dproj, win_p[l], mode="nt", name=f"proj_in_dx{l}")
            pending = in_sends
        else:
            dh1, got = _matmul(dproj, win_p[l], mode="nt", sends=in_sends, name=f"proj_in_dx{l}")
            for k, r in zip(in_group, got):
                recv[k][l] = r
        dcur, dcur_b, dg1 = _rms_bwd(s["x"], norm1_g[l][None], dh1, dx2, f"rms1_bwd{l}")
        small_g["norm1_g"][l] = dg1[0]
    grad_x = dcur.reshape(x.shape)

    out = {}
    for k in SHARDED:
        shp = W[k].shape
        as3 = lambda t: t.reshape((shp[0], -1, shp[-1]))
        slots = [r.reshape((N_DEV, -1, shp[-1])) for r in recv[k]]
        res = _adamw(as3(W[k]), as3(Mo[k]), as3(Vo[k]), slots, f"adamw_{k}")
        out[k] = [t.reshape(shp) for t in res]

    small_list = [jnp.stack(small_g[k]) if k != "final_norm_g" else dgf[0] for k in SMALL]
    small_list = [t.reshape(W[k].shape) for t, k in zip(small_list, SMALL)]
    packed_parts = _all_gather([_pack(small_list)], "gather_small_grads")[0]
    res = _adamw_packed(_pack([W[k] for k in SMALL]), _pack([Mo[k] for k in SMALL]), _pack([Vo[k] for k in SMALL]),
                        packed_parts, "adamw_small")
    unpacked = [_unpack(t, [W[k] for k in SMALL]) for t in res]
    for i, k in enumerate(SMALL):
        out[k] = [unpacked[j][i] for j in range(4)]

    return (loss, grad_x, *[out[k][0] for k in WEIGHTS], *[out[k][1] for k in WEIGHTS],
            *[out[k][2] for k in WEIGHTS], *[out[k][3] for k in WEIGHTS])
```

```python
import functools
import math

import jax
import jax.numpy as jnp
from jax import lax
from jax.experimental import pallas as pl
from jax.experimental.pallas import tpu as pltpu

f32 = jnp.float32
bf16 = jnp.bfloat16
HI = lax.Precision.HIGHEST

N_DEV = 8
AXES = ("x", "y", "c")
NORM_EPS = 1e-6
ROPE_THETA = 500000.0
HEAD_DIM = 64
ROPE_DIM = 16
ATT_BLOCK = 128
BRANCH_WIDTH = 512
GLA_HEADS, GLA_DK, GLA_DV, GLA_LOWRANK, GLA_TAU, GLA_CHUNK, GLA_SUB = 4, 64, 128, 16, 16.0, 64, 16
S5_GROUPS, S5_GROUP, S5_STATE = 32, 16, 64
S5_CHUNK = 128
S5_LANE_BLOCKS = 4
DIL_CONFIGS = ((128, 1), (512, 4), (2048, 16))
SWA_HEADS, SWA_KV_HEADS, SWA_WINDOW = 8, 2, 128
ADAM_LR, ADAM_B1, ADAM_B2, ADAM_EPS, ADAM_WD, ADAM_STEP = 0.001, 0.9, 0.999, 1e-08, 0.01, 10
O_GLR, O_S5U, O_GATES = 1536, 1552, 4368
MIX_COLS = 4480
P_GQ, P_GK, P_GV, P_GR, P_S5U, P_CQ, P_CK, P_CV, P_SQ, P_SK, P_SV, P_GLR = (
    0, 256, 512, 1024, 1536, 2048, 2560, 3072, 3584, 4096, 4224, 4352)
VMEM_LIMIT = 56 * 1024 * 1024


def _tile(n, cap, q=128):
    if n <= cap:
        return n
    t = (cap // q) * q
    while t >= q:
        if n % t == 0:
            return t
        t -= q
    return n


def _params(sem=None):
    return pltpu.CompilerParams(dimension_semantics=sem, vmem_limit_bytes=VMEM_LIMIT)


@functools.partial(jax.custom_vjp, nondiff_argnums=(1,))
def _sroll(x, d):
    return pltpu.roll(x, d, 0)


def _sroll_fwd(x, d):
    return pltpu.roll(x, d, 0), None


def _sroll_bwd(d, _, g):
    n = g.shape[0]
    return (pltpu.roll(g, (n - d) % n, 0),)


_sroll.defvjp(_sroll_fwd, _sroll_bwd)


def _mesh_pos():
    return lax.axis_index("x"), lax.axis_index("y"), lax.axis_index("c")


class _Gather:
    def __init__(self, ins, outs, send_sems, recv_sems, local_sems):
        self.ins, self.outs = ins, outs
        self.send_sems, self.recv_sems, self.local_sems = send_sems, recv_sems, local_sems
        x, y, c = _mesh_pos()
        self.x, self.y, self.c = x, y, c
        self.me, self.sibling = (x, y, c), (x, y, 1 - c)
        self.chips = [(1 - x, y), (x, 1 - y), (1 - x, 1 - y)]

    def copy(self, a, k, block, to, src=None):
        slot = self.outs[a].at[4 * block[0] + 2 * block[1] + block[2]]
        return pltpu.make_async_remote_copy(
            src_ref=slot if src is None else src, dst_ref=slot,
            send_sem=self.send_sems.at[a, k], recv_sem=self.recv_sems.at[a, k],
            device_id=to, device_id_type=pl.DeviceIdType.MESH)

    def mine(self, a):
        return pltpu.make_async_copy(self.ins[a], self.outs[a].at[4 * self.x + 2 * self.y + self.c], self.local_sems.at[a])

    def first(self, a):
        return [self.copy(a, 0, self.me, self.sibling, src=self.ins[a])] + [
            self.copy(a, 1 + j, self.me, (*chip, self.c), src=self.ins[a]) for j, chip in enumerate(self.chips)]

    def start(self):
        for a in range(len(self.ins)):
            self.mine(a).start()
            for cp in self.first(a):
                cp.start()

    def finish(self):
        c = self.c
        for a in range(len(self.ins)):
            passed = [self.copy(a, 4 + j, (*chip, c), self.sibling) for j, chip in enumerate(self.chips)]
            for j, chip in enumerate(self.chips):
                self.copy(a, 1 + j, (*chip, c), self.me).wait_recv()
                passed[j].start()
            self.copy(a, 0, self.sibling, self.me).wait_recv()
            for j, chip in enumerate(self.chips):
                self.copy(a, 4 + j, (*chip, 1 - c), self.me).wait_recv()
            for cp in self.first(a) + passed:
                cp.wait_send()
            self.mine(a).wait()


def _all_gather(shards, name):
    n = len(shards)
    any_spec = pl.BlockSpec(memory_space=pl.ANY)

    def body(*refs):
        g = _Gather(refs[:n], refs[n:2 * n], *refs[2 * n:])
        g.start()
        g.finish()

    outs = pl.pallas_call(
        body, name=name,
        out_shape=[jax.ShapeDtypeStruct((N_DEV,) + s.shape, s.dtype) for s in shards],
        in_specs=[any_spec] * n, out_specs=[any_spec] * n,
        scratch_shapes=[pltpu.SemaphoreType.DMA((n, 7)), pltpu.SemaphoreType.DMA((n, 7)),
                        pltpu.SemaphoreType.DMA((n,))],
    )(*shards)
    return list(outs)


def _a2a_copies(ins, outs, send_sems, recv_sems, local_sems):
    x, y, c = _mesh_pos()
    me = 4 * x + 2 * y + c
    copies = []
    for a in range(len(ins)):
        copies.append(pltpu.make_async_copy(ins[a].at[me], outs[a].at[me], local_sems.at[a]))
        for k in range(1, N_DEV):
            px = 1 - x if k & 4 else x
            py = 1 - y if k & 2 else y
            pc = 1 - c if k & 1 else c
            copies.append(pltpu.make_async_remote_copy(
                src_ref=ins[a].at[4 * px + 2 * py + pc], dst_ref=outs[a].at[me],
                send_sem=send_sems.at[a, k - 1], recv_sem=recv_sems.at[a, k - 1],
                device_id=(px, py, pc), device_id_type=pl.DeviceIdType.MESH))
    return copies


def _host_call(body, sends, args, *, name, grid, out_shape, in_specs, out_specs, scratch_shapes, compiler_params,
               gather=False):
    single = not isinstance(out_shape, (list, tuple))
    out_shape = [out_shape] if single else list(out_shape)
    out_specs = [out_specs] if single else list(out_specs)
    sends = list(sends or ())
    n, n_in, n_out, n_scr = len(sends), len(args), len(out_shape), len(scratch_shapes)
    if n == 0:
        outs = pl.pallas_call(body, name=name, grid=grid, out_shape=out_shape, in_specs=in_specs, out_specs=out_specs,
                              scratch_shapes=list(scratch_shapes), compiler_params=compiler_params)(*args)
        return (outs[0] if single else list(outs)), []
    any_spec = pl.BlockSpec(memory_space=pl.ANY)

    def hosted(*refs):
        ins, s_in = refs[:n_in], refs[n_in:n_in + n]
        pos = n_in + n
        outs, s_out = refs[pos:pos + n_out], refs[pos + n_out:pos + n_out + n]
        pos += n_out + n
        scr, sems = refs[pos:pos + n_scr], refs[pos + n_scr:]
        ids = [pl.program_id(i) for i in range(len(grid))]
        first = functools.reduce(lambda p, q: p & q, [i == 0 for i in ids])
        last = functools.reduce(lambda p, q: p & q, [i == g - 1 for i, g in zip(ids, grid)])

        @pl.when(first)
        def _():
            if gather:
                _Gather(s_in, s_out, *sems).start()
            else:
                for cp in _a2a_copies(s_in, s_out, *sems):
                    cp.start()

        body(*ins, *outs, *scr)

        @pl.when(last)
        def _():
            if gather:
                _Gather(s_in, s_out, *sems).finish()
            else:
                for cp in _a2a_copies(s_in, s_out, *sems):
                    cp.wait()

    lead = (N_DEV,) if gather else ()
    outs = pl.pallas_call(
        hosted, name=name, grid=grid,
        out_shape=out_shape + [jax.ShapeDtypeStruct(lead + s.shape, s.dtype) for s in sends],
        in_specs=list(in_specs) + [any_spec] * n, out_specs=out_specs + [any_spec] * n,
        scratch_shapes=list(scratch_shapes) + [pltpu.SemaphoreType.DMA((n, 7)), pltpu.SemaphoreType.DMA((n, 7)),
                                               pltpu.SemaphoreType.DMA((n,))],
        compiler_params=compiler_params,
    )(*args, *sends)
    main = list(outs[:n_out])
    return (main[0] if single else main), list(outs[n_out:])


def _matmul(a, b, *, mode="nn", out_dtype=f32, res=None, extras=(), epilogue=None, sends=None, gather=False, name):
    if mode == "tn":
        K, M = a.shape
    else:
        M, K = a.shape
    N = b.shape[0] if mode == "nt" else b.shape[1]
    k_cap = 2048 if (a.dtype == bf16 and b.dtype == bf16) else 1024
    tm, tn, tk = _tile(M, 1024), _tile(N, 1152), _tile(K, k_cap)
    nk = K // tk
    dims = {"nn": (((1,), (0,)), ((), ())), "nt": (((1,), (1,)), ((), ())), "tn": (((0,), (0,)), ((), ()))}[mode]
    out_dtypes = list(out_dtype) if epilogue is not None else [out_dtype]
    n_side = (1 if res is not None else 0) + len(extras)

    def body(*refs):
        a_ref, b_ref = refs[:2]
        r_ref = refs[2] if res is not None else None
        x_refs = refs[2 + n_side - len(extras):2 + n_side]
        o_refs = refs[2 + n_side:2 + n_side + len(out_dtypes)]
        acc = refs[-1] if nk > 1 else None
        k = pl.program_id(2)
        part = lax.dot_general(a_ref[...].astype(bf16), b_ref[...].astype(bf16), dims, preferred_element_type=f32)

        def finish(r):
            if res is not None:
                r = r + r_ref[...]
            outs = epilogue(r, *[x[...] for x in x_refs]) if epilogue is not None else (r,)
            for o_ref, val in zip(o_refs, outs):
                o_ref[...] = val.astype(o_ref.dtype)

        if nk == 1:
            finish(part)
            return

        @pl.when(k == 0)
        def _():
            acc[...] = part

        @pl.when((k > 0) & (k < nk - 1))
        def _():
            acc[...] += part

        @pl.when(k == nk - 1)
        def _():
            finish(acc[...] + part)

    a_spec = pl.BlockSpec((tk, tm), lambda i, j, k: (k, i)) if mode == "tn" else pl.BlockSpec((tm, tk), lambda i, j, k: (i, k))
    b_spec = pl.BlockSpec((tn, tk), lambda i, j, k: (j, k)) if mode == "nt" else pl.BlockSpec((tk, tn), lambda i, j, k: (k, j))
    o_spec = pl.BlockSpec((tm, tn), lambda i, j, k: (i, j))
    in_specs = [a_spec, b_spec] + [o_spec] * n_side
    args = (a, b) + ((res,) if res is not None else ()) + tuple(extras)
    out, recvs = _host_call(
        body, sends, args, name=name, grid=(M // tm, N // tn, nk),
        out_shape=[jax.ShapeDtypeStruct((M, N), dt) for dt in out_dtypes],
        in_specs=in_specs, out_specs=[o_spec] * len(out_dtypes),
        scratch_shapes=[pltpu.VMEM((tm, tn), f32)] if nk > 1 else [],
        compiler_params=_params(("parallel", "parallel", "arbitrary")), gather=gather)
    if epilogue is None:
        out = out[0]
    return out if sends is None else (out, recvs)


def _rms(x, g):
    return x * lax.rsqrt(jnp.mean(x * x, axis=-1, keepdims=True) + NORM_EPS) * g


def _rms_fwd(x, g, name):
    L, D = x.shape
    tm = _tile(L, 256, 8)

    def body(x_ref, g_ref, o_ref):
        o_ref[...] = _rms(x_ref[...], g_ref[...]).astype(bf16)

    return pl.pallas_call(
        body, name=name, grid=(L // tm,), out_shape=jax.ShapeDtypeStruct((L, D), bf16),
        in_specs=[pl.BlockSpec((tm, D), lambda i: (i, 0)), pl.BlockSpec((1, D), lambda i: (0, 0))],
        out_specs=pl.BlockSpec((tm, D), lambda i: (i, 0)),
        compiler_params=_params(("parallel",)),
    )(x, g)


def _rms_bwd(x, g, dh, dres, name):
    L, D = x.shape
    tm = _tile(L, 256, 8)

    def body(x_ref, g_ref, dh_ref, dres_ref, dx_ref, dxb_ref, dg_ref):
        _, vjp = jax.vjp(_rms, x_ref[...], g_ref[...])
        dx, dg = vjp(dh_ref[...])
        dx = dres_ref[...] + dx
        dx_ref[...] = dx
        dxb_ref[...] = dx.astype(bf16)

        @pl.when(pl.program_id(0) == 0)
        def _():
            dg_ref[...] = jnp.zeros_like(dg_ref)

        dg_ref[...] += dg

    row = pl.BlockSpec((tm, D), lambda i: (i, 0))
    vec = pl.BlockSpec((1, D), lambda i: (0, 0))
    return pl.pallas_call(
        body, name=name, grid=(L // tm,),
        out_shape=[jax.ShapeDtypeStruct((L, D), f32), jax.ShapeDtypeStruct((L, D), bf16), jax.ShapeDtypeStruct((1, D), f32)],
        in_specs=[row, vec, row, row], out_specs=[row, row, vec],
        compiler_params=_params(("arbitrary",)),
    )(x, g, dh, dres)


def _final_loss(x, g, target, name):
    L, D = x.shape
    tm = _tile(L, 256, 8)

    def body(x_ref, g_ref, t_ref, loss_ref, dx_ref, dxb_ref, dg_ref):
        tgt = t_ref[...]

        def f(xv, gv):
            err = _rms(xv, gv) - tgt
            return 0.5 * jnp.sum(jnp.mean(err * err, axis=-1, keepdims=True), axis=0, keepdims=True)

        val, vjp = jax.vjp(f, x_ref[...], g_ref[...])
        dx, dg = vjp(jnp.ones((1, 1), f32))
        dx_ref[...] = dx
        dxb_ref[...] = dx.astype(bf16)

        @pl.when(pl.program_id(0) == 0)
        def _():
            dg_ref[...] = jnp.zeros_like(dg_ref)
            loss_ref[...] = jnp.zeros_like(loss_ref)

        dg_ref[...] += dg
        loss_ref[...] += jnp.broadcast_to(val, loss_ref.shape)

    row = pl.BlockSpec((tm, D), lambda i: (i, 0))
    vec = pl.BlockSpec((1, D), lambda i: (0, 0))
    acc = pl.BlockSpec((8, 128), lambda i: (0, 0))
    loss, dx, dxb, dg = pl.pallas_call(
        body, name=name, grid=(L // tm,),
        out_shape=[jax.ShapeDtypeStruct((8, 128), f32), jax.ShapeDtypeStruct((L, D), f32), jax.ShapeDtypeStruct((L, D), bf16),
                   jax.ShapeDtypeStruct((1, D), f32)],
        in_specs=[row, vec, row], out_specs=[acc, row, row, vec],
        compiler_params=_params(("arbitrary",)),
    )(x, g, target)
    return loss[0, 0], dx, dxb, dg


def _swiglu_f(a, b):
    return jax.nn.silu(a) * b


def _swiglu_epilogue(up, gate):
    return up, _swiglu_f(gate.astype(f32), up)


def _swiglu_grad_epilogue(dact, gate, up):
    _, vjp = jax.vjp(_swiglu_f, gate.astype(f32), up.astype(f32))
    return vjp(dact)


def _merge_f(g0, g1, g2, g3, y0, y1, y2, y3):
    s = jax.nn.sigmoid
    return s(g0) * y0 + s(g1) * y1 + s(g2) * y2 + s(g3) * y3


def _merge_fwd(proj, ys, D, name):
    L = proj.shape[0]
    tm, tn = _tile(L, 512, 8), _tile(D, 512)
    nj = D // tn

    def body(g0, g1, g2, g3, y0, y1, y2, y3, o_ref):
        o_ref[...] = _merge_f(g0[...], g1[...], g2[...], g3[...], *[y[...].astype(f32) for y in (y0, y1, y2, y3)]).astype(bf16)

    gspecs = [pl.BlockSpec((tm, tn), functools.partial(lambda i, j, m: (i, m * nj + j), m=m)) for m in range(4)]
    blk = pl.BlockSpec((tm, tn), lambda i, j: (i, j))
    return pl.pallas_call(
        body, name=name, grid=(L // tm, nj), out_shape=jax.ShapeDtypeStruct((L, D), bf16),
        in_specs=gspecs + [blk] * 4, out_specs=blk, compiler_params=_params(("parallel", "parallel")),
    )(proj, proj, proj, proj, *ys)


def _merge_bwd(proj, ys, dmixed, D, name):
    L = proj.shape[0]
    tm, tn = _tile(L, 512, 8), _tile(D, 512)
    nj = D // tn

    def body(g0, g1, g2, g3, y0, y1, y2, y3, d_ref, dy0, dy1, dy2, dy3, dg_ref):
        _, vjp = jax.vjp(_merge_f, g0[...], g1[...], g2[...], g3[...], *[y[...].astype(f32) for y in (y0, y1, y2, y3)])
        grads = vjp(d_ref[...])
        for m, r in enumerate((dy0, dy1, dy2, dy3)):
            r[...] = grads[4 + m].astype(bf16)
        for m in range(4):
            dg_ref[m] = grads[m].astype(bf16)

    gspecs = [pl.BlockSpec((tm, tn), functools.partial(lambda i, j, m: (i, m * nj + j), m=m)) for m in range(4)]
    blk = pl.BlockSpec((tm, tn), lambda i, j: (i, j))
    dgspec = pl.BlockSpec((4, tm, tn), lambda i, j: (0, i, j))
    outs = pl.pallas_call(
        body, name=name, grid=(L // tm, nj),
        out_shape=[jax.ShapeDtypeStruct((L, D), bf16)] * 4 + [jax.ShapeDtypeStruct((4, L, D), bf16)],
        in_specs=gspecs + [blk] * 5, out_specs=[blk] * 4 + [dgspec],
        compiler_params=_params(("parallel", "parallel")),
    )(proj, proj, proj, proj, *ys, dmixed)
    return outs[:4], outs[4]


def _gla_head(q, k, v, r, glr, st, a2, ab, ng):
    C, T = GLA_CHUNK, GLA_SUB
    row = lax.broadcasted_iota(jnp.int32, (C, C), 0)
    col = lax.broadcasted_iota(jnp.int32, (C, C), 1)
    tri = (col <= row).astype(f32)
    sel = (col == (row // T) * T).astype(f32)
    z = jnp.dot(glr, a2, preferred_element_type=f32) + ab
    g = jax.nn.log_sigmoid(z) / GLA_TAU
    cum = jnp.dot(tri, g, precision=HI, preferred_element_type=f32)
    excl = cum - g
    ref = jnp.dot(sel, excl, precision=HI, preferred_element_type=f32)
    qs = q * (GLA_DK ** -0.5)
    q_ref = qs * jnp.exp(cum - ref)
    rowk = lax.broadcasted_iota(jnp.int32, (C, GLA_DK), 0)
    a = jnp.zeros((C, C), f32)
    for s in range(1, C // T):
        ref_s = jnp.sum(jnp.where(rowk == s * T, excl, 0.0), axis=0, keepdims=True)
        k_ref = k * jnp.exp(jnp.where(rowk < s * T, ref_s - cum, -jnp.inf))
        a_s = lax.dot_general(q_ref, k_ref, (((1,), (1,)), ((), ())), preferred_element_type=f32)
        a = a + jnp.where(row // T == s, a_s, 0.0)
    o = jnp.dot(a, v, preferred_element_type=f32)
    sub = rowk % T
    for d in range(T):
        ks = _sroll(k, d) if d else k
        cs = _sroll(cum, d) if d else cum
        vs = _sroll(v, d) if d else v
        w = jnp.sum(qs * ks * jnp.exp(jnp.where(sub >= d, cum - cs, -jnp.inf)), axis=-1, keepdims=True)
        o = o + w * vs
    o = o + lax.dot_general(qs * jnp.exp(cum), st, (((1,), (1,)), ((), ())), preferred_element_type=f32)
    last = jnp.sum(jnp.where(rowk == C - 1, cum, 0.0), axis=0, keepdims=True)
    st_new = st * jnp.exp(last) + lax.dot_general(v, k * jnp.exp(last - cum), (((0,), (0,)), ((), ())),
                                                  preferred_element_type=f32)
    out = _rms(o, ng) * jax.nn.silu(r)
    return out, st_new


def _gla_specs(L, base128, rev):
    n = L // GLA_CHUNK
    ch = (lambda i: n - 1 - i) if rev else (lambda i: i)
    b = base128
    return n, ch, [
        pl.BlockSpec((GLA_CHUNK, 256), lambda i: (ch(i), (b + P_GQ // 128) // 2)),
        pl.BlockSpec((GLA_CHUNK, 256), lambda i: (ch(i), (b + P_GK // 128) // 2)),
        pl.BlockSpec((GLA_CHUNK, 512), lambda i: (ch(i), (b + P_GV // 128) // 4)),
        pl.BlockSpec((GLA_CHUNK, 512), lambda i: (ch(i), (b + P_GR // 128) // 4)),
        pl.BlockSpec((GLA_CHUNK, 128), lambda i: (ch(i), b + P_GLR // 128)),
    ]


def _gla_fwd(proj, a2p, ab, ng, base128, name, shards=None):
    L = proj.shape[0]
    n, _, pspecs = _gla_specs(L, base128, False)
    H, DK, DV = GLA_HEADS, GLA_DK, GLA_DV

    def body(q_ref, k_ref, v_ref, r_ref, l_ref, a2_ref, ab_ref, ng_ref, o_ref, st_ref, state):
        @pl.when(pl.program_id(0) == 0)
        def _():
            state[...] = jnp.zeros_like(state)

        st_ref[0] = state[...]
        glr = l_ref[...]
        for h in range(H):
            kk, vv = slice(h * DK, (h + 1) * DK), slice(h * DV, (h + 1) * DV)
            out, st_new = _gla_head(q_ref[:, kk], k_ref[:, kk], v_ref[:, vv], r_ref[:, vv], glr, state[h],
                                    a2_ref[:, kk], ab_ref[:, kk], ng_ref[:, vv])
            o_ref[:, vv] = out.astype(bf16)
            state[h] = st_new

    full = lambda shape: pl.BlockSpec(shape, lambda i: (0,) * len(shape))
    return _host_call(
        body, shards, (proj, proj, proj, proj, proj, a2p, ab, ng), name=name, grid=(n,),
        out_shape=[jax.ShapeDtypeStruct((L, H * DV), bf16), jax.ShapeDtypeStruct((n, H, DV, DK), f32)],
        in_specs=pspecs + [full((128, 256)), full((1, 256)), full((1, 512))],
        out_specs=[pl.BlockSpec((GLA_CHUNK, 512), lambda i: (i, 0)), pl.BlockSpec((1, H, DV, DK), lambda i: (i, 0, 0, 0))],
        scratch_shapes=[pltpu.VMEM((H, DV, DK), f32)],
        compiler_params=_params(("arbitrary",)), gather=True)


def _gla_bwd(proj, a2p, ab, ng, states, dout, base128, name, sends=None):
    L = proj.shape[0]
    n, ch, pspecs = _gla_specs(L, base128, True)
    H, DK, DV = GLA_HEADS, GLA_DK, GLA_DV

    def body(q_ref, k_ref, v_ref, r_ref, l_ref, a2_ref, ab_ref, ng_ref, st_ref, do_ref,
             dq_ref, dk_ref, dv_ref, dr_ref, dl_ref, da2_ref, dab_ref, dng_ref, dstate):
        @pl.when(pl.program_id(0) == 0)
        def _():
            dstate[...] = jnp.zeros_like(dstate)
            da2_ref[...] = jnp.zeros_like(da2_ref)
            dab_ref[...] = jnp.zeros_like(dab_ref)
            dng_ref[...] = jnp.zeros_like(dng_ref)

        glr = l_ref[...]
        dglr = jnp.zeros(glr.shape, f32)
        for h in range(H):
            kk, vv = slice(h * DK, (h + 1) * DK), slice(h * DV, (h + 1) * DV)
            _, vjp = jax.vjp(_gla_head, q_ref[:, kk], k_ref[:, kk], v_ref[:, vv], r_ref[:, vv], glr, st_ref[0, h],
                             a2_ref[:, kk], ab_ref[:, kk], ng_ref[:, vv])
            dq, dk, dv, dr, dl, dst, da2, dab, dng = vjp((do_ref[:, vv].astype(f32), dstate[h]))
            dq_ref[:, kk] = dq.astype(bf16)
            dk_ref[:, kk] = dk.astype(bf16)
            dv_ref[:, vv] = dv.astype(bf16)
            dr_ref[:, vv] = dr.astype(bf16)
            dglr = dglr + dl
            dstate[h] = dst
            da2_ref[:, kk] += da2
            dab_ref[:, kk] += dab
            dng_ref[:, vv] += dng
        dl_ref[...] = dglr.astype(bf16)

    full = lambda shape: pl.BlockSpec(shape, lambda i: (0,) * len(shape))
    rowspec = lambda w: pl.BlockSpec((GLA_CHUNK, w), lambda i: (ch(i), 0))
    return _host_call(
        body, sends, (proj, proj, proj, proj, proj, a2p, ab, ng, states, dout), name=name, grid=(n,),
        out_shape=[jax.ShapeDtypeStruct((L, 256), bf16), jax.ShapeDtypeStruct((L, 256), bf16),
                   jax.ShapeDtypeStruct((L, 512), bf16), jax.ShapeDtypeStruct((L, 512), bf16),
                   jax.ShapeDtypeStruct((L, 128), bf16), jax.ShapeDtypeStruct((128, 256), f32),
                   jax.ShapeDtypeStruct((1, 256), f32), jax.ShapeDtypeStruct((1, 512), f32)],
        in_specs=pspecs + [full((128, 256)), full((1, 256)), full((1, 512)),
                           pl.BlockSpec((1, H, DV, DK), lambda i: (ch(i), 0, 0, 0)), rowspec(512)],
        out_specs=[rowspec(256), rowspec(256), rowspec(512), rowspec(512), rowspec(128),
                   full((128, 256)), full((1, 256)), full((1, 512))],
        scratch_shapes=[pltpu.VMEM((H, DV, DK), f32)],
        compiler_params=_params(("arbitrary",)))


def _s5_prep(lam_re, lam_im, log_dt, b_re, b_im, c_re, c_im, d):
    G, N, Cn = S5_GROUPS, S5_STATE, S5_GROUP
    J, GB = S5_LANE_BLOCKS, S5_GROUPS // S5_LANE_BLOCKS
    dt = jnp.exp(log_dt)[:, None]
    mag = jnp.exp(lam_re * dt)
    ab_re, ab_im = mag * jnp.cos(lam_im * dt), mag * jnp.sin(lam_im * dt)
    den = lam_re * lam_re + lam_im * lam_im
    z_re = ((ab_re - 1.0) * lam_re + ab_im * lam_im) / den
    z_im = (ab_im * lam_re - (ab_re - 1.0) * lam_im) / den
    bb_re = z_re[..., None] * b_re - z_im[..., None] * b_im
    bb_im = z_re[..., None] * b_im + z_im[..., None] * b_re
    eye = jnp.eye(GB, dtype=f32)

    def in_blocks(bb):
        return jnp.einsum("jgnc,gh->jgchn", bb.reshape(J, GB, N, Cn), eye).reshape(J, GB * Cn, GB * N)

    def out_blocks(cc):
        return jnp.einsum("jgcn,gh->jgnhc", cc.reshape(J, GB, Cn, N), eye).reshape(J, GB * N, GB * Cn)

    return (ab_re.reshape(1, G * N), ab_im.reshape(1, G * N), in_blocks(bb_re), in_blocks(bb_im),
            out_blocks(c_re), out_blocks(c_im), d.reshape(1, G * Cn))


def _s5_chunk(u, hin_r, hin_i, a_r, a_i, bb_r, bb_i, cc_r, cc_i, dvec):
    T = u.shape[0]
    hr = jnp.dot(u, bb_r, preferred_element_type=f32)
    hi = jnp.dot(u, bb_i, preferred_element_type=f32)
    row = lax.broadcasted_iota(jnp.int32, hr.shape, 0)
    hr = hr + jnp.where(row == 0, a_r * hin_r - a_i * hin_i, 0.0)
    hi = hi + jnp.where(row == 0, a_r * hin_i + a_i * hin_r, 0.0)
    pr, pi = a_r, a_i
    d = 1
    while d < T:
        sr = jnp.where(row >= d, _sroll(hr, d), 0.0)
        si = jnp.where(row >= d, _sroll(hi, d), 0.0)
        hr, hi = hr + pr * sr - pi * si, hi + pr * si + pi * sr
        pr, pi = pr * pr - pi * pi, 2.0 * pr * pi
        d *= 2
    y = (jnp.dot(hr, cc_r, preferred_element_type=f32)
         - jnp.dot(hi, cc_i, preferred_element_type=f32) + dvec * u)
    out_r = jnp.sum(jnp.where(row == T - 1, hr, 0.0), axis=0, keepdims=True)
    out_i = jnp.sum(jnp.where(row == T - 1, hi, 0.0), axis=0, keepdims=True)
    return y, out_r, out_i


def _s5_specs(L, base128, rev):
    T, J = S5_CHUNK, S5_LANE_BLOCKS
    n = L // T
    ch = (lambda c: n - 1 - c) if rev else (lambda c: c)
    ub = base128 + P_S5U // 128
    specs = [
        pl.BlockSpec((T, 128), lambda j, c: (ch(c), ub + j)),
        pl.BlockSpec((1, 512), lambda j, c: (0, j)), pl.BlockSpec((1, 512), lambda j, c: (0, j)),
        pl.BlockSpec((None, 128, 512), lambda j, c: (j, 0, 0)), pl.BlockSpec((None, 128, 512), lambda j, c: (j, 0, 0)),
        pl.BlockSpec((None, 512, 128), lambda j, c: (j, 0, 0)), pl.BlockSpec((None, 512, 128), lambda j, c: (j, 0, 0)),
        pl.BlockSpec((1, 128), lambda j, c: (0, j)),
    ]
    return n, ch, specs


def _s5_fwd(proj, prep, base128, name, shards=None):
    L = proj.shape[0]
    T, J = S5_CHUNK, S5_LANE_BLOCKS
    n, _, specs = _s5_specs(L, base128, False)

    def body(u_ref, ar, ai, bbr, bbi, ccr, cci, dv, y_ref, sr_ref, si_ref, carry):
        @pl.when(pl.program_id(1) == 0)
        def _():
            carry[...] = jnp.zeros_like(carry)

        hin_r, hin_i = carry[0:1, :], carry[1:2, :]
        sr_ref[0] = jnp.broadcast_to(hin_r, (8, 512))
        si_ref[0] = jnp.broadcast_to(hin_i, (8, 512))
        y, out_r, out_i = _s5_chunk(u_ref[...], hin_r, hin_i, ar[...], ai[...], bbr[...], bbi[...], ccr[...], cci[...], dv[...])
        y_ref[...] = y
        carry[0:1, :] = out_r
        carry[1:2, :] = out_i

    st = pl.BlockSpec((1, 8, 512), lambda j, c: (c, 0, j))
    return _host_call(
        body, shards, (proj, *prep), name=name, grid=(J, n),
        out_shape=[jax.ShapeDtypeStruct((L, 512), f32), jax.ShapeDtypeStruct((n, 8, 2048), f32), jax.ShapeDtypeStruct((n, 8, 2048), f32)],
        in_specs=specs, out_specs=[pl.BlockSpec((T, 128), lambda j, c: (c, j)), st, st],
        scratch_shapes=[pltpu.VMEM((8, 512), f32)],
        compiler_params=_params(("parallel", "arbitrary")), gather=True)


def _s5_bwd(proj, prep, st_r, st_i, dy, base128, name, sends=None):
    L = proj.shape[0]
    T, J = S5_CHUNK, S5_LANE_BLOCKS
    n, ch, specs = _s5_specs(L, base128, True)

    def body(u_ref, ar, ai, bbr, bbi, ccr, cci, dv, sr_ref, si_ref, dy_ref,
             du_ref, dar, dai, dbbr, dbbi, dccr, dcci, ddv, dcarry):
        @pl.when(pl.program_id(1) == 0)
        def _():
            dcarry[...] = jnp.zeros_like(dcarry)
            for r in (dar, dai, dbbr, dbbi, dccr, dcci, ddv):
                r[...] = jnp.zeros_like(r)

        _, vjp = jax.vjp(_s5_chunk, u_ref[...], sr_ref[0, 0:1, :], si_ref[0, 0:1, :], ar[...], ai[...],
                         bbr[...], bbi[...], ccr[...], cci[...], dv[...])
        g = vjp((dy_ref[...], dcarry[0:1, :], dcarry[1:2, :]))
        du_ref[...] = g[0].astype(bf16)
        dcarry[0:1, :] = g[1]
        dcarry[1:2, :] = g[2]
        for r, val in zip((dar, dai, dbbr, dbbi, dccr, dcci, ddv), g[3:]):
            r[...] += val

    st = pl.BlockSpec((1, 8, 512), lambda j, c: (ch(c), 0, j))
    outs, recvs = _host_call(
        body, sends, (proj, *prep, st_r, st_i, dy), name=name, grid=(J, n),
        out_shape=[jax.ShapeDtypeStruct((L, 512), bf16),
                   jax.ShapeDtypeStruct((1, 2048), f32), jax.ShapeDtypeStruct((1, 2048), f32),
                   jax.ShapeDtypeStruct((J, 128, 512), f32), jax.ShapeDtypeStruct((J, 128, 512), f32),
                   jax.ShapeDtypeStruct((J, 512, 128), f32), jax.ShapeDtypeStruct((J, 512, 128), f32),
                   jax.ShapeDtypeStruct((1, 512), f32)],
        in_specs=specs + [st, st, pl.BlockSpec((T, 128), lambda j, c: (ch(c), j))],
        out_specs=[pl.BlockSpec((T, 128), lambda j, c: (ch(c), j))] + specs[1:],
        scratch_shapes=[pltpu.VMEM((8, 512), f32)],
        compiler_params=_params(("parallel", "arbitrary")))
    return outs[0], tuple(outs[1:]), recvs


def _glu_f(y, w, b):
    z = jax.nn.gelu(y)
    return z * jax.nn.sigmoid(jnp.dot(z.astype(bf16), w.astype(bf16), preferred_element_type=f32) + b)


def _glu_fwd(y, w, b, name):
    L = y.shape[0]
    tm = _tile(L, 512, 8)

    def body(y_ref, w_ref, b_ref, o_ref):
        o_ref[...] = _glu_f(y_ref[...], w_ref[...], b_ref[...]).astype(bf16)

    row = pl.BlockSpec((tm, 512), lambda i: (i, 0))
    return pl.pallas_call(
        body, name=name, grid=(L // tm,), out_shape=jax.ShapeDtypeStruct((L, 512), bf16),
        in_specs=[row, pl.BlockSpec((512, 512), lambda i: (0, 0)), pl.BlockSpec((1, 512), lambda i: (0, 0))],
        out_specs=row, compiler_params=_params(("parallel",)),
    )(y, w, b)


def _glu_bwd(y, w, b, dout, name):
    L = y.shape[0]
    tm = _tile(L, 512, 8)

    def body(y_ref, w_ref, b_ref, do_ref, dy_ref, dw_ref, db_ref):
        @pl.when(pl.program_id(0) == 0)
        def _():
            dw_ref[...] = jnp.zeros_like(dw_ref)
            db_ref[...] = jnp.zeros_like(db_ref)

        _, vjp = jax.vjp(_glu_f, y_ref[...], w_ref[...], b_ref[...])
        dy, dw, db = vjp(do_ref[...])
        dy_ref[...] = dy
        dw_ref[...] += dw
        db_ref[...] += db

    row = pl.BlockSpec((tm, 512), lambda i: (i, 0))
    wspec, bspec = pl.BlockSpec((512, 512), lambda i: (0, 0)), pl.BlockSpec((1, 512), lambda i: (0, 0))
    return pl.pallas_call(
        body, name=name, grid=(L // tm,),
        out_shape=[jax.ShapeDtypeStruct((L, 512), f32), jax.ShapeDtypeStruct((512, 512), f32), jax.ShapeDtypeStruct((1, 512), f32)],
        in_specs=[row, wspec, bspec, row], out_specs=[row, wspec, bspec],
        compiler_params=_params(("arbitrary",)),
    )(y, w, b, dout)


def _rope_tables(positions):
    half = ROPE_DIM // 2
    inv_freq = ROPE_THETA ** (-jnp.arange(half, dtype=f32) / half)
    ang = positions.astype(f32)[:, None] * inv_freq
    L = positions.shape[0]
    cos = jnp.concatenate([jnp.cos(ang), jnp.cos(ang), jnp.ones((L, HEAD_DIM - ROPE_DIM), f32)], axis=1)
    sin = jnp.concatenate([jnp.sin(ang), jnp.sin(ang), jnp.zeros((L, HEAD_DIM - ROPE_DIM), f32)], axis=1)
    return jnp.tile(cos, (1, 8)), jnp.tile(sin, (1, 8))


def _rope_matrix(w):
    half = ROPE_DIM // 2
    r = lax.broadcasted_iota(jnp.int32, (w, w), 0)
    c = lax.broadcasted_iota(jnp.int32, (w, w), 1)
    same = (r // HEAD_DIM) == (c // HEAD_DIM)
    rr, cc = r % HEAD_DIM, c % HEAD_DIM
    return (jnp.where(same & (cc < half) & (rr == cc + half), -1.0, 0.0)
            + jnp.where(same & (cc >= half) & (cc < ROPE_DIM) & (rr == cc - half), 1.0, 0.0))


def _rope(items, cos, sin, *, transpose, out_dtype, name):
    L = cos.shape[0]
    tm = _tile(L, 512, 8)
    n = len(items)

    def body(*refs):
        xs, c_ref, s_ref, outs = refs[:n], refs[n], refs[n + 1], refs[n + 2:]
        for x_ref, o_ref in zip(xs, outs):
            w = x_ref.shape[1]
            x = x_ref[...].astype(f32)
            c, s = c_ref[:, :w], s_ref[:, :w]
            rot = _rope_matrix(w)
            if transpose:
                y = x * c + lax.dot_general(x * s, rot, (((1,), (1,)), ((), ())), preferred_element_type=f32)
            else:
                y = x * c + jnp.dot(x, rot, preferred_element_type=f32) * s
            o_ref[...] = y.astype(o_ref.dtype)

    in_specs = [pl.BlockSpec((tm, w), functools.partial(lambda i, col: (i, col), col=col)) for _, col, w in items]
    tab = pl.BlockSpec((tm, 512), lambda i: (i, 0))
    outs = pl.pallas_call(
        body, name=name, grid=(L // tm,),
        out_shape=[jax.ShapeDtypeStruct((L, w), out_dtype) for _, _, w in items],
        in_specs=in_specs + [tab, tab], out_specs=[pl.BlockSpec((tm, w), lambda i: (i, 0)) for _, _, w in items],
        compiler_params=_params(("parallel",)),
    )(*[a for a, _, _ in items], cos, sin)
    return list(outs)


def _attn_head(q, kp, kc, vp, vc, sink, *, lim, max_dist):
    T = ATT_BLOCK
    k2 = jnp.concatenate([kp, kc], axis=0)
    v2 = jnp.concatenate([vp, vc], axis=0)
    s = lax.dot_general(q, k2, (((1,), (1,)), ((), ())), preferred_element_type=f32) * (HEAD_DIM ** -0.5)
    t = lax.broadcasted_iota(jnp.int32, (T, 2 * T), 0)
    j = lax.broadcasted_iota(jnp.int32, (T, 2 * T), 1)
    dist = T + t - j
    valid = (dist >= 0) & (dist <= max_dist) & (j >= lim)
    s = jnp.where(valid, s, -jnp.inf)
    m = lax.stop_gradient(jnp.max(s, axis=-1, keepdims=True))
    p = jnp.exp(s - m)
    den = jnp.sum(p, axis=-1, keepdims=True)
    o = jnp.dot(p, v2, preferred_element_type=f32) / den
    lse = jnp.broadcast_to(m + jnp.log(den), (T, HEAD_DIM))
    if sink is None:
        return o, lse
    return o * jax.nn.sigmoid(lse - sink)


def _attn_specs(L, q_col, k_col, v_col, wk, rev):
    T = ATT_BLOCK
    n = L // T
    blk = (lambda i: n - 1 - i) if rev else (lambda i: i)
    prev = lambda i: jnp.maximum(blk(i) - 1, 0)
    specs = [
        pl.BlockSpec((T, 512), lambda i: (blk(i), q_col)),
        pl.BlockSpec((T, wk), lambda i: (prev(i), k_col)), pl.BlockSpec((T, wk), lambda i: (blk(i), k_col)),
        pl.BlockSpec((T, wk), lambda i: (prev(i), v_col)), pl.BlockSpec((T, wk), lambda i: (blk(i), v_col)),
    ]
    return n, blk, specs


def _attn_fwd(qa, ka, va, sinks, *, q_col, k_col, v_col, hkv, nbc, max_dist, name, shards=None):
    L = qa.shape[0]
    T, HQ, HD = ATT_BLOCK, 8, HEAD_DIM
    wk = hkv * HD
    n, _, specs = _attn_specs(L, q_col, k_col, v_col, wk, False)
    grp = HQ // hkv
    gated = sinks is not None

    def body(*refs):
        q_ref, kp_ref, kc_ref, vp_ref, vc_ref = refs[:5]
        rest = refs[5:]
        lim = jnp.where(pl.program_id(0) % nbc == 0, T, 0)
        for h in range(HQ):
            hs, ks = slice(h * HD, (h + 1) * HD), slice((h // grp) * HD, (h // grp + 1) * HD)
            res = _attn_head(q_ref[:, hs], kp_ref[:, ks], kc_ref[:, ks], vp_ref[:, ks], vc_ref[:, ks],
                             rest[0][:, hs] if gated else None, lim=lim, max_dist=max_dist)
            if gated:
                rest[1][:, hs] = res.astype(bf16)
            else:
                rest[0][:, hs] = res[0]
                rest[1][:, hs] = res[1]

    row = pl.BlockSpec((T, 512), lambda i: (i, 0))
    if gated:
        return _host_call(
            body, shards, (qa, ka, ka, va, va, sinks), name=name, grid=(n,),
            out_shape=jax.ShapeDtypeStruct((L, 512), bf16),
            in_specs=specs + [pl.BlockSpec((1, 512), lambda i: (0, 0))], out_specs=row, scratch_shapes=[],
            compiler_params=_params(("parallel",)), gather=True)
    return _host_call(
        body, shards, (qa, ka, ka, va, va), name=name, grid=(n,),
        out_shape=[jax.ShapeDtypeStruct((L, 512), f32)] * 2,
        in_specs=specs, out_specs=[row, row], scratch_shapes=[], compiler_params=_params(("parallel",)), gather=True)


def _attn_bwd(qa, ka, va, sinks, douts, *, q_col, k_col, v_col, hkv, nbc, max_dist, name, sends=None):
    L = qa.shape[0]
    T, HQ, HD = ATT_BLOCK, 8, HEAD_DIM
    wk = hkv * HD
    n, blk, specs = _attn_specs(L, q_col, k_col, v_col, wk, True)
    grp = HQ // hkv
    gated = sinks is not None
    nd = len(douts)

    def body(*refs):
        q_ref, kp_ref, kc_ref, vp_ref, vc_ref = refs[:5]
        pos = 5
        sink_ref = None
        if gated:
            sink_ref = refs[pos]
            pos += 1
        d_refs = refs[pos:pos + nd]
        pos += nd
        dq_ref, dk_ref, dv_ref = refs[pos:pos + 3]
        pos += 3
        dsink_ref = None
        if gated:
            dsink_ref = refs[pos]
            pos += 1
        carry_k, carry_v = refs[pos:pos + 2]

        @pl.when(pl.program_id(0) == 0)
        def _():
            carry_k[...] = jnp.zeros_like(carry_k)
            carry_v[...] = jnp.zeros_like(carry_v)
            if gated:
                dsink_ref[...] = jnp.zeros_like(dsink_ref)

        lim = jnp.where(blk(pl.program_id(0)) % nbc == 0, T, 0)
        dkp = [jnp.zeros((T, HD), f32) for _ in range(hkv)]
        dkc = [jnp.zeros((T, HD), f32) for _ in range(hkv)]
        dvp = [jnp.zeros((T, HD), f32) for _ in range(hkv)]
        dvc = [jnp.zeros((T, HD), f32) for _ in range(hkv)]
        for h in range(HQ):
            g = h // grp
            hs, ks = slice(h * HD, (h + 1) * HD), slice(g * HD, (g + 1) * HD)
            fn = functools.partial(_attn_head, lim=lim, max_dist=max_dist)
            prim = (q_ref[:, hs], kp_ref[:, ks], kc_ref[:, ks], vp_ref[:, ks], vc_ref[:, ks])
            if gated:
                _, vjp = jax.vjp(fn, *prim, sink_ref[:, hs])
                dq, a, b, c, d, ds = vjp(d_refs[0][:, hs].astype(f32))
                dsink_ref[:, hs] += ds
            else:
                _, vjp = jax.vjp(lambda *p: fn(*p, None), *prim)
                dq, a, b, c, d = vjp((d_refs[0][:, hs], d_refs[1][:, hs]))
            dq_ref[:, hs] = dq.astype(bf16)
            dkp[g], dkc[g], dvp[g], dvc[g] = dkp[g] + a, dkc[g] + b, dvp[g] + c, dvc[g] + d
        for g in range(hkv):
            ks = slice(g * HD, (g + 1) * HD)
            dk_ref[:, ks] = (dkc[g] + carry_k[:, ks]).astype(bf16)
            dv_ref[:, ks] = (dvc[g] + carry_v[:, ks]).astype(bf16)
            carry_k[:, ks] = dkp[g]
            carry_v[:, ks] = dvp[g]

    row = lambda w: pl.BlockSpec((T, w), lambda i: (blk(i), 0))
    vec = pl.BlockSpec((1, 512), lambda i: (0, 0))
    in_specs = specs + ([vec] if gated else []) + [row(512)] * nd
    out_shape = [jax.ShapeDtypeStruct((L, 512), bf16), jax.ShapeDtypeStruct((L, wk), bf16), jax.ShapeDtypeStruct((L, wk), bf16)]
    out_specs = [row(512), row(wk), row(wk)]
    if gated:
        out_shape.append(jax.ShapeDtypeStruct((1, 512), f32))
        out_specs.append(vec)
    args = (qa, ka, ka, va, va) + ((sinks,) if gated else ()) + tuple(douts)
    outs, recvs = _host_call(
        body, sends, args, name=name, grid=(n,), out_shape=out_shape, in_specs=in_specs, out_specs=out_specs,
        scratch_shapes=[pltpu.VMEM((T, wk), f32), pltpu.VMEM((T, wk), f32)],
        compiler_params=_params(("arbitrary",)))
    return outs if sends is None else (outs, recvs)


def _dilmix_f(o0, o1, o2, l0, l1, l2):
    m = jnp.maximum(jnp.maximum(l0, l1), l2)
    e0, e1, e2 = jnp.exp(l0 - m), jnp.exp(l1 - m), jnp.exp(l2 - m)
    return (e0 * o0 + e1 * o1 + e2 * o2) / (e0 + e1 + e2)


def _dilmix_fwd(os_, ls, name):
    L = os_[0].shape[0]
    tm = _tile(L, 512, 8)

    def body(o0, o1, o2, l0, l1, l2, out):
        out[...] = _dilmix_f(o0[...], o1[...], o2[...], l0[...], l1[...], l2[...]).astype(bf16)

    row = pl.BlockSpec((tm, 512), lambda i: (i, 0))
    return pl.pallas_call(
        body, name=name, grid=(L // tm,), out_shape=jax.ShapeDtypeStruct((L, 512), bf16),
        in_specs=[row] * 6, out_specs=row, compiler_params=_params(("parallel",)),
    )(*os_, *ls)


def _dilmix_bwd(os_, ls, dout, name):
    L = os_[0].shape[0]
    tm = _tile(L, 512, 8)

    def body(o0, o1, o2, l0, l1, l2, d, *outs):
        _, vjp = jax.vjp(_dilmix_f, o0[...], o1[...], o2[...], l0[...], l1[...], l2[...])
        for r, val in zip(outs, vjp(d[...].astype(f32))):
            r[...] = val

    row = pl.BlockSpec((tm, 512), lambda i: (i, 0))
    outs = pl.pallas_call(
        body, name=name, grid=(L // tm,), out_shape=[jax.ShapeDtypeStruct((L, 512), f32)] * 6,
        in_specs=[row] * 7, out_specs=[row] * 6, compiler_params=_params(("parallel",)),
    )(*os_, *ls, dout)
    return outs[:3], outs[3:]


def _to_strided(z, dil):
    L, w = z.shape
    return z.reshape(L // dil, dil, w).transpose(1, 0, 2).reshape(L, w)


def _from_strided(z, dil):
    L, w = z.shape
    return z.reshape(dil, L // dil, w).transpose(1, 0, 2).reshape(L, w)


def _adamw_math(w, g, m, v):
    m = ADAM_B1 * m + (1.0 - ADAM_B1) * g
    v = ADAM_B2 * v + (1.0 - ADAM_B2) * (g * g)
    m_hat = m / (1.0 - ADAM_B1 ** ADAM_STEP)
    v_hat = v / (1.0 - ADAM_B2 ** ADAM_STEP)
    delta = -ADAM_LR * (m_hat / (jnp.sqrt(v_hat) + ADAM_EPS) + ADAM_WD * w)
    return delta, m, v


def _adamw(w, m, v, slots, name):
    depth, R, C = w.shape
    tr = _tile(R, max(8, 131072 // C), 8)
    outs = None
    for l in range(depth):
        def body(w_ref, m_ref, v_ref, s_ref, *rest):
            g_ref, d_ref, nm_ref, nv_ref = rest[-4:]
            g = s_ref[0].astype(f32)
            for i in range(1, N_DEV):
                g = g + s_ref[i].astype(f32)
            delta, nm, nv = _adamw_math(w_ref[0], g, m_ref[0], v_ref[0])
            g_ref[0], d_ref[0], nm_ref[0], nv_ref[0] = g, delta, nm, nv

        blk = pl.BlockSpec((1, tr, C), functools.partial(lambda i, l: (l, i, 0), l=l))
        carried = [] if outs is None else list(outs)
        outs = pl.pallas_call(
            body, name=f"{name}_{l}", grid=(R // tr,), out_shape=[jax.ShapeDtypeStruct(w.shape, f32)] * 4,
            in_specs=[blk, blk, blk, pl.BlockSpec((N_DEV, tr, C), lambda i: (0, i, 0))]
            + [pl.BlockSpec(memory_space=pl.ANY)] * len(carried),
            out_specs=[blk] * 4, input_output_aliases={4 + j: j for j in range(len(carried))},
            compiler_params=_params(("parallel",)),
        )(w, m, v, slots[l], *carried)
    return outs


def _adamw_packed(w, m, v, slots, name):
    R = w.shape[0]
    tr = _tile(R, 512, 8)

    def body(w_ref, m_ref, v_ref, s_ref, g_ref, d_ref, nm_ref, nv_ref):
        g = s_ref[0]
        for i in range(1, N_DEV):
            g = g + s_ref[i]
        delta, nm, nv = _adamw_math(w_ref[...], g, m_ref[...], v_ref[...])
        g_ref[...], d_ref[...], nm_ref[...], nv_ref[...] = g, delta, nm, nv

    blk = pl.BlockSpec((tr, 128), lambda i: (i, 0))
    return pl.pallas_call(
        body, name=name, grid=(R // tr,), out_shape=[jax.ShapeDtypeStruct(w.shape, f32)] * 4,
        in_specs=[blk, blk, blk, pl.BlockSpec((N_DEV, tr, 128), lambda i: (0, i, 0))], out_specs=[blk] * 4,
        compiler_params=_params(("parallel",)),
    )(w, m, v, slots)


def _cols_gathered(g):
    return jnp.concatenate([g[d] for d in range(N_DEV)], axis=-1)


def _cols_scatter(full):
    c = full.shape[-1] // N_DEV
    return jnp.stack([full[..., d * c:(d + 1) * c] for d in range(N_DEV)])


def _win_segments(D):
    b = 4 * D
    return (((O_GATES, O_GATES + b), 0), ((0, O_GLR), b), ((O_S5U, O_GATES), b + O_GLR), ((O_GLR, O_S5U), b + P_GLR))


def _win_from_shards(g):
    rows, c = g.shape[1], g.shape[2]
    D = (N_DEV * c - O_GATES) // 4
    pieces = []
    for (lo, hi), _ in sorted(_win_segments(D), key=lambda t: t[1]):
        for d in range(N_DEV):
            a, b = max(lo, d * c), min(hi, (d + 1) * c)
            if a < b:
                pieces.append(g[d][:, a - d * c:b - d * c])
    pieces.append(jnp.zeros((rows, 128 - GLA_LOWRANK), g.dtype))
    return jnp.concatenate(pieces, axis=1)


def _win_to_shards(wp, D):
    c = (O_GATES + 4 * D) // N_DEV
    segs = sorted(_win_segments(D), key=lambda t: t[0][0])
    out = []
    for d in range(N_DEV):
        pieces = []
        for (lo, hi), off in segs:
            a, b = max(lo, d * c), min(hi, (d + 1) * c)
            if a < b:
                pieces.append(wp[:, off + a - lo:off + b - lo])
        out.append(jnp.concatenate(pieces, axis=1))
    return jnp.stack(out)


SMALL = ("norm1_g", "gla_a_b", "gla_norm_g", "s5_lambda_re", "s5_lambda_im", "s5_log_dt", "s5_b_re", "s5_b_im",
         "s5_c_re", "s5_c_im", "s5_d", "s5_glu_b", "swa_sinks", "norm2_g", "final_norm_g")
SHARDED = ("w_in", "gla_a2", "s5_glu_w", "w_branch", "w_out", "w_ffn_gate", "w_ffn_up", "w_ffn_down")
WEIGHTS = ("norm1_g", "w_in", "gla_a2", "gla_a_b", "gla_norm_g", "s5_lambda_re", "s5_lambda_im", "s5_log_dt", "s5_b_re",
           "s5_b_im", "s5_c_re", "s5_c_im", "s5_d", "s5_glu_w", "s5_glu_b", "swa_sinks", "w_branch", "w_out", "norm2_g",
           "w_ffn_gate", "w_ffn_up", "w_ffn_down", "final_norm_g")


def _pack(arrs):
    flat = jnp.concatenate([a.reshape(-1) for a in arrs])
    n = flat.shape[0]
    rows = -(-n // 1024) * 8
    return jnp.pad(flat, (0, rows * 128 - n)).reshape(rows, 128)


def _unpack(packed, like):
    flat = packed.reshape(-1)
    out, pos = [], 0
    for a in like:
        out.append(flat[pos:pos + a.size].reshape(a.shape))
        pos += a.size
    return out


def kernel(x, positions, norm1_g, w_in, gla_a2, gla_a_b, gla_norm_g, s5_lambda_re, s5_lambda_im, s5_log_dt, s5_b_re, s5_b_im, s5_c_re, s5_c_im, s5_d, s5_glu_w, s5_glu_b, swa_sinks, w_branch, w_out, norm2_g, w_ffn_gate, w_ffn_up, w_ffn_down, final_norm_g, loss_target, m_norm1_g, m_w_in, m_gla_a2, m_gla_a_b, m_gla_norm_g, m_s5_lambda_re, m_s5_lambda_im, m_s5_log_dt, m_s5_b_re, m_s5_b_im, m_s5_c_re, m_s5_c_im, m_s5_d, m_s5_glu_w, m_s5_glu_b, m_swa_sinks, m_w_branch, m_w_out, m_norm2_g, m_w_ffn_gate, m_w_ffn_up, m_w_ffn_down, m_final_norm_g, v_norm1_g, v_w_in, v_gla_a2, v_gla_a_b, v_gla_norm_g, v_s5_lambda_re, v_s5_lambda_im, v_s5_log_dt, v_s5_b_re, v_s5_b_im, v_s5_c_re, v_s5_c_im, v_s5_d, v_s5_glu_w, v_s5_glu_b, v_swa_sinks, v_w_branch, v_w_out, v_norm2_g, v_w_ffn_gate, v_w_ffn_up, v_w_ffn_down, v_final_norm_g):
    W = dict(norm1_g=norm1_g, w_in=w_in, gla_a2=gla_a2, gla_a_b=gla_a_b, gla_norm_g=gla_norm_g, s5_lambda_re=s5_lambda_re, s5_lambda_im=s5_lambda_im, s5_log_dt=s5_log_dt, s5_b_re=s5_b_re, s5_b_im=s5_b_im, s5_c_re=s5_c_re, s5_c_im=s5_c_im, s5_d=s5_d, s5_glu_w=s5_glu_w, s5_glu_b=s5_glu_b, swa_sinks=swa_sinks, w_branch=w_branch, w_out=w_out, norm2_g=norm2_g, w_ffn_gate=w_ffn_gate, w_ffn_up=w_ffn_up, w_ffn_down=w_ffn_down, final_norm_g=final_norm_g)
    Mo = dict(norm1_g=m_norm1_g, w_in=m_w_in, gla_a2=m_gla_a2, gla_a_b=m_gla_a_b, gla_norm_g=m_gla_norm_g, s5_lambda_re=m_s5_lambda_re, s5_lambda_im=m_s5_lambda_im, s5_log_dt=m_s5_log_dt, s5_b_re=m_s5_b_re, s5_b_im=m_s5_b_im, s5_c_re=m_s5_c_re, s5_c_im=m_s5_c_im, s5_d=m_s5_d, s5_glu_w=m_s5_glu_w, s5_glu_b=m_s5_glu_b, swa_sinks=m_swa_sinks, w_branch=m_w_branch, w_out=m_w_out, norm2_g=m_norm2_g, w_ffn_gate=m_w_ffn_gate, w_ffn_up=m_w_ffn_up, w_ffn_down=m_w_ffn_down, final_norm_g=m_final_norm_g)
    Vo = dict(norm1_g=v_norm1_g, w_in=v_w_in, gla_a2=v_gla_a2, gla_a_b=v_gla_a_b, gla_norm_g=v_gla_norm_g, s5_lambda_re=v_s5_lambda_re, s5_lambda_im=v_s5_lambda_im, s5_log_dt=v_s5_log_dt, s5_b_re=v_s5_b_re, s5_b_im=v_s5_b_im, s5_c_re=v_s5_c_re, s5_c_im=v_s5_c_im, s5_d=v_s5_d, s5_glu_w=v_s5_glu_w, s5_glu_b=v_s5_glu_b, swa_sinks=v_swa_sinks, w_branch=v_w_branch, w_out=v_w_out, norm2_g=v_norm2_g, w_ffn_gate=v_w_ffn_gate, w_ffn_up=v_w_ffn_up, w_ffn_down=v_w_ffn_down, final_norm_g=v_final_norm_g)

    L, D = x.shape[1], x.shape[2]
    depth = norm1_g.shape[0]
    xs = x.reshape(L, D)
    target = loss_target.reshape(L, D)
    base128 = 4 * D // 128

    in_group = ("w_in", "gla_a2", "s5_glu_w")
    full = {}

    def shards(keys, l):
        if l >= depth:
            return None
        return [W[k][l] if k in ("gla_a2", "s5_glu_w") else W[k][l].astype(bf16) for k in keys]

    def landed(keys, l, gathered):
        for k, g in zip(keys, gathered):
            if k == "w_in":
                full[k, l] = _win_from_shards(g)
            elif k == "gla_a2":
                full[k, l] = jnp.pad(_cols_gathered(g), ((0, 128 - GLA_LOWRANK), (0, 0)))
            elif k in ("w_branch", "w_ffn_gate", "w_ffn_up"):
                full[k, l] = _cols_gathered(g)
            else:
                full[k, l] = g.reshape((-1, g.shape[-1]))

    landed(in_group, 0, _all_gather(shards(in_group, 0), "gather_w_in0"))

    cos, sin = _rope_tables(positions.reshape(L))

    saved = []
    cur = xs
    for l in range(depth):
        s = {"x": cur}
        nxt = l + 1
        h1 = _rms_fwd(cur, norm1_g[l][None], f"rms1_fwd{l}")
        if l == 0:
            keys = ("w_branch", "w_out", "w_ffn_gate")
            proj, got = _matmul(h1, full["w_in", l], sends=shards(keys, 0), gather=True, name=f"proj_in{l}")
            landed(keys, 0, got)
        else:
            proj = _matmul(h1, full["w_in", l], name=f"proj_in{l}")
        s["h1"], s["proj"] = h1, proj
        ab, ng = gla_a_b[l][None], gla_norm_g[l].reshape(1, 512)
        (o_gla, s["gla_st"]), got = _gla_fwd(proj, full["gla_a2", l], ab, ng, base128, f"gla_fwd{l}", shards=shards(in_group, nxt))
        landed(in_group, nxt, got)
        prep, s["prep_vjp"] = jax.vjp(_s5_prep, s5_lambda_re[l], s5_lambda_im[l], s5_log_dt[l], s5_b_re[l], s5_b_im[l],
                                      s5_c_re[l], s5_c_im[l], s5_d[l])
        s["prep"] = prep
        (y_s5, s["s5_r"], s["s5_i"]), got = _s5_fwd(proj, prep, base128, f"s5_fwd{l}", shards=shards(("w_branch", "w_out"), nxt))
        landed(("w_branch", "w_out"), nxt, got)
        s["y_s5"] = y_s5
        o_s5 = _glu_fwd(y_s5, full["s5_glu_w", l], s5_glu_b[l][None], f"glu_fwd{l}")
        sinks_b = jnp.repeat(swa_sinks[l], HEAD_DIM)[None]
        s["sinks_b"] = sinks_b
        nb = L // ATT_BLOCK
        cq, ck, cv = (base128 + P_CQ // 128) // 4, (base128 + P_CK // 128) // 4, (base128 + P_CV // 128) // 4
        sq_col, sk_col, sv_col = (base128 + P_SQ // 128) // 4, base128 + P_SK // 128, base128 + P_SV // 128
        cq_r, ck_r, sq_r, sk_r = _rope([(proj, cq, 512), (proj, ck, 512), (proj, sq_col, 512), (proj, sk_col, 128)], cos, sin,
                                       transpose=False, out_dtype=f32, name=f"rope_fwd{l}")
        s["rot"] = (cq_r, ck_r, sq_r, sk_r)
        o_swa, got = _attn_fwd(sq_r, sk_r, proj, sinks_b, q_col=0, k_col=0, v_col=sv_col, hkv=SWA_KV_HEADS, nbc=nb,
                               max_dist=SWA_WINDOW - 1, name=f"swa_fwd{l}", shards=shards(("w_ffn_up",), 0) if l == 0 else None)
        landed(("w_ffn_up",), 0, got)
        dil_o, dil_l, s["dil_in"] = [], [], []
        riders = ((("w_ffn_down",), 0 if l == 0 else depth), (("w_ffn_gate",), nxt), (("w_ffn_up",), nxt))
        for (window, dil), (keys, kl) in zip(DIL_CONFIGS, riders):
            if dil == 1:
                (o, lse), got = _attn_fwd(cq_r, ck_r, proj, None, q_col=0, k_col=0, v_col=cv, hkv=8, nbc=nb,
                                          max_dist=window // dil, name=f"dil{dil}_fwd{l}", shards=shards(keys, kl))
                s["dil_in"].append(None)
            else:
                qs_, ks_ = _to_strided(cq_r, dil), _to_strided(ck_r, dil)
                vs_ = _to_strided(proj[:, 4 * D + P_CV:4 * D + P_CV + 512], dil)
                (o, lse), got = _attn_fwd(qs_, ks_, vs_, None, q_col=0, k_col=0, v_col=0, hkv=8,
                                          nbc=nb // dil, max_dist=window // dil, name=f"dil{dil}_fwd{l}", shards=shards(keys, kl))
                o, lse = _from_strided(o, dil), _from_strided(lse, dil)
                s["dil_in"].append((qs_, ks_, vs_))
            landed(keys, kl, got)
            dil_o.append(o)
            dil_l.append(lse)
        s["dil_o"], s["dil_l"] = dil_o, dil_l
        o_dil = _dilmix_fwd(dil_o, dil_l, f"dilmix_fwd{l}")
        branches = (o_gla, o_s5, o_dil, o_swa)
        s["branches"] = branches
        ys = [_matmul(br, full["w_branch", l][m], out_dtype=bf16, name=f"branch{m}_fwd{l}") for m, br in enumerate(branches)]
        s["ys"] = ys
        mixed = _merge_fwd(proj, ys, D, f"merge_fwd{l}")
        s["mixed"] = mixed
        x2 = _matmul(mixed, full["w_out", l], res=cur, name=f"out_fwd{l}")
        s["x2"] = x2
        h2 = _rms_fwd(x2, norm2_g[l][None], f"rms2_fwd{l}")
        rider = shards(("w_ffn_down",), nxt)
        if rider:
            a, got = _matmul(h2, full["w_ffn_gate", l], out_dtype=bf16, sends=rider, gather=True, name=f"ffn_gate_fwd{l}")
            landed(("w_ffn_down",), nxt, got)
        else:
            a = _matmul(h2, full["w_ffn_gate", l], out_dtype=bf16, name=f"ffn_gate_fwd{l}")
        b, act = _matmul(h2, full["w_ffn_up", l], extras=(a,), epilogue=_swiglu_epilogue, out_dtype=(bf16, bf16),
                         name=f"ffn_up_fwd{l}")
        s["h2"], s["a"], s["b"], s["act"] = h2, a, b, act
        cur = _matmul(act, full["w_ffn_down", l], res=x2, name=f"ffn_down_fwd{l}")
        saved.append(s)
    win_p, a2p, glu_w, wb, wout, wg, wu, wd = (
        [full[k, l] for l in range(depth)]
        for k in ("w_in", "gla_a2", "s5_glu_w", "w_branch", "w_out", "w_ffn_gate", "w_ffn_up", "w_ffn_down"))

    loss_part, dcur, dcur_b, dgf = _final_loss(cur, final_norm_g[None], target, "final_loss")
    loss = lax.psum(loss_part, AXES)

    small_g = {k: [None] * depth for k in SMALL if k != "final_norm_g"}
    recv = {k: [None] * depth for k in SHARDED}
    in_group = ("w_in", "gla_a2", "s5_glu_w")
    pending = None
    for l in reversed(range(depth)):
        s = saved[l]
        proj = s["proj"]
        da, db = _matmul(dcur_b, wd[l], mode="nt", extras=(s["a"], s["b"]), epilogue=_swiglu_grad_epilogue,
                         out_dtype=(bf16, bf16), name=f"ffn_down_dx{l}")
        g_down = _matmul(s["act"], dcur_b, mode="tn", out_dtype=bf16, name=f"ffn_down_dw{l}")
        dh2 = _matmul(da, wg[l], mode="nt", name=f"ffn_gate_dx{l}")
        dh2 = _matmul(db, wu[l], mode="nt", res=dh2, name=f"ffn_up_dx{l}")
        g_gate = _matmul(s["h2"], da, mode="tn", out_dtype=bf16, name=f"ffn_gate_dw{l}")
        g_up = _matmul(s["h2"], db, mode="tn", out_dtype=bf16, name=f"ffn_up_dw{l}")
        ffn_sends = (("w_ffn_down", g_down.reshape((N_DEV, -1, D))), ("w_ffn_gate", _cols_scatter(g_gate)),
                     ("w_ffn_up", _cols_scatter(g_up)))
        dx2, dx2_b, dg2 = _rms_bwd(s["x2"], norm2_g[l][None], dh2, dcur, f"rms2_bwd{l}")
        small_g["norm2_g"][l] = dg2[0]
        dmixed = _matmul(dx2_b, wout[l], mode="nt", name=f"out_dx{l}")
        g_out = _matmul(s["mixed"], dx2_b, mode="tn", out_dtype=bf16, name=f"out_dw{l}")
        dys, dgates = _merge_bwd(proj, s["ys"], dmixed, D, f"merge_bwd{l}")
        dbr = [_matmul(dys[m], wb[l][m], mode="nt", name=f"branch{m}_dx{l}") for m in range(4)]
        g_branch = jnp.stack([_matmul(s["branches"][m], dys[m], mode="tn", out_dtype=bf16, name=f"branch{m}_dw{l}")
                              for m in range(4)])
        d_gla, d_s5, d_dil, d_swa = dbr
        ab, ng = gla_a_b[l][None], gla_norm_g[l].reshape(1, 512)
        (dgq, dgk, dgv, dgr, dglr, da2, dab, dng), got = _gla_bwd(proj, a2p[l], ab, ng, s["gla_st"], d_gla, base128,
                                                                   f"gla_bwd{l}", sends=pending)
        for k, r in zip(in_group, got):
            recv[k][l + 1] = r
        small_g["gla_a_b"][l] = dab[0]
        small_g["gla_norm_g"][l] = dng.reshape(GLA_HEADS, GLA_DV)
        dy_s5, dglu_w, dglu_b = _glu_bwd(s["y_s5"], glu_w[l], s5_glu_b[l][None], d_s5, f"glu_bwd{l}")
        small_g["s5_glu_b"][l] = dglu_b[0]
        ds5u, dprep, got = _s5_bwd(proj, s["prep"], s["s5_r"], s["s5_i"], dy_s5, base128, f"s5_bwd{l}",
                                   sends=[g_out.reshape((N_DEV, -1, D)), _cols_scatter(g_branch)])
        recv["w_out"][l], recv["w_branch"][l] = got
        draw = s["prep_vjp"](dprep)
        for k, val in zip(("s5_lambda_re", "s5_lambda_im", "s5_log_dt", "s5_b_re", "s5_b_im", "s5_c_re", "s5_c_im", "s5_d"), draw):
            small_g[k][l] = val
        nb = L // ATT_BLOCK
        cq_r, ck_r, sq_r, sk_r = s["rot"]
        dsq, dsk, dsv, dsinks = _attn_bwd(sq_r, sk_r, proj, s["sinks_b"], (d_swa,), q_col=0, k_col=0,
                                          v_col=base128 + P_SV // 128, hkv=SWA_KV_HEADS, nbc=nb, max_dist=SWA_WINDOW - 1,
                                          name=f"swa_bwd{l}")
        small_g["swa_sinks"][l] = dsinks.reshape(SWA_HEADS, HEAD_DIM).sum(axis=1)
        dos, dls = _dilmix_bwd(s["dil_o"], s["dil_l"], d_dil, f"dilmix_bwd{l}")
        cv = (base128 + P_CV // 128) // 4
        dcq = dck = dcv = None
        for i, (window, dil) in enumerate(DIL_CONFIGS):
            key, send = ffn_sends[i]
            if dil == 1:
                g3, got = _attn_bwd(cq_r, ck_r, proj, None, (dos[i], dls[i]), q_col=0, k_col=0, v_col=cv, hkv=8,
                                    nbc=nb, max_dist=window // dil, name=f"dil{dil}_bwd{l}", sends=[send])
            else:
                qs_, ks_, vs_ = s["dil_in"][i]
                g3, got = _attn_bwd(qs_, ks_, vs_, None, (_to_strided(dos[i], dil), _to_strided(dls[i], dil)),
                                    q_col=0, k_col=0, v_col=0, hkv=8, nbc=nb // dil, max_dist=window // dil,
                                    name=f"dil{dil}_bwd{l}", sends=[send])
                g3 = [_from_strided(t, dil) for t in g3]
            recv[key][l] = got[0]
            g3 = [t.astype(f32) for t in g3]
            dcq, dck, dcv = (g3[0], g3[1], g3[2]) if dcq is None else (dcq + g3[0], dck + g3[1], dcv + g3[2])
        dcq, dck, dsq, dsk = _rope([(dcq, 0, 512), (dck, 0, 512), (dsq, 0, 512), (dsk, 0, 128)], cos, sin,
                                   transpose=True, out_dtype=bf16, name=f"rope_bwd{l}")
        dproj = jnp.concatenate([dgates.transpose(1, 0, 2).reshape(L, 4 * D), dgq, dgk, dgv, dgr, ds5u,
                                 dcq, dck, dcv.astype(bf16), dsq, dsk, dsv, dglr], axis=1)
        g_in = _matmul(s["h1"], dproj, mode="tn", out_dtype=bf16, name=f"proj_in_dw{l}")
        in_sends = [_win_to_shards(g_in, D),_cols_scatter(da2[:GLA_LOWRANK]), dglu_w.reshape((N_DEV, -1, 512))]
        if l > 0:
            dh1 = _matmul(dproj, win_p[l], mode="nt", name=f"proj_in_dx{l}")
            pending = in_sends
        else:
            dh1, got = _matmul(dproj, win_p[l], mode="nt", sends=in_sends, name=f"proj_in_dx{l}")
            for k, r in zip(in_group, got):
                recv[k][l] = r
        dcur, dcur_b, dg1 = _rms_bwd(s["x"], norm1_g[l][None], dh1, dx2, f"rms1_bwd{l}")
        small_g["norm1_g"][l] = dg1[0]
    grad_x = dcur.reshape(x.shape)

    out = {}
    for k in SHARDED:
        shp = W[k].shape
        as3 = lambda t: t.reshape((shp[0], -1, shp[-1]))
        slots = [r.reshape((N_DEV, -1, shp[-1])) for r in recv[k]]
        res = _adamw(as3(W[k]), as3(Mo[k]), as3(Vo[k]), slots, f"adamw_{k}")
        out[k] = [t.reshape(shp) for t in res]

    small_list = [jnp.stack(small_g[k]) if k != "final_norm_g" else dgf[0] for k in SMALL]
    small_list = [t.reshape(W[k].shape) for t, k in zip(small_list, SMALL)]
    packed_parts = _all_gather([_pack(small_list)], "gather_small_grads")[0]
    res = _adamw_packed(_pack([W[k] for k in SMALL]), _pack([Mo[k] for k in SMALL]), _pack([Vo[k] for k in SMALL]),
                        packed_parts, "adamw_small")
    unpacked = [_unpack(t, [W[k] for k in SMALL]) for t in res]
    for i, k in enumerate(SMALL):
        out[k] = [unpacked[j][i] for j in range(4)]

    return (loss, grad_x, *[out[k][0] for k in WEIGHTS], *[out[k][1] for k in WEIGHTS],
            *[out[k][2] for k in WEIGHTS], *[out[k][3] for k in WEIGHTS])
```

```python
import functools
import math

import jax
import jax.numpy as jnp
from jax import lax
from jax.experimental import pallas as pl
from jax.experimental.pallas import tpu as pltpu

f32 = jnp.float32
bf16 = jnp.bfloat16
HI = lax.Precision.HIGHEST

N_DEV = 8
AXES = ("x", "y", "c")
NORM_EPS = 1e-6
ROPE_THETA = 500000.0
HEAD_DIM = 64
ROPE_DIM = 16
ATT_BLOCK = 128
BRANCH_WIDTH = 512
GLA_HEADS, GLA_DK, GLA_DV, GLA_LOWRANK, GLA_TAU, GLA_CHUNK, GLA_SUB = 4, 64, 128, 16, 16.0, 64, 16
S5_GROUPS, S5_GROUP, S5_STATE = 32, 16, 64
S5_CHUNK = 128
S5_LANE_BLOCKS = 4
DIL_CONFIGS = ((128, 1), (512, 4), (2048, 16))
SWA_HEADS, SWA_KV_HEADS, SWA_WINDOW = 8, 2, 128
ADAM_LR, ADAM_B1, ADAM_B2, ADAM_EPS, ADAM_WD, ADAM_STEP = 0.001, 0.9, 0.999, 1e-08, 0.01, 10
O_GLR, O_S5U, O_GATES = 1536, 1552, 4368
MIX_COLS = 4480
P_GQ, P_GK, P_GV, P_GR, P_S5U, P_CQ, P_CK, P_CV, P_SQ, P_SK, P_SV, P_GLR = (
    0, 256, 512, 1024, 1536, 2048, 2560, 3072, 3584, 4096, 4224, 4352)
VMEM_LIMIT = 56 * 1024 * 1024


def _tile(n, cap, q=128):
    if n <= cap:
        return n
    t = (cap // q) * q
    while t >= q:
        if n % t == 0:
            return t
        t -= q
    return n


def _params(sem=None):
    return pltpu.CompilerParams(dimension_semantics=sem, vmem_limit_bytes=VMEM_LIMIT)


@functools.partial(jax.custom_vjp, nondiff_argnums=(1,))
def _sroll(x, d):
    return pltpu.roll(x, d, 0)


def _sroll_fwd(x, d):
    return pltpu.roll(x, d, 0), None


def _sroll_bwd(d, _, g):
    n = g.shape[0]
    return (pltpu.roll(g, (n - d) % n, 0),)


_sroll.defvjp(_sroll_fwd, _sroll_bwd)


def _mesh_pos():
    return lax.axis_index("x"), lax.axis_index("y"), lax.axis_index("c")


class _Gather:
    def __init__(self, ins, outs, send_sems, recv_sems, local_sems):
        self.ins, self.outs = ins, outs
        self.send_sems, self.recv_sems, self.local_sems = send_sems, recv_sems, local_sems
        x, y, c = _mesh_pos()
        self.x, self.y, self.c = x, y, c
        self.me, self.sibling = (x, y, c), (x, y, 1 - c)
        self.chips = [(1 - x, y), (x, 1 - y), (1 - x, 1 - y)]

    def copy(self, a, k, block, to, src=None):
        slot = self.outs[a].at[4 * block[0] + 2 * block[1] + block[2]]
        return pltpu.make_async_remote_copy(
            src_ref=slot if src is None else src, dst_ref=slot,
            send_sem=self.send_sems.at[a, k], recv_sem=self.recv_sems.at[a, k],
            device_id=to, device_id_type=pl.DeviceIdType.MESH)

    def mine(self, a):
        return pltpu.make_async_copy(self.ins[a], self.outs[a].at[4 * self.x + 2 * self.y + self.c], self.local_sems.at[a])

    def first(self, a):
        return [self.copy(a, 0, self.me, self.sibling, src=self.ins[a])] + [
            self.copy(a, 1 + j, self.me, (*chip, self.c), src=self.ins[a]) for j, chip in enumerate(self.chips)]

    def start(self):
        for a in range(len(self.ins)):
            self.mine(a).start()
            for cp in self.first(a):
                cp.start()

    def finish(self):
        c = self.c
        for a in range(len(self.ins)):
            passed = [self.copy(a, 4 + j, (*chip, c), self.sibling) for j, chip in enumerate(self.chips)]
            for j, chip in enumerate(self.chips):
                self.copy(a, 1 + j, (*chip, c), self.me).wait_recv()
                passed[j].start()
            self.copy(a, 0, self.sibling, self.me).wait_recv()
            for j, chip in enumerate(self.chips):
                self.copy(a, 4 + j, (*chip, 1 - c), self.me).wait_recv()
            for cp in self.first(a) + passed:
                cp.wait_send()
            self.mine(a).wait()


def _all_gather(shards, name):
    n = len(shards)
    any_spec = pl.BlockSpec(memory_space=pl.ANY)

    def body(*refs):
        g = _Gather(refs[:n], refs[n:2 * n], *refs[2 * n:])
        g.start()
        g.finish()

    outs = pl.pallas_call(
        body, name=name,
        out_shape=[jax.ShapeDtypeStruct((N_DEV,) + s.shape, s.dtype) for s in shards],
        in_specs=[any_spec] * n, out_specs=[any_spec] * n,
        scratch_shapes=[pltpu.SemaphoreType.DMA((n, 7)), pltpu.SemaphoreType.DMA((n, 7)),
                        pltpu.SemaphoreType.DMA((n,))],
    )(*shards)
    return list(outs)


def _a2a_copies(ins, outs, send_sems, recv_sems, local_sems):
    x, y, c = _mesh_pos()
    me = 4 * x + 2 * y + c
    copies = []
    for a in range(len(ins)):
        copies.append(pltpu.make_async_copy(ins[a].at[me], outs[a].at[me], local_sems.at[a]))
        for k in range(1, N_DEV):
            px = 1 - x if k & 4 else x
            py = 1 - y if k & 2 else y
            pc = 1 - c if k & 1 else c
            copies.append(pltpu.make_async_remote_copy(
                src_ref=ins[a].at[4 * px + 2 * py + pc], dst_ref=outs[a].at[me],
                send_sem=send_sems.at[a, k - 1], recv_sem=recv_sems.at[a, k - 1],
                device_id=(px, py, pc), device_id_type=pl.DeviceIdType.MESH))
    return copies


def _host_call(body, sends, args, *, name, grid, out_shape, in_specs, out_specs, scratch_shapes, compiler_params,
               gather=False):
    single = not isinstance(out_shape, (list, tuple))
    out_shape = [out_shape] if single else list(out_shape)
    out_specs = [out_specs] if single else list(out_specs)
    sends = list(sends or ())
    n, n_in, n_out, n_scr = len(sends), len(args), len(out_shape), len(scratch_shapes)
    if n == 0:
        outs = pl.pallas_call(body, name=name, grid=grid, out_shape=out_shape, in_specs=in_specs, out_specs=out_specs,
                              scratch_shapes=list(scratch_shapes), compiler_params=compiler_params)(*args)
        return (outs[0] if single else list(outs)), []
    any_spec = pl.BlockSpec(memory_space=pl.ANY)

    def hosted(*refs):
        ins, s_in = refs[:n_in], refs[n_in:n_in + n]
        pos = n_in + n
        outs, s_out = refs[pos:pos + n_out], refs[pos + n_out:pos + n_out + n]
        pos += n_out + n
        scr, sems = refs[pos:pos + n_scr], refs[pos + n_scr:]
        ids = [pl.program_id(i) for i in range(len(grid))]
        first = functools.reduce(lambda p, q: p & q, [i == 0 for i in ids])
        last = functools.reduce(lambda p, q: p & q, [i == g - 1 for i, g in zip(ids, grid)])

        @pl.when(first)
        def _():
            if gather:
                _Gather(s_in, s_out, *sems).start()
            else:
                for cp in _a2a_copies(s_in, s_out, *sems):
                    cp.start()

        body(*ins, *outs, *scr)

        @pl.when(last)
        def _():
            if gather:
                _Gather(s_in, s_out, *sems).finish()
            else:
                for cp in _a2a_copies(s_in, s_out, *sems):
                    cp.wait()

    lead = (N_DEV,) if gather else ()
    outs = pl.pallas_call(
        hosted, name=name, grid=grid,
        out_shape=out_shape + [jax.ShapeDtypeStruct(lead + s.shape, s.dtype) for s in sends],
        in_specs=list(in_specs) + [any_spec] * n, out_specs=out_specs + [any_spec] * n,
        scratch_shapes=list(scratch_shapes) + [pltpu.SemaphoreType.DMA((n, 7)), pltpu.SemaphoreType.DMA((n, 7)),
                                               pltpu.SemaphoreType.DMA((n,))],
        compiler_params=compiler_params,
    )(*args, *sends)
    main = list(outs[:n_out])
    return (main[0] if single else main), list(outs[n_out:])


def _matmul(a, b, *, mode="nn", out_dtype=f32, res=None, extras=(), epilogue=None, sends=None, gather=False, name):
    if mode == "tn":
        K, M = a.shape
    else:
        M, K = a.shape
    N = b.shape[0] if mode == "nt" else b.shape[1]
    k_cap = 2048 if (a.dtype == bf16 and b.dtype == bf16) else 1024
    tm, tn, tk = _tile(M, 1024), _tile(N, 1152), _tile(K, k_cap)
    nk = K // tk
    dims = {"nn": (((1,), (0,)), ((), ())), "nt": (((1,), (1,)), ((), ())), "tn": (((0,), (0,)), ((), ()))}[mode]
    out_dtypes = list(out_dtype) if epilogue is not None else [out_dtype]
    n_side = (1 if res is not None else 0) + len(extras)

    def body(*refs):
        a_ref, b_ref = refs[:2]
        r_ref = refs[2] if res is not None else None
        x_refs = refs[2 + n_side - len(extras):2 + n_side]
        o_refs = refs[2 + n_side:2 + n_side + len(out_dtypes)]
        acc = refs[-1] if nk > 1 else None
        k = pl.program_id(2)
        part = lax.dot_general(a_ref[...].astype(bf16), b_ref[...].astype(bf16), dims, preferred_element_type=f32)

        def finish(r):
            if res is not None:
                r = r + r_ref[...]
            outs = epilogue(r, *[x[...] for x in x_refs]) if epilogue is not None else (r,)
            for o_ref, val in zip(o_refs, outs):
                o_ref[...] = val.astype(o_ref.dtype)

        if nk == 1:
            finish(part)
            return

        @pl.when(k == 0)
        def _():
            acc[...] = part

        @pl.when((k > 0) & (k < nk - 1))
        def _():
            acc[...] += part

        @pl.when(k == nk - 1)
        def _():
            finish(acc[...] + part)

    a_spec = pl.BlockSpec((tk, tm), lambda i, j, k: (k, i)) if mode == "tn" else pl.BlockSpec((tm, tk), lambda i, j, k: (i, k))
    b_spec = pl.BlockSpec((tn, tk), lambda i, j, k: (j, k)) if mode == "nt" else pl.BlockSpec((tk, tn), lambda i, j, k: (k, j))
    o_spec = pl.BlockSpec((tm, tn), lambda i, j, k: (i, j))
    in_specs = [a_spec, b_spec] + [o_spec] * n_side
    args = (a, b) + ((res,) if res is not None else ()) + tuple(extras)
    out, recvs = _host_call(
        body, sends, args, name=name, grid=(M // tm, N // tn, nk),
        out_shape=[jax.ShapeDtypeStruct((M, N), dt) for dt in out_dtypes],
        in_specs=in_specs, out_specs=[o_spec] * len(out_dtypes),
        scratch_shapes=[pltpu.VMEM((tm, tn), f32)] if nk > 1 else [],
        compiler_params=_params(("parallel", "parallel", "arbitrary")), gather=gather)
    if epilogue is None:
        out = out[0]
    return out if sends is None else (out, recvs)


def _rms(x, g):
    return x * lax.rsqrt(jnp.mean(x * x, axis=-1, keepdims=True) + NORM_EPS) * g


def _rms_fwd(x, g, name):
    L, D = x.shape
    tm = _tile(L, 256, 8)

    def body(x_ref, g_ref, o_ref):
        o_ref[...] = _rms(x_ref[...], g_ref[...]).astype(bf16)

    return pl.pallas_call(
        body, name=name, grid=(L // tm,), out_shape=jax.ShapeDtypeStruct((L, D), bf16),
        in_specs=[pl.BlockSpec((tm, D), lambda i: (i, 0)), pl.BlockSpec((1, D), lambda i: (0, 0))],
        out_specs=pl.BlockSpec((tm, D), lambda i: (i, 0)),
        compiler_params=_params(("parallel",)),
    )(x, g)


def _rms_bwd(x, g, dh, dres, name):
    L, D = x.shape
    tm = _tile(L, 256, 8)

    def body(x_ref, g_ref, dh_ref, dres_ref, dx_ref, dxb_ref, dg_ref):
        _, vjp = jax.vjp(_rms, x_ref[...], g_ref[...])
        dx, dg = vjp(dh_ref[...])
        dx = dres_ref[...] + dx
        dx_ref[...] = dx
        dxb_ref[...] = dx.astype(bf16)

        @pl.when(pl.program_id(0) == 0)
        def _():
            dg_ref[...] = jnp.zeros_like(dg_ref)

        dg_ref[...] += dg

    row = pl.BlockSpec((tm, D), lambda i: (i, 0))
    vec = pl.BlockSpec((1, D), lambda i: (0, 0))
    return pl.pallas_call(
        body, name=name, grid=(L // tm,),
        out_shape=[jax.ShapeDtypeStruct((L, D), f32), jax.ShapeDtypeStruct((L, D), bf16), jax.ShapeDtypeStruct((1, D), f32)],
        in_specs=[row, vec, row, row], out_specs=[row, row, vec],
        compiler_params=_params(("arbitrary",)),
    )(x, g, dh, dres)


def _final_loss(x, g, target, name):
    L, D = x.shape
    tm = _tile(L, 256, 8)

    def body(x_ref, g_ref, t_ref, loss_ref, dx_ref, dxb_ref, dg_ref):
        tgt = t_ref[...]

        def f(xv, gv):
            err = _rms(xv, gv) - tgt
            return 0.5 * jnp.sum(jnp.mean(err * err, axis=-1, keepdims=True), axis=0, keepdims=True)

        val, vjp = jax.vjp(f, x_ref[...], g_ref[...])
        dx, dg = vjp(jnp.ones((1, 1), f32))
        dx_ref[...] = dx
        dxb_ref[...] = dx.astype(bf16)

        @pl.when(pl.program_id(0) == 0)
        def _():
            dg_ref[...] = jnp.zeros_like(dg_ref)
            loss_ref[...] = jnp.zeros_like(loss_ref)

        dg_ref[...] += dg
        loss_ref[...] += jnp.broadcast_to(val, loss_ref.shape)

    row = pl.BlockSpec((tm, D), lambda i: (i, 0))
    vec = pl.BlockSpec((1, D), lambda i: (0, 0))
    acc = pl.BlockSpec((8, 128), lambda i: (0, 0))
    loss, dx, dxb, dg = pl.pallas_call(
        body, name=name, grid=(L // tm,),
        out_shape=[jax.ShapeDtypeStruct((8, 128), f32), jax.ShapeDtypeStruct((L, D), f32), jax.ShapeDtypeStruct((L, D), bf16),
                   jax.ShapeDtypeStruct((1, D), f32)],
        in_specs=[row, vec, row], out_specs=[acc, row, row, vec],
        compiler_params=_params(("arbitrary",)),
    )(x, g, target)
    return loss[0, 0], dx, dxb, dg


def _swiglu_f(a, b):
    return jax.nn.silu(a) * b


def _swiglu_epilogue(up, gate):
    return up, _swiglu_f(gate.astype(f32), up)


def _swiglu_grad_epilogue(dact, gate, up):
    _, vjp = jax.vjp(_swiglu_f, gate.astype(f32), up.astype(f32))
    return vjp(dact)


def _merge_f(g0, g1, g2, g3, y0, y1, y2, y3):
    s = jax.nn.sigmoid
    return s(g0) * y0 + s(g1) * y1 + s(g2) * y2 + s(g3) * y3


def _merge_fwd(proj, ys, D, name):
    L = proj.shape[0]
    tm, tn = _tile(L, 512, 8), _tile(D, 512)
    nj = D // tn

    def body(g0, g1, g2, g3, y0, y1, y2, y3, o_ref):
        o_ref[...] = _merge_f(g0[...], g1[...], g2[...], g3[...], *[y[...].astype(f32) for y in (y0, y1, y2, y3)]).astype(bf16)

    gspecs = [pl.BlockSpec((tm, tn), functools.partial(lambda i, j, m: (i, m * nj + j), m=m)) for m in range(4)]
    blk = pl.BlockSpec((tm, tn), lambda i, j: (i, j))
    return pl.pallas_call(
        body, name=name, grid=(L // tm, nj), out_shape=jax.ShapeDtypeStruct((L, D), bf16),
        in_specs=gspecs + [blk] * 4, out_specs=blk, compiler_params=_params(("parallel", "parallel")),
    )(proj, proj, proj, proj, *ys)


def _merge_bwd(proj, ys, dmixed, D, name):
    L = proj.shape[0]
    tm, tn = _tile(L, 512, 8), _tile(D, 512)
    nj = D // tn

    def body(g0, g1, g2, g3, y0, y1, y2, y3, d_ref, dy0, dy1, dy2, dy3, dg_ref):
        _, vjp = jax.vjp(_merge_f, g0[...], g1[...], g2[...], g3[...], *[y[...].astype(f32) for y in (y0, y1, y2, y3)])
        grads = vjp(d_ref[...])
        for m, r in enumerate((dy0, dy1, dy2, dy3)):
            r[...] = grads[4 + m].astype(bf16)
        for m in range(4):
            dg_ref[m] = grads[m].astype(bf16)

    gspecs = [pl.BlockSpec((tm, tn), functools.partial(lambda i, j, m: (i, m * nj + j), m=m)) for m in range(4)]
    blk = pl.BlockSpec((tm, tn), lambda i, j: (i, j))
    dgspec = pl.BlockSpec((4, tm, tn), lambda i, j: (0, i, j))
    outs = pl.pallas_call(
        body, name=name, grid=(L // tm, nj),
        out_shape=[jax.ShapeDtypeStruct((L, D), bf16)] * 4 + [jax.ShapeDtypeStruct((4, L, D), bf16)],
        in_specs=gspecs + [blk] * 5, out_specs=[blk] * 4 + [dgspec],
        compiler_params=_params(("parallel", "parallel")),
    )(proj, proj, proj, proj, *ys, dmixed)
    return outs[:4], outs[4]


def _gla_head(q, k, v, r, glr, st, a2, ab, ng):
    C, T = GLA_CHUNK, GLA_SUB
    row = lax.broadcasted_iota(jnp.int32, (C, C), 0)
    col = lax.broadcasted_iota(jnp.int32, (C, C), 1)
    tri = (col <= row).astype(f32)
    sel = (col == (row // T) * T).astype(f32)
    z = jnp.dot(glr, a2, preferred_element_type=f32) + ab
    g = jax.nn.log_sigmoid(z) / GLA_TAU
    cum = jnp.dot(tri, g, precision=HI, preferred_element_type=f32)
    excl = cum - g
    ref = jnp.dot(sel, excl, precision=HI, preferred_element_type=f32)
    qs = q * (GLA_DK ** -0.5)
    q_ref = qs * jnp.exp(cum - ref)
    rowk = lax.broadcasted_iota(jnp.int32, (C, GLA_DK), 0)
    a = jnp.zeros((C, C), f32)
    for s in range(1, C // T):
        ref_s = jnp.sum(jnp.where(rowk == s * T, excl, 0.0), axis=0, keepdims=True)
        k_ref = k * jnp.exp(jnp.where(rowk < s * T, ref_s - cum, -jnp.inf))
        a_s = lax.dot_general(q_ref, k_ref, (((1,), (1,)), ((), ())), preferred_element_type=f32)
        a = a + jnp.where(row // T == s, a_s, 0.0)
    o = jnp.dot(a, v, preferred_element_type=f32)
    sub = rowk % T
    for d in range(T):
        ks = _sroll(k, d) if d else k
        cs = _sroll(cum, d) if d else cum
        vs = _sroll(v, d) if d else v
        w = jnp.sum(qs * ks * jnp.exp(jnp.where(sub >= d, cum - cs, -jnp.inf)), axis=-1, keepdims=True)
        o = o + w * vs
    o = o + lax.dot_general(qs * jnp.exp(cum), st, (((1,), (1,)), ((), ())), preferred_element_type=f32)
    last = jnp.sum(jnp.where(rowk == C - 1, cum, 0.0), axis=0, keepdims=True)
    st_new = st * jnp.exp(last) + lax.dot_general(v, k * jnp.exp(last - cum), (((0,), (0,)), ((), ())),
                                                  preferred_element_type=f32)
    out = _rms(o, ng) * jax.nn.silu(r)
    return out, st_new


def _gla_specs(L, base128, rev):
    n = L // GLA_CHUNK
    ch = (lambda i: n - 1 - i) if rev else (lambda i: i)
    b = base128
    return n, ch, [
        pl.BlockSpec((GLA_CHUNK, 256), lambda i: (ch(i), (b + P_GQ // 128) // 2)),
        pl.BlockSpec((GLA_CHUNK, 256), lambda i: (ch(i), (b + P_GK // 128) // 2)),
        pl.BlockSpec((GLA_CHUNK, 512), lambda i: (ch(i), (b + P_GV // 128) // 4)),
        pl.BlockSpec((GLA_CHUNK, 512), lambda i: (ch(i), (b + P_GR // 128) // 4)),
        pl.BlockSpec((GLA_CHUNK, 128), lambda i: (ch(i), b + P_GLR // 128)),
    ]


def _gla_fwd(proj, a2p, ab, ng, base128, name, shards=None):
    L = proj.shape[0]
    n, _, pspecs = _gla_specs(L, base128, False)
    H, DK, DV = GLA_HEADS, GLA_DK, GLA_DV

    def body(q_ref, k_ref, v_ref, r_ref, l_ref, a2_ref, ab_ref, ng_ref, o_ref, st_ref, state):
        @pl.when(pl.program_id(0) == 0)
        def _():
            state[...] = jnp.zeros_like(state)

        glr = l_ref[...]
        states = [state[h] for h in range(H)]
        res = []
        for h in range(H):
            kk, vv = slice(h * DK, (h + 1) * DK), slice(h * DV, (h + 1) * DV)
            res.append(_gla_head(q_ref[:, kk], k_ref[:, kk], v_ref[:, vv], r_ref[:, vv], glr, states[h],
                                 a2_ref[:, kk], ab_ref[:, kk], ng_ref[:, vv]))
        for h in range(H):
            st_ref[0, h] = states[h]
            o_ref[:, h * DV:(h + 1) * DV] = res[h][0].astype(bf16)
            state[h] = res[h][1]

    full = lambda shape: pl.BlockSpec(shape, lambda i: (0,) * len(shape))
    return _host_call(
        body, shards, (proj, proj, proj, proj, proj, a2p, ab, ng), name=name, grid=(n,),
        out_shape=[jax.ShapeDtypeStruct((L, H * DV), bf16), jax.ShapeDtypeStruct((n, H, DV, DK), f32)],
        in_specs=pspecs + [full((128, 256)), full((1, 256)), full((1, 512))],
        out_specs=[pl.BlockSpec((GLA_CHUNK, 512), lambda i: (i, 0)), pl.BlockSpec((1, H, DV, DK), lambda i: (i, 0, 0, 0))],
        scratch_shapes=[pltpu.VMEM((H, DV, DK), f32)],
        compiler_params=_params(("arbitrary",)), gather=True)


def _gla_bwd(proj, a2p, ab, ng, states, dout, base128, name, sends=None):
    L = proj.shape[0]
    n, ch, pspecs = _gla_specs(L, base128, True)
    H, DK, DV = GLA_HEADS, GLA_DK, GLA_DV

    def body(q_ref, k_ref, v_ref, r_ref, l_ref, a2_ref, ab_ref, ng_ref, st_ref, do_ref,
             dq_ref, dk_ref, dv_ref, dr_ref, dl_ref, da2_ref, dab_ref, dng_ref, dstate):
        @pl.when(pl.program_id(0) == 0)
        def _():
            dstate[...] = jnp.zeros_like(dstate)
            da2_ref[...] = jnp.zeros_like(da2_ref)
            dab_ref[...] = jnp.zeros_like(dab_ref)
            dng_ref[...] = jnp.zeros_like(dng_ref)

        glr = l_ref[...]
        dglr = jnp.zeros(glr.shape, f32)
        vjps = []
        for h in range(H):
            kk, vv = slice(h * DK, (h + 1) * DK), slice(h * DV, (h + 1) * DV)
            vjps.append(jax.vjp(_gla_head, q_ref[:, kk], k_ref[:, kk], v_ref[:, vv], r_ref[:, vv], glr, st_ref[0, h],
                                a2_ref[:, kk], ab_ref[:, kk], ng_ref[:, vv])[1])
        cots = [(do_ref[:, h * DV:(h + 1) * DV].astype(f32), dstate[h]) for h in range(H)]
        grads = [vjps[h](cots[h]) for h in range(H)]
        for h in range(H):
            kk, vv = slice(h * DK, (h + 1) * DK), slice(h * DV, (h + 1) * DV)
            dq, dk, dv, dr, dl, dst, da2, dab, dng = grads[h]
            dq_ref[:, kk] = dq.astype(bf16)
            dk_ref[:, kk] = dk.astype(bf16)
            dv_ref[:, vv] = dv.astype(bf16)
            dr_ref[:, vv] = dr.astype(bf16)
            dglr = dglr + dl
            dstate[h] = dst
            da2_ref[:, kk] += da2
            dab_ref[:, kk] += dab
            dng_ref[:, vv] += dng
        dl_ref[...] = dglr.astype(bf16)

    full = lambda shape: pl.BlockSpec(shape, lambda i: (0,) * len(shape))
    rowspec = lambda w: pl.BlockSpec((GLA_CHUNK, w), lambda i: (ch(i), 0))
    return _host_call(
        body, sends, (proj, proj, proj, proj, proj, a2p, ab, ng, states, dout), name=name, grid=(n,),
        out_shape=[jax.ShapeDtypeStruct((L, 256), bf16), jax.ShapeDtypeStruct((L, 256), bf16),
                   jax.ShapeDtypeStruct((L, 512), bf16), jax.ShapeDtypeStruct((L, 512), bf16),
                   jax.ShapeDtypeStruct((L, 128), bf16), jax.ShapeDtypeStruct((128, 256), f32),
                   jax.ShapeDtypeStruct((1, 256), f32), jax.ShapeDtypeStruct((1, 512), f32)],
        in_specs=pspecs + [full((128, 256)), full((1, 256)), full((1, 512)),
                           pl.BlockSpec((1, H, DV, DK), lambda i: (ch(i), 0, 0, 0)), rowspec(512)],
        out_specs=[rowspec(256), rowspec(256), rowspec(512), rowspec(512), rowspec(128),
                   full((128, 256)), full((1, 256)), full((1, 512))],
        scratch_shapes=[pltpu.VMEM((H, DV, DK), f32)],
        compiler_params=_params(("arbitrary",)))


def _s5_prep(lam_re, lam_im, log_dt, b_re, b_im, c_re, c_im, d):
    G, N, Cn = S5_GROUPS, S5_STATE, S5_GROUP
    J, GB = S5_LANE_BLOCKS, S5_GROUPS // S5_LANE_BLOCKS
    dt = jnp.exp(log_dt)[:, None]
    mag = jnp.exp(lam_re * dt)
    ab_re, ab_im = mag * jnp.cos(lam_im * dt), mag * jnp.sin(lam_im * dt)
    den = lam_re * lam_re + lam_im * lam_im
    z_re = ((ab_re - 1.0) * lam_re + ab_im * lam_im) / den
    z_im = (ab_im * lam_re - (ab_re - 1.0) * lam_im) / den
    bb_re = z_re[..., None] * b_re - z_im[..., None] * b_im
    bb_im = z_re[..., None] * b_im + z_im[..., None] * b_re
    eye = jnp.eye(GB, dtype=f32)

    def in_blocks(bb):
        return jnp.einsum("jgnc,gh->jgchn", bb.reshape(J, GB, N, Cn), eye).reshape(J, GB * Cn, GB * N)

    def out_blocks(cc):
        return jnp.einsum("jgcn,gh->jgnhc", cc.reshape(J, GB, Cn, N), eye).reshape(J, GB * N, GB * Cn)

    return (ab_re.reshape(1, G * N), ab_im.reshape(1, G * N), in_blocks(bb_re), in_blocks(bb_im),
            out_blocks(c_re), out_blocks(c_im), d.reshape(1, G * Cn))


def _s5_chunk(u, hin_r, hin_i, a_r, a_i, bb_r, bb_i, cc_r, cc_i, dvec):
    T = u.shape[0]
    hr = jnp.dot(u, bb_r, preferred_element_type=f32)
    hi = jnp.dot(u, bb_i, preferred_element_type=f32)
    row = lax.broadcasted_iota(jnp.int32, hr.shape, 0)
    hr = hr + jnp.where(row == 0, a_r * hin_r - a_i * hin_i, 0.0)
    hi = hi + jnp.where(row == 0, a_r * hin_i + a_i * hin_r, 0.0)
    pr, pi = a_r, a_i
    d = 1
    while d < T:
        sr = jnp.where(row >= d, _sroll(hr, d), 0.0)
        si = jnp.where(row >= d, _sroll(hi, d), 0.0)
        hr, hi = hr + pr * sr - pi * si, hi + pr * si + pi * sr
        pr, pi = pr * pr - pi * pi, 2.0 * pr * pi
        d *= 2
    y = (jnp.dot(hr, cc_r, preferred_element_type=f32)
         - jnp.dot(hi, cc_i, preferred_element_type=f32) + dvec * u)
    out_r = jnp.sum(jnp.where(row == T - 1, hr, 0.0), axis=0, keepdims=True)
    out_i = jnp.sum(jnp.where(row == T - 1, hi, 0.0), axis=0, keepdims=True)
    return y, out_r, out_i


def _s5_specs(L, base128, rev):
    T, J = S5_CHUNK, S5_LANE_BLOCKS
    n = L // T
    ch = (lambda c: n - 1 - c) if rev else (lambda c: c)
    ub = base128 + P_S5U // 128
    specs = [
        pl.BlockSpec((T, 128), lambda j, c: (ch(c), ub + j)),
        pl.BlockSpec((1, 512), lambda j, c: (0, j)), pl.BlockSpec((1, 512), lambda j, c: (0, j)),
        pl.BlockSpec((None, 128, 512), lambda j, c: (j, 0, 0)), pl.BlockSpec((None, 128, 512), lambda j, c: (j, 0, 0)),
        pl.BlockSpec((None, 512, 128), lambda j, c: (j, 0, 0)), pl.BlockSpec((None, 512, 128), lambda j, c: (j, 0, 0)),
        pl.BlockSpec((1, 128), lambda j, c: (0, j)),
    ]
    return n, ch, specs


def _s5_fwd(proj, prep, base128, name, shards=None):
    L = proj.shape[0]
    T, J = S5_CHUNK, S5_LANE_BLOCKS
    n, _, specs = _s5_specs(L, base128, False)

    def body(u_ref, ar, ai, bbr, bbi, ccr, cci, dv, y_ref, sr_ref, si_ref, carry):
        @pl.when(pl.program_id(1) == 0)
        def _():
            carry[...] = jnp.zeros_like(carry)

        hin_r, hin_i = carry[0:1, :], carry[1:2, :]
        sr_ref[0] = jnp.broadcast_to(hin_r, (8, 512))
        si_ref[0] = jnp.broadcast_to(hin_i, (8, 512))
        y, out_r, out_i = _s5_chunk(u_ref[...], hin_r, hin_i, ar[...], ai[...], bbr[...], bbi[...], ccr[...], cci[...], dv[...])
        y_ref[...] = y
        carry[0:1, :] = out_r
        carry[1:2, :] = out_i

    st = pl.BlockSpec((1, 8, 512), lambda j, c: (c, 0, j))
    return _host_call(
        body, shards, (proj, *prep), name=name, grid=(J, n),
        out_shape=[jax.ShapeDtypeStruct((L, 512), f32), jax.ShapeDtypeStruct((n, 8, 2048), f32), jax.ShapeDtypeStruct((n, 8, 2048), f32)],
        in_specs=specs, out_specs=[pl.BlockSpec((T, 128), lambda j, c: (c, j)), st, st],
        scratch_shapes=[pltpu.VMEM((8, 512), f32)],
        compiler_params=_params(("parallel", "arbitrary")), gather=True)


def _s5_bwd(proj, prep, st_r, st_i, dy, base128, name, sends=None):
    L = proj.shape[0]
    T, J = S5_CHUNK, S5_LANE_BLOCKS
    n, ch, specs = _s5_specs(L, base128, True)

    def body(u_ref, ar, ai, bbr, bbi, ccr, cci, dv, sr_ref, si_ref, dy_ref,
             du_ref, dar, dai, dbbr, dbbi, dccr, dcci, ddv, dcarry):
        @pl.when(pl.program_id(1) == 0)
        def _():
            dcarry[...] = jnp.zeros_like(dcarry)
            for r in (dar, dai, dbbr, dbbi, dccr, dcci, ddv):
                r[...] = jnp.zeros_like(r)

        _, vjp = jax.vjp(_s5_chunk, u_ref[...], sr_ref[0, 0:1, :], si_ref[0, 0:1, :], ar[...], ai[...],
                         bbr[...], bbi[...], ccr[...], cci[...], dv[...])
        g = vjp((dy_ref[...], dcarry[0:1, :], dcarry[1:2, :]))
        du_ref[...] = g[0].astype(bf16)
        dcarry[0:1, :] = g[1]
        dcarry[1:2, :] = g[2]
        for r, val in zip((dar, dai, dbbr, dbbi, dccr, dcci, ddv), g[3:]):
            r[...] += val

    st = pl.BlockSpec((1, 8, 512), lambda j, c: (ch(c), 0, j))
    outs, recvs = _host_call(
        body, sends, (proj, *prep, st_r, st_i, dy), name=name, grid=(J, n),
        out_shape=[jax.ShapeDtypeStruct((L, 512), bf16),
                   jax.ShapeDtypeStruct((1, 2048), f32), jax.ShapeDtypeStruct((1, 2048), f32),
                   jax.ShapeDtypeStruct((J, 128, 512), f32), jax.ShapeDtypeStruct((J, 128, 512), f32),
                   jax.ShapeDtypeStruct((J, 512, 128), f32), jax.ShapeDtypeStruct((J, 512, 128), f32),
                   jax.ShapeDtypeStruct((1, 512), f32)],
        in_specs=specs + [st, st, pl.BlockSpec((T, 128), lambda j, c: (ch(c), j))],
        out_specs=[pl.BlockSpec((T, 128), lambda j, c: (ch(c), j))] + specs[1:],
        scratch_shapes=[pltpu.VMEM((8, 512), f32)],
        compiler_params=_params(("parallel", "arbitrary")))
    return outs[0], tuple(outs[1:]), recvs


def _glu_f(y, w, b):
    z = jax.nn.gelu(y)
    return z * jax.nn.sigmoid(jnp.dot(z.astype(bf16), w.astype(bf16), preferred_element_type=f32) + b)


def _glu_fwd(y, w, b, name):
    L = y.shape[0]
    tm = _tile(L, 512, 8)

    def body(y_ref, w_ref, b_ref, o_ref):
        o_ref[...] = _glu_f(y_ref[...], w_ref[...], b_ref[...]).astype(bf16)

    row = pl.BlockSpec((tm, 512), lambda i: (i, 0))
    return pl.pallas_call(
        body, name=name, grid=(L // tm,), out_shape=jax.ShapeDtypeStruct((L, 512), bf16),
        in_specs=[row, pl.BlockSpec((512, 512), lambda i: (0, 0)), pl.BlockSpec((1, 512), lambda i: (0, 0))],
        out_specs=row, compiler_params=_params(("parallel",)),
    )(y, w, b)


def _glu_bwd(y, w, b, dout, name):
    L = y.shape[0]
    tm = _tile(L, 512, 8)

    def body(y_ref, w_ref, b_ref, do_ref, dy_ref, dw_ref, db_ref):
        @pl.when(pl.program_id(0) == 0)
        def _():
            dw_ref[...] = jnp.zeros_like(dw_ref)
            db_ref[...] = jnp.zeros_like(db_ref)

        _, vjp = jax.vjp(_glu_f, y_ref[...], w_ref[...], b_ref[...])
        dy, dw, db = vjp(do_ref[...])
        dy_ref[...] = dy
        dw_ref[...] += dw
        db_ref[...] += db

    row = pl.BlockSpec((tm, 512), lambda i: (i, 0))
    wspec, bspec = pl.BlockSpec((512, 512), lambda i: (0, 0)), pl.BlockSpec((1, 512), lambda i: (0, 0))
    return pl.pallas_call(
        body, name=name, grid=(L // tm,),
        out_shape=[jax.ShapeDtypeStruct((L, 512), f32), jax.ShapeDtypeStruct((512, 512), f32), jax.ShapeDtypeStruct((1, 512), f32)],
        in_specs=[row, wspec, bspec, row], out_specs=[row, wspec, bspec],
        compiler_params=_params(("arbitrary",)),
    )(y, w, b, dout)


def _rope_tables(positions):
    half = ROPE_DIM // 2
    inv_freq = ROPE_THETA ** (-jnp.arange(half, dtype=f32) / half)
    ang = positions.astype(f32)[:, None] * inv_freq
    L = positions.shape[0]
    cos = jnp.concatenate([jnp.cos(ang), jnp.cos(ang), jnp.ones((L, HEAD_DIM - ROPE_DIM), f32)], axis=1)
    sin = jnp.concatenate([jnp.sin(ang), jnp.sin(ang), jnp.zeros((L, HEAD_DIM - ROPE_DIM), f32)], axis=1)
    return jnp.tile(cos, (1, 8)), jnp.tile(sin, (1, 8))


def _rope_matrix(w):
    half = ROPE_DIM // 2
    r = lax.broadcasted_iota(jnp.int32, (w, w), 0)
    c = lax.broadcasted_iota(jnp.int32, (w, w), 1)
    same = (r // HEAD_DIM) == (c // HEAD_DIM)
    rr, cc = r % HEAD_DIM, c % HEAD_DIM
    return (jnp.where(same & (cc < half) & (rr == cc + half), -1.0, 0.0)
            + jnp.where(same & (cc >= half) & (cc < ROPE_DIM) & (rr == cc - half), 1.0, 0.0))


def _rope(items, cos, sin, *, transpose, out_dtype, name):
    L = cos.shape[0]
    tm = _tile(L, 512, 8)
    n = len(items)

    def body(*refs):
        xs, c_ref, s_ref, outs = refs[:n], refs[n], refs[n + 1], refs[n + 2:]
        for x_ref, o_ref in zip(xs, outs):
            w = x_ref.shape[1]
            x = x_ref[...].astype(f32)
            c, s = c_ref[:, :w], s_ref[:, :w]
            rot = _rope_matrix(w)
            if transpose:
                y = x * c + lax.dot_general(x * s, rot, (((1,), (1,)), ((), ())), preferred_element_type=f32)
            else:
                y = x * c + jnp.dot(x, rot, preferred_element_type=f32) * s
            o_ref[...] = y.astype(o_ref.dtype)

    in_specs = [pl.BlockSpec((tm, w), functools.partial(lambda i, col: (i, col), col=col)) for _, col, w in items]
    tab = pl.BlockSpec((tm, 512), lambda i: (i, 0))
    outs = pl.pallas_call(
        body, name=name, grid=(L // tm,),
        out_shape=[jax.ShapeDtypeStruct((L, w), out_dtype) for _, _, w in items],
        in_specs=in_specs + [tab, tab], out_specs=[pl.BlockSpec((tm, w), lambda i: (i, 0)) for _, _, w in items],
        compiler_params=_params(("parallel",)),
    )(*[a for a, _, _ in items], cos, sin)
    return list(outs)


def _attn_head(q, kp, kc, vp, vc, sink, *, lim, max_dist):
    T = ATT_BLOCK
    k2 = jnp.concatenate([kp, kc], axis=0)
    v2 = jnp.concatenate([vp, vc], axis=0)
    s = lax.dot_general(q, k2, (((1,), (1,)), ((), ())), preferred_element_type=f32) * (HEAD_DIM ** -0.5)
    t = lax.broadcasted_iota(jnp.int32, (T, 2 * T), 0)
    j = lax.broadcasted_iota(jnp.int32, (T, 2 * T), 1)
    dist = T + t - j
    valid = (dist >= 0) & (dist <= max_dist) & (j >= lim)
    s = jnp.where(valid, s, -jnp.inf)
    m = lax.stop_gradient(jnp.max(s, axis=-1, keepdims=True))
    p = jnp.exp(s - m)
    den = jnp.sum(p, axis=-1, keepdims=True)
    o = jnp.dot(p, v2, preferred_element_type=f32) / den
    lse = jnp.broadcast_to(m + jnp.log(den), (T, HEAD_DIM))
    if sink is None:
        return o, lse
    return o * jax.nn.sigmoid(lse - sink)


def _attn_specs(L, q_col, k_col, v_col, wk, rev):
    T = ATT_BLOCK
    n = L // T
    blk = (lambda i: n - 1 - i) if rev else (lambda i: i)
    prev = lambda i: jnp.maximum(blk(i) - 1, 0)
    specs = [
        pl.BlockSpec((T, 512), lambda i: (blk(i), q_col)),
        pl.BlockSpec((T, wk), lambda i: (prev(i), k_col)), pl.BlockSpec((T, wk), lambda i: (blk(i), k_col)),
        pl.BlockSpec((T, wk), lambda i: (prev(i), v_col)), pl.BlockSpec((T, wk), lambda i: (blk(i), v_col)),
    ]
    return n, blk, specs


def _attn_fwd(qa, ka, va, sinks, *, q_col, k_col, v_col, hkv, nbc, max_dist, name, shards=None):
    L = qa.shape[0]
    T, HQ, HD = ATT_BLOCK, 8, HEAD_DIM
    wk = hkv * HD
    n, _, specs = _attn_specs(L, q_col, k_col, v_col, wk, False)
    grp = HQ // hkv
    gated = sinks is not None

    def body(*refs):
        q_ref, kp_ref, kc_ref, vp_ref, vc_ref = refs[:5]
        rest = refs[5:]
        lim = jnp.where(pl.program_id(0) % nbc == 0, T, 0)
        results = []
        for h in range(HQ):
            hs, ks = slice(h * HD, (h + 1) * HD), slice((h // grp) * HD, (h // grp + 1) * HD)
            results.append(_attn_head(q_ref[:, hs], kp_ref[:, ks], kc_ref[:, ks], vp_ref[:, ks], vc_ref[:, ks],
                                      rest[0][:, hs] if gated else None, lim=lim, max_dist=max_dist))
        for h, res in enumerate(results):
            hs = slice(h * HD, (h + 1) * HD)
            if gated:
                rest[1][:, hs] = res.astype(bf16)
            else:
                rest[0][:, hs] = res[0]
                rest[1][:, hs] = res[1]

    row = pl.BlockSpec((T, 512), lambda i: (i, 0))
    if gated:
        return _host_call(
            body, shards, (qa, ka, ka, va, va, sinks), name=name, grid=(n,),
            out_shape=jax.ShapeDtypeStruct((L, 512), bf16),
            in_specs=specs + [pl.BlockSpec((1, 512), lambda i: (0, 0))], out_specs=row, scratch_shapes=[],
            compiler_params=_params(("parallel",)), gather=True)
    return _host_call(
        body, shards, (qa, ka, ka, va, va), name=name, grid=(n,),
        out_shape=[jax.ShapeDtypeStruct((L, 512), f32)] * 2,
        in_specs=specs, out_specs=[row, row], scratch_shapes=[], compiler_params=_params(("parallel",)), gather=True)


def _attn_bwd(qa, ka, va, sinks, douts, *, q_col, k_col, v_col, hkv, nbc, max_dist, name, sends=None):
    L = qa.shape[0]
    T, HQ, HD = ATT_BLOCK, 8, HEAD_DIM
    wk = hkv * HD
    n, blk, specs = _attn_specs(L, q_col, k_col, v_col, wk, True)
    grp = HQ // hkv
    gated = sinks is not None
    nd = len(douts)

    def body(*refs):
        q_ref, kp_ref, kc_ref, vp_ref, vc_ref = refs[:5]
        pos = 5
        sink_ref = None
        if gated:
            sink_ref = refs[pos]
            pos += 1
        d_refs = refs[pos:pos + nd]
        pos += nd
        dq_ref, dk_ref, dv_ref = refs[pos:pos + 3]
        pos += 3
        dsink_ref = None
        if gated:
            dsink_ref = refs[pos]
            pos += 1
        carry_k, carry_v = refs[pos:pos + 2]

        @pl.when(pl.program_id(0) == 0)
        def _():
            carry_k[...] = jnp.zeros_like(carry_k)
            carry_v[...] = jnp.zeros_like(carry_v)
            if gated:
                dsink_ref[...] = jnp.zeros_like(dsink_ref)

        lim = jnp.where(blk(pl.program_id(0)) % nbc == 0, T, 0)
        dkp = [jnp.zeros((T, HD), f32) for _ in range(hkv)]
        dkc = [jnp.zeros((T, HD), f32) for _ in range(hkv)]
        dvp = [jnp.zeros((T, HD), f32) for _ in range(hkv)]
        dvc = [jnp.zeros((T, HD), f32) for _ in range(hkv)]
        dqs, dsinks = [], []
        for h in range(HQ):
            g = h // grp
            hs, ks = slice(h * HD, (h + 1) * HD), slice(g * HD, (g + 1) * HD)
            fn = functools.partial(_attn_head, lim=lim, max_dist=max_dist)
            prim = (q_ref[:, hs], kp_ref[:, ks], kc_ref[:, ks], vp_ref[:, ks], vc_ref[:, ks])
            if gated:
                _, vjp = jax.vjp(fn, *prim, sink_ref[:, hs])
                dq, a, b, c, d, ds = vjp(d_refs[0][:, hs].astype(f32))
                dsinks.append(ds)
            else:
                _, vjp = jax.vjp(lambda *p: fn(*p, None), *prim)
                dq, a, b, c, d = vjp((d_refs[0][:, hs], d_refs[1][:, hs]))
            dqs.append(dq)
            dkp[g], dkc[g], dvp[g], dvc[g] = dkp[g] + a, dkc[g] + b, dvp[g] + c, dvc[g] + d
        for h in range(HQ):
            hs = slice(h * HD, (h + 1) * HD)
            dq_ref[:, hs] = dqs[h].astype(bf16)
            if gated:
                dsink_ref[:, hs] += dsinks[h]
        for g in range(hkv):
            ks = slice(g * HD, (g + 1) * HD)
            dk_ref[:, ks] = (dkc[g] + carry_k[:, ks]).astype(bf16)
            dv_ref[:, ks] = (dvc[g] + carry_v[:, ks]).astype(bf16)
            carry_k[:, ks] = dkp[g]
            carry_v[:, ks] = dvp[g]

    row = lambda w: pl.BlockSpec((T, w), lambda i: (blk(i), 0))
    vec = pl.BlockSpec((1, 512), lambda i: (0, 0))
    in_specs = specs + ([vec] if gated else []) + [row(512)] * nd
    out_shape = [jax.ShapeDtypeStruct((L, 512), bf16), jax.ShapeDtypeStruct((L, wk), bf16), jax.ShapeDtypeStruct((L, wk), bf16)]
    out_specs = [row(512), row(wk), row(wk)]
    if gated:
        out_shape.append(jax.ShapeDtypeStruct((1, 512), f32))
        out_specs.append(vec)
    args = (qa, ka, ka, va, va) + ((sinks,) if gated else ()) + tuple(douts)
    outs, recvs = _host_call(
        body, sends, args, name=name, grid=(n,), out_shape=out_shape, in_specs=in_specs, out_specs=out_specs,
        scratch_shapes=[pltpu.VMEM((T, wk), f32), pltpu.VMEM((T, wk), f32)],
        compiler_params=_params(("arbitrary",)))
    return outs if sends is None else (outs, recvs)


def _dilmix_f(o0, o1, o2, l0, l1, l2):
    m = jnp.maximum(jnp.maximum(l0, l1), l2)
    e0, e1, e2 = jnp.exp(l0 - m), jnp.exp(l1 - m), jnp.exp(l2 - m)
    return (e0 * o0 + e1 * o1 + e2 * o2) / (e0 + e1 + e2)


def _dilmix_fwd(os_, ls, name):
    L = os_[0].shape[0]
    tm = _tile(L, 512, 8)

    def body(o0, o1, o2, l0, l1, l2, out):
        out[...] = _dilmix_f(o0[...], o1[...], o2[...], l0[...], l1[...], l2[...]).astype(bf16)

    row = pl.BlockSpec((tm, 512), lambda i: (i, 0))
    return pl.pallas_call(
        body, name=name, grid=(L // tm,), out_shape=jax.ShapeDtypeStruct((L, 512), bf16),
        in_specs=[row] * 6, out_specs=row, compiler_params=_params(("parallel",)),
    )(*os_, *ls)


def _dilmix_bwd(os_, ls, dout, name):
    L = os_[0].shape[0]
    tm = _tile(L, 512, 8)

    def body(o0, o1, o2, l0, l1, l2, d, *outs):
        _, vjp = jax.vjp(_dilmix_f, o0[...], o1[...], o2[...], l0[...], l1[...], l2[...])
        for r, val in zip(outs, vjp(d[...].astype(f32))):
            r[...] = val

    row = pl.BlockSpec((tm, 512), lambda i: (i, 0))
    outs = pl.pallas_call(
        body, name=name, grid=(L // tm,), out_shape=[jax.ShapeDtypeStruct((L, 512), f32)] * 6,
        in_specs=[row] * 7, out_specs=[row] * 6, compiler_params=_params(("parallel",)),
    )(*os_, *ls, dout)
    return outs[:3], outs[3:]


def _to_strided(z, dil):
    L, w = z.shape
    return z.reshape(L // dil, dil, w).transpose(1, 0, 2).reshape(L, w)


def _from_strided(z, dil):
    L, w = z.shape
    return z.reshape(dil, L // dil, w).transpose(1, 0, 2).reshape(L, w)


def _adamw_math(w, g, m, v):
    m = ADAM_B1 * m + (1.0 - ADAM_B1) * g
    v = ADAM_B2 * v + (1.0 - ADAM_B2) * (g * g)
    m_hat = m / (1.0 - ADAM_B1 ** ADAM_STEP)
    v_hat = v / (1.0 - ADAM_B2 ** ADAM_STEP)
    delta = -ADAM_LR * (m_hat / (jnp.sqrt(v_hat) + ADAM_EPS) + ADAM_WD * w)
    return delta, m, v


def _adamw(w, m, v, slots, name):
    depth, R, C = w.shape
    tr = _tile(R, max(8, 131072 // C), 8)
    outs = None
    for l in range(depth):
        def body(w_ref, m_ref, v_ref, s_ref, *rest):
            g_ref, d_ref, nm_ref, nv_ref = rest[-4:]
            g = s_ref[0].astype(f32)
            for i in range(1, N_DEV):
                g = g + s_ref[i].astype(f32)
            delta, nm, nv = _adamw_math(w_ref[0], g, m_ref[0], v_ref[0])
            g_ref[0], d_ref[0], nm_ref[0], nv_ref[0] = g, delta, nm, nv

        blk = pl.BlockSpec((1, tr, C), functools.partial(lambda i, l: (l, i, 0), l=l))
        carried = [] if outs is None else list(outs)
        outs = pl.pallas_call(
            body, name=f"{name}_{l}", grid=(R // tr,), out_shape=[jax.ShapeDtypeStruct(w.shape, f32)] * 4,
            in_specs=[blk, blk, blk, pl.BlockSpec((N_DEV, tr, C), lambda i: (0, i, 0))]
            + [pl.BlockSpec(memory_space=pl.ANY)] * len(carried),
            out_specs=[blk] * 4, input_output_aliases={4 + j: j for j in range(len(carried))},
            compiler_params=_params(("parallel",)),
        )(w, m, v, slots[l], *carried)
    return outs


def _adamw_packed(w, m, v, slots, name):
    R = w.shape[0]
    tr = _tile(R, 512, 8)

    def body(w_ref, m_ref, v_ref, s_ref, g_ref, d_ref, nm_ref, nv_ref):
        g = s_ref[0]
        for i in range(1, N_DEV):
            g = g + s_ref[i]
        delta, nm, nv = _adamw_math(w_ref[...], g, m_ref[...], v_ref[...])
        g_ref[...], d_ref[...], nm_ref[...], nv_ref[...] = g, delta, nm, nv

    blk = pl.BlockSpec((tr, 128), lambda i: (i, 0))
    return pl.pallas_call(
        body, name=name, grid=(R // tr,), out_shape=[jax.ShapeDtypeStruct(w.shape, f32)] * 4,
        in_specs=[blk, blk, blk, pl.BlockSpec((N_DEV, tr, 128), lambda i: (0, i, 0))], out_specs=[blk] * 4,
        compiler_params=_params(("parallel",)),
    )(w, m, v, slots)


def _cols_gathered(g):
    return jnp.concatenate([g[d] for d in range(N_DEV)], axis=-1)


def _cols_scatter(full):
    c = full.shape[-1] // N_DEV
    return jnp.stack([full[..., d * c:(d + 1) * c] for d in range(N_DEV)])


def _win_segments(D):
    b = 4 * D
    return (((O_GATES, O_GATES + b), 0), ((0, O_GLR), b), ((O_S5U, O_GATES), b + O_GLR), ((O_GLR, O_S5U), b + P_GLR))


def _win_from_shards(g):
    rows, c = g.shape[1], g.shape[2]
    D = (N_DEV * c - O_GATES) // 4
    pieces = []
    for (lo, hi), _ in sorted(_win_segments(D), key=lambda t: t[1]):
        for d in range(N_DEV):
            a, b = max(lo, d * c), min(hi, (d + 1) * c)
            if a < b:
                pieces.append(g[d][:, a - d * c:b - d * c])
    pieces.append(jnp.zeros((rows, 128 - GLA_LOWRANK), g.dtype))
    return jnp.concatenate(pieces, axis=1)


def _win_to_shards(wp, D):
    c = (O_GATES + 4 * D) // N_DEV
    segs = sorted(_win_segments(D), key=lambda t: t[0][0])
    out = []
    for d in range(N_DEV):
        pieces = []
        for (lo, hi), off in segs:
            a, b = max(lo, d * c), min(hi, (d + 1) * c)
            if a < b:
                pieces.append(wp[:, off + a - lo:off + b - lo])
        out.append(jnp.concatenate(pieces, axis=1))
    return jnp.stack(out)


SMALL = ("norm1_g", "gla_a_b", "gla_norm_g", "s5_lambda_re", "s5_lambda_im", "s5_log_dt", "s5_b_re", "s5_b_im",
         "s5_c_re", "s5_c_im", "s5_d", "s5_glu_b", "swa_sinks", "norm2_g", "final_norm_g")
SHARDED = ("w_in", "gla_a2", "s5_glu_w", "w_branch", "w_out", "w_ffn_gate", "w_ffn_up", "w_ffn_down")
WEIGHTS = ("norm1_g", "w_in", "gla_a2", "gla_a_b", "gla_norm_g", "s5_lambda_re", "s5_lambda_im", "s5_log_dt", "s5_b_re",
           "s5_b_im", "s5_c_re", "s5_c_im", "s5_d", "s5_glu_w", "s5_glu_b", "swa_sinks", "w_branch", "w_out", "norm2_g",
           "w_ffn_gate", "w_ffn_up", "w_ffn_down", "final_norm_g")


def _pack(arrs):
    flat = jnp.concatenate([a.reshape(-1) for a in arrs])
    n = flat.shape[0]
    rows = -(-n // 1024) * 8
    return jnp.pad(flat, (0, rows * 128 - n)).reshape(rows, 128)


def _unpack(packed, like):
    flat = packed.reshape(-1)
    out, pos = [], 0
    for a in like:
        out.append(flat[pos:pos + a.size].reshape(a.shape))
        pos += a.size
    return out


def kernel(x, positions, norm1_g, w_in, gla_a2, gla_a_b, gla_norm_g, s5_lambda_re, s5_lambda_im, s5_log_dt, s5_b_re, s5_b_im, s5_c_re, s5_c_im, s5_d, s5_glu_w, s5_glu_b, swa_sinks, w_branch, w_out, norm2_g, w_ffn_gate, w_ffn_up, w_ffn_down, final_norm_g, loss_target, m_norm1_g, m_w_in, m_gla_a2, m_gla_a_b, m_gla_norm_g, m_s5_lambda_re, m_s5_lambda_im, m_s5_log_dt, m_s5_b_re, m_s5_b_im, m_s5_c_re, m_s5_c_im, m_s5_d, m_s5_glu_w, m_s5_glu_b, m_swa_sinks, m_w_branch, m_w_out, m_norm2_g, m_w_ffn_gate, m_w_ffn_up, m_w_ffn_down, m_final_norm_g, v_norm1_g, v_w_in, v_gla_a2, v_gla_a_b, v_gla_norm_g, v_s5_lambda_re, v_s5_lambda_im, v_s5_log_dt, v_s5_b_re, v_s5_b_im, v_s5_c_re, v_s5_c_im, v_s5_d, v_s5_glu_w, v_s5_glu_b, v_swa_sinks, v_w_branch, v_w_out, v_norm2_g, v_w_ffn_gate, v_w_ffn_up, v_w_ffn_down, v_final_norm_g):
    W = dict(norm1_g=norm1_g, w_in=w_in, gla_a2=gla_a2, gla_a_b=gla_a_b, gla_norm_g=gla_norm_g, s5_lambda_re=s5_lambda_re, s5_lambda_im=s5_lambda_im, s5_log_dt=s5_log_dt, s5_b_re=s5_b_re, s5_b_im=s5_b_im, s5_c_re=s5_c_re, s5_c_im=s5_c_im, s5_d=s5_d, s5_glu_w=s5_glu_w, s5_glu_b=s5_glu_b, swa_sinks=swa_sinks, w_branch=w_branch, w_out=w_out, norm2_g=norm2_g, w_ffn_gate=w_ffn_gate, w_ffn_up=w_ffn_up, w_ffn_down=w_ffn_down, final_norm_g=final_norm_g)
    Mo = dict(norm1_g=m_norm1_g, w_in=m_w_in, gla_a2=m_gla_a2, gla_a_b=m_gla_a_b, gla_norm_g=m_gla_norm_g, s5_lambda_re=m_s5_lambda_re, s5_lambda_im=m_s5_lambda_im, s5_log_dt=m_s5_log_dt, s5_b_re=m_s5_b_re, s5_b_im=m_s5_b_im, s5_c_re=m_s5_c_re, s5_c_im=m_s5_c_im, s5_d=m_s5_d, s5_glu_w=m_s5_glu_w, s5_glu_b=m_s5_glu_b, swa_sinks=m_swa_sinks, w_branch=m_w_branch, w_out=m_w_out, norm2_g=m_norm2_g, w_ffn_gate=m_w_ffn_gate, w_ffn_up=m_w_ffn_up, w_ffn_down=m_w_ffn_down, final_norm_g=m_final_norm_g)
    Vo = dict(norm1_g=v_norm1_g, w_in=v_w_in, gla_a2=v_gla_a2, gla_a_b=v_gla_a_b, gla_norm_g=v_gla_norm_g, s5_lambda_re=v_s5_lambda_re, s5_lambda_im=v_s5_lambda_im, s5_log_dt=v_s5_log_dt, s5_b_re=v_s5_b_re, s5_b_im=v_s5_b_im, s5_c_re=v_s5_c_re, s5_c_im=v_s5_c_im, s5_d=v_s5_d, s5_glu_w=v_s5_glu_w, s5_glu_b=v_s5_glu_b, swa_sinks=v_swa_sinks, w_branch=v_w_branch, w_out=v_w_out, norm2_g=v_norm2_g, w_ffn_gate=v_w_ffn_gate, w_ffn_up=v_w_ffn_up, w_ffn_down=v_w_ffn_down, final_norm_g=v_final_norm_g)

    L, D = x.shape[1], x.shape[2]
    depth = norm1_g.shape[0]
    xs = x.reshape(L, D)
    target = loss_target.reshape(L, D)
    base128 = 4 * D // 128

    in_group = ("w_in", "gla_a2", "s5_glu_w")
    full = {}

    def shards(keys, l):
        if l >= depth:
            return None
        return [W[k][l] if k in ("gla_a2", "s5_glu_w") else W[k][l].astype(bf16) for k in keys]

    def landed(keys, l, gathered):
        for k, g in zip(keys, gathered):
            if k == "w_in":
                full[k, l] = _win_from_shards(g)
            elif k == "gla_a2":
                full[k, l] = jnp.pad(_cols_gathered(g), ((0, 128 - GLA_LOWRANK), (0, 0)))
            elif k in ("w_branch", "w_ffn_gate", "w_ffn_up"):
                full[k, l] = _cols_gathered(g)
            else:
                full[k, l] = g.reshape((-1, g.shape[-1]))

    landed(in_group, 0, _all_gather(shards(in_group, 0), "gather_w_in0"))

    cos, sin = _rope_tables(positions.reshape(L))

    saved = []
    cur = xs
    for l in range(depth):
        s = {"x": cur}
        nxt = l + 1
        h1 = _rms_fwd(cur, norm1_g[l][None], f"rms1_fwd{l}")
        if l == 0:
            keys = ("w_branch", "w_out", "w_ffn_gate")
            proj, got = _matmul(h1, full["w_in", l], sends=shards(keys, 0), gather=True, name=f"proj_in{l}")
            landed(keys, 0, got)
        else:
            proj = _matmul(h1, full["w_in", l], name=f"proj_in{l}")
        s["h1"], s["proj"] = h1, proj
        ab, ng = gla_a_b[l][None], gla_norm_g[l].reshape(1, 512)
        (o_gla, s["gla_st"]), got = _gla_fwd(proj, full["gla_a2", l], ab, ng, base128, f"gla_fwd{l}", shards=shards(in_group, nxt))
        landed(in_group, nxt, got)
        prep, s["prep_vjp"] = jax.vjp(_s5_prep, s5_lambda_re[l], s5_lambda_im[l], s5_log_dt[l], s5_b_re[l], s5_b_im[l],
                                      s5_c_re[l], s5_c_im[l], s5_d[l])
        s["prep"] = prep
        (y_s5, s["s5_r"], s["s5_i"]), got = _s5_fwd(proj, prep, base128, f"s5_fwd{l}", shards=shards(("w_branch", "w_out"), nxt))
        landed(("w_branch", "w_out"), nxt, got)
        s["y_s5"] = y_s5
        o_s5 = _glu_fwd(y_s5, full["s5_glu_w", l], s5_glu_b[l][None], f"glu_fwd{l}")
        sinks_b = jnp.repeat(swa_sinks[l], HEAD_DIM)[None]
        s["sinks_b"] = sinks_b
        nb = L // ATT_BLOCK
        cq, ck, cv = (base128 + P_CQ // 128) // 4, (base128 + P_CK // 128) // 4, (base128 + P_CV // 128) // 4
        sq_col, sk_col, sv_col = (base128 + P_SQ // 128) // 4, base128 + P_SK // 128, base128 + P_SV // 128
        cq_r, ck_r, sq_r, sk_r = _rope([(proj, cq, 512), (proj, ck, 512), (proj, sq_col, 512), (proj, sk_col, 128)], cos, sin,
                                       transpose=False, out_dtype=f32, name=f"rope_fwd{l}")
        s["rot"] = (cq_r, ck_r, sq_r, sk_r)
        o_swa, got = _attn_fwd(sq_r, sk_r, proj, sinks_b, q_col=0, k_col=0, v_col=sv_col, hkv=SWA_KV_HEADS, nbc=nb,
                               max_dist=SWA_WINDOW - 1, name=f"swa_fwd{l}", shards=shards(("w_ffn_up",), 0) if l == 0 else None)
        landed(("w_ffn_up",), 0, got)
        dil_o, dil_l, s["dil_in"] = [], [], []
        riders = ((("w_ffn_down",), 0 if l == 0 else depth), (("w_ffn_gate",), nxt), (("w_ffn_up",), nxt))
        for (window, dil), (keys, kl) in zip(DIL_CONFIGS, riders):
            if dil == 1:
                (o, lse), got = _attn_fwd(cq_r, ck_r, proj, None, q_col=0, k_col=0, v_col=cv, hkv=8, nbc=nb,
                                          max_dist=window // dil, name=f"dil{dil}_fwd{l}", shards=shards(keys, kl))
                s["dil_in"].append(None)
            else:
                qs_, ks_ = _to_strided(cq_r, dil), _to_strided(ck_r, dil)
                vs_ = _to_strided(proj[:, 4 * D + P_CV:4 * D + P_CV + 512], dil)
                (o, lse), got = _attn_fwd(qs_, ks_, vs_, None, q_col=0, k_col=0, v_col=0, hkv=8,
                                          nbc=nb // dil, max_dist=window // dil, name=f"dil{dil}_fwd{l}", shards=shards(keys, kl))
                o, lse = _from_strided(o, dil), _from_strided(lse, dil)
                s["dil_in"].append((qs_, ks_, vs_))
            landed(keys, kl, got)
            dil_o.append(o)
            dil_l.append(lse)
        s["dil_o"], s["dil_l"] = dil_o, dil_l
        o_dil = _dilmix_fwd(dil_o, dil_l, f"dilmix_fwd{l}")
        branches = (o_gla, o_s5, o_dil, o_swa)
        s["branches"] = branches
        ys = [_matmul(br, full["w_branch", l][m], out_dtype=bf16, name=f"branch{m}_fwd{l}") for m, br in enumerate(branches)]
        s["ys"] = ys
        mixed = _merge_fwd(proj, ys, D, f"merge_fwd{l}")
        s["mixed"] = mixed
        x2 = _matmul(mixed, full["w_out", l], res=cur, name=f"out_fwd{l}")
        s["x2"] = x2
        h2 = _rms_fwd(x2, norm2_g[l][None], f"rms2_fwd{l}")
        rider = shards(("w_ffn_down",), nxt)
        if rider:
            a, got = _matmul(h2, full["w_ffn_gate", l], out_dtype=bf16, sends=rider, gather=True, name=f"ffn_gate_fwd{l}")
            landed(("w_ffn_down",), nxt, got)
        else:
            a = _matmul(h2, full["w_ffn_gate", l], out_dtype=bf16, name=f"ffn_gate_fwd{l}")
        b, act = _matmul(h2, full["w_ffn_up", l], extras=(a,), epilogue=_swiglu_epilogue, out_dtype=(bf16, bf16),
                         name=f"ffn_up_fwd{l}")
        s["h2"], s["a"], s["b"], s["act"] = h2, a, b, act
        cur = _matmul(act, full["w_ffn_down", l], res=x2, name=f"ffn_down_fwd{l}")
        saved.append(s)
    win_p, a2p, glu_w, wb, wout, wg, wu, wd = (
        [full[k, l] for l in range(depth)]
        for k in ("w_in", "gla_a2", "s5_glu_w", "w_branch", "w_out", "w_ffn_gate", "w_ffn_up", "w_ffn_down"))

    loss_part, dcur, dcur_b, dgf = _final_loss(cur, final_norm_g[None], target, "final_loss")
    loss = lax.psum(loss_part, AXES)

    small_g = {k: [None] * depth for k in SMALL if k != "final_norm_g"}
    recv = {k: [None] * depth for k in SHARDED}
    in_group = ("w_in", "gla_a2", "s5_glu_w")
    pending = None
    for l in reversed(range(depth)):
        s = saved[l]
        proj = s["proj"]
        da, db = _matmul(dcur_b, wd[l], mode="nt", extras=(s["a"], s["b"]), epilogue=_swiglu_grad_epilogue,
                         out_dtype=(bf16, bf16), name=f"ffn_down_dx{l}")
        g_down = _matmul(s["act"], dcur_b, mode="tn", out_dtype=bf16, name=f"ffn_down_dw{l}")
        dh2 = _matmul(da, wg[l], mode="nt", name=f"ffn_gate_dx{l}")
        dh2 = _matmul(db, wu[l], mode="nt", res=dh2, name=f"ffn_up_dx{l}")
        g_gate = _matmul(s["h2"], da, mode="tn", out_dtype=bf16, name=f"ffn_gate_dw{l}")
        g_up = _matmul(s["h2"], db, mode="tn", out_dtype=bf16, name=f"ffn_up_dw{l}")
        ffn_sends = (("w_ffn_down", g_down.reshape((N_DEV, -1, D))), ("w_ffn_gate", _cols_scatter(g_gate)),
                     ("w_ffn_up", _cols_scatter(g_up)))
        dx2, dx2_b, dg2 = _rms_bwd(s["x2"], norm2_g[l][None], dh2, dcur, f"rms2_bwd{l}")
        small_g["norm2_g"][l] = dg2[0]
        dmixed = _matmul(dx2_b, wout[l], mode="nt", name=f"out_dx{l}")
        g_out = _matmul(s["mixed"], dx2_b, mode="tn", out_dtype=bf16, name=f"out_dw{l}")
        dys, dgates = _merge_bwd(proj, s["ys"], dmixed, D, f"merge_bwd{l}")
        dbr = [_matmul(dys[m], wb[l][m], mode="nt", name=f"branch{m}_dx{l}") for m in range(4)]
        g_branch = jnp.stack([_matmul(s["branches"][m], dys[m], mode="tn", out_dtype=bf16, name=f"branch{m}_dw{l}")
                              for m in range(4)])
        d_gla, d_s5, d_dil, d_swa = dbr
        ab, ng = gla_a_b[l][None], gla_norm_g[l].reshape(1, 512)
        (dgq, dgk, dgv, dgr, dglr, da2, dab, dng), got = _gla_bwd(proj, a2p[l], ab, ng, s["gla_st"], d_gla, base128,
                                                                   f"gla_bwd{l}", sends=pending)
        for k, r in zip(in_group, got):
            recv[k][l + 1] = r
        small_g["gla_a_b"][l] = dab[0]
        small_g["gla_norm_g"][l] = dng.reshape(GLA_HEADS, GLA_DV)
        dy_s5, dglu_w, dglu_b = _glu_bwd(s["y_s5"], glu_w[l], s5_glu_b[l][None], d_s5, f"glu_bwd{l}")
        small_g["s5_glu_b"][l] = dglu_b[0]
        ds5u, dprep, got = _s5_bwd(proj, s["prep"], s["s5_r"], s["s5_i"], dy_s5, base128, f"s5_bwd{l}",
                                   sends=[g_out.reshape((N_DEV, -1, D)), _cols_scatter(g_branch)])
        recv["w_out"][l], recv["w_branch"][l] = got
        draw = s["prep_vjp"](dprep)
        for k, val in zip(("s5_lambda_re", "s5_lambda_im", "s5_log_dt", "s5_b_re", "s5_b_im", "s5_c_re", "s5_c_im", "s5_d"), draw):
            small_g[k][l] = val
        nb = L // ATT_BLOCK
        cq_r, ck_r, sq_r, sk_r = s["rot"]
        dsq, dsk, dsv, dsinks = _attn_bwd(sq_r, sk_r, proj, s["sinks_b"], (d_swa,), q_col=0, k_col=0,
                                          v_col=base128 + P_SV // 128, hkv=SWA_KV_HEADS, nbc=nb, max_dist=SWA_WINDOW - 1,
                                          name=f"swa_bwd{l}")
        small_g["swa_sinks"][l] = dsinks.reshape(SWA_HEADS, HEAD_DIM).sum(axis=1)
        dos, dls = _dilmix_bwd(s["dil_o"], s["dil_l"], d_dil, f"dilmix_bwd{l}")
        cv = (base128 + P_CV // 128) // 4
        dcq = dck = dcv = None
        for i, (window, dil) in enumerate(DIL_CONFIGS):
            key, send = ffn_sends[i]
            if dil == 1:
                g3, got = _attn_bwd(cq_r, ck_r, proj, None, (dos[i], dls[i]), q_col=0, k_col=0, v_col=cv, hkv=8,
                                    nbc=nb, max_dist=window // dil, name=f"dil{dil}_bwd{l}", sends=[send])
            else:
                qs_, ks_, vs_ = s["dil_in"][i]
                g3, got = _attn_bwd(qs_, ks_, vs_, None, (_to_strided(dos[i], dil), _to_strided(dls[i], dil)),
                                    q_col=0, k_col=0, v_col=0, hkv=8, nbc=nb // dil, max_dist=window // dil,
                                    name=f"dil{dil}_bwd{l}", sends=[send])
                g3 = [_from_strided(t, dil) for t in g3]
            recv[key][l] = got[0]
            g3 = [t.astype(f32) for t in g3]
            dcq, dck, dcv = (g3[0], g3[1], g3[2]) if dcq is None else (dcq + g3[0], dck + g3[1], dcv + g3[2])
        dcq, dck, dsq, dsk = _rope([(dcq, 0, 512), (dck, 0, 512), (dsq, 0, 512), (dsk, 0, 128)], cos, sin,
                                   transpose=True, out_dtype=bf16, name=f"rope_bwd{l}")
        dproj = jnp.concatenate([dgates.transpose(1, 0, 2).reshape(L, 4 * D), dgq, dgk, dgv, dgr, ds5u,
                                 dcq, dck, dcv.astype(bf16), dsq, dsk, dsv, dglr], axis=1)
        g_in = _matmul(s["h1"], dproj, mode="tn", out_dtype=bf16, name=f"proj_in_dw{l}")
        in_sends = [_win_to_shards(g_in, D),_cols_scatter(da2[:GLA_LOWRANK]), dglu_w.reshape((N_DEV, -1, 512))]
        if l > 0:
            dh1 = _matmul(dproj, win_p[l], mode="nt", name=f"proj_in_dx{l}")
            pending = in_sends
        else:
            dh1, got = _matmul(dproj, win_p[l], mode="nt", sends=in_sends, name=f"proj_in_dx{l}")
            for k, r in zip(in_group, got):
                recv[k][l] = r
        dcur, dcur_b, dg1 = _rms_bwd(s["x"], norm1_g[l][None], dh1, dx2, f"rms1_bwd{l}")
        small_g["norm1_g"][l] = dg1[0]
    grad_x = dcur.reshape(x.shape)

    out = {}
    for k in SHARDED:
        shp = W[k].shape
        as3 = lambda t: t.reshape((shp[0], -1, shp[-1]))
        slots = [r.reshape((N_DEV, -1, shp[-1])) for r in recv[k]]
        res = _adamw(as3(W[k]), as3(Mo[k]), as3(Vo[k]), slots, f"adamw_{k}")
        out[k] = [t.reshape(shp) for t in res]

    small_list = [jnp.stack(small_g[k]) if k != "final_norm_g" else dgf[0] for k in SMALL]
    small_list = [t.reshape(W[k].shape) for t, k in zip(small_list, SMALL)]
    packed_parts = _all_gather([_pack(small_list)], "gather_small_grads")[0]
    res = _adamw_packed(_pack([W[k] for k in SMALL]), _pack([Mo[k] for k in SMALL]), _pack([Vo[k] for k in SMALL]),
                        packed_parts, "adamw_small")
    unpacked = [_unpack(t, [W[k] for k in SMALL]) for t in res]
    for i, k in enumerate(SMALL):
        out[k] = [unpacked[j][i] for j in range(4)]

    return (loss, grad_x, *[out[k][0] for k in WEIGHTS], *[out[k][1] for k in WEIGHTS],
            *[out[k][2] for k in WEIGHTS], *[out[k][3] for k in WEIGHTS])
```

```python
import functools
import math

import jax
import jax.numpy as jnp
from jax import lax
from jax.experimental import pallas as pl
from jax.experimental.pallas import tpu as pltpu

f32 = jnp.float32
bf16 = jnp.bfloat16
HI = lax.Precision.HIGHEST

N_DEV = 8
AXES = ("x", "y", "c")
NORM_EPS = 1e-6
ROPE_THETA = 500000.0
HEAD_DIM = 64
ROPE_DIM = 16
ATT_BLOCK = 128
BRANCH_WIDTH = 512
GLA_HEADS, GLA_DK, GLA_DV, GLA_LOWRANK, GLA_TAU, GLA_CHUNK, GLA_SUB = 4, 64, 128, 16, 16.0, 64, 16
S5_GROUPS, S5_GROUP, S5_STATE = 32, 16, 64
S5_CHUNK = 128
S5_LANE_BLOCKS = 4
DIL_CONFIGS = ((128, 1), (512, 4), (2048, 16))
SWA_HEADS, SWA_KV_HEADS, SWA_WINDOW = 8, 2, 128
ADAM_LR, ADAM_B1, ADAM_B2, ADAM_EPS, ADAM_WD, ADAM_STEP = 0.001, 0.9, 0.999, 1e-08, 0.01, 10
O_GLR, O_S5U, O_GATES = 1536, 1552, 4368
MIX_COLS = 4480
P_GQ, P_GK, P_GV, P_GR, P_S5U, P_CQ, P_CK, P_CV, P_SQ, P_SK, P_SV, P_GLR = (
    0, 256, 512, 1024, 1536, 2048, 2560, 3072, 3584, 4096, 4224, 4352)
VMEM_LIMIT = 56 * 1024 * 1024


def _tile(n, cap, q=128):
    if n <= cap:
        return n
    t = (cap // q) * q
    while t >= q:
        if n % t == 0:
            return t
        t -= q
    return n


def _params(sem=None):
    return pltpu.CompilerParams(dimension_semantics=sem, vmem_limit_bytes=VMEM_LIMIT)


@functools.partial(jax.custom_vjp, nondiff_argnums=(1,))
def _sroll(x, d):
    return pltpu.roll(x, d, 0)


def _sroll_fwd(x, d):
    return pltpu.roll(x, d, 0), None


def _sroll_bwd(d, _, g):
    n = g.shape[0]
    return (pltpu.roll(g, (n - d) % n, 0),)


_sroll.defvjp(_sroll_fwd, _sroll_bwd)


def _mesh_pos():
    return lax.axis_index("x"), lax.axis_index("y"), lax.axis_index("c")


class _Gather:
    def __init__(self, ins, outs, send_sems, recv_sems, local_sems):
        self.ins, self.outs = ins, outs
        self.send_sems, self.recv_sems, self.local_sems = send_sems, recv_sems, local_sems
        x, y, c = _mesh_pos()
        self.x, self.y, self.c = x, y, c
        self.me, self.sibling = (x, y, c), (x, y, 1 - c)
        self.chips = [(1 - x, y), (x, 1 - y), (1 - x, 1 - y)]

    def copy(self, a, k, block, to, src=None):
        slot = self.outs[a].at[4 * block[0] + 2 * block[1] + block[2]]
        return pltpu.make_async_remote_copy(
            src_ref=slot if src is None else src, dst_ref=slot,
            send_sem=self.send_sems.at[a, k], recv_sem=self.recv_sems.at[a, k],
            device_id=to, device_id_type=pl.DeviceIdType.MESH)

    def mine(self, a):
        return pltpu.make_async_copy(self.ins[a], self.outs[a].at[4 * self.x + 2 * self.y + self.c], self.local_sems.at[a])

    def first(self, a):
        return [self.copy(a, 0, self.me, self.sibling, src=self.ins[a])] + [
            self.copy(a, 1 + j, self.me, (*chip, self.c), src=self.ins[a]) for j, chip in enumerate(self.chips)]

    def start(self):
        for a in range(len(self.ins)):
            self.mine(a).start()
            for cp in self.first(a):
                cp.start()

    def finish(self):
        c = self.c
        for a in range(len(self.ins)):
            passed = [self.copy(a, 4 + j, (*chip, c), self.sibling) for j, chip in enumerate(self.chips)]
            for j, chip in enumerate(self.chips):
                self.copy(a, 1 + j, (*chip, c), self.me).wait_recv()
                passed[j].start()
            self.copy(a, 0, self.sibling, self.me).wait_recv()
            for j, chip in enumerate(self.chips):
                self.copy(a, 4 + j, (*chip, 1 - c), self.me).wait_recv()
            for cp in self.first(a) + passed:
                cp.wait_send()
            self.mine(a).wait()


def _all_gather(shards, name):
    n = len(shards)
    any_spec = pl.BlockSpec(memory_space=pl.ANY)

    def body(*refs):
        g = _Gather(refs[:n], refs[n:2 * n], *refs[2 * n:])
        g.start()
        g.finish()

    outs = pl.pallas_call(
        body, name=name,
        out_shape=[jax.ShapeDtypeStruct((N_DEV,) + s.shape, s.dtype) for s in shards],
        in_specs=[any_spec] * n, out_specs=[any_spec] * n,
        scratch_shapes=[pltpu.SemaphoreType.DMA((n, 7)), pltpu.SemaphoreType.DMA((n, 7)),
                        pltpu.SemaphoreType.DMA((n,))],
    )(*shards)
    return list(outs)


def _a2a_copies(ins, outs, send_sems, recv_sems, local_sems):
    x, y, c = _mesh_pos()
    me = 4 * x + 2 * y + c
    copies = []
    for a in range(len(ins)):
        copies.append(pltpu.make_async_copy(ins[a].at[me], outs[a].at[me], local_sems.at[a]))
        for k in range(1, N_DEV):
            px = 1 - x if k & 4 else x
            py = 1 - y if k & 2 else y
            pc = 1 - c if k & 1 else c
            copies.append(pltpu.make_async_remote_copy(
                src_ref=ins[a].at[4 * px + 2 * py + pc], dst_ref=outs[a].at[me],
                send_sem=send_sems.at[a, k - 1], recv_sem=recv_sems.at[a, k - 1],
                device_id=(px, py, pc), device_id_type=pl.DeviceIdType.MESH))
    return copies


def _host_call(body, sends, args, *, name, grid, out_shape, in_specs, out_specs, scratch_shapes, compiler_params,
               gather=False):
    single = not isinstance(out_shape, (list, tuple))
    out_shape = [out_shape] if single else list(out_shape)
    out_specs = [out_specs] if single else list(out_specs)
    sends = list(sends or ())
    n, n_in, n_out, n_scr = len(sends), len(args), len(out_shape), len(scratch_shapes)
    if n == 0:
        outs = pl.pallas_call(body, name=name, grid=grid, out_shape=out_shape, in_specs=in_specs, out_specs=out_specs,
                              scratch_shapes=list(scratch_shapes), compiler_params=compiler_params)(*args)
        return (outs[0] if single else list(outs)), []
    any_spec = pl.BlockSpec(memory_space=pl.ANY)

    def hosted(*refs):
        ins, s_in = refs[:n_in], refs[n_in:n_in + n]
        pos = n_in + n
        outs, s_out = refs[pos:pos + n_out], refs[pos + n_out:pos + n_out + n]
        pos += n_out + n
        scr, sems = refs[pos:pos + n_scr], refs[pos + n_scr:]
        ids = [pl.program_id(i) for i in range(len(grid))]
        first = functools.reduce(lambda p, q: p & q, [i == 0 for i in ids])
        last = functools.reduce(lambda p, q: p & q, [i == g - 1 for i, g in zip(ids, grid)])

        @pl.when(first)
        def _():
            if gather:
                _Gather(s_in, s_out, *sems).start()
            else:
                for cp in _a2a_copies(s_in, s_out, *sems):
                    cp.start()

        body(*ins, *outs, *scr)

        @pl.when(last)
        def _():
            if gather:
                _Gather(s_in, s_out, *sems).finish()
            else:
                for cp in _a2a_copies(s_in, s_out, *sems):
                    cp.wait()

    lead = (N_DEV,) if gather else ()
    outs = pl.pallas_call(
        hosted, name=name, grid=grid,
        out_shape=out_shape + [jax.ShapeDtypeStruct(lead + s.shape, s.dtype) for s in sends],
        in_specs=list(in_specs) + [any_spec] * n, out_specs=out_specs + [any_spec] * n,
        scratch_shapes=list(scratch_shapes) + [pltpu.SemaphoreType.DMA((n, 7)), pltpu.SemaphoreType.DMA((n, 7)),
                                               pltpu.SemaphoreType.DMA((n,))],
        compiler_params=compiler_params,
    )(*args, *sends)
    main = list(outs[:n_out])
    return (main[0] if single else main), list(outs[n_out:])


def _matmul(a, b, *, mode="nn", out_dtype=f32, res=None, extras=(), epilogue=None, sends=None, gather=False, name):
    if mode == "tn":
        K, M = a.shape
    else:
        M, K = a.shape
    N = b.shape[0] if mode == "nt" else b.shape[1]
    k_cap = 2048 if (a.dtype == bf16 and b.dtype == bf16) else 1024
    tm, tn, tk = _tile(M, 1024), _tile(N, 1152), _tile(K, k_cap)
    nk = K // tk
    dims = {"nn": (((1,), (0,)), ((), ())), "nt": (((1,), (1,)), ((), ())), "tn": (((0,), (0,)), ((), ()))}[mode]
    out_dtypes = list(out_dtype) if epilogue is not None else [out_dtype]
    n_side = (1 if res is not None else 0) + len(extras)

    def body(*refs):
        a_ref, b_ref = refs[:2]
        r_ref = refs[2] if res is not None else None
        x_refs = refs[2 + n_side - len(extras):2 + n_side]
        o_refs = refs[2 + n_side:2 + n_side + len(out_dtypes)]
        acc = refs[-1] if nk > 1 else None
        k = pl.program_id(2)
        part = lax.dot_general(a_ref[...].astype(bf16), b_ref[...].astype(bf16), dims, preferred_element_type=f32)

        def finish(r):
            if res is not None:
                r = r + r_ref[...]
            outs = epilogue(r, *[x[...] for x in x_refs]) if epilogue is not None else (r,)
            for o_ref, val in zip(o_refs, outs):
                o_ref[...] = val.astype(o_ref.dtype)

        if nk == 1:
            finish(part)
            return

        @pl.when(k == 0)
        def _():
            acc[...] = part

        @pl.when((k > 0) & (k < nk - 1))
        def _():
            acc[...] += part

        @pl.when(k == nk - 1)
        def _():
            finish(acc[...] + part)

    a_spec = pl.BlockSpec((tk, tm), lambda i, j, k: (k, i)) if mode == "tn" else pl.BlockSpec((tm, tk), lambda i, j, k: (i, k))
    b_spec = pl.BlockSpec((tn, tk), lambda i, j, k: (j, k)) if mode == "nt" else pl.BlockSpec((tk, tn), lambda i, j, k: (k, j))
    o_spec = pl.BlockSpec((tm, tn), lambda i, j, k: (i, j))
    in_specs = [a_spec, b_spec] + [o_spec] * n_side
    args = (a, b) + ((res,) if res is not None else ()) + tuple(extras)
    out, recvs = _host_call(
        body, sends, args, name=name, grid=(M // tm, N // tn, nk),
        out_shape=[jax.ShapeDtypeStruct((M, N), dt) for dt in out_dtypes],
        in_specs=in_specs, out_specs=[o_spec] * len(out_dtypes),
        scratch_shapes=[pltpu.VMEM((tm, tn), f32)] if nk > 1 else [],
        compiler_params=_params(("parallel", "parallel", "arbitrary")), gather=gather)
    if epilogue is None:
        out = out[0]
    return out if sends is None else (out, recvs)


def _rms(x, g):
    return x * lax.rsqrt(jnp.mean(x * x, axis=-1, keepdims=True) + NORM_EPS) * g


def _rms_fwd(x, g, name):
    L, D = x.shape
    tm = _tile(L, 256, 8)

    def body(x_ref, g_ref, o_ref):
        o_ref[...] = _rms(x_ref[...], g_ref[...]).astype(bf16)

    return pl.pallas_call(
        body, name=name, grid=(L // tm,), out_shape=jax.ShapeDtypeStruct((L, D), bf16),
        in_specs=[pl.BlockSpec((tm, D), lambda i: (i, 0)), pl.BlockSpec((1, D), lambda i: (0, 0))],
        out_specs=pl.BlockSpec((tm, D), lambda i: (i, 0)),
        compiler_params=_params(("parallel",)),
    )(x, g)


def _rms_bwd(x, g, dh, dres, name):
    L, D = x.shape
    tm = _tile(L, 256, 8)

    def body(x_ref, g_ref, dh_ref, dres_ref, dx_ref, dxb_ref, dg_ref):
        _, vjp = jax.vjp(_rms, x_ref[...], g_ref[...])
        dx, dg = vjp(dh_ref[...])
        dx = dres_ref[...] + dx
        dx_ref[...] = dx
        dxb_ref[...] = dx.astype(bf16)

        @pl.when(pl.program_id(0) == 0)
        def _():
            dg_ref[...] = jnp.zeros_like(dg_ref)

        dg_ref[...] += dg

    row = pl.BlockSpec((tm, D), lambda i: (i, 0))
    vec = pl.BlockSpec((1, D), lambda i: (0, 0))
    return pl.pallas_call(
        body, name=name, grid=(L // tm,),
        out_shape=[jax.ShapeDtypeStruct((L, D), f32), jax.ShapeDtypeStruct((L, D), bf16), jax.ShapeDtypeStruct((1, D), f32)],
        in_specs=[row, vec, row, row], out_specs=[row, row, vec],
        compiler_params=_params(("arbitrary",)),
    )(x, g, dh, dres)


def _final_loss(x, g, target, name):
    L, D = x.shape
    tm = _tile(L, 256, 8)

    def body(x_ref, g_ref, t_ref, loss_ref, dx_ref, dxb_ref, dg_ref):
        tgt = t_ref[...]

        def f(xv, gv):
            err = _rms(xv, gv) - tgt
            return 0.5 * jnp.sum(jnp.mean(err * err, axis=-1, keepdims=True), axis=0, keepdims=True)

        val, vjp = jax.vjp(f, x_ref[...], g_ref[...])
        dx, dg = vjp(jnp.ones((1, 1), f32))
        dx_ref[...] = dx
        dxb_ref[...] = dx.astype(bf16)

        @pl.when(pl.program_id(0) == 0)
        def _():
            dg_ref[...] = jnp.zeros_like(dg_ref)
            loss_ref[...] = jnp.zeros_like(loss_ref)

        dg_ref[...] += dg
        loss_ref[...] += jnp.broadcast_to(val, loss_ref.shape)

    row = pl.BlockSpec((tm, D), lambda i: (i, 0))
    vec = pl.BlockSpec((1, D), lambda i: (0, 0))
    acc = pl.BlockSpec((8, 128), lambda i: (0, 0))
    loss, dx, dxb, dg = pl.pallas_call(
        body, name=name, grid=(L // tm,),
        out_shape=[jax.ShapeDtypeStruct((8, 128), f32), jax.ShapeDtypeStruct((L, D), f32), jax.ShapeDtypeStruct((L, D), bf16),
                   jax.ShapeDtypeStruct((1, D), f32)],
        in_specs=[row, vec, row], out_specs=[acc, row, row, vec],
        compiler_params=_params(("arbitrary",)),
    )(x, g, target)
    return loss[0, 0], dx, dxb, dg


def _swiglu_f(a, b):
    return jax.nn.silu(a) * b


def _swiglu_epilogue(up, gate):
    return up, _swiglu_f(gate.astype(f32), up)


def _swiglu_grad_epilogue(dact, gate, up):
    _, vjp = jax.vjp(_swiglu_f, gate.astype(f32), up.astype(f32))
    return vjp(dact)


def _merge_f(g0, g1, g2, g3, y0, y1, y2, y3):
    s = jax.nn.sigmoid
    return s(g0) * y0 + s(g1) * y1 + s(g2) * y2 + s(g3) * y3


def _merge_fwd(proj, ys, D, name):
    L = proj.shape[0]
    tm, tn = _tile(L, 512, 8), _tile(D, 512)
    nj = D // tn

    def body(g0, g1, g2, g3, y0, y1, y2, y3, o_ref):
        o_ref[...] = _merge_f(g0[...], g1[...], g2[...], g3[...], *[y[...].astype(f32) for y in (y0, y1, y2, y3)]).astype(bf16)

    gspecs = [pl.BlockSpec((tm, tn), functools.partial(lambda i, j, m: (i, m * nj + j), m=m)) for m in range(4)]
    blk = pl.BlockSpec((tm, tn), lambda i, j: (i, j))
    return pl.pallas_call(
        body, name=name, grid=(L // tm, nj), out_shape=jax.ShapeDtypeStruct((L, D), bf16),
        in_specs=gspecs + [blk] * 4, out_specs=blk, compiler_params=_params(("parallel", "parallel")),
    )(proj, proj, proj, proj, *ys)


def _merge_bwd(proj, ys, dmixed, D, name):
    L = proj.shape[0]
    tm, tn = _tile(L, 512, 8), _tile(D, 512)
    nj = D // tn

    def body(g0, g1, g2, g3, y0, y1, y2, y3, d_ref, dy0, dy1, dy2, dy3, dg_ref):
        _, vjp = jax.vjp(_merge_f, g0[...], g1[...], g2[...], g3[...], *[y[...].astype(f32) for y in (y0, y1, y2, y3)])
        grads = vjp(d_ref[...])
        for m, r in enumerate((dy0, dy1, dy2, dy3)):
            r[...] = grads[4 + m].astype(bf16)
        for m in range(4):
            dg_ref[m] = grads[m].astype(bf16)

    gspecs = [pl.BlockSpec((tm, tn), functools.partial(lambda i, j, m: (i, m * nj + j), m=m)) for m in range(4)]
    blk = pl.BlockSpec((tm, tn), lambda i, j: (i, j))
    dgspec = pl.BlockSpec((4, tm, tn), lambda i, j: (0, i, j))
    outs = pl.pallas_call(
        body, name=name, grid=(L // tm, nj),
        out_shape=[jax.ShapeDtypeStruct((L, D), bf16)] * 4 + [jax.ShapeDtypeStruct((4, L, D), bf16)],
        in_specs=gspecs + [blk] * 5, out_specs=[blk] * 4 + [dgspec],
        compiler_params=_params(("parallel", "parallel")),
    )(proj, proj, proj, proj, *ys, dmixed)
    return outs[:4], outs[4]


def _gla_head(q, k, v, r, glr, st, a2, ab, ng):
    C, T = GLA_CHUNK, GLA_SUB
    row = lax.broadcasted_iota(jnp.int32, (C, C), 0)
    col = lax.broadcasted_iota(jnp.int32, (C, C), 1)
    tri = (col <= row).astype(f32)
    sel = (col == (row // T) * T).astype(f32)
    z = jnp.dot(glr, a2, preferred_element_type=f32) + ab
    g = jax.nn.log_sigmoid(z) / GLA_TAU
    cum = jnp.dot(tri, g, precision=HI, preferred_element_type=f32)
    excl = cum - g
    ref = jnp.dot(sel, excl, precision=HI, preferred_element_type=f32)
    qs = q * (GLA_DK ** -0.5)
    q_ref = qs * jnp.exp(cum - ref)
    rowk = lax.broadcasted_iota(jnp.int32, (C, GLA_DK), 0)
    a = jnp.zeros((C, C), f32)
    for s in range(1, C // T):
        ref_s = jnp.sum(jnp.where(rowk == s * T, excl, 0.0), axis=0, keepdims=True)
        k_ref = k * jnp.exp(jnp.where(rowk < s * T, ref_s - cum, -jnp.inf))
        a_s = lax.dot_general(q_ref, k_ref, (((1,), (1,)), ((), ())), preferred_element_type=f32)
        a = a + jnp.where(row // T == s, a_s, 0.0)
    o = jnp.dot(a, v, preferred_element_type=f32)
    sub = rowk % T
    for d in range(T):
        ks = _sroll(k, d) if d else k
        cs = _sroll(cum, d) if d else cum
        vs = _sroll(v, d) if d else v
        w = jnp.sum(qs * ks * jnp.exp(jnp.where(sub >= d, cum - cs, -jnp.inf)), axis=-1, keepdims=True)
        o = o + w * vs
    o = o + lax.dot_general(qs * jnp.exp(cum), st, (((1,), (1,)), ((), ())), preferred_element_type=f32)
    last = jnp.sum(jnp.where(rowk == C - 1, cum, 0.0), axis=0, keepdims=True)
    st_new = st * jnp.exp(last) + lax.dot_general(v, k * jnp.exp(last - cum), (((0,), (0,)), ((), ())),
                                                  preferred_element_type=f32)
    out = _rms(o, ng) * jax.nn.silu(r)
    return out, st_new


def _gla_specs(L, base128, rev):
    n = L // GLA_CHUNK
    ch = (lambda i: n - 1 - i) if rev else (lambda i: i)
    b = base128
    return n, ch, [
        pl.BlockSpec((GLA_CHUNK, 256), lambda i: (ch(i), (b + P_GQ // 128) // 2)),
        pl.BlockSpec((GLA_CHUNK, 256), lambda i: (ch(i), (b + P_GK // 128) // 2)),
        pl.BlockSpec((GLA_CHUNK, 512), lambda i: (ch(i), (b + P_GV // 128) // 4)),
        pl.BlockSpec((GLA_CHUNK, 512), lambda i: (ch(i), (b + P_GR // 128) // 4)),
        pl.BlockSpec((GLA_CHUNK, 128), lambda i: (ch(i), b + P_GLR // 128)),
    ]


def _gla_fwd(proj, a2p, ab, ng, base128, name, shards=None):
    L = proj.shape[0]
    n, _, pspecs = _gla_specs(L, base128, False)
    H, DK, DV = GLA_HEADS, GLA_DK, GLA_DV

    def body(q_ref, k_ref, v_ref, r_ref, l_ref, a2_ref, ab_ref, ng_ref, o_ref, st_ref, state):
        @pl.when(pl.program_id(0) == 0)
        def _():
            state[...] = jnp.zeros_like(state)

        glr = l_ref[...]
        states = [state[h] for h in range(H)]
        res = []
        for h in range(H):
            kk, vv = slice(h * DK, (h + 1) * DK), slice(h * DV, (h + 1) * DV)
            res.append(_gla_head(q_ref[:, kk], k_ref[:, kk], v_ref[:, vv], r_ref[:, vv], glr, states[h],
                                 a2_ref[:, kk], ab_ref[:, kk], ng_ref[:, vv]))
        for h in range(H):
            st_ref[0, h] = states[h]
            o_ref[:, h * DV:(h + 1) * DV] = res[h][0].astype(bf16)
            state[h] = res[h][1]

    full = lambda shape: pl.BlockSpec(shape, lambda i: (0,) * len(shape))
    return _host_call(
        body, shards, (proj, proj, proj, proj, proj, a2p, ab, ng), name=name, grid=(n,),
        out_shape=[jax.ShapeDtypeStruct((L, H * DV), bf16), jax.ShapeDtypeStruct((n, H, DV, DK), f32)],
        in_specs=pspecs + [full((128, 256)), full((1, 256)), full((1, 512))],
        out_specs=[pl.BlockSpec((GLA_CHUNK, 512), lambda i: (i, 0)), pl.BlockSpec((1, H, DV, DK), lambda i: (i, 0, 0, 0))],
        scratch_shapes=[pltpu.VMEM((H, DV, DK), f32)],
        compiler_params=_params(("arbitrary",)), gather=True)


def _gla_bwd(proj, a2p, ab, ng, states, dout, base128, name, sends=None):
    L = proj.shape[0]
    n, ch, pspecs = _gla_specs(L, base128, True)
    H, DK, DV = GLA_HEADS, GLA_DK, GLA_DV

    def body(q_ref, k_ref, v_ref, r_ref, l_ref, a2_ref, ab_ref, ng_ref, st_ref, do_ref,
             dq_ref, dk_ref, dv_ref, dr_ref, dl_ref, da2_ref, dab_ref, dng_ref, dstate):
        @pl.when(pl.program_id(0) == 0)
        def _():
            dstate[...] = jnp.zeros_like(dstate)
            da2_ref[...] = jnp.zeros_like(da2_ref)
            dab_ref[...] = jnp.zeros_like(dab_ref)
            dng_ref[...] = jnp.zeros_like(dng_ref)

        glr = l_ref[...]
        dglr = jnp.zeros(glr.shape, f32)
        vjps = []
        for h in range(H):
            kk, vv = slice(h * DK, (h + 1) * DK), slice(h * DV, (h + 1) * DV)
            vjps.append(jax.vjp(_gla_head, q_ref[:, kk], k_ref[:, kk], v_ref[:, vv], r_ref[:, vv], glr, st_ref[0, h],
                                a2_ref[:, kk], ab_ref[:, kk], ng_ref[:, vv])[1])
        cots = [(do_ref[:, h * DV:(h + 1) * DV].astype(f32), dstate[h]) for h in range(H)]
        grads = [vjps[h](cots[h]) for h in range(H)]
        for h in range(H):
            kk, vv = slice(h * DK, (h + 1) * DK), slice(h * DV, (h + 1) * DV)
            dq, dk, dv, dr, dl, dst, da2, dab, dng = grads[h]
            dq_ref[:, kk] = dq.astype(bf16)
            dk_ref[:, kk] = dk.astype(bf16)
            dv_ref[:, vv] = dv.astype(bf16)
            dr_ref[:, vv] = dr.astype(bf16)
            dglr = dglr + dl
            dstate[h] = dst
            da2_ref[:, kk] += da2
            dab_ref[:, kk] += dab
            dng_ref[:, vv] += dng
        dl_ref[...] = dglr.astype(bf16)

    full = lambda shape: pl.BlockSpec(shape, lambda i: (0,) * len(shape))
    rowspec = lambda w: pl.BlockSpec((GLA_CHUNK, w), lambda i: (ch(i), 0))
    return _host_call(
        body, sends, (proj, proj, proj, proj, proj, a2p, ab, ng, states, dout), name=name, grid=(n,),
        out_shape=[jax.ShapeDtypeStruct((L, 256), bf16), jax.ShapeDtypeStruct((L, 256), bf16),
                   jax.ShapeDtypeStruct((L, 512), bf16), jax.ShapeDtypeStruct((L, 512), bf16),
                   jax.ShapeDtypeStruct((L, 128), bf16), jax.ShapeDtypeStruct((128, 256), f32),
                   jax.ShapeDtypeStruct((1, 256), f32), jax.ShapeDtypeStruct((1, 512), f32)],
        in_specs=pspecs + [full((128, 256)), full((1, 256)), full((1, 512)),
                           pl.BlockSpec((1, H, DV, DK), lambda i: (ch(i), 0, 0, 0)), rowspec(512)],
        out_specs=[rowspec(256), rowspec(256), rowspec(512), rowspec(512), rowspec(128),
                   full((128, 256)), full((1, 256)), full((1, 512))],
        scratch_shapes=[pltpu.VMEM((H, DV, DK), f32)],
        compiler_params=_params(("arbitrary",)))


def _s5_prep(lam_re, lam_im, log_dt, b_re, b_im, c_re, c_im, d):
    G, N, Cn = S5_GROUPS, S5_STATE, S5_GROUP
    J, GB = S5_LANE_BLOCKS, S5_GROUPS // S5_LANE_BLOCKS
    dt = jnp.exp(log_dt)[:, None]
    mag = jnp.exp(lam_re * dt)
    ab_re, ab_im = mag * jnp.cos(lam_im * dt), mag * jnp.sin(lam_im * dt)
    den = lam_re * lam_re + lam_im * lam_im
    z_re = ((ab_re - 1.0) * lam_re + ab_im * lam_im) / den
    z_im = (ab_im * lam_re - (ab_re - 1.0) * lam_im) / den
    bb_re = z_re[..., None] * b_re - z_im[..., None] * b_im
    bb_im = z_re[..., None] * b_im + z_im[..., None] * b_re
    eye = jnp.eye(GB, dtype=f32)

    def in_blocks(bb):
        return jnp.einsum("jgnc,gh->jgchn", bb.reshape(J, GB, N, Cn), eye).reshape(J, GB * Cn, GB * N)

    def out_blocks(cc):
        return jnp.einsum("jgcn,gh->jgnhc", cc.reshape(J, GB, Cn, N), eye).reshape(J, GB * N, GB * Cn)

    return (ab_re.reshape(1, G * N), ab_im.reshape(1, G * N), in_blocks(bb_re), in_blocks(bb_im),
            out_blocks(c_re), out_blocks(c_im), d.reshape(1, G * Cn))


def _s5_chunk(u, hin_r, hin_i, a_r, a_i, bb_r, bb_i, cc_r, cc_i, dvec):
    T = u.shape[0]
    hr = jnp.dot(u, bb_r, preferred_element_type=f32)
    hi = jnp.dot(u, bb_i, preferred_element_type=f32)
    row = lax.broadcasted_iota(jnp.int32, hr.shape, 0)
    hr = hr + jnp.where(row == 0, a_r * hin_r - a_i * hin_i, 0.0)
    hi = hi + jnp.where(row == 0, a_r * hin_i + a_i * hin_r, 0.0)
    pr, pi = a_r, a_i
    d = 1
    while d < T:
        sr = jnp.where(row >= d, _sroll(hr, d), 0.0)
        si = jnp.where(row >= d, _sroll(hi, d), 0.0)
        hr, hi = hr + pr * sr - pi * si, hi + pr * si + pi * sr
        pr, pi = pr * pr - pi * pi, 2.0 * pr * pi
        d *= 2
    y = (jnp.dot(hr, cc_r, preferred_element_type=f32)
         - jnp.dot(hi, cc_i, preferred_element_type=f32) + dvec * u)
    out_r = jnp.sum(jnp.where(row == T - 1, hr, 0.0), axis=0, keepdims=True)
    out_i = jnp.sum(jnp.where(row == T - 1, hi, 0.0), axis=0, keepdims=True)
    return y, out_r, out_i


def _s5_specs(L, base128, rev):
    T, J = S5_CHUNK, S5_LANE_BLOCKS
    n = L // T
    ch = (lambda c: n - 1 - c) if rev else (lambda c: c)
    ub = base128 + P_S5U // 128
    specs = [
        pl.BlockSpec((T, 128), lambda j, c: (ch(c), ub + j)),
        pl.BlockSpec((1, 512), lambda j, c: (0, j)), pl.BlockSpec((1, 512), lambda j, c: (0, j)),
        pl.BlockSpec((None, 128, 512), lambda j, c: (j, 0, 0)), pl.BlockSpec((None, 128, 512), lambda j, c: (j, 0, 0)),
        pl.BlockSpec((None, 512, 128), lambda j, c: (j, 0, 0)), pl.BlockSpec((None, 512, 128), lambda j, c: (j, 0, 0)),
        pl.BlockSpec((1, 128), lambda j, c: (0, j)),
    ]
    return n, ch, specs


def _s5_fwd(proj, prep, base128, name, shards=None):
    L = proj.shape[0]
    T, J = S5_CHUNK, S5_LANE_BLOCKS
    n, _, specs = _s5_specs(L, base128, False)

    def body(u_ref, ar, ai, bbr, bbi, ccr, cci, dv, y_ref, sr_ref, si_ref, carry):
        @pl.when(pl.program_id(1) == 0)
        def _():
            carry[...] = jnp.zeros_like(carry)

        hin_r, hin_i = carry[0:1, :], carry[1:2, :]
        sr_ref[0] = jnp.broadcast_to(hin_r, (8, 512))
        si_ref[0] = jnp.broadcast_to(hin_i, (8, 512))
        y, out_r, out_i = _s5_chunk(u_ref[...], hin_r, hin_i, ar[...], ai[...], bbr[...], bbi[...], ccr[...], cci[...], dv[...])
        y_ref[...] = y
        carry[0:1, :] = out_r
        carry[1:2, :] = out_i

    st = pl.BlockSpec((1, 8, 512), lambda j, c: (c, 0, j))
    return _host_call(
        body, shards, (proj, *prep), name=name, grid=(J, n),
        out_shape=[jax.ShapeDtypeStruct((L, 512), f32), jax.ShapeDtypeStruct((n, 8, 2048), f32), jax.ShapeDtypeStruct((n, 8, 2048), f32)],
        in_specs=specs, out_specs=[pl.BlockSpec((T, 128), lambda j, c: (c, j)), st, st],
        scratch_shapes=[pltpu.VMEM((8, 512), f32)],
        compiler_params=_params(("parallel", "arbitrary")), gather=True)


def _s5_bwd(proj, prep, st_r, st_i, dy, base128, name, sends=None):
    L = proj.shape[0]
    T, J = S5_CHUNK, S5_LANE_BLOCKS
    n, ch, specs = _s5_specs(L, base128, True)

    def body(u_ref, ar, ai, bbr, bbi, ccr, cci, dv, sr_ref, si_ref, dy_ref,
             du_ref, dar, dai, dbbr, dbbi, dccr, dcci, ddv, dcarry):
        @pl.when(pl.program_id(1) == 0)
        def _():
            dcarry[...] = jnp.zeros_like(dcarry)
            for r in (dar, dai, dbbr, dbbi, dccr, dcci, ddv):
                r[...] = jnp.zeros_like(r)

        _, vjp = jax.vjp(_s5_chunk, u_ref[...], sr_ref[0, 0:1, :], si_ref[0, 0:1, :], ar[...], ai[...],
                         bbr[...], bbi[...], ccr[...], cci[...], dv[...])
        g = vjp((dy_ref[...], dcarry[0:1, :], dcarry[1:2, :]))
        du_ref[...] = g[0].astype(bf16)
        dcarry[0:1, :] = g[1]
        dcarry[1:2, :] = g[2]
        for r, val in zip((dar, dai, dbbr, dbbi, dccr, dcci, ddv), g[3:]):
            r[...] += val

    st = pl.BlockSpec((1, 8, 512), lambda j, c: (ch(c), 0, j))
    outs, recvs = _host_call(
        body, sends, (proj, *prep, st_r, st_i, dy), name=name, grid=(J, n),
        out_shape=[jax.ShapeDtypeStruct((L, 512), bf16),
                   jax.ShapeDtypeStruct((1, 2048), f32), jax.ShapeDtypeStruct((1, 2048), f32),
                   jax.ShapeDtypeStruct((J, 128, 512), f32), jax.ShapeDtypeStruct((J, 128, 512), f32),
                   jax.ShapeDtypeStruct((J, 512, 128), f32), jax.ShapeDtypeStruct((J, 512, 128), f32),
                   jax.ShapeDtypeStruct((1, 512), f32)],
        in_specs=specs + [st, st, pl.BlockSpec((T, 128), lambda j, c: (ch(c), j))],
        out_specs=[pl.BlockSpec((T, 128), lambda j, c: (ch(c), j))] + specs[1:],
        scratch_shapes=[pltpu.VMEM((8, 512), f32)],
        compiler_params=_params(("parallel", "arbitrary")))
    return outs[0], tuple(outs[1:]), recvs


def _glu_f(y, w, b):
    z = jax.nn.gelu(y)
    return z * jax.nn.sigmoid(jnp.dot(z.astype(bf16), w.astype(bf16), preferred_element_type=f32) + b)


def _glu_fwd(y, w, b, name):
    L = y.shape[0]
    tm = _tile(L, 512, 8)

    def body(y_ref, w_ref, b_ref, o_ref):
        o_ref[...] = _glu_f(y_ref[...], w_ref[...], b_ref[...]).astype(bf16)

    row = pl.BlockSpec((tm, 512), lambda i: (i, 0))
    return pl.pallas_call(
        body, name=name, grid=(L // tm,), out_shape=jax.ShapeDtypeStruct((L, 512), bf16),
        in_specs=[row, pl.BlockSpec((512, 512), lambda i: (0, 0)), pl.BlockSpec((1, 512), lambda i: (0, 0))],
        out_specs=row, compiler_params=_params(("parallel",)),
    )(y, w, b)


def _glu_bwd(y, w, b, dout, name):
    L = y.shape[0]
    tm = _tile(L, 512, 8)

    def body(y_ref, w_ref, b_ref, do_ref, dy_ref, dw_ref, db_ref):
        @pl.when(pl.program_id(0) == 0)
        def _():
            dw_ref[...] = jnp.zeros_like(dw_ref)
            db_ref[...] = jnp.zeros_like(db_ref)

        _, vjp = jax.vjp(_glu_f, y_ref[...], w_ref[...], b_ref[...])
        dy, dw, db = vjp(do_ref[...])
        dy_ref[...] = dy
        dw_ref[...] += dw
        db_ref[...] += db

    row = pl.BlockSpec((tm, 512), lambda i: (i, 0))
    wspec, bspec = pl.BlockSpec((512, 512), lambda i: (0, 0)), pl.BlockSpec((1, 512), lambda i: (0, 0))
    return pl.pallas_call(
        body, name=name, grid=(L // tm,),
        out_shape=[jax.ShapeDtypeStruct((L, 512), f32), jax.ShapeDtypeStruct((512, 512), f32), jax.ShapeDtypeStruct((1, 512), f32)],
        in_specs=[row, wspec, bspec, row], out_specs=[row, wspec, bspec],
        compiler_params=_params(("arbitrary",)),
    )(y, w, b, dout)


def _rope_tables(positions):
    half = ROPE_DIM // 2
    inv_freq = ROPE_THETA ** (-jnp.arange(half, dtype=f32) / half)
    ang = positions.astype(f32)[:, None] * inv_freq
    L = positions.shape[0]
    cos = jnp.concatenate([jnp.cos(ang), jnp.cos(ang), jnp.ones((L, HEAD_DIM - ROPE_DIM), f32)], axis=1)
    sin = jnp.concatenate([jnp.sin(ang), jnp.sin(ang), jnp.zeros((L, HEAD_DIM - ROPE_DIM), f32)], axis=1)
    return jnp.tile(cos, (1, 8)), jnp.tile(sin, (1, 8))


def _rope_matrix(w):
    half = ROPE_DIM // 2
    r = lax.broadcasted_iota(jnp.int32, (w, w), 0)
    c = lax.broadcasted_iota(jnp.int32, (w, w), 1)
    same = (r // HEAD_DIM) == (c // HEAD_DIM)
    rr, cc = r % HEAD_DIM, c % HEAD_DIM
    return (jnp.where(same & (cc < half) & (rr == cc + half), -1.0, 0.0)
            + jnp.where(same & (cc >= half) & (cc < ROPE_DIM) & (rr == cc - half), 1.0, 0.0))


def _rope(items, cos, sin, *, transpose, out_dtype, name):
    L = cos.shape[0]
    tm = _tile(L, 512, 8)
    n = len(items)

    def body(*refs):
        xs, c_ref, s_ref, outs = refs[:n], refs[n], refs[n + 1], refs[n + 2:]
        for x_ref, o_ref in zip(xs, outs):
            w = x_ref.shape[1]
            x = x_ref[...].astype(f32)
            c, s = c_ref[:, :w], s_ref[:, :w]
            rot = _rope_matrix(w)
            if transpose:
                y = x * c + lax.dot_general(x * s, rot, (((1,), (1,)), ((), ())), preferred_element_type=f32)
            else:
                y = x * c + jnp.dot(x, rot, preferred_element_type=f32) * s
            o_ref[...] = y.astype(o_ref.dtype)

    in_specs = [pl.BlockSpec((tm, w), functools.partial(lambda i, col: (i, col), col=col)) for _, col, w in items]
    tab = pl.BlockSpec((tm, 512), lambda i: (i, 0))
    outs = pl.pallas_call(
        body, name=name, grid=(L // tm,),
        out_shape=[jax.ShapeDtypeStruct((L, w), out_dtype) for _, _, w in items],
        in_specs=in_specs + [tab, tab], out_specs=[pl.BlockSpec((tm, w), lambda i: (i, 0)) for _, _, w in items],
        compiler_params=_params(("parallel",)),
    )(*[a for a, _, _ in items], cos, sin)
    return list(outs)


def _attn_head(q, kp, kc, vp, vc, sink, *, lim, max_dist):
    T = ATT_BLOCK
    k2 = jnp.concatenate([kp, kc], axis=0)
    v2 = jnp.concatenate([vp, vc], axis=0)
    s = lax.dot_general(q, k2, (((1,), (1,)), ((), ())), preferred_element_type=f32) * (HEAD_DIM ** -0.5)
    t = lax.broadcasted_iota(jnp.int32, (T, 2 * T), 0)
    j = lax.broadcasted_iota(jnp.int32, (T, 2 * T), 1)
    dist = T + t - j
    valid = (dist >= 0) & (dist <= max_dist) & (j >= lim)
    s = jnp.where(valid, s, -jnp.inf)
    m = lax.stop_gradient(jnp.max(s, axis=-1, keepdims=True))
    p = jnp.exp(s - m)
    den = jnp.sum(p, axis=-1, keepdims=True)
    o = jnp.dot(p, v2, preferred_element_type=f32) / den
    lse = jnp.broadcast_to(m + jnp.log(den), (T, HEAD_DIM))
    if sink is None:
        return o, lse
    return o * jax.nn.sigmoid(lse - sink)


def _attn_specs(L, q_col, k_col, v_col, wk, rev):
    T = ATT_BLOCK
    n = L // T
    blk = (lambda i: n - 1 - i) if rev else (lambda i: i)
    prev = lambda i: jnp.maximum(blk(i) - 1, 0)
    specs = [
        pl.BlockSpec((T, 512), lambda i: (blk(i), q_col)),
        pl.BlockSpec((T, wk), lambda i: (prev(i), k_col)), pl.BlockSpec((T, wk), lambda i: (blk(i), k_col)),
        pl.BlockSpec((T, wk), lambda i: (prev(i), v_col)), pl.BlockSpec((T, wk), lambda i: (blk(i), v_col)),
    ]
    return n, blk, specs


def _attn_fwd(qa, ka, va, sinks, *, q_col, k_col, v_col, hkv, nbc, max_dist, name, shards=None):
    L = qa.shape[0]
    T, HQ, HD = ATT_BLOCK, 8, HEAD_DIM
    wk = hkv * HD
    n, _, specs = _attn_specs(L, q_col, k_col, v_col, wk, False)
    grp = HQ // hkv
    gated = sinks is not None

    def body(*refs):
        q_ref, kp_ref, kc_ref, vp_ref, vc_ref = refs[:5]
        rest = refs[5:]
        lim = jnp.where(pl.program_id(0) % nbc == 0, T, 0)
        results = []
        for h in range(HQ):
            hs, ks = slice(h * HD, (h + 1) * HD), slice((h // grp) * HD, (h // grp + 1) * HD)
            results.append(_attn_head(q_ref[:, hs], kp_ref[:, ks], kc_ref[:, ks], vp_ref[:, ks], vc_ref[:, ks],
                                      rest[0][:, hs] if gated else None, lim=lim, max_dist=max_dist))
        for h, res in enumerate(results):
            hs = slice(h * HD, (h + 1) * HD)
            if gated:
                rest[1][:, hs] = res.astype(bf16)
            else:
                rest[0][:, hs] = res[0]
                rest[1][:, hs] = res[1]

    row = pl.BlockSpec((T, 512), lambda i: (i, 0))
    if gated:
        return _host_call(
            body, shards, (qa, ka, ka, va, va, sinks), name=name, grid=(n,),
            out_shape=jax.ShapeDtypeStruct((L, 512), bf16),
            in_specs=specs + [pl.BlockSpec((1, 512), lambda i: (0, 0))], out_specs=row, scratch_shapes=[],
            compiler_params=_params(("parallel",)), gather=True)
    return _host_call(
        body, shards, (qa, ka, ka, va, va), name=name, grid=(n,),
        out_shape=[jax.ShapeDtypeStruct((L, 512), f32)] * 2,
        in_specs=specs, out_specs=[row, row], scratch_shapes=[], compiler_params=_params(("parallel",)), gather=True)


def _attn_bwd(qa, ka, va, sinks, douts, *, q_col, k_col, v_col, hkv, nbc, max_dist, name, sends=None):
    L = qa.shape[0]
    T, HQ, HD = ATT_BLOCK, 8, HEAD_DIM
    wk = hkv * HD
    n, blk, specs = _attn_specs(L, q_col, k_col, v_col, wk, True)
    grp = HQ // hkv
    gated = sinks is not None
    nd = len(douts)

    def body(*refs):
        q_ref, kp_ref, kc_ref, vp_ref, vc_ref = refs[:5]
        pos = 5
        sink_ref = None
        if gated:
            sink_ref = refs[pos]
            pos += 1
        d_refs = refs[pos:pos + nd]
        pos += nd
        dq_ref, dk_ref, dv_ref = refs[pos:pos + 3]
        pos += 3
        dsink_ref = None
        if gated:
            dsink_ref = refs[pos]
            pos += 1
        carry_k, carry_v = refs[pos:pos + 2]

        @pl.when(pl.program_id(0) == 0)
        def _():
            carry_k[...] = jnp.zeros_like(carry_k)
            carry_v[...] = jnp.zeros_like(carry_v)
            if gated:
                dsink_ref[...] = jnp.zeros_like(dsink_ref)

        lim = jnp.where(blk(pl.program_id(0)) % nbc == 0, T, 0)
        dkp = [jnp.zeros((T, HD), f32) for _ in range(hkv)]
        dkc = [jnp.zeros((T, HD), f32) for _ in range(hkv)]
        dvp = [jnp.zeros((T, HD), f32) for _ in range(hkv)]
        dvc = [jnp.zeros((T, HD), f32) for _ in range(hkv)]
        dqs, dsinks = [], []
        for h in range(HQ):
            g = h // grp
            hs, ks = slice(h * HD, (h + 1) * HD), slice(g * HD, (g + 1) * HD)
            fn = functools.partial(_attn_head, lim=lim, max_dist=max_dist)
            prim = (q_ref[:, hs], kp_ref[:, ks], kc_ref[:, ks], vp_ref[:, ks], vc_ref[:, ks])
            if gated:
                _, vjp = jax.vjp(fn, *prim, sink_ref[:, hs])
                dq, a, b, c, d, ds = vjp(d_refs[0][:, hs].astype(f32))
                dsinks.append(ds)
            else:
                _, vjp = jax.vjp(lambda *p: fn(*p, None), *prim)
                dq, a, b, c, d = vjp((d_refs[0][:, hs], d_refs[1][:, hs]))
            dqs.append(dq)
            dkp[g], dkc[g], dvp[g], dvc[g] = dkp[g] + a, dkc[g] + b, dvp[g] + c, dvc[g] + d
        for h in range(HQ):
            hs = slice(h * HD, (h + 1) * HD)
            dq_ref[:, hs] = dqs[h].astype(bf16)
            if gated:
                dsink_ref[:, hs] += dsinks[h]
        for g in range(hkv):
            ks = slice(g * HD, (g + 1) * HD)
            dk_ref[:, ks] = (dkc[g] + carry_k[:, ks]).astype(bf16)
            dv_ref[:, ks] = (dvc[g] + carry_v[:, ks]).astype(bf16)
            carry_k[:, ks] = dkp[g]
            carry_v[:, ks] = dvp[g]

    row = lambda w: pl.BlockSpec((T, w), lambda i: (blk(i), 0))
    vec = pl.BlockSpec((1, 512), lambda i: (0, 0))
    in_specs = specs + ([vec] if gated else []) + [row(512)] * nd
    out_shape = [jax.ShapeDtypeStruct((L, 512), bf16), jax.ShapeDtypeStruct((L, wk), bf16), jax.ShapeDtypeStruct((L, wk), bf16)]
    out_specs = [row(512), row(wk), row(wk)]
    if gated:
        out_shape.append(jax.ShapeDtypeStruct((1, 512), f32))
        out_specs.append(vec)
    args = (qa, ka, ka, va, va) + ((sinks,) if gated else ()) + tuple(douts)
    outs, recvs = _host_call(
        body, sends, args, name=name, grid=(n,), out_shape=out_shape, in_specs=in_specs, out_specs=out_specs,
        scratch_shapes=[pltpu.VMEM((T, wk), f32), pltpu.VMEM((T, wk), f32)],
        compiler_params=_params(("arbitrary",)))
    return outs if sends is None else (outs, recvs)


def _dilmix_f(o0, o1, o2, l0, l1, l2):
    m = jnp.maximum(jnp.maximum(l0, l1), l2)
    e0, e1, e2 = jnp.exp(l0 - m), jnp.exp(l1 - m), jnp.exp(l2 - m)
    return (e0 * o0 + e1 * o1 + e2 * o2) / (e0 + e1 + e2)


def _dilmix_fwd(os_, ls, name):
    L = os_[0].shape[0]
    tm = _tile(L, 512, 8)

    def body(o0, o1, o2, l0, l1, l2, out):
        out[...] = _dilmix_f(o0[...], o1[...], o2[...], l0[...], l1[...], l2[...]).astype(bf16)

    row = pl.BlockSpec((tm, 512), lambda i: (i, 0))
    return pl.pallas_call(
        body, name=name, grid=(L // tm,), out_shape=jax.ShapeDtypeStruct((L, 512), bf16),
        in_specs=[row] * 6, out_specs=row, compiler_params=_params(("parallel",)),
    )(*os_, *ls)


def _dilmix_bwd(os_, ls, dout, name):
    L = os_[0].shape[0]
    tm = _tile(L, 512, 8)

    def body(o0, o1, o2, l0, l1, l2, d, *outs):
        _, vjp = jax.vjp(_dilmix_f, o0[...], o1[...], o2[...], l0[...], l1[...], l2[...])
        for r, val in zip(outs, vjp(d[...].astype(f32))):
            r[...] = val

    row = pl.BlockSpec((tm, 512), lambda i: (i, 0))
    outs = pl.pallas_call(
        body, name=name, grid=(L // tm,), out_shape=[jax.ShapeDtypeStruct((L, 512), f32)] * 6,
        in_specs=[row] * 7, out_specs=[row] * 6, compiler_params=_params(("parallel",)),
    )(*os_, *ls, dout)
    return outs[:3], outs[3:]


def _to_strided(z, dil):
    L, w = z.shape
    return z.reshape(L // dil, dil, w).transpose(1, 0, 2).reshape(L, w)


def _from_strided(z, dil):
    L, w = z.shape
    return z.reshape(dil, L // dil, w).transpose(1, 0, 2).reshape(L, w)


def _adamw_math(w, g, m, v):
    m = ADAM_B1 * m + (1.0 - ADAM_B1) * g
    v = ADAM_B2 * v + (1.0 - ADAM_B2) * (g * g)
    m_hat = m / (1.0 - ADAM_B1 ** ADAM_STEP)
    v_hat = v / (1.0 - ADAM_B2 ** ADAM_STEP)
    delta = -ADAM_LR * (m_hat / (jnp.sqrt(v_hat) + ADAM_EPS) + ADAM_WD * w)
    return delta, m, v


def _adamw(w, m, v, slots, name):
    depth, R, C = w.shape
    tr = _tile(R, max(8, 131072 // C), 8)
    outs = None
    for l in range(depth):
        def body(w_ref, m_ref, v_ref, s_ref, *rest):
            g_ref, d_ref, nm_ref, nv_ref = rest[-4:]
            g = s_ref[0].astype(f32)
            for i in range(1, N_DEV):
                g = g + s_ref[i].astype(f32)
            delta, nm, nv = _adamw_math(w_ref[0], g, m_ref[0], v_ref[0])
            g_ref[0], d_ref[0], nm_ref[0], nv_ref[0] = g, delta, nm, nv

        blk = pl.BlockSpec((1, tr, C), functools.partial(lambda i, l: (l, i, 0), l=l))
        carried = [] if outs is None else list(outs)
        outs = pl.pallas_call(
            body, name=f"{name}_{l}", grid=(R // tr,), out_shape=[jax.ShapeDtypeStruct(w.shape, f32)] * 4,
            in_specs=[blk, blk, blk, pl.BlockSpec((N_DEV, tr, C), lambda i: (0, i, 0))]
            + [pl.BlockSpec(memory_space=pl.ANY)] * len(carried),
            out_specs=[blk] * 4, input_output_aliases={4 + j: j for j in range(len(carried))},
            compiler_params=_params(("parallel",)),
        )(w, m, v, slots[l], *carried)
    return outs


def _adamw_packed(w, m, v, slots, name):
    R = w.shape[0]
    tr = _tile(R, 512, 8)

    def body(w_ref, m_ref, v_ref, s_ref, g_ref, d_ref, nm_ref, nv_ref):
        g = s_ref[0]
        for i in range(1, N_DEV):
            g = g + s_ref[i]
        delta, nm, nv = _adamw_math(w_ref[...], g, m_ref[...], v_ref[...])
        g_ref[...], d_ref[...], nm_ref[...], nv_ref[...] = g, delta, nm, nv

    blk = pl.BlockSpec((tr, 128), lambda i: (i, 0))
    return pl.pallas_call(
        body, name=name, grid=(R // tr,), out_shape=[jax.ShapeDtypeStruct(w.shape, f32)] * 4,
        in_specs=[blk, blk, blk, pl.BlockSpec((N_DEV, tr, 128), lambda i: (0, i, 0))], out_specs=[blk] * 4,
        compiler_params=_params(("parallel",)),
    )(w, m, v, slots)


def _cols_gathered(g):
    return jnp.concatenate([g[d] for d in range(N_DEV)], axis=-1)


def _cols_scatter(full):
    c = full.shape[-1] // N_DEV
    return jnp.stack([full[..., d * c:(d + 1) * c] for d in range(N_DEV)])


def _win_segments(D):
    b = 4 * D
    return (((O_GATES, O_GATES + b), 0), ((0, O_GLR), b), ((O_S5U, O_GATES), b + O_GLR), ((O_GLR, O_S5U), b + P_GLR))


def _win_from_shards(g):
    rows, c = g.shape[1], g.shape[2]
    D = (N_DEV * c - O_GATES) // 4
    pieces = []
    for (lo, hi), _ in sorted(_win_segments(D), key=lambda t: t[1]):
        for d in range(N_DEV):
            a, b = max(lo, d * c), min(hi, (d + 1) * c)
            if a < b:
                pieces.append(g[d][:, a - d * c:b - d * c])
    pieces.append(jnp.zeros((rows, 128 - GLA_LOWRANK), g.dtype))
    return jnp.concatenate(pieces, axis=1)


def _win_to_shards(wp, D):
    c = (O_GATES + 4 * D) // N_DEV
    segs = sorted(_win_segments(D), key=lambda t: t[0][0])
    out = []
    for d in range(N_DEV):
        pieces = []
        for (lo, hi), off in segs:
            a, b = max(lo, d * c), min(hi, (d + 1) * c)
            if a < b:
                pieces.append(wp[:, off + a - lo:off + b - lo])
        out.append(jnp.concatenate(pieces, axis=1))
    return jnp.stack(out)


SMALL = ("norm1_g", "gla_a_b", "gla_norm_g", "s5_lambda_re", "s5_lambda_im", "s5_log_dt", "s5_b_re", "s5_b_im",
         "s5_c_re", "s5_c_im", "s5_d", "s5_glu_b", "swa_sinks", "norm2_g", "final_norm_g")
SHARDED = ("w_in", "gla_a2", "s5_glu_w", "w_branch", "w_out", "w_ffn_gate", "w_ffn_up", "w_ffn_down")
WEIGHTS = ("norm1_g", "w_in", "gla_a2", "gla_a_b", "gla_norm_g", "s5_lambda_re", "s5_lambda_im", "s5_log_dt", "s5_b_re",
           "s5_b_im", "s5_c_re", "s5_c_im", "s5_d", "s5_glu_w", "s5_glu_b", "swa_sinks", "w_branch", "w_out", "norm2_g",
           "w_ffn_gate", "w_ffn_up", "w_ffn_down", "final_norm_g")


def _pack(arrs):
    flat = jnp.concatenate([a.reshape(-1) for a in arrs])
    n = flat.shape[0]
    rows = -(-n // 1024) * 8
    return jnp.pad(flat, (0, rows * 128 - n)).reshape(rows, 128)


def _unpack(packed, like):
    flat = packed.reshape(-1)
    out, pos = [], 0
    for a in like:
        out.append(flat[pos:pos + a.size].reshape(a.shape))
        pos += a.size
    return out


def kernel(x, positions, norm1_g, w_in, gla_a2, gla_a_b, gla_norm_g, s5_lambda_re, s5_lambda_im, s5_log_dt, s5_b_re, s5_b_im, s5_c_re, s5_c_im, s5_d, s5_glu_w, s5_glu_b, swa_sinks, w_branch, w_out, norm2_g, w_ffn_gate, w_ffn_up, w_ffn_down, final_norm_g, loss_target, m_norm1_g, m_w_in, m_gla_a2, m_gla_a_b, m_gla_norm_g, m_s5_lambda_re, m_s5_lambda_im, m_s5_log_dt, m_s5_b_re, m_s5_b_im, m_s5_c_re, m_s5_c_im, m_s5_d, m_s5_glu_w, m_s5_glu_b, m_swa_sinks, m_w_branch, m_w_out, m_norm2_g, m_w_ffn_gate, m_w_ffn_up, m_w_ffn_down, m_final_norm_g, v_norm1_g, v_w_in, v_gla_a2, v_gla_a_b, v_gla_norm_g, v_s5_lambda_re, v_s5_lambda_im, v_s5_log_dt, v_s5_b_re, v_s5_b_im, v_s5_c_re, v_s5_c_im, v_s5_d, v_s5_glu_w, v_s5_glu_b, v_swa_sinks, v_w_branch, v_w_out, v_norm2_g, v_w_ffn_gate, v_w_ffn_up, v_w_ffn_down, v_final_norm_g):
    W = dict(norm1_g=norm1_g, w_in=w_in, gla_a2=gla_a2, gla_a_b=gla_a_b, gla_norm_g=gla_norm_g, s5_lambda_re=s5_lambda_re, s5_lambda_im=s5_lambda_im, s5_log_dt=s5_log_dt, s5_b_re=s5_b_re, s5_b_im=s5_b_im, s5_c_re=s5_c_re, s5_c_im=s5_c_im, s5_d=s5_d, s5_glu_w=s5_glu_w, s5_glu_b=s5_glu_b, swa_sinks=swa_sinks, w_branch=w_branch, w_out=w_out, norm2_g=norm2_g, w_ffn_gate=w_ffn_gate, w_ffn_up=w_ffn_up, w_ffn_down=w_ffn_down, final_norm_g=final_norm_g)
    Mo = dict(norm1_g=m_norm1_g, w_in=m_w_in, gla_a2=m_gla_a2, gla_a_b=m_gla_a_b, gla_norm_g=m_gla_norm_g, s5_lambda_re=m_s5_lambda_re, s5_lambda_im=m_s5_lambda_im, s5_log_dt=m_s5_log_dt, s5_b_re=m_s5_b_re, s5_b_im=m_s5_b_im, s5_c_re=m_s5_c_re, s5_c_im=m_s5_c_im, s5_d=m_s5_d, s5_glu_w=m_s5_glu_w, s5_glu_b=m_s5_glu_b, swa_sinks=m_swa_sinks, w_branch=m_w_branch, w_out=m_w_out, norm2_g=m_norm2_g, w_ffn_gate=m_w_ffn_gate, w_ffn_up=m_w_ffn_up, w_ffn_down=m_w_ffn_down, final_norm_g=m_final_norm_g)
    Vo = dict(norm1_g=v_norm1_g, w_in=v_w_in, gla_a2=v_gla_a2, gla_a_b=v_gla_a_b, gla_norm_g=v_gla_norm_g, s5_lambda_re=v_s5_lambda_re, s5_lambda_im=v_s5_lambda_im, s5_log_dt=v_s5_log_dt, s5_b_re=v_s5_b_re, s5_b_im=v_s5_b_im, s5_c_re=v_s5_c_re, s5_c_im=v_s5_c_im, s5_d=v_s5_d, s5_glu_w=v_s5_glu_w, s5_glu_b=v_s5_glu_b, swa_sinks=v_swa_sinks, w_branch=v_w_branch, w_out=v_w_out, norm2_g=v_norm2_g, w_ffn_gate=v_w_ffn_gate, w_ffn_up=v_w_ffn_up, w_ffn_down=v_w_ffn_down, final_norm_g=v_final_norm_g)

    L, D = x.shape[1], x.shape[2]
    depth = norm1_g.shape[0]
    xs = x.reshape(L, D)
    target = loss_target.reshape(L, D)
    base128 = 4 * D // 128

    in_group = ("w_in", "gla_a2", "s5_glu_w")
    full = {}

    def riders(*pairs):
        pairs = [(k, l) for k, l in pairs if l < depth]
        return pairs, [W[k][l] if k in ("gla_a2", "s5_glu_w") else W[k][l].astype(bf16) for k, l in pairs]

    def landed(pairs, gathered):
        for (k, l), g in zip(pairs, gathered):
            if k == "w_in":
                full[k, l] = _win_from_shards(g)
            elif k == "gla_a2":
                full[k, l] = jnp.pad(_cols_gathered(g), ((0, 128 - GLA_LOWRANK), (0, 0)))
            elif k in ("w_branch", "w_ffn_gate", "w_ffn_up"):
                full[k, l] = _cols_gathered(g)
            else:
                full[k, l] = g.reshape((-1, g.shape[-1]))

    def hosted_matmul(pairs, *args, **kw):
        pairs, sh = riders(*pairs)
        if not pairs:
            return _matmul(*args, **kw)
        out, got = _matmul(*args, sends=sh, gather=True, **kw)
        landed(pairs, got)
        return out

    pairs, sh = riders(*[(k, 0) for k in in_group])
    landed(pairs, _all_gather(sh, "gather_w_in0"))

    cos, sin = _rope_tables(positions.reshape(L))

    saved = []
    cur = xs
    for l in range(depth):
        s = {"x": cur}
        nxt = l + 1
        h1 = _rms_fwd(cur, norm1_g[l][None], f"rms1_fwd{l}")
        first = [("w_branch", 0), ("w_out", 0), ("w_ffn_gate", 0)] if l == 0 else [("w_ffn_gate", l), ("w_ffn_up", l)]
        proj = hosted_matmul(first, h1, full["w_in", l], name=f"proj_in{l}")
        s["h1"], s["proj"] = h1, proj
        ab, ng = gla_a_b[l][None], gla_norm_g[l].reshape(1, 512)
        pairs, sh = riders(*[(k, nxt) for k in in_group], *([("w_ffn_down", l)] if l > 0 else []))
        (o_gla, s["gla_st"]), got = _gla_fwd(proj, full["gla_a2", l], ab, ng, base128, f"gla_fwd{l}", shards=sh)
        landed(pairs, got)
        prep, s["prep_vjp"] = jax.vjp(_s5_prep, s5_lambda_re[l], s5_lambda_im[l], s5_log_dt[l], s5_b_re[l], s5_b_im[l],
                                      s5_c_re[l], s5_c_im[l], s5_d[l])
        s["prep"] = prep
        pairs, sh = riders(("w_branch", nxt), ("w_out", nxt))
        (y_s5, s["s5_r"], s["s5_i"]), got = _s5_fwd(proj, prep, base128, f"s5_fwd{l}", shards=sh)
        landed(pairs, got)
        s["y_s5"] = y_s5
        o_s5 = _glu_fwd(y_s5, full["s5_glu_w", l], s5_glu_b[l][None], f"glu_fwd{l}")
        sinks_b = jnp.repeat(swa_sinks[l], HEAD_DIM)[None]
        s["sinks_b"] = sinks_b
        nb = L // ATT_BLOCK
        cq, ck, cv = (base128 + P_CQ // 128) // 4, (base128 + P_CK // 128) // 4, (base128 + P_CV // 128) // 4
        sq_col, sk_col, sv_col = (base128 + P_SQ // 128) // 4, base128 + P_SK // 128, base128 + P_SV // 128
        cq_r, ck_r, sq_r, sk_r = _rope([(proj, cq, 512), (proj, ck, 512), (proj, sq_col, 512), (proj, sk_col, 128)], cos, sin,
                                       transpose=False, out_dtype=f32, name=f"rope_fwd{l}")
        s["rot"] = (cq_r, ck_r, sq_r, sk_r)
        o_swa, _ = _attn_fwd(sq_r, sk_r, proj, sinks_b, q_col=0, k_col=0, v_col=sv_col, hkv=SWA_KV_HEADS, nbc=nb,
                             max_dist=SWA_WINDOW - 1, name=f"swa_fwd{l}")
        dil_o, dil_l, s["dil_in"] = [], [], []
        for window, dil in DIL_CONFIGS:
            if dil == 1:
                (o, lse), _ = _attn_fwd(cq_r, ck_r, proj, None, q_col=0, k_col=0, v_col=cv, hkv=8, nbc=nb,
                                        max_dist=window // dil, name=f"dil{dil}_fwd{l}")
                s["dil_in"].append(None)
            else:
                qs_, ks_ = _to_strided(cq_r, dil), _to_strided(ck_r, dil)
                vs_ = _to_strided(proj[:, 4 * D + P_CV:4 * D + P_CV + 512], dil)
                (o, lse), _ = _attn_fwd(qs_, ks_, vs_, None, q_col=0, k_col=0, v_col=0, hkv=8,
                                        nbc=nb // dil, max_dist=window // dil, name=f"dil{dil}_fwd{l}")
                o, lse = _from_strided(o, dil), _from_strided(lse, dil)
                s["dil_in"].append((qs_, ks_, vs_))
            dil_o.append(o)
            dil_l.append(lse)
        s["dil_o"], s["dil_l"] = dil_o, dil_l
        o_dil = _dilmix_fwd(dil_o, dil_l, f"dilmix_fwd{l}")
        branches = (o_gla, o_s5, o_dil, o_swa)
        s["branches"] = branches
        ys = [_matmul(br, full["w_branch", l][m], out_dtype=bf16, name=f"branch{m}_fwd{l}") for m, br in enumerate(branches)]
        s["ys"] = ys
        mixed = _merge_fwd(proj, ys, D, f"merge_fwd{l}")
        s["mixed"] = mixed
        x2 = _matmul(mixed, full["w_out", l], res=cur, name=f"out_fwd{l}")
        s["x2"] = x2
        h2 = _rms_fwd(x2, norm2_g[l][None], f"rms2_fwd{l}")
        a = hosted_matmul([("w_ffn_up", 0)] if l == 0 else [], h2, full["w_ffn_gate", l], out_dtype=bf16, name=f"ffn_gate_fwd{l}")
        b, act = hosted_matmul([("w_ffn_down", 0)] if l == 0 else [], h2, full["w_ffn_up", l], extras=(a,),
                               epilogue=_swiglu_epilogue, out_dtype=(bf16, bf16), name=f"ffn_up_fwd{l}")
        s["h2"], s["a"], s["b"], s["act"] = h2, a, b, act
        cur = _matmul(act, full["w_ffn_down", l], res=x2, name=f"ffn_down_fwd{l}")
        saved.append(s)
    win_p, a2p, glu_w, wb, wout, wg, wu, wd = (
        [full[k, l] for l in range(depth)]
        for k in ("w_in", "gla_a2", "s5_glu_w", "w_branch", "w_out", "w_ffn_gate", "w_ffn_up", "w_ffn_down"))

    loss_part, dcur, dcur_b, dgf = _final_loss(cur, final_norm_g[None], target, "final_loss")
    loss = lax.psum(loss_part, AXES)

    small_g = {k: [None] * depth for k in SMALL if k != "final_norm_g"}
    recv = {k: [None] * depth for k in SHARDED}
    in_group = ("w_in", "gla_a2", "s5_glu_w")
    pending = None
    for l in reversed(range(depth)):
        s = saved[l]
        proj = s["proj"]
        da, db = _matmul(dcur_b, wd[l], mode="nt", extras=(s["a"], s["b"]), epilogue=_swiglu_grad_epilogue,
                         out_dtype=(bf16, bf16), name=f"ffn_down_dx{l}")
        g_down = _matmul(s["act"], dcur_b, mode="tn", out_dtype=bf16, name=f"ffn_down_dw{l}")
        dh2 = _matmul(da, wg[l], mode="nt", name=f"ffn_gate_dx{l}")
        dh2 = _matmul(db, wu[l], mode="nt", res=dh2, name=f"ffn_up_dx{l}")
        g_gate = _matmul(s["h2"], da, mode="tn", out_dtype=bf16, name=f"ffn_gate_dw{l}")
        g_up = _matmul(s["h2"], db, mode="tn", out_dtype=bf16, name=f"ffn_up_dw{l}")
        ffn_sends = (("w_ffn_down", g_down.reshape((N_DEV, -1, D))), ("w_ffn_gate", _cols_scatter(g_gate)),
                     ("w_ffn_up", _cols_scatter(g_up)))
        dx2, dx2_b, dg2 = _rms_bwd(s["x2"], norm2_g[l][None], dh2, dcur, f"rms2_bwd{l}")
        small_g["norm2_g"][l] = dg2[0]
        dmixed = _matmul(dx2_b, wout[l], mode="nt", name=f"out_dx{l}")
        g_out = _matmul(s["mixed"], dx2_b, mode="tn", out_dtype=bf16, name=f"out_dw{l}")
        dys, dgates = _merge_bwd(proj, s["ys"], dmixed, D, f"merge_bwd{l}")
        dbr = [_matmul(dys[m], wb[l][m], mode="nt", name=f"branch{m}_dx{l}") for m in range(4)]
        g_branch = jnp.stack([_matmul(s["branches"][m], dys[m], mode="tn", out_dtype=bf16, name=f"branch{m}_dw{l}")
                              for m in range(4)])
        d_gla, d_s5, d_dil, d_swa = dbr
        ab, ng = gla_a_b[l][None], gla_norm_g[l].reshape(1, 512)
        ffn_in_gla = pending is None
        gla_sends = [t for _, t in ffn_sends] if ffn_in_gla else pending
        (dgq, dgk, dgv, dgr, dglr, da2, dab, dng), got = _gla_bwd(proj, a2p[l], ab, ng, s["gla_st"], d_gla, base128,
                                                                   f"gla_bwd{l}", sends=gla_sends)
        if ffn_in_gla:
            for (k, _), r in zip(ffn_sends, got):
                recv[k][l] = r
        else:
            for k, r in zip(in_group, got):
                recv[k][l + 1] = r
        small_g["gla_a_b"][l] = dab[0]
        small_g["gla_norm_g"][l] = dng.reshape(GLA_HEADS, GLA_DV)
        dy_s5, dglu_w, dglu_b = _glu_bwd(s["y_s5"], glu_w[l], s5_glu_b[l][None], d_s5, f"glu_bwd{l}")
        small_g["s5_glu_b"][l] = dglu_b[0]
        ds5u, dprep, got = _s5_bwd(proj, s["prep"], s["s5_r"], s["s5_i"], dy_s5, base128, f"s5_bwd{l}",
                                   sends=[g_out.reshape((N_DEV, -1, D)), _cols_scatter(g_branch)])
        recv["w_out"][l], recv["w_branch"][l] = got
        draw = s["prep_vjp"](dprep)
        for k, val in zip(("s5_lambda_re", "s5_lambda_im", "s5_log_dt", "s5_b_re", "s5_b_im", "s5_c_re", "s5_c_im", "s5_d"), draw):
            small_g[k][l] = val
        nb = L // ATT_BLOCK
        cq_r, ck_r, sq_r, sk_r = s["rot"]
        dsq, dsk, dsv, dsinks = _attn_bwd(sq_r, sk_r, proj, s["sinks_b"], (d_swa,), q_col=0, k_col=0,
                                          v_col=base128 + P_SV // 128, hkv=SWA_KV_HEADS, nbc=nb, max_dist=SWA_WINDOW - 1,
                                          name=f"swa_bwd{l}")
        small_g["swa_sinks"][l] = dsinks.reshape(SWA_HEADS, HEAD_DIM).sum(axis=1)
        dos, dls = _dilmix_bwd(s["dil_o"], s["dil_l"], d_dil, f"dilmix_bwd{l}")
        cv = (base128 + P_CV // 128) // 4
        dcq = dck = dcv = None
        for i, (window, dil) in enumerate(DIL_CONFIGS):
            key, send = ffn_sends[i]
            sends = [] if ffn_in_gla else [send]
            if dil == 1:
                g3, got = _attn_bwd(cq_r, ck_r, proj, None, (dos[i], dls[i]), q_col=0, k_col=0, v_col=cv, hkv=8,
                                    nbc=nb, max_dist=window // dil, name=f"dil{dil}_bwd{l}", sends=sends)
            else:
                qs_, ks_, vs_ = s["dil_in"][i]
                g3, got = _attn_bwd(qs_, ks_, vs_, None, (_to_strided(dos[i], dil), _to_strided(dls[i], dil)),
                                    q_col=0, k_col=0, v_col=0, hkv=8, nbc=nb // dil, max_dist=window // dil,
                                    name=f"dil{dil}_bwd{l}", sends=sends)
                g3 = [_from_strided(t, dil) for t in g3]
            if got:
                recv[key][l] = got[0]
            g3 = [t.astype(f32) for t in g3]
            dcq, dck, dcv = (g3[0], g3[1], g3[2]) if dcq is None else (dcq + g3[0], dck + g3[1], dcv + g3[2])
        dcq, dck, dsq, dsk = _rope([(dcq, 0, 512), (dck, 0, 512), (dsq, 0, 512), (dsk, 0, 128)], cos, sin,
                                   transpose=True, out_dtype=bf16, name=f"rope_bwd{l}")
        dproj = jnp.concatenate([dgates.transpose(1, 0, 2).reshape(L, 4 * D), dgq, dgk, dgv, dgr, ds5u,
                                 dcq, dck, dcv.astype(bf16), dsq, dsk, dsv, dglr], axis=1)
        g_in = _matmul(s["h1"], dproj, mode="tn", out_dtype=bf16, name=f"proj_in_dw{l}")
        in_sends = [_win_to_shards(g_in, D),_cols_scatter(da2[:GLA_LOWRANK]), dglu_w.reshape((N_DEV, -1, 512))]
        if l > 0:
            dh1 = _matmul(dproj, win_p[l], mode="nt", name=f"proj_in_dx{l}")
            pending = in_sends
        else:
            dh1, got = _matmul(dproj, win_p[l], mode="nt", sends=in_sends, name=f"proj_in_dx{l}")
            for k, r in zip(in_group, got):
                recv[k][l] = r
        dcur, dcur_b, dg1 = _rms_bwd(s["x"], norm1_g[l][None], dh1, dx2, f"rms1_bwd{l}")
        small_g["norm1_g"][l] = dg1[0]
    grad_x = dcur.reshape(x.shape)

    out = {}
    for k in SHARDED:
        shp = W[k].shape
        as3 = lambda t: t.reshape((shp[0], -1, shp[-1]))
        slots = [r.reshape((N_DEV, -1, shp[-1])) for r in recv[k]]
        res = _adamw(as3(W[k]), as3(Mo[k]), as3(Vo[k]), slots, f"adamw_{k}")
        out[k] = [t.reshape(shp) for t in res]

    small_list = [jnp.stack(small_g[k]) if k != "final_norm_g" else dgf[0] for k in SMALL]
    small_list = [t.reshape(W[k].shape) for t, k in zip(small_list, SMALL)]
    packed_parts = _all_gather([_pack(small_list)], "gather_small_grads")[0]
    res = _adamw_packed(_pack([W[k] for k in SMALL]), _pack([Mo[k] for k in SMALL]), _pack([Vo[k] for k in SMALL]),
                        packed_parts, "adamw_small")
    unpacked = [_unpack(t, [W[k] for k in SMALL]) for t in res]
    for i, k in enumerate(SMALL):
        out[k] = [unpacked[j][i] for j in range(4)]

    return (loss, grad_x, *[out[k][0] for k in WEIGHTS], *[out[k][1] for k in WEIGHTS],
            *[out[k][2] for k in WEIGHTS], *[out[k][3] for k in WEIGHTS])
```

```python
import functools
import math

import jax
import jax.numpy as jnp
from jax import lax
from jax.experimental import pallas as pl
from jax.experimental.pallas import tpu as pltpu

f32 = jnp.float32
bf16 = jnp.bfloat16
HI = lax.Precision.HIGHEST

N_DEV = 8
AXES = ("x", "y", "c")
NORM_EPS = 1e-6
ROPE_THETA = 500000.0
HEAD_DIM = 64
ROPE_DIM = 16
ATT_BLOCK = 128
BRANCH_WIDTH = 512
GLA_HEADS, GLA_DK, GLA_DV, GLA_LOWRANK, GLA_TAU, GLA_CHUNK, GLA_SUB = 4, 64, 128, 16, 16.0, 64, 16
S5_GROUPS, S5_GROUP, S5_STATE = 32, 16, 64
S5_CHUNK = 128
S5_LANE_BLOCKS = 4
DIL_CONFIGS = ((128, 1), (512, 4), (2048, 16))
SWA_HEADS, SWA_KV_HEADS, SWA_WINDOW = 8, 2, 128
ADAM_LR, ADAM_B1, ADAM_B2, ADAM_EPS, ADAM_WD, ADAM_STEP = 0.001, 0.9, 0.999, 1e-08, 0.01, 10
O_GLR, O_S5U, O_GATES = 1536, 1552, 4368
MIX_COLS = 4480
P_GQ, P_GK, P_GV, P_GR, P_S5U, P_CQ, P_CK, P_CV, P_SQ, P_SK, P_SV, P_GLR = (
    0, 256, 512, 1024, 1536, 2048, 2560, 3072, 3584, 4096, 4224, 4352)
VMEM_LIMIT = 56 * 1024 * 1024


def _tile(n, cap, q=128):
    if n <= cap:
        return n
    t = (cap // q) * q
    while t >= q:
        if n % t == 0:
            return t
        t -= q
    return n


def _params(sem=None):
    return pltpu.CompilerParams(dimension_semantics=sem, vmem_limit_bytes=VMEM_LIMIT)


@functools.partial(jax.custom_vjp, nondiff_argnums=(1,))
def _sroll(x, d):
    return pltpu.roll(x, d, 0)


def _sroll_fwd(x, d):
    return pltpu.roll(x, d, 0), None


def _sroll_bwd(d, _, g):
    n = g.shape[0]
    return (pltpu.roll(g, (n - d) % n, 0),)


_sroll.defvjp(_sroll_fwd, _sroll_bwd)


def _mesh_pos():
    return lax.axis_index("x"), lax.axis_index("y"), lax.axis_index("c")


class _Gather:
    def __init__(self, ins, outs, send_sems, recv_sems, local_sems):
        self.ins, self.outs = ins, outs
        self.send_sems, self.recv_sems, self.local_sems = send_sems, recv_sems, local_sems
        x, y, c = _mesh_pos()
        self.x, self.y, self.c = x, y, c
        self.me, self.sibling = (x, y, c), (x, y, 1 - c)
        self.chips = [(1 - x, y), (x, 1 - y), (1 - x, 1 - y)]

    def copy(self, a, k, block, to, src=None):
        slot = self.outs[a].at[4 * block[0] + 2 * block[1] + block[2]]
        return pltpu.make_async_remote_copy(
            src_ref=slot if src is None else src, dst_ref=slot,
            send_sem=self.send_sems.at[a, k], recv_sem=self.recv_sems.at[a, k],
            device_id=to, device_id_type=pl.DeviceIdType.MESH)

    def mine(self, a):
        return pltpu.make_async_copy(self.ins[a], self.outs[a].at[4 * self.x + 2 * self.y + self.c], self.local_sems.at[a])

    def first(self, a):
        return [self.copy(a, 0, self.me, self.sibling, src=self.ins[a])] + [
            self.copy(a, 1 + j, self.me, (*chip, self.c), src=self.ins[a]) for j, chip in enumerate(self.chips)]

    def start(self):
        for a in range(len(self.ins)):
            self.mine(a).start()
            for cp in self.first(a):
                cp.start()

    def finish(self):
        c = self.c
        for a in range(len(self.ins)):
            passed = [self.copy(a, 4 + j, (*chip, c), self.sibling) for j, chip in enumerate(self.chips)]
            for j, chip in enumerate(self.chips):
                self.copy(a, 1 + j, (*chip, c), self.me).wait_recv()
                passed[j].start()
            self.copy(a, 0, self.sibling, self.me).wait_recv()
            for j, chip in enumerate(self.chips):
                self.copy(a, 4 + j, (*chip, 1 - c), self.me).wait_recv()
            for cp in self.first(a) + passed:
                cp.wait_send()
            self.mine(a).wait()


def _all_gather(shards, name):
    n = len(shards)
    any_spec = pl.BlockSpec(memory_space=pl.ANY)

    def body(*refs):
        g = _Gather(refs[:n], refs[n:2 * n], *refs[2 * n:])
        g.start()
        g.finish()

    outs = pl.pallas_call(
        body, name=name,
        out_shape=[jax.ShapeDtypeStruct((N_DEV,) + s.shape, s.dtype) for s in shards],
        in_specs=[any_spec] * n, out_specs=[any_spec] * n,
        scratch_shapes=[pltpu.SemaphoreType.DMA((n, 7)), pltpu.SemaphoreType.DMA((n, 7)),
                        pltpu.SemaphoreType.DMA((n,))],
    )(*shards)
    return list(outs)


def _a2a_copies(ins, outs, send_sems, recv_sems, local_sems):
    x, y, c = _mesh_pos()
    me = 4 * x + 2 * y + c
    copies = []
    for a in range(len(ins)):
        copies.append(pltpu.make_async_copy(ins[a].at[me], outs[a].at[me], local_sems.at[a]))
        for k in range(1, N_DEV):
            px = 1 - x if k & 4 else x
            py = 1 - y if k & 2 else y
            pc = 1 - c if k & 1 else c
            copies.append(pltpu.make_async_remote_copy(
                src_ref=ins[a].at[4 * px + 2 * py + pc], dst_ref=outs[a].at[me],
                send_sem=send_sems.at[a, k - 1], recv_sem=recv_sems.at[a, k - 1],
                device_id=(px, py, pc), device_id_type=pl.DeviceIdType.MESH))
    return copies


def _host_call(body, sends, args, *, name, grid, out_shape, in_specs, out_specs, scratch_shapes, compiler_params,
               gather=False):
    single = not isinstance(out_shape, (list, tuple))
    out_shape = [out_shape] if single else list(out_shape)
    out_specs = [out_specs] if single else list(out_specs)
    sends = list(sends or ())
    n, n_in, n_out, n_scr = len(sends), len(args), len(out_shape), len(scratch_shapes)
    if n == 0:
        outs = pl.pallas_call(body, name=name, grid=grid, out_shape=out_shape, in_specs=in_specs, out_specs=out_specs,
                              scratch_shapes=list(scratch_shapes), compiler_params=compiler_params)(*args)
        return (outs[0] if single else list(outs)), []
    any_spec = pl.BlockSpec(memory_space=pl.ANY)

    def hosted(*refs):
        ins, s_in = refs[:n_in], refs[n_in:n_in + n]
        pos = n_in + n
        outs, s_out = refs[pos:pos + n_out], refs[pos + n_out:pos + n_out + n]
        pos += n_out + n
        scr, sems = refs[pos:pos + n_scr], refs[pos + n_scr:]
        ids = [pl.program_id(i) for i in range(len(grid))]
        first = functools.reduce(lambda p, q: p & q, [i == 0 for i in ids])
        last = functools.reduce(lambda p, q: p & q, [i == g - 1 for i, g in zip(ids, grid)])

        @pl.when(first)
        def _():
            if gather:
                _Gather(s_in, s_out, *sems).start()
            else:
                for cp in _a2a_copies(s_in, s_out, *sems):
                    cp.start()

        body(*ins, *outs, *scr)

        @pl.when(last)
        def _():
            if gather:
                _Gather(s_in, s_out, *sems).finish()
            else:
                for cp in _a2a_copies(s_in, s_out, *sems):
                    cp.wait()

    lead = (N_DEV,) if gather else ()
    outs = pl.pallas_call(
        hosted, name=name, grid=grid,
        out_shape=out_shape + [jax.ShapeDtypeStruct(lead + s.shape, s.dtype) for s in sends],
        in_specs=list(in_specs) + [any_spec] * n, out_specs=out_specs + [any_spec] * n,
        scratch_shapes=list(scratch_shapes) + [pltpu.SemaphoreType.DMA((n, 7)), pltpu.SemaphoreType.DMA((n, 7)),
                                               pltpu.SemaphoreType.DMA((n,))],
        compiler_params=compiler_params,
    )(*args, *sends)
    main = list(outs[:n_out])
    return (main[0] if single else main), list(outs[n_out:])


def _matmul(a, b, *, mode="nn", out_dtype=f32, res=None, extras=(), epilogue=None, sends=None, gather=False, name):
    if mode == "tn":
        K, M = a.shape
    else:
        M, K = a.shape
    N = b.shape[0] if mode == "nt" else b.shape[1]
    k_cap = 2048 if (a.dtype == bf16 and b.dtype == bf16) else 1024
    tm, tn, tk = _tile(M, 1024), _tile(N, 1152), _tile(K, k_cap)
    nk = K // tk
    dims = {"nn": (((1,), (0,)), ((), ())), "nt": (((1,), (1,)), ((), ())), "tn": (((0,), (0,)), ((), ()))}[mode]
    out_dtypes = list(out_dtype) if epilogue is not None else [out_dtype]
    n_side = (1 if res is not None else 0) + len(extras)

    def body(*refs):
        a_ref, b_ref = refs[:2]
        r_ref = refs[2] if res is not None else None
        x_refs = refs[2 + n_side - len(extras):2 + n_side]
        o_refs = refs[2 + n_side:2 + n_side + len(out_dtypes)]
        acc = refs[-1] if nk > 1 else None
        k = pl.program_id(2)
        part = lax.dot_general(a_ref[...].astype(bf16), b_ref[...].astype(bf16), dims, preferred_element_type=f32)

        def finish(r):
            if res is not None:
                r = r + r_ref[...]
            outs = epilogue(r, *[x[...] for x in x_refs]) if epilogue is not None else (r,)
            for o_ref, val in zip(o_refs, outs):
                o_ref[...] = val.astype(o_ref.dtype)

        if nk == 1:
            finish(part)
            return

        @pl.when(k == 0)
        def _():
            acc[...] = part

        @pl.when((k > 0) & (k < nk - 1))
        def _():
            acc[...] += part

        @pl.when(k == nk - 1)
        def _():
            finish(acc[...] + part)

    a_spec = pl.BlockSpec((tk, tm), lambda i, j, k: (k, i)) if mode == "tn" else pl.BlockSpec((tm, tk), lambda i, j, k: (i, k))
    b_spec = pl.BlockSpec((tn, tk), lambda i, j, k: (j, k)) if mode == "nt" else pl.BlockSpec((tk, tn), lambda i, j, k: (k, j))
    o_spec = pl.BlockSpec((tm, tn), lambda i, j, k: (i, j))
    in_specs = [a_spec, b_spec] + [o_spec] * n_side
    args = (a, b) + ((res,) if res is not None else ()) + tuple(extras)
    out, recvs = _host_call(
        body, sends, args, name=name, grid=(M // tm, N // tn, nk),
        out_shape=[jax.ShapeDtypeStruct((M, N), dt) for dt in out_dtypes],
        in_specs=in_specs, out_specs=[o_spec] * len(out_dtypes),
        scratch_shapes=[pltpu.VMEM((tm, tn), f32)] if nk > 1 else [],
        compiler_params=_params(("parallel", "parallel", "arbitrary")), gather=gather)
    if epilogue is None:
        out = out[0]
    return out if sends is None else (out, recvs)


def _rms(x, g):
    return x * lax.rsqrt(jnp.mean(x * x, axis=-1, keepdims=True) + NORM_EPS) * g


def _rms_fwd(x, g, name):
    L, D = x.shape
    tm = _tile(L, 256, 8)

    def body(x_ref, g_ref, o_ref):
        o_ref[...] = _rms(x_ref[...], g_ref[...]).astype(bf16)

    return pl.pallas_call(
        body, name=name, grid=(L // tm,), out_shape=jax.ShapeDtypeStruct((L, D), bf16),
        in_specs=[pl.BlockSpec((tm, D), lambda i: (i, 0)), pl.BlockSpec((1, D), lambda i: (0, 0))],
        out_specs=pl.BlockSpec((tm, D), lambda i: (i, 0)),
        compiler_params=_params(("parallel",)),
    )(x, g)


def _rms_bwd(x, g, dh, dres, name):
    L, D = x.shape
    tm = _tile(L, 256, 8)

    def body(x_ref, g_ref, dh_ref, dres_ref, dx_ref, dxb_ref, dg_ref):
        _, vjp = jax.vjp(_rms, x_ref[...], g_ref[...])
        dx, dg = vjp(dh_ref[...])
        dx = dres_ref[...] + dx
        dx_ref[...] = dx
        dxb_ref[...] = dx.astype(bf16)

        @pl.when(pl.program_id(0) == 0)
        def _():
            dg_ref[...] = jnp.zeros_like(dg_ref)

        dg_ref[...] += dg

    row = pl.BlockSpec((tm, D), lambda i: (i, 0))
    vec = pl.BlockSpec((1, D), lambda i: (0, 0))
    return pl.pallas_call(
        body, name=name, grid=(L // tm,),
        out_shape=[jax.ShapeDtypeStruct((L, D), f32), jax.ShapeDtypeStruct((L, D), bf16), jax.ShapeDtypeStruct((1, D), f32)],
        in_specs=[row, vec, row, row], out_specs=[row, row, vec],
        compiler_params=_params(("arbitrary",)),
    )(x, g, dh, dres)


def _final_loss(x, g, target, name):
    L, D = x.shape
    tm = _tile(L, 256, 8)

    def body(x_ref, g_ref, t_ref, loss_ref, dx_ref, dxb_ref, dg_ref):
        tgt = t_ref[...]

        def f(xv, gv):
            err = _rms(xv, gv) - tgt
            return 0.5 * jnp.sum(jnp.mean(err * err, axis=-1, keepdims=True), axis=0, keepdims=True)

        val, vjp = jax.vjp(f, x_ref[...], g_ref[...])
        dx, dg = vjp(jnp.ones((1, 1), f32))
        dx_ref[...] = dx
        dxb_ref[...] = dx.astype(bf16)

        @pl.when(pl.program_id(0) == 0)
        def _():
            dg_ref[...] = jnp.zeros_like(dg_ref)
            loss_ref[...] = jnp.zeros_like(loss_ref)

        dg_ref[...] += dg
        loss_ref[...] += jnp.broadcast_to(val, loss_ref.shape)

    row = pl.BlockSpec((tm, D), lambda i: (i, 0))
    vec = pl.BlockSpec((1, D), lambda i: (0, 0))
    acc = pl.BlockSpec((8, 128), lambda i: (0, 0))
    loss, dx, dxb, dg = pl.pallas_call(
        body, name=name, grid=(L // tm,),
        out_shape=[jax.ShapeDtypeStruct((8, 128), f32), jax.ShapeDtypeStruct((L, D), f32), jax.ShapeDtypeStruct((L, D), bf16),
                   jax.ShapeDtypeStruct((1, D), f32)],
        in_specs=[row, vec, row], out_specs=[acc, row, row, vec],
        compiler_params=_params(("arbitrary",)),
    )(x, g, target)
    return loss[0, 0], dx, dxb, dg


def _swiglu_f(a, b):
    return jax.nn.silu(a) * b


def _swiglu_epilogue(up, gate):
    return up, _swiglu_f(gate.astype(f32), up)


def _swiglu_grad_epilogue(dact, gate, up):
    _, vjp = jax.vjp(_swiglu_f, gate.astype(f32), up.astype(f32))
    return vjp(dact)


def _merge_f(g0, g1, g2, g3, y0, y1, y2, y3):
    s = jax.nn.sigmoid
    return s(g0) * y0 + s(g1) * y1 + s(g2) * y2 + s(g3) * y3


def _merge_fwd(proj, ys, D, name):
    L = proj.shape[0]
    tm, tn = _tile(L, 512, 8), _tile(D, 512)
    nj = D // tn

    def body(g0, g1, g2, g3, y0, y1, y2, y3, o_ref):
        o_ref[...] = _merge_f(g0[...], g1[...], g2[...], g3[...], *[y[...].astype(f32) for y in (y0, y1, y2, y3)]).astype(bf16)

    gspecs = [pl.BlockSpec((tm, tn), functools.partial(lambda i, j, m: (i, m * nj + j), m=m)) for m in range(4)]
    blk = pl.BlockSpec((tm, tn), lambda i, j: (i, j))
    return pl.pallas_call(
        body, name=name, grid=(L // tm, nj), out_shape=jax.ShapeDtypeStruct((L, D), bf16),
        in_specs=gspecs + [blk] * 4, out_specs=blk, compiler_params=_params(("parallel", "parallel")),
    )(proj, proj, proj, proj, *ys)


def _merge_bwd(proj, ys, dmixed, D, name):
    L = proj.shape[0]
    tm, tn = _tile(L, 512, 8), _tile(D, 512)
    nj = D // tn

    def body(g0, g1, g2, g3, y0, y1, y2, y3, d_ref, dy0, dy1, dy2, dy3, dg_ref):
        _, vjp = jax.vjp(_merge_f, g0[...], g1[...], g2[...], g3[...], *[y[...].astype(f32) for y in (y0, y1, y2, y3)])
        grads = vjp(d_ref[...])
        for m, r in enumerate((dy0, dy1, dy2, dy3)):
            r[...] = grads[4 + m].astype(bf16)
        for m in range(4):
            dg_ref[m] = grads[m].astype(bf16)

    gspecs = [pl.BlockSpec((tm, tn), functools.partial(lambda i, j, m: (i, m * nj + j), m=m)) for m in range(4)]
    blk = pl.BlockSpec((tm, tn), lambda i, j: (i, j))
    dgspec = pl.BlockSpec((4, tm, tn), lambda i, j: (0, i, j))
    outs = pl.pallas_call(
        body, name=name, grid=(L // tm, nj),
        out_shape=[jax.ShapeDtypeStruct((L, D), bf16)] * 4 + [jax.ShapeDtypeStruct((4, L, D), bf16)],
        in_specs=gspecs + [blk] * 5, out_specs=[blk] * 4 + [dgspec],
        compiler_params=_params(("parallel", "parallel")),
    )(proj, proj, proj, proj, *ys, dmixed)
    return outs[:4], outs[4]


def _gla_head(q, k, v, r, glr, st, a2, ab, ng):
    C, T = GLA_CHUNK, GLA_SUB
    row = lax.broadcasted_iota(jnp.int32, (C, C), 0)
    col = lax.broadcasted_iota(jnp.int32, (C, C), 1)
    tri = (col <= row).astype(f32)
    sel = (col == (row // T) * T).astype(f32)
    z = jnp.dot(glr, a2, preferred_element_type=f32) + ab
    g = jax.nn.log_sigmoid(z) / GLA_TAU
    cum = jnp.dot(tri, g, precision=HI, preferred_element_type=f32)
    excl = cum - g
    ref = jnp.dot(sel, excl, precision=HI, preferred_element_type=f32)
    qs = q * (GLA_DK ** -0.5)
    q_ref = qs * jnp.exp(cum - ref)
    rowk = lax.broadcasted_iota(jnp.int32, (C, GLA_DK), 0)
    a = jnp.zeros((C, C), f32)
    for s in range(1, C // T):
        ref_s = jnp.sum(jnp.where(rowk == s * T, excl, 0.0), axis=0, keepdims=True)
        k_ref = k * jnp.exp(jnp.where(rowk < s * T, ref_s - cum, -jnp.inf))
        a_s = lax.dot_general(q_ref, k_ref, (((1,), (1,)), ((), ())), preferred_element_type=f32)
        a = a + jnp.where(row // T == s, a_s, 0.0)
    o = jnp.dot(a, v, preferred_element_type=f32)
    sub = rowk % T
    for d in range(T):
        ks = _sroll(k, d) if d else k
        cs = _sroll(cum, d) if d else cum
        vs = _sroll(v, d) if d else v
        w = jnp.sum(qs * ks * jnp.exp(jnp.where(sub >= d, cum - cs, -jnp.inf)), axis=-1, keepdims=True)
        o = o + w * vs
    o = o + lax.dot_general(qs * jnp.exp(cum), st, (((1,), (1,)), ((), ())), preferred_element_type=f32)
    last = jnp.sum(jnp.where(rowk == C - 1, cum, 0.0), axis=0, keepdims=True)
    st_new = st * jnp.exp(last) + lax.dot_general(v, k * jnp.exp(last - cum), (((0,), (0,)), ((), ())),
                                                  preferred_element_type=f32)
    out = _rms(o, ng) * jax.nn.silu(r)
    return out, st_new


def _gla_specs(L, base128, rev):
    n = L // GLA_CHUNK
    ch = (lambda i: n - 1 - i) if rev else (lambda i: i)
    b = base128
    return n, ch, [
        pl.BlockSpec((GLA_CHUNK, 256), lambda i: (ch(i), (b + P_GQ // 128) // 2)),
        pl.BlockSpec((GLA_CHUNK, 256), lambda i: (ch(i), (b + P_GK // 128) // 2)),
        pl.BlockSpec((GLA_CHUNK, 512), lambda i: (ch(i), (b + P_GV // 128) // 4)),
        pl.BlockSpec((GLA_CHUNK, 512), lambda i: (ch(i), (b + P_GR // 128) // 4)),
        pl.BlockSpec((GLA_CHUNK, 128), lambda i: (ch(i), b + P_GLR // 128)),
    ]


def _gla_fwd(proj, a2p, ab, ng, base128, name, shards=None):
    L = proj.shape[0]
    n, _, pspecs = _gla_specs(L, base128, False)
    H, DK, DV = GLA_HEADS, GLA_DK, GLA_DV

    def body(q_ref, k_ref, v_ref, r_ref, l_ref, a2_ref, ab_ref, ng_ref, o_ref, st_ref, state):
        @pl.when(pl.program_id(0) == 0)
        def _():
            state[...] = jnp.zeros_like(state)

        glr = l_ref[...]
        states = [state[h] for h in range(H)]
        res = []
        for h in range(H):
            kk, vv = slice(h * DK, (h + 1) * DK), slice(h * DV, (h + 1) * DV)
            res.append(_gla_head(q_ref[:, kk], k_ref[:, kk], v_ref[:, vv], r_ref[:, vv], glr, states[h],
                                 a2_ref[:, kk], ab_ref[:, kk], ng_ref[:, vv]))
        for h in range(H):
            st_ref[0, h] = states[h]
            o_ref[:, h * DV:(h + 1) * DV] = res[h][0].astype(bf16)
            state[h] = res[h][1]

    full = lambda shape: pl.BlockSpec(shape, lambda i: (0,) * len(shape))
    return _host_call(
        body, shards, (proj, proj, proj, proj, proj, a2p, ab, ng), name=name, grid=(n,),
        out_shape=[jax.ShapeDtypeStruct((L, H * DV), bf16), jax.ShapeDtypeStruct((n, H, DV, DK), f32)],
        in_specs=pspecs + [full((128, 256)), full((1, 256)), full((1, 512))],
        out_specs=[pl.BlockSpec((GLA_CHUNK, 512), lambda i: (i, 0)), pl.BlockSpec((1, H, DV, DK), lambda i: (i, 0, 0, 0))],
        scratch_shapes=[pltpu.VMEM((H, DV, DK), f32)],
        compiler_params=_params(("arbitrary",)), gather=True)


def _gla_bwd(proj, a2p, ab, ng, states, dout, base128, name, sends=None):
    L = proj.shape[0]
    n, ch, pspecs = _gla_specs(L, base128, True)
    H, DK, DV = GLA_HEADS, GLA_DK, GLA_DV

    def body(q_ref, k_ref, v_ref, r_ref, l_ref, a2_ref, ab_ref, ng_ref, st_ref, do_ref,
             dq_ref, dk_ref, dv_ref, dr_ref, dl_ref, da2_ref, dab_ref, dng_ref, dstate):
        @pl.when(pl.program_id(0) == 0)
        def _():
            dstate[...] = jnp.zeros_like(dstate)
            da2_ref[...] = jnp.zeros_like(da2_ref)
            dab_ref[...] = jnp.zeros_like(dab_ref)
            dng_ref[...] = jnp.zeros_like(dng_ref)

        glr = l_ref[...]
        dglr = jnp.zeros(glr.shape, f32)
        vjps = []
        for h in range(H):
            kk, vv = slice(h * DK, (h + 1) * DK), slice(h * DV, (h + 1) * DV)
            vjps.append(jax.vjp(_gla_head, q_ref[:, kk], k_ref[:, kk], v_ref[:, vv], r_ref[:, vv], glr, st_ref[0, h],
                                a2_ref[:, kk], ab_ref[:, kk], ng_ref[:, vv])[1])
        cots = [(do_ref[:, h * DV:(h + 1) * DV].astype(f32), dstate[h]) for h in range(H)]
        grads = [vjps[h](cots[h]) for h in range(H)]
        for h in range(H):
            kk, vv = slice(h * DK, (h + 1) * DK), slice(h * DV, (h + 1) * DV)
            dq, dk, dv, dr, dl, dst, da2, dab, dng = grads[h]
            dq_ref[:, kk] = dq.astype(bf16)
            dk_ref[:, kk] = dk.astype(bf16)
            dv_ref[:, vv] = dv.astype(bf16)
            dr_ref[:, vv] = dr.astype(bf16)
            dglr = dglr + dl
            dstate[h] = dst
            da2_ref[:, kk] += da2
            dab_ref[:, kk] += dab
            dng_ref[:, vv] += dng
        dl_ref[...] = dglr.astype(bf16)

    full = lambda shape: pl.BlockSpec(shape, lambda i: (0,) * len(shape))
    rowspec = lambda w: pl.BlockSpec((GLA_CHUNK, w), lambda i: (ch(i), 0))
    return _host_call(
        body, sends, (proj, proj, proj, proj, proj, a2p, ab, ng, states, dout), name=name, grid=(n,),
        out_shape=[jax.ShapeDtypeStruct((L, 256), bf16), jax.ShapeDtypeStruct((L, 256), bf16),
                   jax.ShapeDtypeStruct((L, 512), bf16), jax.ShapeDtypeStruct((L, 512), bf16),
                   jax.ShapeDtypeStruct((L, 128), bf16), jax.ShapeDtypeStruct((128, 256), f32),
                   jax.ShapeDtypeStruct((1, 256), f32), jax.ShapeDtypeStruct((1, 512), f32)],
        in_specs=pspecs + [full((128, 256)), full((1, 256)), full((1, 512)),
                           pl.BlockSpec((1, H, DV, DK), lambda i: (ch(i), 0, 0, 0)), rowspec(512)],
        out_specs=[rowspec(256), rowspec(256), rowspec(512), rowspec(512), rowspec(128),
                   full((128, 256)), full((1, 256)), full((1, 512))],
        scratch_shapes=[pltpu.VMEM((H, DV, DK), f32)],
        compiler_params=_params(("arbitrary",)))


def _s5_prep(lam_re, lam_im, log_dt, b_re, b_im, c_re, c_im, d):
    G, N, Cn = S5_GROUPS, S5_STATE, S5_GROUP
    J, GB = S5_LANE_BLOCKS, S5_GROUPS // S5_LANE_BLOCKS
    dt = jnp.exp(log_dt)[:, None]
    mag = jnp.exp(lam_re * dt)
    ab_re, ab_im = mag * jnp.cos(lam_im * dt), mag * jnp.sin(lam_im * dt)
    den = lam_re * lam_re + lam_im * lam_im
    z_re = ((ab_re - 1.0) * lam_re + ab_im * lam_im) / den
    z_im = (ab_im * lam_re - (ab_re - 1.0) * lam_im) / den
    bb_re = z_re[..., None] * b_re - z_im[..., None] * b_im
    bb_im = z_re[..., None] * b_im + z_im[..., None] * b_re
    eye = jnp.eye(GB, dtype=f32)

    def in_blocks(bb):
        return jnp.einsum("jgnc,gh->jgchn", bb.reshape(J, GB, N, Cn), eye).reshape(J, GB * Cn, GB * N)

    def out_blocks(cc):
        return jnp.einsum("jgcn,gh->jgnhc", cc.reshape(J, GB, Cn, N), eye).reshape(J, GB * N, GB * Cn)

    return (ab_re.reshape(1, G * N), ab_im.reshape(1, G * N), in_blocks(bb_re), in_blocks(bb_im),
            out_blocks(c_re), out_blocks(c_im), d.reshape(1, G * Cn))


def _s5_chunk(u, hin_r, hin_i, a_r, a_i, bb_r, bb_i, cc_r, cc_i, dvec):
    T = u.shape[0]
    hr = jnp.dot(u, bb_r, preferred_element_type=f32)
    hi = jnp.dot(u, bb_i, preferred_element_type=f32)
    row = lax.broadcasted_iota(jnp.int32, hr.shape, 0)
    hr = hr + jnp.where(row == 0, a_r * hin_r - a_i * hin_i, 0.0)
    hi = hi + jnp.where(row == 0, a_r * hin_i + a_i * hin_r, 0.0)
    pr, pi = a_r, a_i
    d = 1
    while d < T:
        sr = jnp.where(row >= d, _sroll(hr, d), 0.0)
        si = jnp.where(row >= d, _sroll(hi, d), 0.0)
        hr, hi = hr + pr * sr - pi * si, hi + pr * si + pi * sr
        pr, pi = pr * pr - pi * pi, 2.0 * pr * pi
        d *= 2
    y = (jnp.dot(hr, cc_r, preferred_element_type=f32)
         - jnp.dot(hi, cc_i, preferred_element_type=f32) + dvec * u)
    out_r = jnp.sum(jnp.where(row == T - 1, hr, 0.0), axis=0, keepdims=True)
    out_i = jnp.sum(jnp.where(row == T - 1, hi, 0.0), axis=0, keepdims=True)
    return y, out_r, out_i


def _s5_specs(L, base128, rev):
    T, J = S5_CHUNK, S5_LANE_BLOCKS
    n = L // T
    ch = (lambda c: n - 1 - c) if rev else (lambda c: c)
    ub = base128 + P_S5U // 128
    specs = [
        pl.BlockSpec((T, 128), lambda j, c: (ch(c), ub + j)),
        pl.BlockSpec((1, 512), lambda j, c: (0, j)), pl.BlockSpec((1, 512), lambda j, c: (0, j)),
        pl.BlockSpec((None, 128, 512), lambda j, c: (j, 0, 0)), pl.BlockSpec((None, 128, 512), lambda j, c: (j, 0, 0)),
        pl.BlockSpec((None, 512, 128), lambda j, c: (j, 0, 0)), pl.BlockSpec((None, 512, 128), lambda j, c: (j, 0, 0)),
        pl.BlockSpec((1, 128), lambda j, c: (0, j)),
    ]
    return n, ch, specs


def _s5_fwd(proj, prep, base128, name, shards=None):
    L = proj.shape[0]
    T, J = S5_CHUNK, S5_LANE_BLOCKS
    n, _, specs = _s5_specs(L, base128, False)

    def body(u_ref, ar, ai, bbr, bbi, ccr, cci, dv, y_ref, sr_ref, si_ref, carry):
        @pl.when(pl.program_id(1) == 0)
        def _():
            carry[...] = jnp.zeros_like(carry)

        hin_r, hin_i = carry[0:1, :], carry[1:2, :]
        sr_ref[0] = jnp.broadcast_to(hin_r, (8, 512))
        si_ref[0] = jnp.broadcast_to(hin_i, (8, 512))
        y, out_r, out_i = _s5_chunk(u_ref[...], hin_r, hin_i, ar[...], ai[...], bbr[...], bbi[...], ccr[...], cci[...], dv[...])
        y_ref[...] = y
        carry[0:1, :] = out_r
        carry[1:2, :] = out_i

    st = pl.BlockSpec((1, 8, 512), lambda j, c: (c, 0, j))
    return _host_call(
        body, shards, (proj, *prep), name=name, grid=(J, n),
        out_shape=[jax.ShapeDtypeStruct((L, 512), f32), jax.ShapeDtypeStruct((n, 8, 2048), f32), jax.ShapeDtypeStruct((n, 8, 2048), f32)],
        in_specs=specs, out_specs=[pl.BlockSpec((T, 128), lambda j, c: (c, j)), st, st],
        scratch_shapes=[pltpu.VMEM((8, 512), f32)],
        compiler_params=_params(("parallel", "arbitrary")), gather=True)


def _s5_bwd(proj, prep, st_r, st_i, dy, base128, name, sends=None):
    L = proj.shape[0]
    T, J = S5_CHUNK, S5_LANE_BLOCKS
    n, ch, specs = _s5_specs(L, base128, True)

    def body(u_ref, ar, ai, bbr, bbi, ccr, cci, dv, sr_ref, si_ref, dy_ref,
             du_ref, dar, dai, dbbr, dbbi, dccr, dcci, ddv, dcarry):
        @pl.when(pl.program_id(1) == 0)
        def _():
            dcarry[...] = jnp.zeros_like(dcarry)
            for r in (dar, dai, dbbr, dbbi, dccr, dcci, ddv):
                r[...] = jnp.zeros_like(r)

        _, vjp = jax.vjp(_s5_chunk, u_ref[...], sr_ref[0, 0:1, :], si_ref[0, 0:1, :], ar[...], ai[...],
                         bbr[...], bbi[...], ccr[...], cci[...], dv[...])
        g = vjp((dy_ref[...], dcarry[0:1, :], dcarry[1:2, :]))
        du_ref[...] = g[0].astype(bf16)
        dcarry[0:1, :] = g[1]
        dcarry[1:2, :] = g[2]
        for r, val in zip((dar, dai, dbbr, dbbi, dccr, dcci, ddv), g[3:]):
            r[...] += val

    st = pl.BlockSpec((1, 8, 512), lambda j, c: (ch(c), 0, j))
    outs, recvs = _host_call(
        body, sends, (proj, *prep, st_r, st_i, dy), name=name, grid=(J, n),
        out_shape=[jax.ShapeDtypeStruct((L, 512), bf16),
                   jax.ShapeDtypeStruct((1, 2048), f32), jax.ShapeDtypeStruct((1, 2048), f32),
                   jax.ShapeDtypeStruct((J, 128, 512), f32), jax.ShapeDtypeStruct((J, 128, 512), f32),
                   jax.ShapeDtypeStruct((J, 512, 128), f32), jax.ShapeDtypeStruct((J, 512, 128), f32),
                   jax.ShapeDtypeStruct((1, 512), f32)],
        in_specs=specs + [st, st, pl.BlockSpec((T, 128), lambda j, c: (ch(c), j))],
        out_specs=[pl.BlockSpec((T, 128), lambda j, c: (ch(c), j))] + specs[1:],
        scratch_shapes=[pltpu.VMEM((8, 512), f32)],
        compiler_params=_params(("parallel", "arbitrary")))
    return outs[0], tuple(outs[1:]), recvs


def _glu_f(y, w, b):
    z = jax.nn.gelu(y)
    return z * jax.nn.sigmoid(jnp.dot(z.astype(bf16), w.astype(bf16), preferred_element_type=f32) + b)


def _glu_fwd(y, w, b, name):
    L = y.shape[0]
    tm = _tile(L, 512, 8)

    def body(y_ref, w_ref, b_ref, o_ref):
        o_ref[...] = _glu_f(y_ref[...], w_ref[...], b_ref[...]).astype(bf16)

    row = pl.BlockSpec((tm, 512), lambda i: (i, 0))
    return pl.pallas_call(
        body, name=name, grid=(L // tm,), out_shape=jax.ShapeDtypeStruct((L, 512), bf16),
        in_specs=[row, pl.BlockSpec((512, 512), lambda i: (0, 0)), pl.BlockSpec((1, 512), lambda i: (0, 0))],
        out_specs=row, compiler_params=_params(("parallel",)),
    )(y, w, b)


def _glu_bwd(y, w, b, dout, name):
    L = y.shape[0]
    tm = _tile(L, 512, 8)

    def body(y_ref, w_ref, b_ref, do_ref, dy_ref, dw_ref, db_ref):
        @pl.when(pl.program_id(0) == 0)
        def _():
            dw_ref[...] = jnp.zeros_like(dw_ref)
            db_ref[...] = jnp.zeros_like(db_ref)

        _, vjp = jax.vjp(_glu_f, y_ref[...], w_ref[...], b_ref[...])
        dy, dw, db = vjp(do_ref[...])
        dy_ref[...] = dy
        dw_ref[...] += dw
        db_ref[...] += db

    row = pl.BlockSpec((tm, 512), lambda i: (i, 0))
    wspec, bspec = pl.BlockSpec((512, 512), lambda i: (0, 0)), pl.BlockSpec((1, 512), lambda i: (0, 0))
    return pl.pallas_call(
        body, name=name, grid=(L // tm,),
        out_shape=[jax.ShapeDtypeStruct((L, 512), f32), jax.ShapeDtypeStruct((512, 512), f32), jax.ShapeDtypeStruct((1, 512), f32)],
        in_specs=[row, wspec, bspec, row], out_specs=[row, wspec, bspec],
        compiler_params=_params(("arbitrary",)),
    )(y, w, b, dout)


def _rope_tables(positions):
    half = ROPE_DIM // 2
    inv_freq = ROPE_THETA ** (-jnp.arange(half, dtype=f32) / half)
    ang = positions.astype(f32)[:, None] * inv_freq
    L = positions.shape[0]
    cos = jnp.concatenate([jnp.cos(ang), jnp.cos(ang), jnp.ones((L, HEAD_DIM - ROPE_DIM), f32)], axis=1)
    sin = jnp.concatenate([jnp.sin(ang), jnp.sin(ang), jnp.zeros((L, HEAD_DIM - ROPE_DIM), f32)], axis=1)
    return jnp.tile(cos, (1, 8)), jnp.tile(sin, (1, 8))


def _rope_matrix(w):
    half = ROPE_DIM // 2
    r = lax.broadcasted_iota(jnp.int32, (w, w), 0)
    c = lax.broadcasted_iota(jnp.int32, (w, w), 1)
    same = (r // HEAD_DIM) == (c // HEAD_DIM)
    rr, cc = r % HEAD_DIM, c % HEAD_DIM
    return (jnp.where(same & (cc < half) & (rr == cc + half), -1.0, 0.0)
            + jnp.where(same & (cc >= half) & (cc < ROPE_DIM) & (rr == cc - half), 1.0, 0.0))


def _rope(items, cos, sin, *, transpose, out_dtype, name):
    L = cos.shape[0]
    tm = _tile(L, 512, 8)
    n = len(items)

    def body(*refs):
        xs, c_ref, s_ref, outs = refs[:n], refs[n], refs[n + 1], refs[n + 2:]
        for x_ref, o_ref in zip(xs, outs):
            w = x_ref.shape[1]
            x = x_ref[...].astype(f32)
            c, s = c_ref[:, :w], s_ref[:, :w]
            rot = _rope_matrix(w)
            if transpose:
                y = x * c + lax.dot_general(x * s, rot, (((1,), (1,)), ((), ())), preferred_element_type=f32)
            else:
                y = x * c + jnp.dot(x, rot, preferred_element_type=f32) * s
            o_ref[...] = y.astype(o_ref.dtype)

    in_specs = [pl.BlockSpec((tm, w), functools.partial(lambda i, col: (i, col), col=col)) for _, col, w in items]
    tab = pl.BlockSpec((tm, 512), lambda i: (i, 0))
    outs = pl.pallas_call(
        body, name=name, grid=(L // tm,),
        out_shape=[jax.ShapeDtypeStruct((L, w), out_dtype) for _, _, w in items],
        in_specs=in_specs + [tab, tab], out_specs=[pl.BlockSpec((tm, w), lambda i: (i, 0)) for _, _, w in items],
        compiler_params=_params(("parallel",)),
    )(*[a for a, _, _ in items], cos, sin)
    return list(outs)


def _attn_head(q, kp, kc, vp, vc, sink, *, lim, max_dist):
    T = ATT_BLOCK
    k2 = jnp.concatenate([kp, kc], axis=0)
    v2 = jnp.concatenate([vp, vc], axis=0)
    s = lax.dot_general(q, k2, (((1,), (1,)), ((), ())), preferred_element_type=f32) * (HEAD_DIM ** -0.5)
    t = lax.broadcasted_iota(jnp.int32, (T, 2 * T), 0)
    j = lax.broadcasted_iota(jnp.int32, (T, 2 * T), 1)
    dist = T + t - j
    valid = (dist >= 0) & (dist <= max_dist) & (j >= lim)
    s = jnp.where(valid, s, -jnp.inf)
    m = lax.stop_gradient(jnp.max(s, axis=-1, keepdims=True))
    p = jnp.exp(s - m)
    den = jnp.sum(p, axis=-1, keepdims=True)
    o = jnp.dot(p, v2, preferred_element_type=f32) / den
    lse = jnp.broadcast_to(m + jnp.log(den), (T, HEAD_DIM))
    if sink is None:
        return o, lse
    return o * jax.nn.sigmoid(lse - sink)


def _attn_specs(L, q_col, k_col, v_col, wk, rev):
    T = ATT_BLOCK
    n = L // T
    blk = (lambda i: n - 1 - i) if rev else (lambda i: i)
    prev = lambda i: jnp.maximum(blk(i) - 1, 0)
    specs = [
        pl.BlockSpec((T, 512), lambda i: (blk(i), q_col)),
        pl.BlockSpec((T, wk), lambda i: (prev(i), k_col)), pl.BlockSpec((T, wk), lambda i: (blk(i), k_col)),
        pl.BlockSpec((T, wk), lambda i: (prev(i), v_col)), pl.BlockSpec((T, wk), lambda i: (blk(i), v_col)),
    ]
    return n, blk, specs


def _attn_fwd(qa, ka, va, sinks, *, q_col, k_col, v_col, hkv, nbc, max_dist, name, shards=None):
    L = qa.shape[0]
    T, HQ, HD = ATT_BLOCK, 8, HEAD_DIM
    wk = hkv * HD
    n, _, specs = _attn_specs(L, q_col, k_col, v_col, wk, False)
    grp = HQ // hkv
    gated = sinks is not None

    def body(*refs):
        q_ref, kp_ref, kc_ref, vp_ref, vc_ref = refs[:5]
        rest = refs[5:]
        lim = jnp.where(pl.program_id(0) % nbc == 0, T, 0)
        results = []
        for h in range(HQ):
            hs, ks = slice(h * HD, (h + 1) * HD), slice((h // grp) * HD, (h // grp + 1) * HD)
            results.append(_attn_head(q_ref[:, hs], kp_ref[:, ks], kc_ref[:, ks], vp_ref[:, ks], vc_ref[:, ks],
                                      rest[0][:, hs] if gated else None, lim=lim, max_dist=max_dist))
        for h, res in enumerate(results):
            hs = slice(h * HD, (h + 1) * HD)
            if gated:
                rest[1][:, hs] = res.astype(bf16)
            else:
                rest[0][:, hs] = res[0]
                rest[1][:, hs] = res[1]

    row = pl.BlockSpec((T, 512), lambda i: (i, 0))
    if gated:
        return _host_call(
            body, shards, (qa, ka, ka, va, va, sinks), name=name, grid=(n,),
            out_shape=jax.ShapeDtypeStruct((L, 512), bf16),
            in_specs=specs + [pl.BlockSpec((1, 512), lambda i: (0, 0))], out_specs=row, scratch_shapes=[],
            compiler_params=_params(("parallel",)), gather=True)
    return _host_call(
        body, shards, (qa, ka, ka, va, va), name=name, grid=(n,),
        out_shape=[jax.ShapeDtypeStruct((L, 512), f32)] * 2,
        in_specs=specs, out_specs=[row, row], scratch_shapes=[], compiler_params=_params(("parallel",)), gather=True)


def _attn_bwd(qa, ka, va, sinks, douts, *, q_col, k_col, v_col, hkv, nbc, max_dist, name, sends=None):
    L = qa.shape[0]
    T, HQ, HD = ATT_BLOCK, 8, HEAD_DIM
    wk = hkv * HD
    n, blk, specs = _attn_specs(L, q_col, k_col, v_col, wk, True)
    grp = HQ // hkv
    gated = sinks is not None
    nd = len(douts)

    def body(*refs):
        q_ref, kp_ref, kc_ref, vp_ref, vc_ref = refs[:5]
        pos = 5
        sink_ref = None
        if gated:
            sink_ref = refs[pos]
            pos += 1
        d_refs = refs[pos:pos + nd]
        pos += nd
        dq_ref, dk_ref, dv_ref = refs[pos:pos + 3]
        pos += 3
        dsink_ref = None
        if gated:
            dsink_ref = refs[pos]
            pos += 1
        carry_k, carry_v = refs[pos:pos + 2]

        @pl.when(pl.program_id(0) == 0)
        def _():
            carry_k[...] = jnp.zeros_like(carry_k)
            carry_v[...] = jnp.zeros_like(carry_v)
            if gated:
                dsink_ref[...] = jnp.zeros_like(dsink_ref)

        lim = jnp.where(blk(pl.program_id(0)) % nbc == 0, T, 0)
        dkp = [jnp.zeros((T, HD), f32) for _ in range(hkv)]
        dkc = [jnp.zeros((T, HD), f32) for _ in range(hkv)]
        dvp = [jnp.zeros((T, HD), f32) for _ in range(hkv)]
        dvc = [jnp.zeros((T, HD), f32) for _ in range(hkv)]
        dqs, dsinks = [], []
        for h in range(HQ):
            g = h // grp
            hs, ks = slice(h * HD, (h + 1) * HD), slice(g * HD, (g + 1) * HD)
            fn = functools.partial(_attn_head, lim=lim, max_dist=max_dist)
            prim = (q_ref[:, hs], kp_ref[:, ks], kc_ref[:, ks], vp_ref[:, ks], vc_ref[:, ks])
            if gated:
                _, vjp = jax.vjp(fn, *prim, sink_ref[:, hs])
                dq, a, b, c, d, ds = vjp(d_refs[0][:, hs].astype(f32))
                dsinks.append(ds)
            else:
                _, vjp = jax.vjp(lambda *p: fn(*p, None), *prim)
                dq, a, b, c, d = vjp((d_refs[0][:, hs], d_refs[1][:, hs]))
            dqs.append(dq)
            dkp[g], dkc[g], dvp[g], dvc[g] = dkp[g] + a, dkc[g] + b, dvp[g] + c, dvc[g] + d
        for h in range(HQ):
            hs = slice(h * HD, (h + 1) * HD)
            dq_ref[:, hs] = dqs[h].astype(bf16)
            if gated:
                dsink_ref[:, hs] += dsinks[h]
        for g in range(hkv):
            ks = slice(g * HD, (g + 1) * HD)
            dk_ref[:, ks] = (dkc[g] + carry_k[:, ks]).astype(bf16)
            dv_ref[:, ks] = (dvc[g] + carry_v[:, ks]).astype(bf16)
            carry_k[:, ks] = dkp[g]
            carry_v[:, ks] = dvp[g]

    row = lambda w: pl.BlockSpec((T, w), lambda i: (blk(i), 0))
    vec = pl.BlockSpec((1, 512), lambda i: (0, 0))
    in_specs = specs + ([vec] if gated else []) + [row(512)] * nd
    out_shape = [jax.ShapeDtypeStruct((L, 512), bf16), jax.ShapeDtypeStruct((L, wk), bf16), jax.ShapeDtypeStruct((L, wk), bf16)]
    out_specs = [row(512), row(wk), row(wk)]
    if gated:
        out_shape.append(jax.ShapeDtypeStruct((1, 512), f32))
        out_specs.append(vec)
    args = (qa, ka, ka, va, va) + ((sinks,) if gated else ()) + tuple(douts)
    outs, recvs = _host_call(
        body, sends, args, name=name, grid=(n,), out_shape=out_shape, in_specs=in_specs, out_specs=out_specs,
        scratch_shapes=[pltpu.VMEM((T, wk), f32), pltpu.VMEM((T, wk), f32)],
        compiler_params=_params(("arbitrary",)))
    return outs if sends is None else (outs, recvs)


def _dilmix_f(o0, o1, o2, l0, l1, l2):
    m = jnp.maximum(jnp.maximum(l0, l1), l2)
    e0, e1, e2 = jnp.exp(l0 - m), jnp.exp(l1 - m), jnp.exp(l2 - m)
    return (e0 * o0 + e1 * o1 + e2 * o2) / (e0 + e1 + e2)


def _dilmix_fwd(os_, ls, name):
    L = os_[0].shape[0]
    tm = _tile(L, 512, 8)

    def body(o0, o1, o2, l0, l1, l2, out):
        out[...] = _dilmix_f(o0[...], o1[...], o2[...], l0[...], l1[...], l2[...]).astype(bf16)

    row = pl.BlockSpec((tm, 512), lambda i: (i, 0))
    return pl.pallas_call(
        body, name=name, grid=(L // tm,), out_shape=jax.ShapeDtypeStruct((L, 512), bf16),
        in_specs=[row] * 6, out_specs=row, compiler_params=_params(("parallel",)),
    )(*os_, *ls)


def _dilmix_bwd(os_, ls, dout, name):
    L = os_[0].shape[0]
    tm = _tile(L, 512, 8)

    def body(o0, o1, o2, l0, l1, l2, d, *outs):
        _, vjp = jax.vjp(_dilmix_f, o0[...], o1[...], o2[...], l0[...], l1[...], l2[...])
        for r, val in zip(outs, vjp(d[...].astype(f32))):
            r[...] = val

    row = pl.BlockSpec((tm, 512), lambda i: (i, 0))
    outs = pl.pallas_call(
        body, name=name, grid=(L // tm,), out_shape=[jax.ShapeDtypeStruct((L, 512), f32)] * 6,
        in_specs=[row] * 7, out_specs=[row] * 6, compiler_params=_params(("parallel",)),
    )(*os_, *ls, dout)
    return outs[:3], outs[3:]


def _to_strided(z, dil):
    L, w = z.shape
    return z.reshape(L // dil, dil, w).transpose(1, 0, 2).reshape(L, w)


def _from_strided(z, dil):
    L, w = z.shape
    return z.reshape(dil, L // dil, w).transpose(1, 0, 2).reshape(L, w)


def _adamw_math(w, g, m, v):
    m = ADAM_B1 * m + (1.0 - ADAM_B1) * g
    v = ADAM_B2 * v + (1.0 - ADAM_B2) * (g * g)
    m_hat = m / (1.0 - ADAM_B1 ** ADAM_STEP)
    v_hat = v / (1.0 - ADAM_B2 ** ADAM_STEP)
    delta = -ADAM_LR * (m_hat / (jnp.sqrt(v_hat) + ADAM_EPS) + ADAM_WD * w)
    return delta, m, v


def _adamw(w, m, v, slots, name):
    depth, R, C = w.shape
    tr = _tile(R, max(8, 131072 // C), 8)
    outs = None
    for l in range(depth):
        def body(w_ref, m_ref, v_ref, s_ref, *rest):
            g_ref, d_ref, nm_ref, nv_ref = rest[-4:]
            g = s_ref[0].astype(f32)
            for i in range(1, N_DEV):
                g = g + s_ref[i].astype(f32)
            delta, nm, nv = _adamw_math(w_ref[0], g, m_ref[0], v_ref[0])
            g_ref[0], d_ref[0], nm_ref[0], nv_ref[0] = g, delta, nm, nv

        blk = pl.BlockSpec((1, tr, C), functools.partial(lambda i, l: (l, i, 0), l=l))
        carried = [] if outs is None else list(outs)
        outs = pl.pallas_call(
            body, name=f"{name}_{l}", grid=(R // tr,), out_shape=[jax.ShapeDtypeStruct(w.shape, f32)] * 4,
            in_specs=[blk, blk, blk, pl.BlockSpec((N_DEV, tr, C), lambda i: (0, i, 0))]
            + [pl.BlockSpec(memory_space=pl.ANY)] * len(carried),
            out_specs=[blk] * 4, input_output_aliases={4 + j: j for j in range(len(carried))},
            compiler_params=_params(("parallel",)),
        )(w, m, v, slots[l], *carried)
    return outs


def _adamw_packed(w, m, v, slots, name):
    R = w.shape[0]
    tr = _tile(R, 512, 8)

    def body(w_ref, m_ref, v_ref, s_ref, g_ref, d_ref, nm_ref, nv_ref):
        g = s_ref[0]
        for i in range(1, N_DEV):
            g = g + s_ref[i]
        delta, nm, nv = _adamw_math(w_ref[...], g, m_ref[...], v_ref[...])
        g_ref[...], d_ref[...], nm_ref[...], nv_ref[...] = g, delta, nm, nv

    blk = pl.BlockSpec((tr, 128), lambda i: (i, 0))
    return pl.pallas_call(
        body, name=name, grid=(R // tr,), out_shape=[jax.ShapeDtypeStruct(w.shape, f32)] * 4,
        in_specs=[blk, blk, blk, pl.BlockSpec((N_DEV, tr, 128), lambda i: (0, i, 0))], out_specs=[blk] * 4,
        compiler_params=_params(("parallel",)),
    )(w, m, v, slots)


def _cols_gathered(g):
    return jnp.concatenate([g[d] for d in range(N_DEV)], axis=-1)


def _cols_scatter(full):
    c = full.shape[-1] // N_DEV
    return jnp.stack([full[..., d * c:(d + 1) * c] for d in range(N_DEV)])


def _win_segments(D):
    b = 4 * D
    return (((O_GATES, O_GATES + b), 0), ((0, O_GLR), b), ((O_S5U, O_GATES), b + O_GLR), ((O_GLR, O_S5U), b + P_GLR))


def _win_from_shards(g):
    rows, c = g.shape[1], g.shape[2]
    D = (N_DEV * c - O_GATES) // 4
    pieces = []
    for (lo, hi), _ in sorted(_win_segments(D), key=lambda t: t[1]):
        for d in range(N_DEV):
            a, b = max(lo, d * c), min(hi, (d + 1) * c)
            if a < b:
                pieces.append(g[d][:, a - d * c:b - d * c])
    pieces.append(jnp.zeros((rows, 128 - GLA_LOWRANK), g.dtype))
    return jnp.concatenate(pieces, axis=1)


def _win_to_shards(wp, D):
    c = (O_GATES + 4 * D) // N_DEV
    segs = sorted(_win_segments(D), key=lambda t: t[0][0])
    out = []
    for d in range(N_DEV):
        pieces = []
        for (lo, hi), off in segs:
            a, b = max(lo, d * c), min(hi, (d + 1) * c)
            if a < b:
                pieces.append(wp[:, off + a - lo:off + b - lo])
        out.append(jnp.concatenate(pieces, axis=1))
    return jnp.stack(out)


SMALL = ("norm1_g", "gla_a_b", "gla_norm_g", "s5_lambda_re", "s5_lambda_im", "s5_log_dt", "s5_b_re", "s5_b_im",
         "s5_c_re", "s5_c_im", "s5_d", "s5_glu_b", "swa_sinks", "norm2_g", "final_norm_g")
SHARDED = ("w_in", "gla_a2", "s5_glu_w", "w_branch", "w_out", "w_ffn_gate", "w_ffn_up", "w_ffn_down")
WEIGHTS = ("norm1_g", "w_in", "gla_a2", "gla_a_b", "gla_norm_g", "s5_lambda_re", "s5_lambda_im", "s5_log_dt", "s5_b_re",
           "s5_b_im", "s5_c_re", "s5_c_im", "s5_d", "s5_glu_w", "s5_glu_b", "swa_sinks", "w_branch", "w_out", "norm2_g",
           "w_ffn_gate", "w_ffn_up", "w_ffn_down", "final_norm_g")


def _pack(arrs):
    flat = jnp.concatenate([a.reshape(-1) for a in arrs])
    n = flat.shape[0]
    rows = -(-n // 1024) * 8
    return jnp.pad(flat, (0, rows * 128 - n)).reshape(rows, 128)


def _unpack(packed, like):
    flat = packed.reshape(-1)
    out, pos = [], 0
    for a in like:
        out.append(flat[pos:pos + a.size].reshape(a.shape))
        pos += a.size
    return out


def kernel(x, positions, norm1_g, w_in, gla_a2, gla_a_b, gla_norm_g, s5_lambda_re, s5_lambda_im, s5_log_dt, s5_b_re, s5_b_im, s5_c_re, s5_c_im, s5_d, s5_glu_w, s5_glu_b, swa_sinks, w_branch, w_out, norm2_g, w_ffn_gate, w_ffn_up, w_ffn_down, final_norm_g, loss_target, m_norm1_g, m_w_in, m_gla_a2, m_gla_a_b, m_gla_norm_g, m_s5_lambda_re, m_s5_lambda_im, m_s5_log_dt, m_s5_b_re, m_s5_b_im, m_s5_c_re, m_s5_c_im, m_s5_d, m_s5_glu_w, m_s5_glu_b, m_swa_sinks, m_w_branch, m_w_out, m_norm2_g, m_w_ffn_gate, m_w_ffn_up, m_w_ffn_down, m_final_norm_g, v_norm1_g, v_w_in, v_gla_a2, v_gla_a_b, v_gla_norm_g, v_s5_lambda_re, v_s5_lambda_im, v_s5_log_dt, v_s5_b_re, v_s5_b_im, v_s5_c_re, v_s5_c_im, v_s5_d, v_s5_glu_w, v_s5_glu_b, v_swa_sinks, v_w_branch, v_w_out, v_norm2_g, v_w_ffn_gate, v_w_ffn_up, v_w_ffn_down, v_final_norm_g):
    W = dict(norm1_g=norm1_g, w_in=w_in, gla_a2=gla_a2, gla_a_b=gla_a_b, gla_norm_g=gla_norm_g, s5_lambda_re=s5_lambda_re, s5_lambda_im=s5_lambda_im, s5_log_dt=s5_log_dt, s5_b_re=s5_b_re, s5_b_im=s5_b_im, s5_c_re=s5_c_re, s5_c_im=s5_c_im, s5_d=s5_d, s5_glu_w=s5_glu_w, s5_glu_b=s5_glu_b, swa_sinks=swa_sinks, w_branch=w_branch, w_out=w_out, norm2_g=norm2_g, w_ffn_gate=w_ffn_gate, w_ffn_up=w_ffn_up, w_ffn_down=w_ffn_down, final_norm_g=final_norm_g)
    Mo = dict(norm1_g=m_norm1_g, w_in=m_w_in, gla_a2=m_gla_a2, gla_a_b=m_gla_a_b, gla_norm_g=m_gla_norm_g, s5_lambda_re=m_s5_lambda_re, s5_lambda_im=m_s5_lambda_im, s5_log_dt=m_s5_log_dt, s5_b_re=m_s5_b_re, s5_b_im=m_s5_b_im, s5_c_re=m_s5_c_re, s5_c_im=m_s5_c_im, s5_d=m_s5_d, s5_glu_w=m_s5_glu_w, s5_glu_b=m_s5_glu_b, swa_sinks=m_swa_sinks, w_branch=m_w_branch, w_out=m_w_out, norm2_g=m_norm2_g, w_ffn_gate=m_w_ffn_gate, w_ffn_up=m_w_ffn_up, w_ffn_down=m_w_ffn_down, final_norm_g=m_final_norm_g)
    Vo = dict(norm1_g=v_norm1_g, w_in=v_w_in, gla_a2=v_gla_a2, gla_a_b=v_gla_a_b, gla_norm_g=v_gla_norm_g, s5_lambda_re=v_s5_lambda_re, s5_lambda_im=v_s5_lambda_im, s5_log_dt=v_s5_log_dt, s5_b_re=v_s5_b_re, s5_b_im=v_s5_b_im, s5_c_re=v_s5_c_re, s5_c_im=v_s5_c_im, s5_d=v_s5_d, s5_glu_w=v_s5_glu_w, s5_glu_b=v_s5_glu_b, swa_sinks=v_swa_sinks, w_branch=v_w_branch, w_out=v_w_out, norm2_g=v_norm2_g, w_ffn_gate=v_w_ffn_gate, w_ffn_up=v_w_ffn_up, w_ffn_down=v_w_ffn_down, final_norm_g=v_final_norm_g)

    L, D = x.shape[1], x.shape[2]
    depth = norm1_g.shape[0]
    xs = x.reshape(L, D)
    target = loss_target.reshape(L, D)
    base128 = 4 * D // 128

    in_group = ("w_in", "gla_a2", "s5_glu_w")
    full = {}

    def riders(*pairs):
        pairs = [(k, l) for k, l in pairs if l < depth]
        return pairs, [W[k][l] if k in ("gla_a2", "s5_glu_w") else W[k][l].astype(bf16) for k, l in pairs]

    def landed(pairs, gathered):
        for (k, l), g in zip(pairs, gathered):
            if k == "w_in":
                full[k, l] = _win_from_shards(g)
            elif k == "gla_a2":
                full[k, l] = jnp.pad(_cols_gathered(g), ((0, 128 - GLA_LOWRANK), (0, 0)))
            elif k in ("w_branch", "w_ffn_gate", "w_ffn_up"):
                full[k, l] = _cols_gathered(g)
            else:
                full[k, l] = g.reshape((-1, g.shape[-1]))

    def hosted_matmul(pairs, *args, **kw):
        pairs, sh = riders(*pairs)
        if not pairs:
            return _matmul(*args, **kw)
        out, got = _matmul(*args, sends=sh, gather=True, **kw)
        landed(pairs, got)
        return out

    pairs, sh = riders(*[(k, 0) for k in in_group])
    landed(pairs, _all_gather(sh, "gather_w_in0"))

    cos, sin = _rope_tables(positions.reshape(L))

    saved = []
    cur = xs
    for l in range(depth):
        s = {"x": cur}
        nxt = l + 1
        h1 = _rms_fwd(cur, norm1_g[l][None], f"rms1_fwd{l}")
        first = [("w_branch", 0), ("w_out", 0), ("w_ffn_gate", 0)] if l == 0 else [("w_ffn_gate", l), ("w_ffn_up", l)]
        proj = hosted_matmul(first, h1, full["w_in", l], name=f"proj_in{l}")
        s["h1"], s["proj"] = h1, proj
        ab, ng = gla_a_b[l][None], gla_norm_g[l].reshape(1, 512)
        pairs, sh = riders(*[(k, nxt) for k in in_group], *([("w_ffn_down", l)] if l > 0 else []))
        (o_gla, s["gla_st"]), got = _gla_fwd(proj, full["gla_a2", l], ab, ng, base128, f"gla_fwd{l}", shards=sh)
        landed(pairs, got)
        prep, s["prep_vjp"] = jax.vjp(_s5_prep, s5_lambda_re[l], s5_lambda_im[l], s5_log_dt[l], s5_b_re[l], s5_b_im[l],
                                      s5_c_re[l], s5_c_im[l], s5_d[l])
        s["prep"] = prep
        pairs, sh = riders(("w_branch", nxt), ("w_out", nxt))
        (y_s5, s["s5_r"], s["s5_i"]), got = _s5_fwd(proj, prep, base128, f"s5_fwd{l}", shards=sh)
        landed(pairs, got)
        s["y_s5"] = y_s5
        o_s5 = _glu_fwd(y_s5, full["s5_glu_w", l], s5_glu_b[l][None], f"glu_fwd{l}")
        sinks_b = jnp.repeat(swa_sinks[l], HEAD_DIM)[None]
        s["sinks_b"] = sinks_b
        nb = L // ATT_BLOCK
        cq, ck, cv = (base128 + P_CQ // 128) // 4, (base128 + P_CK // 128) // 4, (base128 + P_CV // 128) // 4
        sq_col, sk_col, sv_col = (base128 + P_SQ // 128) // 4, base128 + P_SK // 128, base128 + P_SV // 128
        cq_r, ck_r, sq_r, sk_r = _rope([(proj, cq, 512), (proj, ck, 512), (proj, sq_col, 512), (proj, sk_col, 128)], cos, sin,
                                       transpose=False, out_dtype=f32, name=f"rope_fwd{l}")
        s["rot"] = (cq_r, ck_r, sq_r, sk_r)
        o_swa, _ = _attn_fwd(sq_r, sk_r, proj, sinks_b, q_col=0, k_col=0, v_col=sv_col, hkv=SWA_KV_HEADS, nbc=nb,
                             max_dist=SWA_WINDOW - 1, name=f"swa_fwd{l}")
        dil_o, dil_l, s["dil_in"] = [], [], []
        for window, dil in DIL_CONFIGS:
            if dil == 1:
                (o, lse), _ = _attn_fwd(cq_r, ck_r, proj, None, q_col=0, k_col=0, v_col=cv, hkv=8, nbc=nb,
                                        max_dist=window // dil, name=f"dil{dil}_fwd{l}")
                s["dil_in"].append(None)
            else:
                qs_, ks_ = _to_strided(cq_r, dil), _to_strided(ck_r, dil)
                vs_ = _to_strided(proj[:, 4 * D + P_CV:4 * D + P_CV + 512], dil)
                (o, lse), _ = _attn_fwd(qs_, ks_, vs_, None, q_col=0, k_col=0, v_col=0, hkv=8,
                                        nbc=nb // dil, max_dist=window // dil, name=f"dil{dil}_fwd{l}")
                o, lse = _from_strided(o, dil), _from_strided(lse, dil)
                s["dil_in"].append((qs_, ks_, vs_))
            dil_o.append(o)
            dil_l.append(lse)
        s["dil_o"], s["dil_l"] = dil_o, dil_l
        o_dil = _dilmix_fwd(dil_o, dil_l, f"dilmix_fwd{l}")
        branches = (o_gla, o_s5, o_dil, o_swa)
        s["branches"] = branches
        ys = [_matmul(br, full["w_branch", l][m], out_dtype=bf16, name=f"branch{m}_fwd{l}") for m, br in enumerate(branches)]
        s["ys"] = ys
        mixed = _merge_fwd(proj, ys, D, f"merge_fwd{l}")
        s["mixed"] = mixed
        x2 = _matmul(mixed, full["w_out", l], res=cur, name=f"out_fwd{l}")
        s["x2"] = x2
        h2 = _rms_fwd(x2, norm2_g[l][None], f"rms2_fwd{l}")
        a = hosted_matmul([("w_ffn_up", 0)] if l == 0 else [], h2, full["w_ffn_gate", l], out_dtype=bf16, name=f"ffn_gate_fwd{l}")
        b, act = hosted_matmul([("w_ffn_down", 0)] if l == 0 else [], h2, full["w_ffn_up", l], extras=(a,),
                               epilogue=_swiglu_epilogue, out_dtype=(bf16, bf16), name=f"ffn_up_fwd{l}")
        s["h2"], s["a"], s["b"], s["act"] = h2, a, b, act
        cur = _matmul(act, full["w_ffn_down", l], res=x2, name=f"ffn_down_fwd{l}")
        saved.append(s)
    win_p, a2p, glu_w, wb, wout, wg, wu, wd = (
        [full[k, l] for l in range(depth)]
        for k in ("w_in", "gla_a2", "s5_glu_w", "w_branch", "w_out", "w_ffn_gate", "w_ffn_up", "w_ffn_down"))

    loss_part, dcur, dcur_b, dgf = _final_loss(cur, final_norm_g[None], target, "final_loss")
    loss = lax.psum(loss_part, AXES)

    small_g = {k: [None] * depth for k in SMALL if k != "final_norm_g"}
    recv = {k: [None] * depth for k in SHARDED}
    in_group = ("w_in", "gla_a2", "s5_glu_w")
    pending = None
    for l in reversed(range(depth)):
        s = saved[l]
        proj = s["proj"]
        da, db = _matmul(dcur_b, wd[l], mode="nt", extras=(s["a"], s["b"]), epilogue=_swiglu_grad_epilogue,
                         out_dtype=(bf16, bf16), name=f"ffn_down_dx{l}")
        g_down = _matmul(s["act"], dcur_b, mode="tn", out_dtype=bf16, name=f"ffn_down_dw{l}")
        dh2 = _matmul(da, wg[l], mode="nt", name=f"ffn_gate_dx{l}")
        dh2 = _matmul(db, wu[l], mode="nt", res=dh2, name=f"ffn_up_dx{l}")
        g_gate = _matmul(s["h2"], da, mode="tn", out_dtype=bf16, name=f"ffn_gate_dw{l}")
        g_up = _matmul(s["h2"], db, mode="tn", out_dtype=bf16, name=f"ffn_up_dw{l}")
        ffn_sends = (("w_ffn_down", g_down.reshape((N_DEV, -1, D))), ("w_ffn_gate", _cols_scatter(g_gate)),
                     ("w_ffn_up", _cols_scatter(g_up)))
        dx2, dx2_b, dg2 = _rms_bwd(s["x2"], norm2_g[l][None], dh2, dcur, f"rms2_bwd{l}")
        small_g["norm2_g"][l] = dg2[0]
        dmixed = _matmul(dx2_b, wout[l], mode="nt", name=f"out_dx{l}")
        g_out = _matmul(s["mixed"], dx2_b, mode="tn", out_dtype=bf16, name=f"out_dw{l}")
        dys, dgates = _merge_bwd(proj, s["ys"], dmixed, D, f"merge_bwd{l}")
        dbr = [_matmul(dys[m], wb[l][m], mode="nt", name=f"branch{m}_dx{l}") for m in range(4)]
        g_branch = jnp.stack([_matmul(s["branches"][m], dys[m], mode="tn", out_dtype=bf16, name=f"branch{m}_dw{l}")
                              for m in range(4)])
        d_gla, d_s5, d_dil, d_swa = dbr
        ab, ng = gla_a_b[l][None], gla_norm_g[l].reshape(1, 512)
        ffn_in_gla = pending is None
        gla_keys = [(k, l) for k, _ in ffn_sends] if ffn_in_gla else [(k, l + 1) for k in in_group] + [("w_ffn_up", l)]
        gla_sends = [t for _, t in ffn_sends] if ffn_in_gla else list(pending) + [ffn_sends[2][1]]
        (dgq, dgk, dgv, dgr, dglr, da2, dab, dng), got = _gla_bwd(proj, a2p[l], ab, ng, s["gla_st"], d_gla, base128,
                                                                   f"gla_bwd{l}", sends=gla_sends)
        for (k, kl), r in zip(gla_keys, got):
            recv[k][kl] = r
        small_g["gla_a_b"][l] = dab[0]
        small_g["gla_norm_g"][l] = dng.reshape(GLA_HEADS, GLA_DV)
        dy_s5, dglu_w, dglu_b = _glu_bwd(s["y_s5"], glu_w[l], s5_glu_b[l][None], d_s5, f"glu_bwd{l}")
        small_g["s5_glu_b"][l] = dglu_b[0]
        ds5u, dprep, got = _s5_bwd(proj, s["prep"], s["s5_r"], s["s5_i"], dy_s5, base128, f"s5_bwd{l}",
                                   sends=[g_out.reshape((N_DEV, -1, D)), _cols_scatter(g_branch)])
        recv["w_out"][l], recv["w_branch"][l] = got
        draw = s["prep_vjp"](dprep)
        for k, val in zip(("s5_lambda_re", "s5_lambda_im", "s5_log_dt", "s5_b_re", "s5_b_im", "s5_c_re", "s5_c_im", "s5_d"), draw):
            small_g[k][l] = val
        nb = L // ATT_BLOCK
        cq_r, ck_r, sq_r, sk_r = s["rot"]
        (dsq, dsk, dsv, dsinks), got = _attn_bwd(sq_r, sk_r, proj, s["sinks_b"], (d_swa,), q_col=0, k_col=0,
                                                 v_col=base128 + P_SV // 128, hkv=SWA_KV_HEADS, nbc=nb,
                                                 max_dist=SWA_WINDOW - 1, name=f"swa_bwd{l}",
                                                 sends=[] if ffn_in_gla else [ffn_sends[0][1]])
        if got:
            recv["w_ffn_down"][l] = got[0]
        small_g["swa_sinks"][l] = dsinks.reshape(SWA_HEADS, HEAD_DIM).sum(axis=1)
        dos, dls = _dilmix_bwd(s["dil_o"], s["dil_l"], d_dil, f"dilmix_bwd{l}")
        cv = (base128 + P_CV // 128) // 4
        dcq = dck = dcv = None
        for i, (window, dil) in enumerate(DIL_CONFIGS):
            if dil == 1:
                g3 = _attn_bwd(cq_r, ck_r, proj, None, (dos[i], dls[i]), q_col=0, k_col=0, v_col=cv, hkv=8,
                               nbc=nb, max_dist=window // dil, name=f"dil{dil}_bwd{l}")
            else:
                qs_, ks_, vs_ = s["dil_in"][i]
                g3 = _attn_bwd(qs_, ks_, vs_, None, (_to_strided(dos[i], dil), _to_strided(dls[i], dil)),
                               q_col=0, k_col=0, v_col=0, hkv=8, nbc=nb // dil, max_dist=window // dil,
                               name=f"dil{dil}_bwd{l}")
                g3 = [_from_strided(t, dil) for t in g3]
            g3 = [t.astype(f32) for t in g3]
            dcq, dck, dcv = (g3[0], g3[1], g3[2]) if dcq is None else (dcq + g3[0], dck + g3[1], dcv + g3[2])
        dcq, dck, dsq, dsk = _rope([(dcq, 0, 512), (dck, 0, 512), (dsq, 0, 512), (dsk, 0, 128)], cos, sin,
                                   transpose=True, out_dtype=bf16, name=f"rope_bwd{l}")
        dproj = jnp.concatenate([dgates.transpose(1, 0, 2).reshape(L, 4 * D), dgq, dgk, dgv, dgr, ds5u,
                                 dcq, dck, dcv.astype(bf16), dsq, dsk, dsv, dglr], axis=1)
        if ffn_in_gla:
            g_in = _matmul(s["h1"], dproj, mode="tn", out_dtype=bf16, name=f"proj_in_dw{l}")
        else:
            g_in, got = _matmul(s["h1"], dproj, mode="tn", out_dtype=bf16, sends=[ffn_sends[1][1]], name=f"proj_in_dw{l}")
            recv["w_ffn_gate"][l] = got[0]
        in_sends = [_win_to_shards(g_in, D),_cols_scatter(da2[:GLA_LOWRANK]), dglu_w.reshape((N_DEV, -1, 512))]
        if l > 0:
            dh1 = _matmul(dproj, win_p[l], mode="nt", name=f"proj_in_dx{l}")
            pending = in_sends
        else:
            dh1, got = _matmul(dproj, win_p[l], mode="nt", sends=in_sends, name=f"proj_in_dx{l}")
            for k, r in zip(in_group, got):
                recv[k][l] = r
        dcur, dcur_b, dg1 = _rms_bwd(s["x"], norm1_g[l][None], dh1, dx2, f"rms1_bwd{l}")
        small_g["norm1_g"][l] = dg1[0]
    grad_x = dcur.reshape(x.shape)

    out = {}
    for k in SHARDED:
        shp = W[k].shape
        as3 = lambda t: t.reshape((shp[0], -1, shp[-1]))
        slots = [r.reshape((N_DEV, -1, shp[-1])) for r in recv[k]]
        res = _adamw(as3(W[k]), as3(Mo[k]), as3(Vo[k]), slots, f"adamw_{k}")
        out[k] = [t.reshape(shp) for t in res]

    small_list = [jnp.stack(small_g[k]) if k != "final_norm_g" else dgf[0] for k in SMALL]
    small_list = [t.reshape(W[k].shape) for t, k in zip(small_list, SMALL)]
    packed_parts = _all_gather([_pack(small_list)], "gather_small_grads")[0]
    res = _adamw_packed(_pack([W[k] for k in SMALL]), _pack([Mo[k] for k in SMALL]), _pack([Vo[k] for k in SMALL]),
                        packed_parts, "adamw_small")
    unpacked = [_unpack(t, [W[k] for k in SMALL]) for t in res]
    for i, k in enumerate(SMALL):
        out[k] = [unpacked[j][i] for j in range(4)]

    return (loss, grad_x, *[out[k][0] for k in WEIGHTS], *[out[k][1] for k in WEIGHTS],
            *[out[k][2] for k in WEIGHTS], *[out[k][3] for k in WEIGHTS])
```

```python
import functools
import math

import jax
import jax.numpy as jnp
from jax import lax
from jax.experimental import pallas as pl
from jax.experimental.pallas import tpu as pltpu

f32 = jnp.float32
bf16 = jnp.bfloat16
HI = lax.Precision.HIGHEST

N_DEV = 8
AXES = ("x", "y", "c")
NORM_EPS = 1e-6
ROPE_THETA = 500000.0
HEAD_DIM = 64
ROPE_DIM = 16
ATT_BLOCK = 128
BRANCH_WIDTH = 512
GLA_HEADS, GLA_DK, GLA_DV, GLA_LOWRANK, GLA_TAU, GLA_CHUNK, GLA_SUB = 4, 64, 128, 16, 16.0, 64, 16
S5_GROUPS, S5_GROUP, S5_STATE = 32, 16, 64
S5_CHUNK = 128
S5_LANE_BLOCKS = 4
DIL_CONFIGS = ((128, 1), (512, 4), (2048, 16))
SWA_HEADS, SWA_KV_HEADS, SWA_WINDOW = 8, 2, 128
ADAM_LR, ADAM_B1, ADAM_B2, ADAM_EPS, ADAM_WD, ADAM_STEP = 0.001, 0.9, 0.999, 1e-08, 0.01, 10
O_GLR, O_S5U, O_GATES = 1536, 1552, 4368
MIX_COLS = 4480
P_GQ, P_GK, P_GV, P_GR, P_S5U, P_CQ, P_CK, P_CV, P_SQ, P_SK, P_SV, P_GLR = (
    0, 256, 512, 1024, 1536, 2048, 2560, 3072, 3584, 4096, 4224, 4352)
VMEM_LIMIT = 56 * 1024 * 1024


def _tile(n, cap, q=128):
    if n <= cap:
        return n
    t = (cap // q) * q
    while t >= q:
        if n % t == 0:
            return t
        t -= q
    return n


def _params(sem=None):
    return pltpu.CompilerParams(dimension_semantics=sem, vmem_limit_bytes=VMEM_LIMIT)


@functools.partial(jax.custom_vjp, nondiff_argnums=(1,))
def _sroll(x, d):
    return pltpu.roll(x, d, 0)


def _sroll_fwd(x, d):
    return pltpu.roll(x, d, 0), None


def _sroll_bwd(d, _, g):
    n = g.shape[0]
    return (pltpu.roll(g, (n - d) % n, 0),)


_sroll.defvjp(_sroll_fwd, _sroll_bwd)


def _mesh_pos():
    return lax.axis_index("x"), lax.axis_index("y"), lax.axis_index("c")


class _Gather:
    def __init__(self, ins, outs, send_sems, recv_sems, local_sems):
        self.ins, self.outs = ins, outs
        self.send_sems, self.recv_sems, self.local_sems = send_sems, recv_sems, local_sems
        x, y, c = _mesh_pos()
        self.x, self.y, self.c = x, y, c
        self.me, self.sibling = (x, y, c), (x, y, 1 - c)
        self.chips = [(1 - x, y), (x, 1 - y), (1 - x, 1 - y)]

    def copy(self, a, k, block, to, src=None):
        slot = self.outs[a].at[4 * block[0] + 2 * block[1] + block[2]]
        return pltpu.make_async_remote_copy(
            src_ref=slot if src is None else src, dst_ref=slot,
            send_sem=self.send_sems.at[a, k], recv_sem=self.recv_sems.at[a, k],
            device_id=to, device_id_type=pl.DeviceIdType.MESH)

    def mine(self, a):
        return pltpu.make_async_copy(self.ins[a], self.outs[a].at[4 * self.x + 2 * self.y + self.c], self.local_sems.at[a])

    def first(self, a):
        return [self.copy(a, 0, self.me, self.sibling, src=self.ins[a])] + [
            self.copy(a, 1 + j, self.me, (*chip, self.c), src=self.ins[a]) for j, chip in enumerate(self.chips)]

    def start(self):
        for a in range(len(self.ins)):
            self.mine(a).start()
            for cp in self.first(a):
                cp.start()

    def finish(self):
        c = self.c
        for a in range(len(self.ins)):
            passed = [self.copy(a, 4 + j, (*chip, c), self.sibling) for j, chip in enumerate(self.chips)]
            for j, chip in enumerate(self.chips):
                self.copy(a, 1 + j, (*chip, c), self.me).wait_recv()
                passed[j].start()
            self.copy(a, 0, self.sibling, self.me).wait_recv()
            for j, chip in enumerate(self.chips):
                self.copy(a, 4 + j, (*chip, 1 - c), self.me).wait_recv()
            for cp in self.first(a) + passed:
                cp.wait_send()
            self.mine(a).wait()


def _all_gather(shards, name):
    n = len(shards)
    any_spec = pl.BlockSpec(memory_space=pl.ANY)

    def body(*refs):
        g = _Gather(refs[:n], refs[n:2 * n], *refs[2 * n:])
        g.start()
        g.finish()

    outs = pl.pallas_call(
        body, name=name,
        out_shape=[jax.ShapeDtypeStruct((N_DEV,) + s.shape, s.dtype) for s in shards],
        in_specs=[any_spec] * n, out_specs=[any_spec] * n,
        scratch_shapes=[pltpu.SemaphoreType.DMA((n, 7)), pltpu.SemaphoreType.DMA((n, 7)),
                        pltpu.SemaphoreType.DMA((n,))],
    )(*shards)
    return list(outs)


def _a2a_copies(ins, outs, send_sems, recv_sems, local_sems):
    x, y, c = _mesh_pos()
    me = 4 * x + 2 * y + c
    copies = []
    for a in range(len(ins)):
        copies.append(pltpu.make_async_copy(ins[a].at[me], outs[a].at[me], local_sems.at[a]))
        for k in range(1, N_DEV):
            px = 1 - x if k & 4 else x
            py = 1 - y if k & 2 else y
            pc = 1 - c if k & 1 else c
            copies.append(pltpu.make_async_remote_copy(
                src_ref=ins[a].at[4 * px + 2 * py + pc], dst_ref=outs[a].at[me],
                send_sem=send_sems.at[a, k - 1], recv_sem=recv_sems.at[a, k - 1],
                device_id=(px, py, pc), device_id_type=pl.DeviceIdType.MESH))
    return copies


def _host_call(body, sends, args, *, name, grid, out_shape, in_specs, out_specs, scratch_shapes, compiler_params,
               gather=False):
    single = not isinstance(out_shape, (list, tuple))
    out_shape = [out_shape] if single else list(out_shape)
    out_specs = [out_specs] if single else list(out_specs)
    sends = list(sends or ())
    n, n_in, n_out, n_scr = len(sends), len(args), len(out_shape), len(scratch_shapes)
    if n == 0:
        outs = pl.pallas_call(body, name=name, grid=grid, out_shape=out_shape, in_specs=in_specs, out_specs=out_specs,
                              scratch_shapes=list(scratch_shapes), compiler_params=compiler_params)(*args)
        return (outs[0] if single else list(outs)), []
    any_spec = pl.BlockSpec(memory_space=pl.ANY)

    def hosted(*refs):
        ins, s_in = refs[:n_in], refs[n_in:n_in + n]
        pos = n_in + n
        outs, s_out = refs[pos:pos + n_out], refs[pos + n_out:pos + n_out + n]
        pos += n_out + n
        scr, sems = refs[pos:pos + n_scr], refs[pos + n_scr:]
        ids = [pl.program_id(i) for i in range(len(grid))]
        first = functools.reduce(lambda p, q: p & q, [i == 0 for i in ids])
        last = functools.reduce(lambda p, q: p & q, [i == g - 1 for i, g in zip(ids, grid)])

        @pl.when(first)
        def _():
            if gather:
                _Gather(s_in, s_out, *sems).start()
            else:
                for cp in _a2a_copies(s_in, s_out, *sems):
                    cp.start()

        body(*ins, *outs, *scr)

        @pl.when(last)
        def _():
            if gather:
                _Gather(s_in, s_out, *sems).finish()
            else:
                for cp in _a2a_copies(s_in, s_out, *sems):
                    cp.wait()

    lead = (N_DEV,) if gather else ()
    outs = pl.pallas_call(
        hosted, name=name, grid=grid,
        out_shape=out_shape + [jax.ShapeDtypeStruct(lead + s.shape, s.dtype) for s in sends],
        in_specs=list(in_specs) + [any_spec] * n, out_specs=out_specs + [any_spec] * n,
        scratch_shapes=list(scratch_shapes) + [pltpu.SemaphoreType.DMA((n, 7)), pltpu.SemaphoreType.DMA((n, 7)),
                                               pltpu.SemaphoreType.DMA((n,))],
        compiler_params=compiler_params,
    )(*args, *sends)
    main = list(outs[:n_out])
    return (main[0] if single else main), list(outs[n_out:])


def _matmul(a, b, *, mode="nn", out_dtype=f32, res=None, extras=(), epilogue=None, sends=None, gather=False, name):
    if mode == "tn":
        K, M = a.shape
    else:
        M, K = a.shape
    N = b.shape[0] if mode == "nt" else b.shape[1]
    k_cap = 2048 if (a.dtype == bf16 and b.dtype == bf16) else 1024
    tm, tn, tk = _tile(M, 1024), _tile(N, 1152), _tile(K, k_cap)
    nk = K // tk
    dims = {"nn": (((1,), (0,)), ((), ())), "nt": (((1,), (1,)), ((), ())), "tn": (((0,), (0,)), ((), ()))}[mode]
    out_dtypes = list(out_dtype) if epilogue is not None else [out_dtype]
    n_side = (1 if res is not None else 0) + len(extras)

    def body(*refs):
        a_ref, b_ref = refs[:2]
        r_ref = refs[2] if res is not None else None
        x_refs = refs[2 + n_side - len(extras):2 + n_side]
        o_refs = refs[2 + n_side:2 + n_side + len(out_dtypes)]
        acc = refs[-1] if nk > 1 else None
        k = pl.program_id(2)
        part = lax.dot_general(a_ref[...].astype(bf16), b_ref[...].astype(bf16), dims, preferred_element_type=f32)

        def finish(r):
            if res is not None:
                r = r + r_ref[...]
            outs = epilogue(r, *[x[...] for x in x_refs]) if epilogue is not None else (r,)
            for o_ref, val in zip(o_refs, outs):
                o_ref[...] = val.astype(o_ref.dtype)

        if nk == 1:
            finish(part)
            return

        @pl.when(k == 0)
        def _():
            acc[...] = part

        @pl.when((k > 0) & (k < nk - 1))
        def _():
            acc[...] += part

        @pl.when(k == nk - 1)
        def _():
            finish(acc[...] + part)

    a_spec = pl.BlockSpec((tk, tm), lambda i, j, k: (k, i)) if mode == "tn" else pl.BlockSpec((tm, tk), lambda i, j, k: (i, k))
    b_spec = pl.BlockSpec((tn, tk), lambda i, j, k: (j, k)) if mode == "nt" else pl.BlockSpec((tk, tn), lambda i, j, k: (k, j))
    o_spec = pl.BlockSpec((tm, tn), lambda i, j, k: (i, j))
    in_specs = [a_spec, b_spec] + [o_spec] * n_side
    args = (a, b) + ((res,) if res is not None else ()) + tuple(extras)
    out, recvs = _host_call(
        body, sends, args, name=name, grid=(M // tm, N // tn, nk),
        out_shape=[jax.ShapeDtypeStruct((M, N), dt) for dt in out_dtypes],
        in_specs=in_specs, out_specs=[o_spec] * len(out_dtypes),
        scratch_shapes=[pltpu.VMEM((tm, tn), f32)] if nk > 1 else [],
        compiler_params=_params(("parallel", "parallel", "arbitrary")), gather=gather)
    if epilogue is None:
        out = out[0]
    return out if sends is None else (out, recvs)


def _rms(x, g):
    return x * lax.rsqrt(jnp.mean(x * x, axis=-1, keepdims=True) + NORM_EPS) * g


def _rms_fwd(x, g, name):
    L, D = x.shape
    tm = _tile(L, 256, 8)

    def body(x_ref, g_ref, o_ref):
        o_ref[...] = _rms(x_ref[...], g_ref[...]).astype(bf16)

    return pl.pallas_call(
        body, name=name, grid=(L // tm,), out_shape=jax.ShapeDtypeStruct((L, D), bf16),
        in_specs=[pl.BlockSpec((tm, D), lambda i: (i, 0)), pl.BlockSpec((1, D), lambda i: (0, 0))],
        out_specs=pl.BlockSpec((tm, D), lambda i: (i, 0)),
        compiler_params=_params(("parallel",)),
    )(x, g)


def _rms_bwd(x, g, dh, dres, name):
    L, D = x.shape
    tm = _tile(L, 256, 8)

    def body(x_ref, g_ref, dh_ref, dres_ref, dx_ref, dxb_ref, dg_ref):
        _, vjp = jax.vjp(_rms, x_ref[...], g_ref[...])
        dx, dg = vjp(dh_ref[...])
        dx = dres_ref[...] + dx
        dx_ref[...] = dx
        dxb_ref[...] = dx.astype(bf16)

        @pl.when(pl.program_id(0) == 0)
        def _():
            dg_ref[...] = jnp.zeros_like(dg_ref)

        dg_ref[...] += dg

    row = pl.BlockSpec((tm, D), lambda i: (i, 0))
    vec = pl.BlockSpec((1, D), lambda i: (0, 0))
    return pl.pallas_call(
        body, name=name, grid=(L // tm,),
        out_shape=[jax.ShapeDtypeStruct((L, D), f32), jax.ShapeDtypeStruct((L, D), bf16), jax.ShapeDtypeStruct((1, D), f32)],
        in_specs=[row, vec, row, row], out_specs=[row, row, vec],
        compiler_params=_params(("arbitrary",)),
    )(x, g, dh, dres)


def _final_loss(x, g, target, name):
    L, D = x.shape
    tm = _tile(L, 256, 8)

    def body(x_ref, g_ref, t_ref, loss_ref, dx_ref, dxb_ref, dg_ref):
        tgt = t_ref[...]

        def f(xv, gv):
            err = _rms(xv, gv) - tgt
            return 0.5 * jnp.sum(jnp.mean(err * err, axis=-1, keepdims=True), axis=0, keepdims=True)

        val, vjp = jax.vjp(f, x_ref[...], g_ref[...])
        dx, dg = vjp(jnp.ones((1, 1), f32))
        dx_ref[...] = dx
        dxb_ref[...] = dx.astype(bf16)

        @pl.when(pl.program_id(0) == 0)
        def _():
            dg_ref[...] = jnp.zeros_like(dg_ref)
            loss_ref[...] = jnp.zeros_like(loss_ref)

        dg_ref[...] += dg
        loss_ref[...] += jnp.broadcast_to(val, loss_ref.shape)

    row = pl.BlockSpec((tm, D), lambda i: (i, 0))
    vec = pl.BlockSpec((1, D), lambda i: (0, 0))
    acc = pl.BlockSpec((8, 128), lambda i: (0, 0))
    loss, dx, dxb, dg = pl.pallas_call(
        body, name=name, grid=(L // tm,),
        out_shape=[jax.ShapeDtypeStruct((8, 128), f32), jax.ShapeDtypeStruct((L, D), f32), jax.ShapeDtypeStruct((L, D), bf16),
                   jax.ShapeDtypeStruct((1, D), f32)],
        in_specs=[row, vec, row], out_specs=[acc, row, row, vec],
        compiler_params=_params(("arbitrary",)),
    )(x, g, target)
    return loss[0, 0], dx, dxb, dg


def _swiglu_f(a, b):
    return jax.nn.silu(a) * b


def _swiglu_epilogue(up, gate):
    return up, _swiglu_f(gate.astype(f32), up)


def _swiglu_grad_epilogue(dact, gate, up):
    _, vjp = jax.vjp(_swiglu_f, gate.astype(f32), up.astype(f32))
    return vjp(dact)


def _merge_f(g0, g1, g2, g3, y0, y1, y2, y3):
    s = jax.nn.sigmoid
    return s(g0) * y0 + s(g1) * y1 + s(g2) * y2 + s(g3) * y3


def _merge_fwd(proj, ys, D, name):
    L = proj.shape[0]
    tm, tn = _tile(L, 512, 8), _tile(D, 512)
    nj = D // tn

    def body(g0, g1, g2, g3, y0, y1, y2, y3, o_ref):
        o_ref[...] = _merge_f(g0[...], g1[...], g2[...], g3[...], *[y[...].astype(f32) for y in (y0, y1, y2, y3)]).astype(bf16)

    gspecs = [pl.BlockSpec((tm, tn), functools.partial(lambda i, j, m: (i, m * nj + j), m=m)) for m in range(4)]
    blk = pl.BlockSpec((tm, tn), lambda i, j: (i, j))
    return pl.pallas_call(
        body, name=name, grid=(L // tm, nj), out_shape=jax.ShapeDtypeStruct((L, D), bf16),
        in_specs=gspecs + [blk] * 4, out_specs=blk, compiler_params=_params(("parallel", "parallel")),
    )(proj, proj, proj, proj, *ys)


def _merge_bwd(proj, ys, dmixed, D, name):
    L = proj.shape[0]
    tm, tn = _tile(L, 512, 8), _tile(D, 512)
    nj = D // tn

    def body(g0, g1, g2, g3, y0, y1, y2, y3, d_ref, dy0, dy1, dy2, dy3, dg_ref):
        _, vjp = jax.vjp(_merge_f, g0[...], g1[...], g2[...], g3[...], *[y[...].astype(f32) for y in (y0, y1, y2, y3)])
        grads = vjp(d_ref[...])
        for m, r in enumerate((dy0, dy1, dy2, dy3)):
            r[...] = grads[4 + m].astype(bf16)
        for m in range(4):
            dg_ref[m] = grads[m].astype(bf16)

    gspecs = [pl.BlockSpec((tm, tn), functools.partial(lambda i, j, m: (i, m * nj + j), m=m)) for m in range(4)]
    blk = pl.BlockSpec((tm, tn), lambda i, j: (i, j))
    dgspec = pl.BlockSpec((4, tm, tn), lambda i, j: (0, i, j))
    outs = pl.pallas_call(
        body, name=name, grid=(L // tm, nj),
        out_shape=[jax.ShapeDtypeStruct((L, D), bf16)] * 4 + [jax.ShapeDtypeStruct((4, L, D), bf16)],
        in_specs=gspecs + [blk] * 5, out_specs=[blk] * 4 + [dgspec],
        compiler_params=_params(("parallel", "parallel")),
    )(proj, proj, proj, proj, *ys, dmixed)
    return outs[:4], outs[4]


def _gla_head(q, k, v, r, glr, st, a2, ab, ng):
    C, T = GLA_CHUNK, GLA_SUB
    row = lax.broadcasted_iota(jnp.int32, (C, C), 0)
    col = lax.broadcasted_iota(jnp.int32, (C, C), 1)
    tri = (col <= row).astype(f32)
    sel = (col == (row // T) * T).astype(f32)
    z = jnp.dot(glr, a2, preferred_element_type=f32) + ab
    g = jax.nn.log_sigmoid(z) / GLA_TAU
    cum = jnp.dot(tri, g, precision=HI, preferred_element_type=f32)
    excl = cum - g
    ref = jnp.dot(sel, excl, precision=HI, preferred_element_type=f32)
    qs = q * (GLA_DK ** -0.5)
    q_ref = qs * jnp.exp(cum - ref)
    rowk = lax.broadcasted_iota(jnp.int32, (C, GLA_DK), 0)
    a = jnp.zeros((C, C), f32)
    for s in range(1, C // T):
        ref_s = jnp.sum(jnp.where(rowk == s * T, excl, 0.0), axis=0, keepdims=True)
        k_ref = k * jnp.exp(jnp.where(rowk < s * T, ref_s - cum, -jnp.inf))
        a_s = lax.dot_general(q_ref, k_ref, (((1,), (1,)), ((), ())), preferred_element_type=f32)
        a = a + jnp.where(row // T == s, a_s, 0.0)
    o = jnp.dot(a, v, preferred_element_type=f32)
    sub = rowk % T
    for d in range(T):
        ks = _sroll(k, d) if d else k
        cs = _sroll(cum, d) if d else cum
        vs = _sroll(v, d) if d else v
        w = jnp.sum(qs * ks * jnp.exp(jnp.where(sub >= d, cum - cs, -jnp.inf)), axis=-1, keepdims=True)
        o = o + w * vs
    o = o + lax.dot_general(qs * jnp.exp(cum), st, (((1,), (1,)), ((), ())), preferred_element_type=f32)
    last = jnp.sum(jnp.where(rowk == C - 1, cum, 0.0), axis=0, keepdims=True)
    st_new = st * jnp.exp(last) + lax.dot_general(v, k * jnp.exp(last - cum), (((0,), (0,)), ((), ())),
                                                  preferred_element_type=f32)
    out = _rms(o, ng) * jax.nn.silu(r)
    return out, st_new


def _gla_specs(L, base128, rev):
    n = L // GLA_CHUNK
    ch = (lambda i: n - 1 - i) if rev else (lambda i: i)
    b = base128
    return n, ch, [
        pl.BlockSpec((GLA_CHUNK, 256), lambda i: (ch(i), (b + P_GQ // 128) // 2)),
        pl.BlockSpec((GLA_CHUNK, 256), lambda i: (ch(i), (b + P_GK // 128) // 2)),
        pl.BlockSpec((GLA_CHUNK, 512), lambda i: (ch(i), (b + P_GV // 128) // 4)),
        pl.BlockSpec((GLA_CHUNK, 512), lambda i: (ch(i), (b + P_GR // 128) // 4)),
        pl.BlockSpec((GLA_CHUNK, 128), lambda i: (ch(i), b + P_GLR // 128)),
    ]


def _gla_fwd(proj, a2p, ab, ng, base128, name, shards=None):
    L = proj.shape[0]
    n, _, pspecs = _gla_specs(L, base128, False)
    H, DK, DV = GLA_HEADS, GLA_DK, GLA_DV

    def body(q_ref, k_ref, v_ref, r_ref, l_ref, a2_ref, ab_ref, ng_ref, o_ref, st_ref, state):
        @pl.when(pl.program_id(0) == 0)
        def _():
            state[...] = jnp.zeros_like(state)

        glr = l_ref[...]
        states = [state[h] for h in range(H)]
        res = []
        for h in range(H):
            kk, vv = slice(h * DK, (h + 1) * DK), slice(h * DV, (h + 1) * DV)
            res.append(_gla_head(q_ref[:, kk], k_ref[:, kk], v_ref[:, vv], r_ref[:, vv], glr, states[h],
                                 a2_ref[:, kk], ab_ref[:, kk], ng_ref[:, vv]))
        for h in range(H):
            st_ref[0, h] = states[h]
            o_ref[:, h * DV:(h + 1) * DV] = res[h][0].astype(bf16)
            state[h] = res[h][1]

    full = lambda shape: pl.BlockSpec(shape, lambda i: (0,) * len(shape))
    return _host_call(
        body, shards, (proj, proj, proj, proj, proj, a2p, ab, ng), name=name, grid=(n,),
        out_shape=[jax.ShapeDtypeStruct((L, H * DV), bf16), jax.ShapeDtypeStruct((n, H, DV, DK), f32)],
        in_specs=pspecs + [full((128, 256)), full((1, 256)), full((1, 512))],
        out_specs=[pl.BlockSpec((GLA_CHUNK, 512), lambda i: (i, 0)), pl.BlockSpec((1, H, DV, DK), lambda i: (i, 0, 0, 0))],
        scratch_shapes=[pltpu.VMEM((H, DV, DK), f32)],
        compiler_params=_params(("arbitrary",)), gather=True)


def _gla_bwd(proj, a2p, ab, ng, states, dout, base128, name, sends=None):
    L = proj.shape[0]
    n, ch, pspecs = _gla_specs(L, base128, True)
    H, DK, DV = GLA_HEADS, GLA_DK, GLA_DV

    def body(q_ref, k_ref, v_ref, r_ref, l_ref, a2_ref, ab_ref, ng_ref, st_ref, do_ref,
             dq_ref, dk_ref, dv_ref, dr_ref, dl_ref, da2_ref, dab_ref, dng_ref, dstate):
        @pl.when(pl.program_id(0) == 0)
        def _():
            dstate[...] = jnp.zeros_like(dstate)
            da2_ref[...] = jnp.zeros_like(da2_ref)
            dab_ref[...] = jnp.zeros_like(dab_ref)
            dng_ref[...] = jnp.zeros_like(dng_ref)

        glr = l_ref[...]
        dglr = jnp.zeros(glr.shape, f32)
        vjps = []
        for h in range(H):
            kk, vv = slice(h * DK, (h + 1) * DK), slice(h * DV, (h + 1) * DV)
            vjps.append(jax.vjp(_gla_head, q_ref[:, kk], k_ref[:, kk], v_ref[:, vv], r_ref[:, vv], glr, st_ref[0, h],
                                a2_ref[:, kk], ab_ref[:, kk], ng_ref[:, vv])[1])
        cots = [(do_ref[:, h * DV:(h + 1) * DV].astype(f32), dstate[h]) for h in range(H)]
        grads = [vjps[h](cots[h]) for h in range(H)]
        for h in range(H):
            kk, vv = slice(h * DK, (h + 1) * DK), slice(h * DV, (h + 1) * DV)
            dq, dk, dv, dr, dl, dst, da2, dab, dng = grads[h]
            dq_ref[:, kk] = dq.astype(bf16)
            dk_ref[:, kk] = dk.astype(bf16)
            dv_ref[:, vv] = dv.astype(bf16)
            dr_ref[:, vv] = dr.astype(bf16)
            dglr = dglr + dl
            dstate[h] = dst
            da2_ref[:, kk] += da2
            dab_ref[:, kk] += dab
            dng_ref[:, vv] += dng
        dl_ref[...] = dglr.astype(bf16)

    full = lambda shape: pl.BlockSpec(shape, lambda i: (0,) * len(shape))
    rowspec = lambda w: pl.BlockSpec((GLA_CHUNK, w), lambda i: (ch(i), 0))
    return _host_call(
        body, sends, (proj, proj, proj, proj, proj, a2p, ab, ng, states, dout), name=name, grid=(n,),
        out_shape=[jax.ShapeDtypeStruct((L, 256), bf16), jax.ShapeDtypeStruct((L, 256), bf16),
                   jax.ShapeDtypeStruct((L, 512), bf16), jax.ShapeDtypeStruct((L, 512), bf16),
                   jax.ShapeDtypeStruct((L, 128), bf16), jax.ShapeDtypeStruct((128, 256), f32),
                   jax.ShapeDtypeStruct((1, 256), f32), jax.ShapeDtypeStruct((1, 512), f32)],
        in_specs=pspecs + [full((128, 256)), full((1, 256)), full((1, 512)),
                           pl.BlockSpec((1, H, DV, DK), lambda i: (ch(i), 0, 0, 0)), rowspec(512)],
        out_specs=[rowspec(256), rowspec(256), rowspec(512), rowspec(512), rowspec(128),
                   full((128, 256)), full((1, 256)), full((1, 512))],
        scratch_shapes=[pltpu.VMEM((H, DV, DK), f32)],
        compiler_params=_params(("arbitrary",)))


def _s5_prep(lam_re, lam_im, log_dt, b_re, b_im, c_re, c_im, d):
    G, N, Cn = S5_GROUPS, S5_STATE, S5_GROUP
    J, GB = S5_LANE_BLOCKS, S5_GROUPS // S5_LANE_BLOCKS
    dt = jnp.exp(log_dt)[:, None]
    mag = jnp.exp(lam_re * dt)
    ab_re, ab_im = mag * jnp.cos(lam_im * dt), mag * jnp.sin(lam_im * dt)
    den = lam_re * lam_re + lam_im * lam_im
    z_re = ((ab_re - 1.0) * lam_re + ab_im * lam_im) / den
    z_im = (ab_im * lam_re - (ab_re - 1.0) * lam_im) / den
    bb_re = z_re[..., None] * b_re - z_im[..., None] * b_im
    bb_im = z_re[..., None] * b_im + z_im[..., None] * b_re
    eye = jnp.eye(GB, dtype=f32)

    def in_blocks(bb):
        return jnp.einsum("jgnc,gh->jgchn", bb.reshape(J, GB, N, Cn), eye).reshape(J, GB * Cn, GB * N)

    def out_blocks(cc):
        return jnp.einsum("jgcn,gh->jgnhc", cc.reshape(J, GB, Cn, N), eye).reshape(J, GB * N, GB * Cn)

    return (ab_re.reshape(1, G * N), ab_im.reshape(1, G * N), in_blocks(bb_re), in_blocks(bb_im),
            out_blocks(c_re), out_blocks(c_im), d.reshape(1, G * Cn))


def _s5_chunk(u, hin_r, hin_i, a_r, a_i, bb_r, bb_i, cc_r, cc_i, dvec):
    T = u.shape[0]
    hr = jnp.dot(u, bb_r, preferred_element_type=f32)
    hi = jnp.dot(u, bb_i, preferred_element_type=f32)
    row = lax.broadcasted_iota(jnp.int32, hr.shape, 0)
    hr = hr + jnp.where(row == 0, a_r * hin_r - a_i * hin_i, 0.0)
    hi = hi + jnp.where(row == 0, a_r * hin_i + a_i * hin_r, 0.0)
    pr, pi = a_r, a_i
    d = 1
    while d < T:
        sr = jnp.where(row >= d, _sroll(hr, d), 0.0)
        si = jnp.where(row >= d, _sroll(hi, d), 0.0)
        hr, hi = hr + pr * sr - pi * si, hi + pr * si + pi * sr
        pr, pi = pr * pr - pi * pi, 2.0 * pr * pi
        d *= 2
    y = (jnp.dot(hr, cc_r, preferred_element_type=f32)
         - jnp.dot(hi, cc_i, preferred_element_type=f32) + dvec * u)
    out_r = jnp.sum(jnp.where(row == T - 1, hr, 0.0), axis=0, keepdims=True)
    out_i = jnp.sum(jnp.where(row == T - 1, hi, 0.0), axis=0, keepdims=True)
    return y, out_r, out_i


def _s5_specs(L, base128, rev):
    T, J = S5_CHUNK, S5_LANE_BLOCKS
    n = L // T
    ch = (lambda c: n - 1 - c) if rev else (lambda c: c)
    ub = base128 + P_S5U // 128
    specs = [
        pl.BlockSpec((T, 128), lambda j, c: (ch(c), ub + j)),
        pl.BlockSpec((1, 512), lambda j, c: (0, j)), pl.BlockSpec((1, 512), lambda j, c: (0, j)),
        pl.BlockSpec((None, 128, 512), lambda j, c: (j, 0, 0)), pl.BlockSpec((None, 128, 512), lambda j, c: (j, 0, 0)),
        pl.BlockSpec((None, 512, 128), lambda j, c: (j, 0, 0)), pl.BlockSpec((None, 512, 128), lambda j, c: (j, 0, 0)),
        pl.BlockSpec((1, 128), lambda j, c: (0, j)),
    ]
    return n, ch, specs


def _s5_fwd(proj, prep, base128, name, shards=None):
    L = proj.shape[0]
    T, J = S5_CHUNK, S5_LANE_BLOCKS
    n, _, specs = _s5_specs(L, base128, False)

    def body(u_ref, ar, ai, bbr, bbi, ccr, cci, dv, y_ref, sr_ref, si_ref, carry):
        @pl.when(pl.program_id(1) == 0)
        def _():
            carry[...] = jnp.zeros_like(carry)

        hin_r, hin_i = carry[0:1, :], carry[1:2, :]
        sr_ref[0] = jnp.broadcast_to(hin_r, (8, 512))
        si_ref[0] = jnp.broadcast_to(hin_i, (8, 512))
        y, out_r, out_i = _s5_chunk(u_ref[...], hin_r, hin_i, ar[...], ai[...], bbr[...], bbi[...], ccr[...], cci[...], dv[...])
        y_ref[...] = y
        carry[0:1, :] = out_r
        carry[1:2, :] = out_i

    st = pl.BlockSpec((1, 8, 512), lambda j, c: (c, 0, j))
    return _host_call(
        body, shards, (proj, *prep), name=name, grid=(J, n),
        out_shape=[jax.ShapeDtypeStruct((L, 512), f32), jax.ShapeDtypeStruct((n, 8, 2048), f32), jax.ShapeDtypeStruct((n, 8, 2048), f32)],
        in_specs=specs, out_specs=[pl.BlockSpec((T, 128), lambda j, c: (c, j)), st, st],
        scratch_shapes=[pltpu.VMEM((8, 512), f32)],
        compiler_params=_params(("parallel", "arbitrary")), gather=True)


def _s5_bwd(proj, prep, st_r, st_i, dy, base128, name, sends=None):
    L = proj.shape[0]
    T, J = S5_CHUNK, S5_LANE_BLOCKS
    n, ch, specs = _s5_specs(L, base128, True)

    def body(u_ref, ar, ai, bbr, bbi, ccr, cci, dv, sr_ref, si_ref, dy_ref,
             du_ref, dar, dai, dbbr, dbbi, dccr, dcci, ddv, dcarry):
        @pl.when(pl.program_id(1) == 0)
        def _():
            dcarry[...] = jnp.zeros_like(dcarry)
            for r in (dar, dai, dbbr, dbbi, dccr, dcci, ddv):
                r[...] = jnp.zeros_like(r)

        _, vjp = jax.vjp(_s5_chunk, u_ref[...], sr_ref[0, 0:1, :], si_ref[0, 0:1, :], ar[...], ai[...],
                         bbr[...], bbi[...], ccr[...], cci[...], dv[...])
        g = vjp((dy_ref[...], dcarry[0:1, :], dcarry[1:2, :]))
        du_ref[...] = g[0].astype(bf16)
        dcarry[0:1, :] = g[1]
        dcarry[1:2, :] = g[2]
        for r, val in zip((dar, dai, dbbr, dbbi, dccr, dcci, ddv), g[3:]):
            r[...] += val

    st = pl.BlockSpec((1, 8, 512), lambda j, c: (ch(c), 0, j))
    outs, recvs = _host_call(
        body, sends, (proj, *prep, st_r, st_i, dy), name=name, grid=(J, n),
        out_shape=[jax.ShapeDtypeStruct((L, 512), bf16),
                   jax.ShapeDtypeStruct((1, 2048), f32), jax.ShapeDtypeStruct((1, 2048), f32),
                   jax.ShapeDtypeStruct((J, 128, 512), f32), jax.ShapeDtypeStruct((J, 128, 512), f32),
                   jax.ShapeDtypeStruct((J, 512, 128), f32), jax.ShapeDtypeStruct((J, 512, 128), f32),
                   jax.ShapeDtypeStruct((1, 512), f32)],
        in_specs=specs + [st, st, pl.BlockSpec((T, 128), lambda j, c: (ch(c), j))],
        out_specs=[pl.BlockSpec((T, 128), lambda j, c: (ch(c), j))] + specs[1:],
        scratch_shapes=[pltpu.VMEM((8, 512), f32)],
        compiler_params=_params(("parallel", "arbitrary")))
    return outs[0], tuple(outs[1:]), recvs


def _glu_f(y, w, b):
    z = jax.nn.gelu(y)
    return z * jax.nn.sigmoid(jnp.dot(z.astype(bf16), w.astype(bf16), preferred_element_type=f32) + b)


def _glu_fwd(y, w, b, name):
    L = y.shape[0]
    tm = _tile(L, 512, 8)

    def body(y_ref, w_ref, b_ref, o_ref):
        o_ref[...] = _glu_f(y_ref[...], w_ref[...], b_ref[...]).astype(bf16)

    row = pl.BlockSpec((tm, 512), lambda i: (i, 0))
    return pl.pallas_call(
        body, name=name, grid=(L // tm,), out_shape=jax.ShapeDtypeStruct((L, 512), bf16),
        in_specs=[row, pl.BlockSpec((512, 512), lambda i: (0, 0)), pl.BlockSpec((1, 512), lambda i: (0, 0))],
        out_specs=row, compiler_params=_params(("parallel",)),
    )(y, w, b)


def _glu_bwd(y, w, b, dout, name):
    L = y.shape[0]
    tm = _tile(L, 512, 8)

    def body(y_ref, w_ref, b_ref, do_ref, dy_ref, dw_ref, db_ref):
        @pl.when(pl.program_id(0) == 0)
        def _():
            dw_ref[...] = jnp.zeros_like(dw_ref)
            db_ref[...] = jnp.zeros_like(db_ref)

        _, vjp = jax.vjp(_glu_f, y_ref[...], w_ref[...], b_ref[...])
        dy, dw, db = vjp(do_ref[...])
        dy_ref[...] = dy
        dw_ref[...] += dw
        db_ref[...] += db

    row = pl.BlockSpec((tm, 512), lambda i: (i, 0))
    wspec, bspec = pl.BlockSpec((512, 512), lambda i: (0, 0)), pl.BlockSpec((1, 512), lambda i: (0, 0))
    return pl.pallas_call(
        body, name=name, grid=(L // tm,),
        out_shape=[jax.ShapeDtypeStruct((L, 512), f32), jax.ShapeDtypeStruct((512, 512), f32), jax.ShapeDtypeStruct((1, 512), f32)],
        in_specs=[row, wspec, bspec, row], out_specs=[row, wspec, bspec],
        compiler_params=_params(("arbitrary",)),
    )(y, w, b, dout)


def _rope_tables(positions):
    half = ROPE_DIM // 2
    inv_freq = ROPE_THETA ** (-jnp.arange(half, dtype=f32) / half)
    ang = positions.astype(f32)[:, None] * inv_freq
    L = positions.shape[0]
    cos = jnp.concatenate([jnp.cos(ang), jnp.cos(ang), jnp.ones((L, HEAD_DIM - ROPE_DIM), f32)], axis=1)
    sin = jnp.concatenate([jnp.sin(ang), jnp.sin(ang), jnp.zeros((L, HEAD_DIM - ROPE_DIM), f32)], axis=1)
    return jnp.tile(cos, (1, 8)), jnp.tile(sin, (1, 8))


def _rope_matrix(w):
    half = ROPE_DIM // 2
    r = lax.broadcasted_iota(jnp.int32, (w, w), 0)
    c = lax.broadcasted_iota(jnp.int32, (w, w), 1)
    same = (r // HEAD_DIM) == (c // HEAD_DIM)
    rr, cc = r % HEAD_DIM, c % HEAD_DIM
    return (jnp.where(same & (cc < half) & (rr == cc + half), -1.0, 0.0)
            + jnp.where(same & (cc >= half) & (cc < ROPE_DIM) & (rr == cc - half), 1.0, 0.0))


def _rope(items, cos, sin, *, transpose, out_dtype, name):
    L = cos.shape[0]
    tm = _tile(L, 512, 8)
    n = len(items)

    def body(*refs):
        xs, c_ref, s_ref, outs = refs[:n], refs[n], refs[n + 1], refs[n + 2:]
        for x_ref, o_ref in zip(xs, outs):
            w = x_ref.shape[1]
            x = x_ref[...].astype(f32)
            c, s = c_ref[:, :w], s_ref[:, :w]
            rot = _rope_matrix(w)
            if transpose:
                y = x * c + lax.dot_general(x * s, rot, (((1,), (1,)), ((), ())), preferred_element_type=f32)
            else:
                y = x * c + jnp.dot(x, rot, preferred_element_type=f32) * s
            o_ref[...] = y.astype(o_ref.dtype)

    in_specs = [pl.BlockSpec((tm, w), functools.partial(lambda i, col: (i, col), col=col)) for _, col, w in items]
    tab = pl.BlockSpec((tm, 512), lambda i: (i, 0))
    outs = pl.pallas_call(
        body, name=name, grid=(L // tm,),
        out_shape=[jax.ShapeDtypeStruct((L, w), out_dtype) for _, _, w in items],
        in_specs=in_specs + [tab, tab], out_specs=[pl.BlockSpec((tm, w), lambda i: (i, 0)) for _, _, w in items],
        compiler_params=_params(("parallel",)),
    )(*[a for a, _, _ in items], cos, sin)
    return list(outs)


def _attn_head(q, kp, kc, vp, vc, sink, *, lim, max_dist):
    T = ATT_BLOCK
    k2 = jnp.concatenate([kp, kc], axis=0)
    v2 = jnp.concatenate([vp, vc], axis=0)
    s = lax.dot_general(q.astype(bf16), k2.astype(bf16), (((1,), (1,)), ((), ())),
                        preferred_element_type=f32) * (HEAD_DIM ** -0.5)
    t = lax.broadcasted_iota(jnp.int32, (T, 2 * T), 0)
    j = lax.broadcasted_iota(jnp.int32, (T, 2 * T), 1)
    dist = T + t - j
    valid = (dist >= 0) & (dist <= max_dist) & (j >= lim)
    s = jnp.where(valid, s, -jnp.inf)
    m = lax.stop_gradient(jnp.max(s, axis=-1, keepdims=True))
    p = jnp.exp(s - m)
    den = jnp.sum(p, axis=-1, keepdims=True)
    o = jnp.dot(p.astype(bf16), v2.astype(bf16), preferred_element_type=f32) / den
    lse = jnp.broadcast_to(m + jnp.log(den), (T, HEAD_DIM))
    if sink is None:
        return o, lse
    return o * jax.nn.sigmoid(lse - sink)


def _attn_specs(L, q_col, k_col, v_col, wk, rev):
    T = ATT_BLOCK
    n = L // T
    blk = (lambda i: n - 1 - i) if rev else (lambda i: i)
    prev = lambda i: jnp.maximum(blk(i) - 1, 0)
    specs = [
        pl.BlockSpec((T, 512), lambda i: (blk(i), q_col)),
        pl.BlockSpec((T, wk), lambda i: (prev(i), k_col)), pl.BlockSpec((T, wk), lambda i: (blk(i), k_col)),
        pl.BlockSpec((T, wk), lambda i: (prev(i), v_col)), pl.BlockSpec((T, wk), lambda i: (blk(i), v_col)),
    ]
    return n, blk, specs


def _attn_fwd(qa, ka, va, sinks, *, q_col, k_col, v_col, hkv, nbc, max_dist, name, shards=None):
    L = qa.shape[0]
    T, HQ, HD = ATT_BLOCK, 8, HEAD_DIM
    wk = hkv * HD
    n, _, specs = _attn_specs(L, q_col, k_col, v_col, wk, False)
    grp = HQ // hkv
    gated = sinks is not None

    def body(*refs):
        q_ref, kp_ref, kc_ref, vp_ref, vc_ref = refs[:5]
        rest = refs[5:]
        lim = jnp.where(pl.program_id(0) % nbc == 0, T, 0)
        results = []
        for h in range(HQ):
            hs, ks = slice(h * HD, (h + 1) * HD), slice((h // grp) * HD, (h // grp + 1) * HD)
            results.append(_attn_head(q_ref[:, hs], kp_ref[:, ks], kc_ref[:, ks], vp_ref[:, ks], vc_ref[:, ks],
                                      rest[0][:, hs] if gated else None, lim=lim, max_dist=max_dist))
        for h, res in enumerate(results):
            hs = slice(h * HD, (h + 1) * HD)
            if gated:
                rest[1][:, hs] = res.astype(bf16)
            else:
                rest[0][:, hs] = res[0]
                rest[1][:, hs] = res[1]

    row = pl.BlockSpec((T, 512), lambda i: (i, 0))
    if gated:
        return _host_call(
            body, shards, (qa, ka, ka, va, va, sinks), name=name, grid=(n,),
            out_shape=jax.ShapeDtypeStruct((L, 512), bf16),
            in_specs=specs + [pl.BlockSpec((1, 512), lambda i: (0, 0))], out_specs=row, scratch_shapes=[],
            compiler_params=_params(("parallel",)), gather=True)
    return _host_call(
        body, shards, (qa, ka, ka, va, va), name=name, grid=(n,),
        out_shape=[jax.ShapeDtypeStruct((L, 512), f32)] * 2,
        in_specs=specs, out_specs=[row, row], scratch_shapes=[], compiler_params=_params(("parallel",)), gather=True)


def _attn_bwd(qa, ka, va, sinks, douts, *, q_col, k_col, v_col, hkv, nbc, max_dist, name, sends=None):
    L = qa.shape[0]
    T, HQ, HD = ATT_BLOCK, 8, HEAD_DIM
    wk = hkv * HD
    n, blk, specs = _attn_specs(L, q_col, k_col, v_col, wk, True)
    grp = HQ // hkv
    gated = sinks is not None
    nd = len(douts)

    def body(*refs):
        q_ref, kp_ref, kc_ref, vp_ref, vc_ref = refs[:5]
        pos = 5
        sink_ref = None
        if gated:
            sink_ref = refs[pos]
            pos += 1
        d_refs = refs[pos:pos + nd]
        pos += nd
        dq_ref, dk_ref, dv_ref = refs[pos:pos + 3]
        pos += 3
        dsink_ref = None
        if gated:
            dsink_ref = refs[pos]
            pos += 1
        carry_k, carry_v = refs[pos:pos + 2]

        @pl.when(pl.program_id(0) == 0)
        def _():
            carry_k[...] = jnp.zeros_like(carry_k)
            carry_v[...] = jnp.zeros_like(carry_v)
            if gated:
                dsink_ref[...] = jnp.zeros_like(dsink_ref)

        lim = jnp.where(blk(pl.program_id(0)) % nbc == 0, T, 0)
        dkp = [jnp.zeros((T, HD), f32) for _ in range(hkv)]
        dkc = [jnp.zeros((T, HD), f32) for _ in range(hkv)]
        dvp = [jnp.zeros((T, HD), f32) for _ in range(hkv)]
        dvc = [jnp.zeros((T, HD), f32) for _ in range(hkv)]
        dqs, dsinks = [], []
        old_k = [carry_k[:, g * HD:(g + 1) * HD] for g in range(hkv)]
        old_v = [carry_v[:, g * HD:(g + 1) * HD] for g in range(hkv)]
        for h in range(HQ):
            g = h // grp
            hs, ks = slice(h * HD, (h + 1) * HD), slice(g * HD, (g + 1) * HD)
            fn = functools.partial(_attn_head, lim=lim, max_dist=max_dist)
            prim = (q_ref[:, hs], kp_ref[:, ks], kc_ref[:, ks], vp_ref[:, ks], vc_ref[:, ks])
            if gated:
                _, vjp = jax.vjp(fn, *prim, sink_ref[:, hs])
                dq, a, b, c, d, ds = vjp(d_refs[0][:, hs].astype(f32))
                dsinks.append(ds)
            else:
                _, vjp = jax.vjp(lambda *p: fn(*p, None), *prim)
                dq, a, b, c, d = vjp((d_refs[0][:, hs], d_refs[1][:, hs]))
            dqs.append(dq)
            dkp[g], dkc[g], dvp[g], dvc[g] = dkp[g] + a, dkc[g] + b, dvp[g] + c, dvc[g] + d
        for h in range(HQ):
            hs = slice(h * HD, (h + 1) * HD)
            dq_ref[:, hs] = dqs[h].astype(bf16)
            if gated:
                dsink_ref[:, hs] += dsinks[h]
        for g in range(hkv):
            ks = slice(g * HD, (g + 1) * HD)
            dk_ref[:, ks] = (dkc[g] + old_k[g]).astype(bf16)
            dv_ref[:, ks] = (dvc[g] + old_v[g]).astype(bf16)
            carry_k[:, ks] = dkp[g]
            carry_v[:, ks] = dvp[g]

    row = lambda w: pl.BlockSpec((T, w), lambda i: (blk(i), 0))
    vec = pl.BlockSpec((1, 512), lambda i: (0, 0))
    in_specs = specs + ([vec] if gated else []) + [row(512)] * nd
    out_shape = [jax.ShapeDtypeStruct((L, 512), bf16), jax.ShapeDtypeStruct((L, wk), bf16), jax.ShapeDtypeStruct((L, wk), bf16)]
    out_specs = [row(512), row(wk), row(wk)]
    if gated:
        out_shape.append(jax.ShapeDtypeStruct((1, 512), f32))
        out_specs.append(vec)
    args = (qa, ka, ka, va, va) + ((sinks,) if gated else ()) + tuple(douts)
    outs, recvs = _host_call(
        body, sends, args, name=name, grid=(n,), out_shape=out_shape, in_specs=in_specs, out_specs=out_specs,
        scratch_shapes=[pltpu.VMEM((T, wk), f32), pltpu.VMEM((T, wk), f32)],
        compiler_params=_params(("arbitrary",)))
    return outs if sends is None else (outs, recvs)


def _dilmix_f(o0, o1, o2, l0, l1, l2):
    m = jnp.maximum(jnp.maximum(l0, l1), l2)
    e0, e1, e2 = jnp.exp(l0 - m), jnp.exp(l1 - m), jnp.exp(l2 - m)
    return (e0 * o0 + e1 * o1 + e2 * o2) / (e0 + e1 + e2)


def _dilmix_fwd(os_, ls, name):
    L = os_[0].shape[0]
    tm = _tile(L, 512, 8)

    def body(o0, o1, o2, l0, l1, l2, out):
        out[...] = _dilmix_f(o0[...], o1[...], o2[...], l0[...], l1[...], l2[...]).astype(bf16)

    row = pl.BlockSpec((tm, 512), lambda i: (i, 0))
    return pl.pallas_call(
        body, name=name, grid=(L // tm,), out_shape=jax.ShapeDtypeStruct((L, 512), bf16),
        in_specs=[row] * 6, out_specs=row, compiler_params=_params(("parallel",)),
    )(*os_, *ls)


def _dilmix_bwd(os_, ls, dout, name):
    L = os_[0].shape[0]
    tm = _tile(L, 512, 8)

    def body(o0, o1, o2, l0, l1, l2, d, *outs):
        _, vjp = jax.vjp(_dilmix_f, o0[...], o1[...], o2[...], l0[...], l1[...], l2[...])
        for r, val in zip(outs, vjp(d[...].astype(f32))):
            r[...] = val

    row = pl.BlockSpec((tm, 512), lambda i: (i, 0))
    outs = pl.pallas_call(
        body, name=name, grid=(L // tm,), out_shape=[jax.ShapeDtypeStruct((L, 512), f32)] * 6,
        in_specs=[row] * 7, out_specs=[row] * 6, compiler_params=_params(("parallel",)),
    )(*os_, *ls, dout)
    return outs[:3], outs[3:]


def _to_strided(z, dil):
    L, w = z.shape
    return z.reshape(L // dil, dil, w).transpose(1, 0, 2).reshape(L, w)


def _from_strided(z, dil):
    L, w = z.shape
    return z.reshape(dil, L // dil, w).transpose(1, 0, 2).reshape(L, w)


def _adamw_math(w, g, m, v):
    m = ADAM_B1 * m + (1.0 - ADAM_B1) * g
    v = ADAM_B2 * v + (1.0 - ADAM_B2) * (g * g)
    m_hat = m / (1.0 - ADAM_B1 ** ADAM_STEP)
    v_hat = v / (1.0 - ADAM_B2 ** ADAM_STEP)
    delta = -ADAM_LR * (m_hat / (jnp.sqrt(v_hat) + ADAM_EPS) + ADAM_WD * w)
    return delta, m, v


def _adamw(w, m, v, slots, name):
    depth, R, C = w.shape
    tr = _tile(R, max(8, 131072 // C), 8)
    outs = None
    for l in range(depth):
        def body(w_ref, m_ref, v_ref, s_ref, *rest):
            g_ref, d_ref, nm_ref, nv_ref = rest[-4:]
            g = s_ref[0].astype(f32)
            for i in range(1, N_DEV):
                g = g + s_ref[i].astype(f32)
            delta, nm, nv = _adamw_math(w_ref[0], g, m_ref[0], v_ref[0])
            g_ref[0], d_ref[0], nm_ref[0], nv_ref[0] = g, delta, nm, nv

        blk = pl.BlockSpec((1, tr, C), functools.partial(lambda i, l: (l, i, 0), l=l))
        carried = [] if outs is None else list(outs)
        outs = pl.pallas_call(
            body, name=f"{name}_{l}", grid=(R // tr,), out_shape=[jax.ShapeDtypeStruct(w.shape, f32)] * 4,
            in_specs=[blk, blk, blk, pl.BlockSpec((N_DEV, tr, C), lambda i: (0, i, 0))]
            + [pl.BlockSpec(memory_space=pl.ANY)] * len(carried),
            out_specs=[blk] * 4, input_output_aliases={4 + j: j for j in range(len(carried))},
            compiler_params=_params(("parallel",)),
        )(w, m, v, slots[l], *carried)
    return outs


def _adamw_packed(w, m, v, slots, name):
    R = w.shape[0]
    tr = _tile(R, 512, 8)

    def body(w_ref, m_ref, v_ref, s_ref, g_ref, d_ref, nm_ref, nv_ref):
        g = s_ref[0]
        for i in range(1, N_DEV):
            g = g + s_ref[i]
        delta, nm, nv = _adamw_math(w_ref[...], g, m_ref[...], v_ref[...])
        g_ref[...], d_ref[...], nm_ref[...], nv_ref[...] = g, delta, nm, nv

    blk = pl.BlockSpec((tr, 128), lambda i: (i, 0))
    return pl.pallas_call(
        body, name=name, grid=(R // tr,), out_shape=[jax.ShapeDtypeStruct(w.shape, f32)] * 4,
        in_specs=[blk, blk, blk, pl.BlockSpec((N_DEV, tr, 128), lambda i: (0, i, 0))], out_specs=[blk] * 4,
        compiler_params=_params(("parallel",)),
    )(w, m, v, slots)


def _cols_gathered(g):
    return jnp.concatenate([g[d] for d in range(N_DEV)], axis=-1)


def _cols_scatter(full):
    c = full.shape[-1] // N_DEV
    return jnp.stack([full[..., d * c:(d + 1) * c] for d in range(N_DEV)])


def _win_segments(D):
    b = 4 * D
    return (((O_GATES, O_GATES + b), 0), ((0, O_GLR), b), ((O_S5U, O_GATES), b + O_GLR), ((O_GLR, O_S5U), b + P_GLR))


def _win_from_shards(g):
    rows, c = g.shape[1], g.shape[2]
    D = (N_DEV * c - O_GATES) // 4
    pieces = []
    for (lo, hi), _ in sorted(_win_segments(D), key=lambda t: t[1]):
        for d in range(N_DEV):
            a, b = max(lo, d * c), min(hi, (d + 1) * c)
            if a < b:
                pieces.append(g[d][:, a - d * c:b - d * c])
    pieces.append(jnp.zeros((rows, 128 - GLA_LOWRANK), g.dtype))
    return jnp.concatenate(pieces, axis=1)


def _win_to_shards(wp, D):
    c = (O_GATES + 4 * D) // N_DEV
    segs = sorted(_win_segments(D), key=lambda t: t[0][0])
    out = []
    for d in range(N_DEV):
        pieces = []
        for (lo, hi), off in segs:
            a, b = max(lo, d * c), min(hi, (d + 1) * c)
            if a < b:
                pieces.append(wp[:, off + a - lo:off + b - lo])
        out.append(jnp.concatenate(pieces, axis=1))
    return jnp.stack(out)


SMALL = ("norm1_g", "gla_a_b", "gla_norm_g", "s5_lambda_re", "s5_lambda_im", "s5_log_dt", "s5_b_re", "s5_b_im",
         "s5_c_re", "s5_c_im", "s5_d", "s5_glu_b", "swa_sinks", "norm2_g", "final_norm_g")
SHARDED = ("w_in", "gla_a2", "s5_glu_w", "w_branch", "w_out", "w_ffn_gate", "w_ffn_up", "w_ffn_down")
WEIGHTS = ("norm1_g", "w_in", "gla_a2", "gla_a_b", "gla_norm_g", "s5_lambda_re", "s5_lambda_im", "s5_log_dt", "s5_b_re",
           "s5_b_im", "s5_c_re", "s5_c_im", "s5_d", "s5_glu_w", "s5_glu_b", "swa_sinks", "w_branch", "w_out", "norm2_g",
           "w_ffn_gate", "w_ffn_up", "w_ffn_down", "final_norm_g")


def _pack(arrs):
    flat = jnp.concatenate([a.reshape(-1) for a in arrs])
    n = flat.shape[0]
    rows = -(-n // 1024) * 8
    return jnp.pad(flat, (0, rows * 128 - n)).reshape(rows, 128)


def _unpack(packed, like):
    flat = packed.reshape(-1)
    out, pos = [], 0
    for a in like:
        out.append(flat[pos:pos + a.size].reshape(a.shape))
        pos += a.size
    return out


def kernel(x, positions, norm1_g, w_in, gla_a2, gla_a_b, gla_norm_g, s5_lambda_re, s5_lambda_im, s5_log_dt, s5_b_re, s5_b_im, s5_c_re, s5_c_im, s5_d, s5_glu_w, s5_glu_b, swa_sinks, w_branch, w_out, norm2_g, w_ffn_gate, w_ffn_up, w_ffn_down, final_norm_g, loss_target, m_norm1_g, m_w_in, m_gla_a2, m_gla_a_b, m_gla_norm_g, m_s5_lambda_re, m_s5_lambda_im, m_s5_log_dt, m_s5_b_re, m_s5_b_im, m_s5_c_re, m_s5_c_im, m_s5_d, m_s5_glu_w, m_s5_glu_b, m_swa_sinks, m_w_branch, m_w_out, m_norm2_g, m_w_ffn_gate, m_w_ffn_up, m_w_ffn_down, m_final_norm_g, v_norm1_g, v_w_in, v_gla_a2, v_gla_a_b, v_gla_norm_g, v_s5_lambda_re, v_s5_lambda_im, v_s5_log_dt, v_s5_b_re, v_s5_b_im, v_s5_c_re, v_s5_c_im, v_s5_d, v_s5_glu_w, v_s5_glu_b, v_swa_sinks, v_w_branch, v_w_out, v_norm2_g, v_w_ffn_gate, v_w_ffn_up, v_w_ffn_down, v_final_norm_g):
    W = dict(norm1_g=norm1_g, w_in=w_in, gla_a2=gla_a2, gla_a_b=gla_a_b, gla_norm_g=gla_norm_g, s5_lambda_re=s5_lambda_re, s5_lambda_im=s5_lambda_im, s5_log_dt=s5_log_dt, s5_b_re=s5_b_re, s5_b_im=s5_b_im, s5_c_re=s5_c_re, s5_c_im=s5_c_im, s5_d=s5_d, s5_glu_w=s5_glu_w, s5_glu_b=s5_glu_b, swa_sinks=swa_sinks, w_branch=w_branch, w_out=w_out, norm2_g=norm2_g, w_ffn_gate=w_ffn_gate, w_ffn_up=w_ffn_up, w_ffn_down=w_ffn_down, final_norm_g=final_norm_g)
    Mo = dict(norm1_g=m_norm1_g, w_in=m_w_in, gla_a2=m_gla_a2, gla_a_b=m_gla_a_b, gla_norm_g=m_gla_norm_g, s5_lambda_re=m_s5_lambda_re, s5_lambda_im=m_s5_lambda_im, s5_log_dt=m_s5_log_dt, s5_b_re=m_s5_b_re, s5_b_im=m_s5_b_im, s5_c_re=m_s5_c_re, s5_c_im=m_s5_c_im, s5_d=m_s5_d, s5_glu_w=m_s5_glu_w, s5_glu_b=m_s5_glu_b, swa_sinks=m_swa_sinks, w_branch=m_w_branch, w_out=m_w_out, norm2_g=m_norm2_g, w_ffn_gate=m_w_ffn_gate, w_ffn_up=m_w_ffn_up, w_ffn_down=m_w_ffn_down, final_norm_g=m_final_norm_g)
    Vo = dict(norm1_g=v_norm1_g, w_in=v_w_in, gla_a2=v_gla_a2, gla_a_b=v_gla_a_b, gla_norm_g=v_gla_norm_g, s5_lambda_re=v_s5_lambda_re, s5_lambda_im=v_s5_lambda_im, s5_log_dt=v_s5_log_dt, s5_b_re=v_s5_b_re, s5_b_im=v_s5_b_im, s5_c_re=v_s5_c_re, s5_c_im=v_s5_c_im, s5_d=v_s5_d, s5_glu_w=v_s5_glu_w, s5_glu_b=v_s5_glu_b, swa_sinks=v_swa_sinks, w_branch=v_w_branch, w_out=v_w_out, norm2_g=v_norm2_g, w_ffn_gate=v_w_ffn_gate, w_ffn_up=v_w_ffn_up, w_ffn_down=v_w_ffn_down, final_norm_g=v_final_norm_g)

    L, D = x.shape[1], x.shape[2]
    depth = norm1_g.shape[0]
    xs = x.reshape(L, D)
    target = loss_target.reshape(L, D)
    base128 = 4 * D // 128

    in_group = ("w_in", "gla_a2", "s5_glu_w")
    full = {}

    def riders(*pairs):
        pairs = [(k, l) for k, l in pairs if l < depth]
        return pairs, [W[k][l] if k in ("gla_a2", "s5_glu_w") else W[k][l].astype(bf16) for k, l in pairs]

    def landed(pairs, gathered):
        for (k, l), g in zip(pairs, gathered):
            if k == "w_in":
                full[k, l] = _win_from_shards(g)
            elif k == "gla_a2":
                full[k, l] = jnp.pad(_cols_gathered(g), ((0, 128 - GLA_LOWRANK), (0, 0)))
            elif k in ("w_branch", "w_ffn_gate", "w_ffn_up"):
                full[k, l] = _cols_gathered(g)
            else:
                full[k, l] = g.reshape((-1, g.shape[-1]))

    def hosted_matmul(pairs, *args, **kw):
        pairs, sh = riders(*pairs)
        if not pairs:
            return _matmul(*args, **kw)
        out, got = _matmul(*args, sends=sh, gather=True, **kw)
        landed(pairs, got)
        return out

    pairs, sh = riders(*[(k, 0) for k in in_group])
    landed(pairs, _all_gather(sh, "gather_w_in0"))

    cos, sin = _rope_tables(positions.reshape(L))

    saved = []
    cur = xs
    for l in range(depth):
        s = {"x": cur}
        nxt = l + 1
        h1 = _rms_fwd(cur, norm1_g[l][None], f"rms1_fwd{l}")
        first = [("w_branch", 0), ("w_out", 0), ("w_ffn_gate", 0)] if l == 0 else [("w_ffn_gate", l), ("w_ffn_up", l)]
        proj = hosted_matmul(first, h1, full["w_in", l], name=f"proj_in{l}")
        s["h1"], s["proj"] = h1, proj
        ab, ng = gla_a_b[l][None], gla_norm_g[l].reshape(1, 512)
        pairs, sh = riders(*[(k, nxt) for k in in_group], *([("w_ffn_down", l)] if l > 0 else []))
        (o_gla, s["gla_st"]), got = _gla_fwd(proj, full["gla_a2", l], ab, ng, base128, f"gla_fwd{l}", shards=sh)
        landed(pairs, got)
        prep, s["prep_vjp"] = jax.vjp(_s5_prep, s5_lambda_re[l], s5_lambda_im[l], s5_log_dt[l], s5_b_re[l], s5_b_im[l],
                                      s5_c_re[l], s5_c_im[l], s5_d[l])
        s["prep"] = prep
        pairs, sh = riders(("w_branch", nxt), ("w_out", nxt))
        (y_s5, s["s5_r"], s["s5_i"]), got = _s5_fwd(proj, prep, base128, f"s5_fwd{l}", shards=sh)
        landed(pairs, got)
        s["y_s5"] = y_s5
        o_s5 = _glu_fwd(y_s5, full["s5_glu_w", l], s5_glu_b[l][None], f"glu_fwd{l}")
        sinks_b = jnp.repeat(swa_sinks[l], HEAD_DIM)[None]
        s["sinks_b"] = sinks_b
        nb = L // ATT_BLOCK
        cq, ck, cv = (base128 + P_CQ // 128) // 4, (base128 + P_CK // 128) // 4, (base128 + P_CV // 128) // 4
        sq_col, sk_col, sv_col = (base128 + P_SQ // 128) // 4, base128 + P_SK // 128, base128 + P_SV // 128
        cq_r, ck_r, sq_r, sk_r = _rope([(proj, cq, 512), (proj, ck, 512), (proj, sq_col, 512), (proj, sk_col, 128)], cos, sin,
                                       transpose=False, out_dtype=f32, name=f"rope_fwd{l}")
        s["rot"] = (cq_r, ck_r, sq_r, sk_r)
        o_swa, _ = _attn_fwd(sq_r, sk_r, proj, sinks_b, q_col=0, k_col=0, v_col=sv_col, hkv=SWA_KV_HEADS, nbc=nb,
                             max_dist=SWA_WINDOW - 1, name=f"swa_fwd{l}")
        dil_o, dil_l, s["dil_in"] = [], [], []
        for window, dil in DIL_CONFIGS:
            if dil == 1:
                (o, lse), _ = _attn_fwd(cq_r, ck_r, proj, None, q_col=0, k_col=0, v_col=cv, hkv=8, nbc=nb,
                                        max_dist=window // dil, name=f"dil{dil}_fwd{l}")
                s["dil_in"].append(None)
            else:
                qs_, ks_ = _to_strided(cq_r, dil), _to_strided(ck_r, dil)
                vs_ = _to_strided(proj[:, 4 * D + P_CV:4 * D + P_CV + 512], dil)
                (o, lse), _ = _attn_fwd(qs_, ks_, vs_, None, q_col=0, k_col=0, v_col=0, hkv=8,
                                        nbc=nb // dil, max_dist=window // dil, name=f"dil{dil}_fwd{l}")
                o, lse = _from_strided(o, dil), _from_strided(lse, dil)
                s["dil_in"].append((qs_, ks_, vs_))
            dil_o.append(o)
            dil_l.append(lse)
        s["dil_o"], s["dil_l"] = dil_o, dil_l
        o_dil = _dilmix_fwd(dil_o, dil_l, f"dilmix_fwd{l}")
        branches = (o_gla, o_s5, o_dil, o_swa)
        s["branches"] = branches
        ys = [_matmul(br, full["w_branch", l][m], out_dtype=bf16, name=f"branch{m}_fwd{l}") for m, br in enumerate(branches)]
        s["ys"] = ys
        mixed = _merge_fwd(proj, ys, D, f"merge_fwd{l}")
        s["mixed"] = mixed
        x2 = _matmul(mixed, full["w_out", l], res=cur, name=f"out_fwd{l}")
        s["x2"] = x2
        h2 = _rms_fwd(x2, norm2_g[l][None], f"rms2_fwd{l}")
        a = hosted_matmul([("w_ffn_up", 0)] if l == 0 else [], h2, full["w_ffn_gate", l], out_dtype=bf16, name=f"ffn_gate_fwd{l}")
        b, act = hosted_matmul([("w_ffn_down", 0)] if l == 0 else [], h2, full["w_ffn_up", l], extras=(a,),
                               epilogue=_swiglu_epilogue, out_dtype=(bf16, bf16), name=f"ffn_up_fwd{l}")
        s["h2"], s["a"], s["b"], s["act"] = h2, a, b, act
        cur = _matmul(act, full["w_ffn_down", l], res=x2, name=f"ffn_down_fwd{l}")
        saved.append(s)
    win_p, a2p, glu_w, wb, wout, wg, wu, wd = (
        [full[k, l] for l in range(depth)]
        for k in ("w_in", "gla_a2", "s5_glu_w", "w_branch", "w_out", "w_ffn_gate", "w_ffn_up", "w_ffn_down"))

    loss_part, dcur, dcur_b, dgf = _final_loss(cur, final_norm_g[None], target, "final_loss")
    loss = lax.psum(loss_part, AXES)

    small_g = {k: [None] * depth for k in SMALL if k != "final_norm_g"}
    recv = {k: [None] * depth for k in SHARDED}
    in_group = ("w_in", "gla_a2", "s5_glu_w")
    pending = None
    for l in reversed(range(depth)):
        s = saved[l]
        proj = s["proj"]
        da, db = _matmul(dcur_b, wd[l], mode="nt", extras=(s["a"], s["b"]), epilogue=_swiglu_grad_epilogue,
                         out_dtype=(bf16, bf16), name=f"ffn_down_dx{l}")
        g_down = _matmul(s["act"], dcur_b, mode="tn", out_dtype=bf16, name=f"ffn_down_dw{l}")
        dh2 = _matmul(da, wg[l], mode="nt", name=f"ffn_gate_dx{l}")
        dh2 = _matmul(db, wu[l], mode="nt", res=dh2, name=f"ffn_up_dx{l}")
        g_gate = _matmul(s["h2"], da, mode="tn", out_dtype=bf16, name=f"ffn_gate_dw{l}")
        g_up = _matmul(s["h2"], db, mode="tn", out_dtype=bf16, name=f"ffn_up_dw{l}")
        ffn_sends = (("w_ffn_down", g_down.reshape((N_DEV, -1, D))), ("w_ffn_gate", _cols_scatter(g_gate)),
                     ("w_ffn_up", _cols_scatter(g_up)))
        dx2, dx2_b, dg2 = _rms_bwd(s["x2"], norm2_g[l][None], dh2, dcur, f"rms2_bwd{l}")
        small_g["norm2_g"][l] = dg2[0]
        dmixed = _matmul(dx2_b, wout[l], mode="nt", name=f"out_dx{l}")
        g_out = _matmul(s["mixed"], dx2_b, mode="tn", out_dtype=bf16, name=f"out_dw{l}")
        dys, dgates = _merge_bwd(proj, s["ys"], dmixed, D, f"merge_bwd{l}")
        dbr = [_matmul(dys[m], wb[l][m], mode="nt", name=f"branch{m}_dx{l}") for m in range(4)]
        g_branch = jnp.stack([_matmul(s["branches"][m], dys[m], mode="tn", out_dtype=bf16, name=f"branch{m}_dw{l}")
                              for m in range(4)])
        d_gla, d_s5, d_dil, d_swa = dbr
        ab, ng = gla_a_b[l][None], gla_norm_g[l].reshape(1, 512)
        ffn_in_gla = pending is None
        gla_keys = [(k, l) for k, _ in ffn_sends] if ffn_in_gla else [(k, l + 1) for k in in_group] + [("w_ffn_up", l)]
        gla_sends = [t for _, t in ffn_sends] if ffn_in_gla else list(pending) + [ffn_sends[2][1]]
        (dgq, dgk, dgv, dgr, dglr, da2, dab, dng), got = _gla_bwd(proj, a2p[l], ab, ng, s["gla_st"], d_gla, base128,
                                                                   f"gla_bwd{l}", sends=gla_sends)
        for (k, kl), r in zip(gla_keys, got):
            recv[k][kl] = r
        small_g["gla_a_b"][l] = dab[0]
        small_g["gla_norm_g"][l] = dng.reshape(GLA_HEADS, GLA_DV)
        dy_s5, dglu_w, dglu_b = _glu_bwd(s["y_s5"], glu_w[l], s5_glu_b[l][None], d_s5, f"glu_bwd{l}")
        small_g["s5_glu_b"][l] = dglu_b[0]
        ds5u, dprep, got = _s5_bwd(proj, s["prep"], s["s5_r"], s["s5_i"], dy_s5, base128, f"s5_bwd{l}",
                                   sends=[g_out.reshape((N_DEV, -1, D)), _cols_scatter(g_branch)])
        recv["w_out"][l], recv["w_branch"][l] = got
        draw = s["prep_vjp"](dprep)
        for k, val in zip(("s5_lambda_re", "s5_lambda_im", "s5_log_dt", "s5_b_re", "s5_b_im", "s5_c_re", "s5_c_im", "s5_d"), draw):
            small_g[k][l] = val
        nb = L // ATT_BLOCK
        cq_r, ck_r, sq_r, sk_r = s["rot"]
        (dsq, dsk, dsv, dsinks), got = _attn_bwd(sq_r, sk_r, proj, s["sinks_b"], (d_swa,), q_col=0, k_col=0,
                                                 v_col=base128 + P_SV // 128, hkv=SWA_KV_HEADS, nbc=nb,
                                                 max_dist=SWA_WINDOW - 1, name=f"swa_bwd{l}",
                                                 sends=[] if ffn_in_gla else [ffn_sends[0][1]])
        if got:
            recv["w_ffn_down"][l] = got[0]
        small_g["swa_sinks"][l] = dsinks.reshape(SWA_HEADS, HEAD_DIM).sum(axis=1)
        dos, dls = _dilmix_bwd(s["dil_o"], s["dil_l"], d_dil, f"dilmix_bwd{l}")
        cv = (base128 + P_CV // 128) // 4
        dcq = dck = dcv = None
        for i, (window, dil) in enumerate(DIL_CONFIGS):
            if dil == 1:
                g3 = _attn_bwd(cq_r, ck_r, proj, None, (dos[i], dls[i]), q_col=0, k_col=0, v_col=cv, hkv=8,
                               nbc=nb, max_dist=window // dil, name=f"dil{dil}_bwd{l}")
            else:
                qs_, ks_, vs_ = s["dil_in"][i]
                g3 = _attn_bwd(qs_, ks_, vs_, None, (_to_strided(dos[i], dil), _to_strided(dls[i], dil)),
                               q_col=0, k_col=0, v_col=0, hkv=8, nbc=nb // dil, max_dist=window // dil,
                               name=f"dil{dil}_bwd{l}")
                g3 = [_from_strided(t, dil) for t in g3]
            g3 = [t.astype(f32) for t in g3]
            dcq, dck, dcv = (g3[0], g3[1], g3[2]) if dcq is None else (dcq + g3[0], dck + g3[1], dcv + g3[2])
        dcq, dck, dsq, dsk = _rope([(dcq, 0, 512), (dck, 0, 512), (dsq, 0, 512), (dsk, 0, 128)], cos, sin,
                                   transpose=True, out_dtype=bf16, name=f"rope_bwd{l}")
        dproj = jnp.concatenate([dgates.transpose(1, 0, 2).reshape(L, 4 * D), dgq, dgk, dgv, dgr, ds5u,
                                 dcq, dck, dcv.astype(bf16), dsq, dsk, dsv, dglr], axis=1)
        if ffn_in_gla:
            g_in = _matmul(s["h1"], dproj, mode="tn", out_dtype=bf16, name=f"proj_in_dw{l}")
        else:
            g_in, got = _matmul(s["h1"], dproj, mode="tn", out_dtype=bf16, sends=[ffn_sends[1][1]], name=f"proj_in_dw{l}")
            recv["w_ffn_gate"][l] = got[0]
        in_sends = [_win_to_shards(g_in, D),_cols_scatter(da2[:GLA_LOWRANK]), dglu_w.reshape((N_DEV, -1, 512))]
        if l > 0:
            dh1 = _matmul(dproj, win_p[l], mode="nt", name=f"proj_in_dx{l}")
            pending = in_sends
        else:
            dh1, got = _matmul(dproj, win_p[l], mode="nt", sends=in_sends, name=f"proj_in_dx{l}")
            for k, r in zip(in_group, got):
                recv[k][l] = r
        dcur, dcur_b, dg1 = _rms_bwd(s["x"], norm1_g[l][None], dh1, dx2, f"rms1_bwd{l}")
        small_g["norm1_g"][l] = dg1[0]
    grad_x = dcur.reshape(x.shape)

    out = {}
    for k in SHARDED:
        shp = W[k].shape
        as3 = lambda t: t.reshape((shp[0], -1, shp[-1]))
        slots = [r.reshape((N_DEV, -1, shp[-1])) for r in recv[k]]
        res = _adamw(as3(W[k]), as3(Mo[k]), as3(Vo[k]), slots, f"adamw_{k}")
        out[k] = [t.reshape(shp) for t in res]

    small_list = [jnp.stack(small_g[k]) if k != "final_norm_g" else dgf[0] for k in SMALL]
    small_list = [t.reshape(W[k].shape) for t, k in zip(small_list, SMALL)]
    packed_parts = _all_gather([_pack(small_list)], "gather_small_grads")[0]
    res = _adamw_packed(_pack([W[k] for k in SMALL]), _pack([Mo[k] for k in SMALL]), _pack([Vo[k] for k in SMALL]),
                        packed_parts, "adamw_small")
    unpacked = [_unpack(t, [W[k] for k in SMALL]) for t in res]
    for i, k in enumerate(SMALL):
        out[k] = [unpacked[j][i] for j in range(4)]

    return (loss, grad_x, *[out[k][0] for k in WEIGHTS], *[out[k][1] for k in WEIGHTS],
            *[out[k][2] for k in WEIGHTS], *[out[k][3] for k in WEIGHTS])
```
